```python
import math
import jax
import jax.numpy as jnp
from jax import lax
import numpy as np

D_MODEL = 1024
BATCH = 16
SEQ = 4096
DEPTH = 2
DEC_BATCH = 8
DEC_SEQ = 64
PAST_LEN = 1024

CHUNK = 64
N_EVEN = (DEPTH + 1) // 2
N_ODD = DEPTH // 2
S5_WIDTH = D_MODEL // 2
S5_GROUP = 16
S5_GROUPS = S5_WIDTH // S5_GROUP
S5_STATE = 64
GLA_WIDTH = D_MODEL // 2
GLA_HEADS = 4
GLA_DK = GLA_WIDTH // 2 // GLA_HEADS
GLA_DV = GLA_WIDTH // GLA_HEADS
GLA_GATE_RANK = 16
GLA_GATE_NORM = 16.0
HEAD_DIM = 64
SWA_WIDTH = D_MODEL // 2
SWA_HEADS = SWA_WIDTH // HEAD_DIM
SWA_KV_HEADS = 2
SWA_GROUP = SWA_HEADS // SWA_KV_HEADS
WINDOW = 128
SSD_INNER = D_MODEL // 2
SSD_HEAD_DIM = 64
SSD_HEADS = SSD_INNER // SSD_HEAD_DIM
SSD_STATE = 128
SSD_GROUPS = 2
SSD_CONV = 4
SSD_CONV_DIM = SSD_INNER + 2 * SSD_GROUPS * SSD_STATE
D_FF = 2816
FFN_CONV = 3
T5_BUCKETS = 32
T5_MAX_DIST = 128
RMS_EPS = 1e-6
NEG_INF = -1e30
EVEN_IN = S5_WIDTH + 2 * GLA_HEADS * GLA_DK + 2 * GLA_WIDTH + GLA_GATE_RANK
ODD_IN = SWA_WIDTH + 2 * SWA_KV_HEADS * HEAD_DIM + SSD_INNER + SSD_CONV_DIM + SSD_HEADS

kernel_name = 'hybrid_streaming_encoder_step'

STATE_NAMES = ('s5_re', 's5_im', 'gla', 'swa_k', 'swa_v', 'ssd', 'ssd_conv', 'ffn_conv')


def _rms(x, g):
    xf = x.astype(jnp.float32)
    y = xf * lax.rsqrt(jnp.mean(xf * xf, axis=-1, keepdims=True) + RMS_EPS)
    return (y * g.astype(jnp.float32)).astype(x.dtype)


def _chunk_len(t):
    return CHUNK if t % CHUNK == 0 else t


def _causal_dwconv(x, prefix, w, b):
    width = w.shape[0]
    t = x.shape[1]
    xp = jnp.concatenate([prefix.astype(x.dtype), x], axis=1)
    out = b
    for j in range(width):
        out = out + xp[:, j:j + t] * w[j]
    return out, xp[:, t:]


def _t5_bucket(rel):
    nb = T5_BUCKETS // 2
    max_exact = nb // 2
    ret = (rel > 0).astype(jnp.int32) * nb
    n = jnp.abs(rel)
    nf = jnp.maximum(n, 1).astype(jnp.float32)
    large = max_exact + (jnp.log(nf / max_exact) / math.log(T5_MAX_DIST / max_exact)
                         * (nb - max_exact)).astype(jnp.int32)
    large = jnp.minimum(large, nb - 1)
    return ret + jnp.where(n < max_exact, n, large)


def _rel_bias(table, n_q, n_k, k_offset):
    rel = (jnp.arange(n_k)[None, :] - k_offset) - jnp.arange(n_q)[:, None]
    bias = jnp.transpose(table.astype(jnp.float32)[_t5_bucket(rel)], (2, 0, 1))
    return bias.reshape(SWA_KV_HEADS, SWA_GROUP, n_q, n_k)


def _swa_attend(q, k, v, bias, valid, sink):
    s = jnp.einsum('bnqkgd,bnskd->bnkgqs', q, k).astype(jnp.float32) * HEAD_DIM ** -0.5 + bias
    s = jnp.where(valid[None, :, None, None, None, :], s, NEG_INF)
    sink_col = jnp.broadcast_to(sink.astype(jnp.float32)[None, None, :, :, None, None], s.shape[:-1] + (1,))
    p = jax.nn.softmax(jnp.concatenate([s, sink_col], axis=-1), axis=-1)[..., :-1]
    return jnp.einsum('bnkgqs,bnskd->bnqkgd', p.astype(v.dtype), v)


def _swa_prompt(q, k, v, table, sink):
    b, t = q.shape[:2]
    nc = t // CHUNK
    nb = WINDOW // CHUNK

    def band(a):
        ap = jnp.concatenate([jnp.zeros((b, WINDOW) + a.shape[2:], a.dtype), a], axis=1)
        ap = ap.reshape((b, nc + nb, CHUNK) + a.shape[2:])
        return jnp.concatenate([ap[:, j:j + nc] for j in range(nb + 1)], axis=2)

    key_pos = jnp.arange(nc)[:, None] * CHUNK - WINDOW + jnp.arange(WINDOW + CHUNK)[None, :]
    bias = _rel_bias(table, CHUNK, WINDOW + CHUNK, WINDOW)
    qc = q.reshape((b, nc, CHUNK) + q.shape[2:])
    o = _swa_attend(qc, band(k), band(v), bias, key_pos >= 0, sink)
    return o.reshape(b, t, SWA_WIDTH), k[:, -WINDOW:], v[:, -WINDOW:]


def _swa_sample(q, k, v, cache_k, cache_v, table, sink):
    b, t = q.shape[:2]
    kk = jnp.concatenate([cache_k.astype(k.dtype), k], axis=1)[:, None]
    vv = jnp.concatenate([cache_v.astype(v.dtype), v], axis=1)[:, None]
    bias = _rel_bias(table, t, WINDOW + t, WINDOW)
    valid = jnp.ones((1, WINDOW + t), bool)
    o = _swa_attend(q[:, None], kk, vv, bias, valid, sink)
    return o.reshape(b, t, SWA_WIDTH)


def _cplx_combine(e1, e2):
    a1r, a1i, b1r, b1i = e1
    a2r, a2i, b2r, b2i = e2
    return (a2r * a1r - a2i * a1i, a2r * a1i + a2i * a1r,
            a2r * b1r - a2i * b1i + b2r, a2r * b1i + a2i * b1r + b2i)


def _s5_scan(u, a_re, a_im, log_dt, b_re, b_im, c_re, c_im, d, x0_re, x0_im):
    f32 = jnp.float32
    a_re, a_im = a_re.astype(f32), a_im.astype(f32)
    b_re, b_im, c_re, c_im = b_re.astype(f32), b_im.astype(f32), c_re.astype(f32), c_im.astype(f32)
    dt = jnp.exp(log_dt.astype(f32))[:, None]
    mag = jnp.exp(a_re * dt)
    ab_re, ab_im = mag * jnp.cos(a_im * dt), mag * jnp.sin(a_im * dt)
    den = a_re * a_re + a_im * a_im
    num_re, num_im = ab_re - 1.0, ab_im
    g_re = (num_re * a_re + num_im * a_im) / den
    g_im = (num_im * a_re - num_re * a_im) / den
    bb_re = g_re[..., None] * b_re - g_im[..., None] * b_im
    bb_im = g_re[..., None] * b_im + g_im[..., None] * b_re
    bsz, t, g, h = u.shape
    L = _chunk_len(t)
    nc = t // L
    uc = jnp.moveaxis(u.reshape(bsz, nc, L, g, h), 1, 0)

    def step(carry, u_blk):
        xr, xi = carry
        bu_re = jnp.einsum('gph,blgh->blgp', bb_re, u_blk)
        bu_im = jnp.einsum('gph,blgh->blgp', bb_im, u_blk)
        bu_re = bu_re.at[:, 0].add(ab_re * xr - ab_im * xi)
        bu_im = bu_im.at[:, 0].add(ab_re * xi + ab_im * xr)
        ar = jnp.broadcast_to(ab_re, bu_re.shape)
        ai = jnp.broadcast_to(ab_im, bu_re.shape)
        _, _, sr, si = lax.associative_scan(_cplx_combine, (ar, ai, bu_re, bu_im), axis=1)
        y = jnp.einsum('ghp,blgp->blgh', c_re, sr) - jnp.einsum('ghp,blgp->blgh', c_im, si)
        return (sr[:, -1], si[:, -1]), y

    (xr, xi), y = lax.scan(step, (x0_re, x0_im), uc)
    y = jnp.moveaxis(y, 0, 1).reshape(bsz, t, g, h) + d.astype(f32).reshape(g, h) * u
    return y, xr, xi


def _gla(q, k, v, g, s0):
    bsz, t, h, dk = q.shape
    dv = v.shape[-1]
    L = _chunk_len(t)
    nc = t // L
    q, k, v, g = [a.reshape(bsz, nc, L, h, a.shape[-1]) for a in (q, k, v, g)]
    cum = jnp.cumsum(g, axis=2)
    cum_last = cum[:, :, -1]
    qe = q * jnp.exp(cum)
    ke = k * jnp.exp(-cum)
    causal = jnp.tril(jnp.ones((L, L), bool))
    att = jnp.where(causal, jnp.einsum('bclhk,bcshk->bchls', qe, ke), 0.0)
    o = jnp.einsum('bchls,bcshv->bclhv', att, v)
    upd = jnp.einsum('bclhk,bclhv->bchkv', k * jnp.exp(cum_last[:, :, None] - cum), v)

    def step(s, inp):
        dec, u = inp
        return dec[..., None] * s + u, s

    s_fin, s_prev = lax.scan(step, s0, (jnp.moveaxis(jnp.exp(cum_last), 1, 0), jnp.moveaxis(upd, 1, 0)))
    o = o + jnp.einsum('bclhk,cbhkv->bclhv', qe, s_prev)
    return o.reshape(bsz, t, h, dv), s_fin


def _ssd(x, dt, a, bm, cm, s0):
    bsz, t, h, p = x.shape
    L = _chunk_len(t)
    nc = t // L
    x = x.reshape(bsz, nc, L, h, p)
    dt = dt.reshape(bsz, nc, L, h)
    bm = bm.reshape(bsz, nc, L, h, -1)
    cm = cm.reshape(bsz, nc, L, h, -1)
    cum = jnp.cumsum(dt * a, axis=2)
    seg = cum[:, :, :, None, :] - cum[:, :, None, :, :]
    causal = jnp.tril(jnp.ones((L, L), bool))[..., None]
    decay = jnp.exp(jnp.where(causal, seg, -jnp.inf))
    cb = jnp.einsum('bclhn,bcshn->bclsh', cm, bm)
    y = jnp.einsum('bclsh,bcsh,bcshp->bclhp', cb * decay, dt, x)
    upd = jnp.einsum('bclh,bclhn,bclhp->bchpn', jnp.exp(cum[:, :, -1:] - cum) * dt, bm, x)

    def step(s, inp):
        dec, u = inp
        return dec[..., None, None] * s + u, s

    s_fin, s_prev = lax.scan(step, s0, (jnp.moveaxis(jnp.exp(cum[:, :, -1]), 1, 0), jnp.moveaxis(upd, 1, 0)))
    y = y + jnp.einsum('bclhn,cbhpn,bclh->bclhp', cm, s_prev, jnp.exp(cum))
    return y.reshape(bsz, t, h, p), s_fin


def _even_mixers(h, P, i, s5_re0, s5_im0, gla_s0):
    f32 = jnp.float32
    bsz, t, _ = h.shape
    z = h @ P['ev_w_in'][i]
    o1 = S5_WIDTH
    o2 = o1 + GLA_HEADS * GLA_DK
    o3 = o2 + GLA_HEADS * GLA_DK
    o4 = o3 + GLA_WIDTH
    o5 = o4 + GLA_GATE_RANK
    u, q, k, v, gl, r = jnp.split(z, [o1, o2, o3, o4, o5], axis=-1)
    ua = u.astype(f32).reshape(bsz, t, S5_GROUPS, S5_GROUP)
    ya, s5_re, s5_im = _s5_scan(ua, P['s5_a_re'][i], P['s5_a_im'][i], P['s5_log_dt'][i],
                                P['s5_b_re'][i], P['s5_b_im'][i], P['s5_c_re'][i], P['s5_c_im'][i],
                                P['s5_d'][i], s5_re0.astype(f32), s5_im0.astype(f32))
    ya = jax.nn.gelu(ya.reshape(bsz, t, S5_WIDTH))
    ya = ya * jax.nn.sigmoid(ya @ P['s5_w_glu'][i].astype(f32) + P['s5_b_glu'][i].astype(f32))
    qh = q.astype(f32).reshape(bsz, t, GLA_HEADS, GLA_DK) * GLA_DK ** -0.5
    kh = k.astype(f32).reshape(bsz, t, GLA_HEADS, GLA_DK)
    vh = v.astype(f32).reshape(bsz, t, GLA_HEADS, GLA_DV)
    gate = jax.nn.log_sigmoid((gl @ P['gla_w_gate2'][i] + P['gla_b_gate'][i]).astype(f32)) / GLA_GATE_NORM
    ob, gla_s = _gla(qh, kh, vh, gate.reshape(bsz, t, GLA_HEADS, GLA_DK), gla_s0.astype(f32))
    ob = _rms(ob, P['gla_norm_g'][i]) * jax.nn.silu(r.astype(f32).reshape(bsz, t, GLA_HEADS, GLA_DV))
    mixed = jnp.concatenate([ya, ob.reshape(bsz, t, GLA_WIDTH)], axis=-1).astype(h.dtype)
    return mixed @ P['ev_w_out'][i], s5_re, s5_im, gla_s


def _odd_mixers(h, P, i, ssd_s0, conv_prefix, cache_k, cache_v):
    f32 = jnp.float32
    bsz, t, _ = h.shape
    z = h @ P['od_w_in'][i]
    o1 = SWA_WIDTH
    o2 = o1 + SWA_KV_HEADS * HEAD_DIM
    o3 = o2 + SWA_KV_HEADS * HEAD_DIM
    o4 = o3 + SSD_INNER
    o5 = o4 + SSD_CONV_DIM
    q, k, v, zg, xbc, dtr = jnp.split(z, [o1, o2, o3, o4, o5], axis=-1)
    q = _rms(q.reshape(bsz, t, SWA_KV_HEADS, SWA_GROUP, HEAD_DIM), P['swa_q_norm'][i])
    k = _rms(k.reshape(bsz, t, SWA_KV_HEADS, HEAD_DIM), P['swa_k_norm'][i])
    v = v.reshape(bsz, t, SWA_KV_HEADS, HEAD_DIM)
    sink = P['swa_sink'][i].reshape(SWA_KV_HEADS, SWA_GROUP)
    if cache_k is None:
        oc, k_new, v_new = _swa_prompt(q, k, v, P['t5_bias'], sink)
    else:
        oc = _swa_sample(q, k, v, cache_k, cache_v, P['t5_bias'], sink)
        k_new, v_new = k, v
    xbc, conv_state = _causal_dwconv(xbc, conv_prefix, P['ssd_conv_w'][i], P['ssd_conv_b'][i])
    xbc = jax.nn.silu(xbc.astype(f32))
    xs, bm, cm = jnp.split(xbc, [SSD_INNER, SSD_INNER + SSD_GROUPS * SSD_STATE], axis=-1)
    xs = xs.reshape(bsz, t, SSD_HEADS, SSD_HEAD_DIM)
    rep = SSD_HEADS // SSD_GROUPS
    bm = jnp.repeat(bm.reshape(bsz, t, SSD_GROUPS, SSD_STATE), rep, axis=2)
    cm = jnp.repeat(cm.reshape(bsz, t, SSD_GROUPS, SSD_STATE), rep, axis=2)
    dt = jax.nn.softplus(dtr.astype(f32) + P['ssd_dt_bias'][i].astype(f32))
    a = -jnp.exp(P['ssd_a_log'][i].astype(f32))
    yd, ssd_s = _ssd(xs, dt, a, bm, cm, ssd_s0.astype(f32))
    yd = yd + P['ssd_d'][i].astype(f32)[:, None] * xs
    yd = _rms(yd.reshape(bsz, t, SSD_INNER) * jax.nn.silu(zg.astype(f32)), P['ssd_norm_g'][i])
    mixed = jnp.concatenate([oc.astype(f32), yd], axis=-1).astype(h.dtype)
    return mixed @ P['od_w_out'][i], k_new, v_new, ssd_s, conv_state


def _conv_ffn(h, P, layer, prefix):
    u = h @ P['ffn_w_up'][layer]
    uc, new_prefix = _causal_dwconv(u, prefix, P['ffn_conv_w'][layer], P['ffn_conv_b'][layer])
    a, gate = jnp.split(uc, 2, axis=-1)
    return (jax.nn.silu(a) * gate) @ P['ffn_w_down'][layer], new_prefix


def _trunk(x, c, P, st, sample):
    new = {name: [] for name in STATE_NAMES}
    cm = jax.nn.silu(c)
    for layer in range(DEPTH):
        i = layer // 2
        mod = cm @ P['w_mod'][layer] + P['b_mod'][layer]
        sh1, sc1, g1, sh2, sc2, g2 = jnp.split(mod[:, None, :], 6, axis=-1)
        hn = _rms(x, P['norm1_g'][layer]) * (1 + sc1) + sh1
        if layer % 2 == 0:
            out, sr, si, sg = _even_mixers(hn, P, i, st['s5_re'][i], st['s5_im'][i], st['gla'][i])
            new['s5_re'].append(sr)
            new['s5_im'].append(si)
            new['gla'].append(sg)
        else:
            ck = st['swa_k'][i] if sample else None
            cv = st['swa_v'][i] if sample else None
            out, kn, vn, ss, sc = _odd_mixers(hn, P, i, st['ssd'][i], st['ssd_conv'][i], ck, cv)
            new['swa_k'].append(kn)
            new['swa_v'].append(vn)
            new['ssd'].append(ss)
            new['ssd_conv'].append(sc)
        x = x + g1 * out.astype(x.dtype)
        hn = _rms(x, P['norm2_g'][layer]) * (1 + sc2) + sh2
        f, fp = _conv_ffn(hn, P, layer, st['ffn_conv'][layer])
        new['ffn_conv'].append(fp)
        x = x + g2 * f.astype(x.dtype)
    return x, {name: jnp.stack(vals) for name, vals in new.items()}


def setup_inputs(seed: int = 0) -> dict:
    key = jax.random.key(seed)
    ks = iter(jax.random.split(key, 64))
    f32 = jnp.float32

    def nrm(shape, scale=1.0):
        return scale * jax.random.normal(next(ks), shape, f32)

    def unif(shape, lo, hi):
        return jax.random.uniform(next(ks), shape, f32, lo, hi)

    E, O = N_EVEN, N_ODD
    dt_ssd = jnp.exp(unif((O, SSD_HEADS), math.log(1e-3), math.log(1e-1)))
    a_im0 = jnp.broadcast_to(math.pi * jnp.arange(S5_STATE, dtype=f32), (E, S5_GROUPS, S5_STATE))
    return {
        'x_prompt': nrm((BATCH, SEQ, D_MODEL)),
        'x_sample': nrm((DEC_BATCH, DEC_SEQ, D_MODEL)),
        'state_s5_re': nrm((E, DEC_BATCH, S5_GROUPS, S5_STATE), 0.3),
        'state_s5_im': nrm((E, DEC_BATCH, S5_GROUPS, S5_STATE), 0.3),
        'state_gla': nrm((E, DEC_BATCH, GLA_HEADS, GLA_DK, GLA_DV), 1.0),
        'cache_swa_k': nrm((O, DEC_BATCH, WINDOW, SWA_KV_HEADS, HEAD_DIM)),
        'cache_swa_v': nrm((O, DEC_BATCH, WINDOW, SWA_KV_HEADS, HEAD_DIM)),
        'state_ssd': nrm((O, DEC_BATCH, SSD_HEADS, SSD_HEAD_DIM, SSD_STATE), 0.3),
        'state_ssd_conv': nrm((O, DEC_BATCH, SSD_CONV - 1, SSD_CONV_DIM)),
        'state_ffn_conv': nrm((DEPTH, DEC_BATCH, FFN_CONV - 1, 2 * D_FF)),
        'c_prompt': nrm((BATCH, D_MODEL)),
        'c_sample': nrm((DEC_BATCH, D_MODEL)),
        't5_bias': nrm((T5_BUCKETS, SWA_HEADS), 0.5),
        'norm1_g': 1.0 + nrm((DEPTH, D_MODEL), 0.02),
        'norm2_g': 1.0 + nrm((DEPTH, D_MODEL), 0.02),
        'w_mod': nrm((DEPTH, D_MODEL, 6 * D_MODEL), 0.5 * D_MODEL ** -0.5),
        'b_mod': nrm((DEPTH, 6 * D_MODEL), 0.02),
        'ffn_w_up': nrm((DEPTH, D_MODEL, 2 * D_FF), D_MODEL ** -0.5),
        'ffn_conv_w': nrm((DEPTH, FFN_CONV, 2 * D_FF), 0.5),
        'ffn_conv_b': nrm((DEPTH, 2 * D_FF), 0.02),
        'ffn_w_down': nrm((DEPTH, D_FF, D_MODEL), D_FF ** -0.5),
        'ev_w_in': nrm((E, D_MODEL, EVEN_IN), D_MODEL ** -0.5),
        'ev_w_out': nrm((E, S5_WIDTH + GLA_WIDTH, D_MODEL), (S5_WIDTH + GLA_WIDTH) ** -0.5),
        's5_a_re': -0.5 + nrm((E, S5_GROUPS, S5_STATE), 0.01),
        's5_a_im': a_im0 + nrm((E, S5_GROUPS, S5_STATE), 0.01),
        's5_log_dt': unif((E, S5_GROUPS), math.log(1e-3), math.log(1e-1)),
        's5_b_re': nrm((E, S5_GROUPS, S5_STATE, S5_GROUP), (2 * S5_GROUP) ** -0.5),
        's5_b_im': nrm((E, S5_GROUPS, S5_STATE, S5_GROUP), (2 * S5_GROUP) ** -0.5),
        's5_c_re': nrm((E, S5_GROUPS, S5_GROUP, S5_STATE), 0.5),
        's5_c_im': nrm((E, S5_GROUPS, S5_GROUP, S5_STATE), 0.5),
        's5_d': nrm((E, S5_WIDTH), 1.0),
        's5_w_glu': nrm((E, S5_WIDTH, S5_WIDTH), S5_WIDTH ** -0.5),
        's5_b_glu': nrm((E, S5_WIDTH), 0.02),
        'gla_w_gate2': nrm((E, GLA_GATE_RANK, GLA_HEADS * GLA_DK), GLA_GATE_RANK ** -0.5),
        'gla_b_gate': nrm((E, GLA_HEADS * GLA_DK), 0.02),
        'gla_norm_g': 1.0 + nrm((E, GLA_DV), 0.02),
        'od_w_in': nrm((O, D_MODEL, ODD_IN), D_MODEL ** -0.5),
        'od_w_out': nrm((O, SWA_WIDTH + SSD_INNER, D_MODEL), (SWA_WIDTH + SSD_INNER) ** -0.5),
        'swa_q_norm': 1.0 + nrm((O, HEAD_DIM), 0.02),
        'swa_k_norm': 1.0 + nrm((O, HEAD_DIM), 0.02),
        'swa_sink': nrm((O, SWA_HEADS), 0.5),
        'ssd_conv_w': nrm((O, SSD_CONV, SSD_CONV_DIM), 0.5),
        'ssd_conv_b': nrm((O, SSD_CONV_DIM), 0.02),
        'ssd_dt_bias': dt_ssd + jnp.log(-jnp.expm1(-dt_ssd)),
        'ssd_a_log': jnp.log(unif((O, SSD_HEADS), 1.0, 16.0)),
        'ssd_d': 1.0 + nrm((O, SSD_HEADS), 0.1),
        'ssd_norm_g': 1.0 + nrm((O, SSD_INNER), 0.02),
    }


def reference(x_prompt, x_sample, state_s5_re, state_s5_im, state_gla, cache_swa_k, cache_swa_v,
              state_ssd, state_ssd_conv, state_ffn_conv, c_prompt, c_sample, t5_bias, norm1_g, norm2_g,
              w_mod, b_mod, ffn_w_up, ffn_conv_w, ffn_conv_b, ffn_w_down, ev_w_in, ev_w_out,
              s5_a_re, s5_a_im, s5_log_dt, s5_b_re, s5_b_im, s5_c_re, s5_c_im, s5_d, s5_w_glu, s5_b_glu,
              gla_w_gate2, gla_b_gate, gla_norm_g, od_w_in, od_w_out, swa_q_norm, swa_k_norm, swa_sink,
              ssd_conv_w, ssd_conv_b, ssd_dt_bias, ssd_a_log, ssd_d, ssd_norm_g):
    f32 = jnp.float32
    P = dict(t5_bias=t5_bias, norm1_g=norm1_g, norm2_g=norm2_g, w_mod=w_mod, b_mod=b_mod,
             ffn_w_up=ffn_w_up, ffn_conv_w=ffn_conv_w, ffn_conv_b=ffn_conv_b, ffn_w_down=ffn_w_down,
             ev_w_in=ev_w_in, ev_w_out=ev_w_out, s5_a_re=s5_a_re, s5_a_im=s5_a_im, s5_log_dt=s5_log_dt,
             s5_b_re=s5_b_re, s5_b_im=s5_b_im, s5_c_re=s5_c_re, s5_c_im=s5_c_im, s5_d=s5_d,
             s5_w_glu=s5_w_glu, s5_b_glu=s5_b_glu, gla_w_gate2=gla_w_gate2, gla_b_gate=gla_b_gate,
             gla_norm_g=gla_norm_g, od_w_in=od_w_in, od_w_out=od_w_out, swa_q_norm=swa_q_norm,
             swa_k_norm=swa_k_norm, swa_sink=swa_sink, ssd_conv_w=ssd_conv_w, ssd_conv_b=ssd_conv_b,
             ssd_dt_bias=ssd_dt_bias, ssd_a_log=ssd_a_log, ssd_d=ssd_d, ssd_norm_g=ssd_norm_g)
    bp = x_prompt.shape[0]
    zero_st = dict(
        s5_re=jnp.zeros((N_EVEN, bp, S5_GROUPS, S5_STATE), f32),
        s5_im=jnp.zeros((N_EVEN, bp, S5_GROUPS, S5_STATE), f32),
        gla=jnp.zeros((N_EVEN, bp, GLA_HEADS, GLA_DK, GLA_DV), f32),
        ssd=jnp.zeros((N_ODD, bp, SSD_HEADS, SSD_HEAD_DIM, SSD_STATE), f32),
        ssd_conv=jnp.zeros((N_ODD, bp, SSD_CONV - 1, SSD_CONV_DIM), x_prompt.dtype),
        ffn_conv=jnp.zeros((DEPTH, bp, FFN_CONV - 1, 2 * D_FF), x_prompt.dtype))
    sample_st = dict(s5_re=state_s5_re, s5_im=state_s5_im, gla=state_gla, swa_k=cache_swa_k,
                     swa_v=cache_swa_v, ssd=state_ssd, ssd_conv=state_ssd_conv, ffn_conv=state_ffn_conv)
    y_prompt, stp = _trunk(x_prompt, c_prompt, P, zero_st, False)
    y_sample, sts = _trunk(x_sample, c_sample, P, sample_st, True)
    p_s5_re, p_s5_im, p_gla = stp['s5_re'], stp['s5_im'], stp['gla']
    p_swa_k, p_swa_v, p_ssd = stp['swa_k'], stp['swa_v'], stp['ssd']
    p_ssd_conv, p_ffn_conv = stp['ssd_conv'], stp['ffn_conv']
    s_s5_re, s_s5_im, s_gla = sts['s5_re'], sts['s5_im'], sts['gla']
    s_swa_k, s_swa_v, s_ssd = sts['swa_k'], sts['swa_v'], sts['ssd']
    s_ssd_conv, s_ffn_conv = sts['ssd_conv'], sts['ffn_conv']
    return (y_prompt, y_sample,
            p_s5_re, p_s5_im, p_gla, p_swa_k, p_swa_v, p_ssd, p_ssd_conv, p_ffn_conv,
            s_s5_re, s_s5_im, s_gla, s_swa_k, s_swa_v, s_ssd, s_ssd_conv, s_ffn_conv)
```

```python
import functools
import math

import jax
import jax.numpy as jnp
import numpy as np
from jax import lax
from jax.experimental import pallas as pl
from jax.experimental.pallas import tpu as pltpu

F32 = jnp.float32
BF16 = jnp.bfloat16

CHUNK = 64
WINDOW = 128
S5_GROUP = 16
S5_STATE = 64
S5_SUB = 16
GLA_HEADS = 4
GLA_GATE_NORM = 16.0
HEAD_DIM = 64
SWA_KV_HEADS = 2
SSD_HEAD_DIM = 64
SSD_STATE = 128
SSD_GROUPS = 2
T5_BUCKETS = 32
T5_MAX_DIST = 128
RMS_EPS = 1e-6
NEG_INF = -1e30
LANES = 128
SUBLANES = 8
MAX_TILE = 512
FFN_COL_BLOCK = 256
VMEM_LIMIT = 56 * 1024 * 1024

_NT = (((1,), (1,)), ((), ()))
_HI = lax.Precision.HIGHEST


def _params(n_axes=2):
    sem = ("parallel",) + ("arbitrary",) * (n_axes - 1)
    return pltpu.CompilerParams(dimension_semantics=sem, vmem_limit_bytes=VMEM_LIMIT)


def _bdot(a, b):
    return jnp.dot(a.astype(BF16), b.astype(BF16), preferred_element_type=F32)


def _bdot_nt(a, b):
    return lax.dot_general(a.astype(BF16), b.astype(BF16), _NT, preferred_element_type=F32)


def _split3(x):
    hi = x.astype(BF16)
    r1 = x - hi.astype(F32)
    mid = r1.astype(BF16)
    lo = (r1 - mid.astype(F32)).astype(BF16)
    return hi, mid, lo


def _dot3_left(c, x):
    hi, mid, lo = _split3(x)
    d = lambda p: jnp.dot(c, p, preferred_element_type=F32)
    return d(hi) + d(mid) + d(lo)


def _dot3_right(x, c):
    hi, mid, lo = _split3(x)
    d = lambda p: jnp.dot(p, c, preferred_element_type=F32)
    return d(hi) + d(mid) + d(lo)


def _silu(x):
    return x * jax.nn.sigmoid(x)


def _softplus(x):
    return jnp.maximum(x, 0.0) + jnp.log1p(jnp.exp(-jnp.abs(x)))


def _log_sigmoid(x):
    return jnp.minimum(x, 0.0) - jnp.log1p(jnp.exp(-jnp.abs(x)))


def _norm_mod(x, g, scale, shift):
    ms = jnp.mean(x * x, axis=-1, keepdims=True)
    return (x * lax.rsqrt(ms + RMS_EPS) * g) * (1.0 + scale) + shift


def _tri(n, lower):
    r = lax.broadcasted_iota(jnp.int32, (n, n), 0)
    c = lax.broadcasted_iota(jnp.int32, (n, n), 1)
    return (r >= c) if lower else (r <= c)


def _causal_conv(u, prev8, w, b):
    width = w.shape[0]
    n = u.shape[0]
    ext = jnp.concatenate([prev8, u[0:SUBLANES]], axis=0)
    full = b
    head = b
    for j in range(width):
        sh = width - 1 - j
        if sh == 0:
            full = full + u * w[j:j + 1]
            head = head + u[0:SUBLANES] * w[j:j + 1]
        else:
            full = full + pltpu.roll(u, sh, axis=0) * w[j:j + 1]
            head = head + pltpu.roll(ext, sh, axis=0)[SUBLANES:2 * SUBLANES] * w[j:j + 1]
    if n == SUBLANES:
        return head
    return jnp.concatenate([head, full[SUBLANES:]], axis=0)


def _mod_kernel(c_ref, w_ref, b_ref, o_ref):
    o_ref[0] = _bdot(_silu(c_ref[...]), w_ref[0]) + b_ref[0]


def _modulation(c, w_mod, b_mod):
    depth, d, n = w_mod.shape
    bc = c.shape[0]
    tn = n // 4
    return pl.pallas_call(
        _mod_kernel,
        grid=(depth, n // tn),
        in_specs=[pl.BlockSpec((bc, d), lambda l, j: (0, 0)),
                  pl.BlockSpec((1, d, tn), lambda l, j: (l, 0, j)),
                  pl.BlockSpec((1, 1, tn), lambda l, j: (l, 0, j))],
        out_specs=pl.BlockSpec((1, bc, tn), lambda l, j: (l, 0, j)),
        out_shape=jax.ShapeDtypeStruct((depth, bc, n), F32),
        compiler_params=_params(2),
        name="modulation",
    )(c, w_mod, b_mod.reshape(depth, 1, n))


def _ev_in_kernel(x_ref, mod_ref, g_ref, w_ref, wg2_ref, bg_ref,
                  u_ref, qk_ref, v_ref, r_ref, gate_ref):
    mod = mod_ref[0]
    hn = _norm_mod(x_ref[0], g_ref[...], mod[1:2], mod[0:1]).astype(BF16)
    dot = lambda lo, hi: jnp.dot(hn, w_ref[:, lo:hi], preferred_element_type=F32)
    u_ref[0] = dot(0, 512)
    qk_ref[0] = dot(512, 1024)
    v_ref[0] = dot(1024, 1536)
    r_ref[0] = dot(1536, 2048)
    gl = dot(2048, 2048 + LANES)
    gate_ref[0] = _log_sigmoid(_bdot(gl, wg2_ref[...]) + bg_ref[...]) * (1.0 / GLA_GATE_NORM)


def _ev_in(x, mod, norm_g, w_in, w_gate2, b_gate, tb):
    b, t, d = x.shape
    rank = w_gate2.shape[0]
    nk = w_gate2.shape[1]
    wu, wq, wk, wv, wgl, wr = jnp.split(w_in, [512, 768, 1024, 1536, 1536 + rank], axis=1)
    w = jnp.concatenate([wu, wq, wk, wv, wr, wgl, jnp.zeros((d, LANES - rank), F32)], axis=1).astype(BF16)
    wg2 = jnp.concatenate([w_gate2, jnp.zeros((LANES - rank, nk), F32)], axis=0).astype(BF16)
    tile = lambda n: pl.BlockSpec((1, tb, n), lambda i, j: (i, j, 0))
    const = lambda a: pl.BlockSpec(a.shape, lambda i, j: (0,) * a.ndim)
    g2 = norm_g.reshape(1, d)
    bg = b_gate.reshape(1, nk)
    return pl.pallas_call(
        _ev_in_kernel,
        grid=(b, t // tb),
        in_specs=[tile(d), pl.BlockSpec((1, 6, d), lambda i, j: (i, 0, 0)), const(g2), const(w),
                  const(wg2), const(bg)],
        out_specs=[tile(512), tile(512), tile(512), tile(512), tile(nk)],
        out_shape=[jax.ShapeDtypeStruct((b, t, n), F32) for n in (512, 512, 512, 512, nk)],
        compiler_params=_params(2),
        name="ev_in",
    )(x, mod, g2, w, wg2, bg)


def _gla_kernel(qk_ref, v_ref, gate_ref, r_ref, s0_ref, ng_ref, o_ref, sfin_ref, st_scr, *, n_chunks):
    t = pl.program_id(1)
    dk = qk_ref.shape[2] // 2 // GLA_HEADS
    dv = v_ref.shape[2] // GLA_HEADS
    nk = GLA_HEADS * dk
    nv = GLA_HEADS * dv
    blk = (lax.broadcasted_iota(jnp.int32, (nv, nk), 0) // dv
           == lax.broadcasted_iota(jnp.int32, (nv, nk), 1) // dk)

    @pl.when(t == 0)
    def _():
        s0 = jnp.concatenate([s0_ref[0, h] for h in range(GLA_HEADS)], axis=0)
        st_scr[...] = jnp.where(blk, jnp.concatenate([s0] * GLA_HEADS, axis=1), 0.0)

    tril = _tri(CHUNK, True)
    tril_b = jnp.where(tril, 1.0, 0.0).astype(BF16)
    lane_head = lax.broadcasted_iota(jnp.int32, (CHUNK, nk), 1) // dk
    ng = ng_ref[...]

    def chunk(c, carry):
        rows = pl.ds(pl.multiple_of(c * CHUNK, CHUNK), CHUNK)
        q = qk_ref[0, rows, 0:nk] * dk ** -0.5
        k = qk_ref[0, rows, nk:2 * nk]
        v = v_ref[0, rows, :]
        cum = _dot3_left(tril_b, gate_ref[0, rows, :])
        cum_last = cum[CHUNK - 1:CHUNK, :]
        qe = q * jnp.exp(cum)
        ke = (k * jnp.exp(-cum)).astype(BF16)
        kd = k * jnp.exp(cum_last - cum)
        st = st_scr[...]
        inter = _bdot_nt(qe, st)
        v_b = v.astype(BF16)
        for h in range(GLA_HEADS):
            qm = jnp.where(lane_head == h, qe, 0.0).astype(BF16)
            att = lax.dot_general(qm, ke, _NT, preferred_element_type=F32)
            att = jnp.where(tril, att, 0.0).astype(BF16)
            cols = slice(h * dv, (h + 1) * dv)
            oh = jnp.dot(att, v_b[:, cols], preferred_element_type=F32) + inter[:, cols]
            ms = jnp.mean(oh * oh, axis=-1, keepdims=True)
            oh = oh * lax.rsqrt(ms + RMS_EPS) * ng
            o_ref[0, rows, cols] = oh * _silu(r_ref[0, rows, cols])
        upd = _bdot(v.T, kd)
        st_scr[...] = jnp.where(blk, st * jnp.exp(cum_last) + upd, 0.0)
        return carry

    lax.fori_loop(0, n_chunks, chunk, 0)

    @pl.when(t == pl.num_programs(1) - 1)
    def _():
        for h in range(GLA_HEADS):
            sfin_ref[0, h] = st_scr[h * dv:(h + 1) * dv, h * dk:(h + 1) * dk]


def _gla(qk, v, gate, r, s0, norm_g, tb):
    b, t, nv = v.shape
    nk = gate.shape[2]
    dk, dv = nk // GLA_HEADS, nv // GLA_HEADS
    s0t = jnp.swapaxes(s0, 2, 3)
    tile = lambda n: pl.BlockSpec((1, tb, n), lambda i, j: (i, j, 0))
    sspec = pl.BlockSpec((1, GLA_HEADS, dv, dk), lambda i, j: (i, 0, 0, 0))
    ng = norm_g.reshape(1, dv)
    o, sfin = pl.pallas_call(
        functools.partial(_gla_kernel, n_chunks=tb // CHUNK),
        grid=(b, t // tb),
        in_specs=[tile(2 * nk), tile(nv), tile(nk), tile(nv), sspec,
                  pl.BlockSpec((1, dv), lambda i, j: (0, 0))],
        out_specs=[tile(nv), sspec],
        out_shape=[jax.ShapeDtypeStruct((b, t, nv), F32),
                   jax.ShapeDtypeStruct((b, GLA_HEADS, dv, dk), F32)],
        scratch_shapes=[pltpu.VMEM((nv, nk), F32)],
        compiler_params=_params(2),
        name="gla",
    )(qk, v, gate, r, s0t, ng)
    return o, jnp.swapaxes(sfin, 2, 3)


def _s5_kernel(u_ref, m_ref, p_ref, n_ref, a1_ref, a2_ref, x0_ref, y_ref, xf_ref, upd_scr, xs_scr,
               *, n_sub, bsz):
    ub = u_ref[0].astype(BF16)
    upd_scr[...] = jnp.dot(ub, p_ref[0], preferred_element_type=F32)
    a1 = a1_ref[0]
    a2 = a2_ref[0]

    def step(j, st):
        rows = pl.ds(pl.multiple_of(j * bsz, bsz), bsz)
        xs_scr[rows, :] = st
        return a1 * st + a2 * pltpu.roll(st, S5_STATE, axis=1) + upd_scr[rows, :]

    xf_ref[0] = lax.fori_loop(0, n_sub, step, x0_ref[0])
    y_ref[0] = (jnp.dot(ub, m_ref[0], preferred_element_type=F32)
                + jnp.dot(xs_scr[...].astype(BF16), n_ref[0], preferred_element_type=F32))


def _cmul(ar, ai, br, bi):
    return ar * br - ai * bi, ar * bi + ai * br


def _s5_matrices(a_re, a_im, log_dt, b_re, b_im, c_re, c_im):
    g, p = a_re.shape
    hdim = b_re.shape[-1]
    dt = jnp.exp(log_dt)[:, None]
    mag = jnp.exp(a_re * dt)
    ab_re, ab_im = mag * jnp.cos(a_im * dt), mag * jnp.sin(a_im * dt)
    den = a_re * a_re + a_im * a_im
    num_re, num_im = ab_re - 1.0, ab_im
    g_re = (num_re * a_re + num_im * a_im) / den
    g_im = (num_im * a_re - num_re * a_im) / den
    bb_re = g_re[..., None] * b_re - g_im[..., None] * b_im
    bb_im = g_re[..., None] * b_im + g_im[..., None] * b_re
    pw_re, pw_im = jnp.ones((1, g, p), F32), jnp.zeros((1, g, p), F32)
    sq_re, sq_im = ab_re, ab_im
    while pw_re.shape[0] < S5_SUB + 1:
        nr, ni = _cmul(pw_re, pw_im, sq_re, sq_im)
        pw_re, pw_im = jnp.concatenate([pw_re, nr]), jnp.concatenate([pw_im, ni])
        sq_re, sq_im = _cmul(sq_re, sq_im, sq_re, sq_im)
    pw_re, pw_im = pw_re[:S5_SUB + 1], pw_im[:S5_SUB + 1]
    ca_re, ca_im = _cmul(c_re[None], c_im[None], pw_re[:, :, None, :], pw_im[:, :, None, :])
    kern = (jnp.einsum('tghp,gpk->tghk', ca_re[:S5_SUB], bb_re, precision=_HI)
            - jnp.einsum('tghp,gpk->tghk', ca_im[:S5_SUB], bb_im, precision=_HI))
    kern = jnp.concatenate([kern, jnp.zeros((1,) + kern.shape[1:], F32)])
    s_in = np.arange(S5_SUB)[:, None]
    s_out = np.arange(S5_SUB)[None, :]
    lag = np.where(s_out >= s_in, s_out - s_in, S5_SUB)
    m = kern[lag]
    m = jnp.transpose(m, (2, 0, 4, 1, 3)).reshape(g, S5_SUB * hdim, S5_SUB * hdim)
    pr, pi = _cmul(pw_re[S5_SUB - 1::-1][..., None], pw_im[S5_SUB - 1::-1][..., None], bb_re[None], bb_im[None])
    pm = jnp.concatenate([pr, pi], axis=2)
    pm = jnp.transpose(pm, (1, 0, 3, 2)).reshape(g, S5_SUB * hdim, 2 * p)
    nm = jnp.concatenate([ca_re[1:], -ca_im[1:]], axis=3)
    nm = jnp.transpose(nm, (1, 3, 0, 2)).reshape(g, 2 * p, S5_SUB * hdim)
    a_r, a_i = pw_re[S5_SUB], pw_im[S5_SUB]
    a1 = jnp.concatenate([a_r, a_r], axis=1)[:, None, :]
    a2 = jnp.concatenate([-a_i, a_i], axis=1)[:, None, :]
    return m.astype(BF16), pm.astype(BF16), nm.astype(BF16), a1, a2


def _s5(u, mats, x0_re, x0_im):
    m, pm, nm, a1, a2 = mats
    b, t, width = u.shape
    g = m.shape[0]
    hdim = width // g
    n_sub = t // S5_SUB
    cols = S5_SUB * hdim
    p2 = pm.shape[2]
    rows = n_sub * b
    ug = jnp.transpose(u.reshape(b, n_sub, S5_SUB, g, hdim), (3, 1, 0, 2, 4)).reshape(g, rows, cols)
    x0 = jnp.transpose(jnp.concatenate([x0_re, x0_im], axis=2), (1, 0, 2))
    gspec = lambda a: pl.BlockSpec((1,) + a.shape[1:], lambda i: (i, 0, 0))
    y, xf = pl.pallas_call(
        functools.partial(_s5_kernel, n_sub=n_sub, bsz=b),
        grid=(g,),
        in_specs=[gspec(ug), gspec(m), gspec(pm), gspec(nm), gspec(a1), gspec(a2), gspec(x0)],
        out_specs=[gspec(ug), gspec(x0)],
        out_shape=[jax.ShapeDtypeStruct(ug.shape, F32), jax.ShapeDtypeStruct(x0.shape, F32)],
        scratch_shapes=[pltpu.VMEM((rows, p2), F32), pltpu.VMEM((rows, p2), F32)],
        compiler_params=_params(1),
        name="s5",
    )(ug, m, pm, nm, a1, a2, x0)
    y = jnp.transpose(y.reshape(g, n_sub, b, S5_SUB, hdim), (2, 1, 3, 0, 4)).reshape(b, t, width)
    xf = jnp.transpose(xf, (1, 0, 2))
    return y, xf[:, :, :p2 // 2], xf[:, :, p2 // 2:]


def _out_ffn_kernel(*refs, with_s5, n_cb):
    if with_s5:
        (x_ref, mod_ref, u_ref, m1_ref, m2_ref, d_ref, wglu_ref, bglu_ref, wout_ref, n2g_ref,
         wa_ref, wg_ref, cwa_ref, cwg_ref, cba_ref, cbg_ref, wd_ref, pre_ref,
         xo_ref, ulast_ref, carry_scr, acc_scr) = refs
    else:
        (x_ref, mod_ref, m1_ref, m2_ref, wout_ref, n2g_ref,
         wa_ref, wg_ref, cwa_ref, cwg_ref, cba_ref, cbg_ref, wd_ref, pre_ref,
         xo_ref, ulast_ref, carry_scr, acc_scr) = refs
    t = pl.program_id(1)

    @pl.when(t == 0)
    def _():
        carry_scr[...] = pre_ref[0]

    mod = mod_ref[0]
    m1 = m1_ref[0]
    if with_s5:
        ya = jax.nn.gelu(m1 + d_ref[...] * u_ref[0])
        m1 = ya * jax.nn.sigmoid(_bdot(ya, wglu_ref[...]) + bglu_ref[...])
    half = m1.shape[1]
    out = (jnp.dot(m1.astype(BF16), wout_ref[0:half, :], preferred_element_type=F32)
           + jnp.dot(m2_ref[0].astype(BF16), wout_ref[half:, :], preferred_element_type=F32))
    x1 = x_ref[0] + mod[2:3] * out
    hn = _norm_mod(x1, n2g_ref[...], mod[4:5], mod[3:4]).astype(BF16)
    tb = x1.shape[0]
    acc_scr[...] = jnp.zeros_like(acc_scr)

    def block(j, carry):
        ua = jnp.dot(hn, wa_ref[j], preferred_element_type=F32)
        ug = jnp.dot(hn, wg_ref[j], preferred_element_type=F32)
        a = _causal_conv(ua, carry_scr[0, j], cwa_ref[j], cba_ref[j])
        gt = _causal_conv(ug, carry_scr[1, j], cwg_ref[j], cbg_ref[j])
        carry_scr[0, j] = ua[tb - SUBLANES:]
        carry_scr[1, j] = ug[tb - SUBLANES:]
        h = (_silu(a) * gt).astype(BF16)
        acc_scr[...] += jnp.dot(h, wd_ref[j], preferred_element_type=F32)
        return carry

    lax.fori_loop(0, n_cb, block, 0)
    xo_ref[0] = x1 + mod[5:6] * acc_scr[...]

    @pl.when(t == pl.num_programs(1) - 1)
    def _():
        ulast_ref[0] = carry_scr[...]


def _out_ffn(x, mod, m1, m2, w_out, norm2_g, w_up, conv_w, conv_b, w_down, prefix, tb, s5_extra=None):
    b, t, d = x.shape
    dff = w_down.shape[0]
    cb = FFN_COL_BLOCK
    n_cb = dff // cb
    width = conv_w.shape[0]
    half = m1.shape[2]
    blocks = lambda a: a.reshape(a.shape[0], 2, n_cb, cb)
    w_up_b = jnp.transpose(blocks(w_up.astype(BF16)), (1, 2, 0, 3))
    cw = jnp.transpose(blocks(conv_w), (1, 2, 0, 3))
    cbias = jnp.transpose(blocks(conv_b.reshape(1, 2 * dff)), (1, 2, 0, 3))
    wd = w_down.astype(BF16).reshape(n_cb, cb, d)
    pre8 = jnp.concatenate([jnp.zeros((b, SUBLANES - (width - 1), 2 * dff), F32), prefix], axis=1)
    pre8 = jnp.transpose(pre8.reshape(b, SUBLANES, 2, n_cb, cb), (0, 2, 3, 1, 4))
    tile = lambda n: pl.BlockSpec((1, tb, n), lambda i, j: (i, j, 0))
    const = lambda a: pl.BlockSpec(a.shape, lambda i, j: (0,) * a.ndim, pipeline_mode=pl.Buffered(1))
    bspec = lambda a: pl.BlockSpec((1,) + a.shape[1:], lambda i, j: (i,) + (0,) * (a.ndim - 1))
    wo = w_out.astype(BF16)
    n2g = norm2_g.reshape(1, d)
    args = [x, mod]
    specs = [tile(d), bspec(mod)]
    if s5_extra is not None:
        u, s5_d, w_glu, b_glu = s5_extra
        s5_d, w_glu, b_glu = s5_d.reshape(1, half), w_glu.astype(BF16), b_glu.reshape(1, half)
        args += [u, m1, m2, s5_d, w_glu, b_glu]
        specs += [tile(half), tile(half), tile(m2.shape[2]), const(s5_d), const(w_glu), const(b_glu)]
    else:
        args += [m1, m2]
        specs += [tile(half), tile(m2.shape[2])]
    weights = [wo, n2g, w_up_b[0], w_up_b[1], cw[0], cw[1], cbias[0], cbias[1], wd]
    args += weights + [pre8]
    specs += [const(a) for a in weights] + [bspec(pre8)]
    xo, ulast = pl.pallas_call(
        functools.partial(_out_ffn_kernel, with_s5=s5_extra is not None, n_cb=n_cb),
        grid=(b, t // tb),
        in_specs=specs,
        out_specs=[tile(d), bspec(pre8)],
        out_shape=[jax.ShapeDtypeStruct((b, t, d), F32), jax.ShapeDtypeStruct(pre8.shape, F32)],
        scratch_shapes=[pltpu.VMEM(pre8.shape[1:], F32), pltpu.VMEM((tb, d), F32)],
        compiler_params=_params(2),
        name="out_ffn_s5" if s5_extra is not None else "out_ffn",
    )(*args)
    ulast = jnp.transpose(ulast, (0, 3, 1, 2, 4)).reshape(b, SUBLANES, 2 * dff)
    return xo, ulast[:, SUBLANES - (width - 1):]


def _head_rms(x, ones_blk, g):
    ss = _dot3_right(x * x, ones_blk) * (1.0 / HEAD_DIM)
    return x * lax.rsqrt(ss + RMS_EPS) * g


def _od_in_kernel(x_ref, mod_ref, g_ref, w_ref, qg_ref, kg_ref, dtb_ref,
                  q_ref, k_ref, v_ref, zg_ref, xbc_ref, dt_ref, dtt_ref):
    mod = mod_ref[0]
    hn = _norm_mod(x_ref[0], g_ref[...], mod[1:2], mod[0:1]).astype(BF16)
    dot = lambda lo, hi: jnp.dot(hn, w_ref[:, lo:hi], preferred_element_type=F32)
    nq = q_ref.shape[2]
    nkv = k_ref.shape[2]
    hr = lax.broadcasted_iota(jnp.int32, (nq, nq), 0) // HEAD_DIM
    hc = lax.broadcasted_iota(jnp.int32, (nq, nq), 1) // HEAD_DIM
    ones_blk = jnp.where(hr == hc, 1.0, 0.0).astype(BF16)
    q_ref[0] = _head_rms(dot(0, nq), ones_blk, qg_ref[...])
    k_ref[0] = _head_rms(dot(nq, nq + nkv), ones_blk[0:nkv, 0:nkv], kg_ref[...])
    o = nq + nkv
    v_ref[0] = dot(o, o + nkv)
    o += nkv
    zg_ref[0] = dot(o, o + zg_ref.shape[2])
    o += zg_ref.shape[2]
    xbc_ref[0] = dot(o, o + xbc_ref.shape[2])
    o += xbc_ref.shape[2]
    dt = _softplus(dot(o, o + LANES) + dtb_ref[...])
    dt_ref[0] = dt
    dtt_ref[0] = dt.T[0:dtt_ref.shape[1], :]


def _od_in(x, mod, norm_g, w_in, q_norm, k_norm, dt_bias, dims, tb):
    b, t, d = x.shape
    nq, nkv, nz, nxbc, nh = dims
    w = jnp.concatenate([w_in, jnp.zeros((d, LANES - nh), F32)], axis=1).astype(BF16)
    qg = jnp.tile(q_norm, nq // HEAD_DIM).reshape(1, nq)
    kg = jnp.tile(k_norm, nkv // HEAD_DIM).reshape(1, nkv)
    dtb = jnp.concatenate([dt_bias, jnp.zeros((LANES - nh,), F32)]).reshape(1, LANES)
    g2 = norm_g.reshape(1, d)
    tile = lambda n: pl.BlockSpec((1, tb, n), lambda i, j: (i, j, 0))
    const = lambda a: pl.BlockSpec(a.shape, lambda i, j: (0,) * a.ndim)
    widths = (nq, nkv, nkv, nz, nxbc, LANES)
    return pl.pallas_call(
        _od_in_kernel,
        grid=(b, t // tb),
        in_specs=[tile(d), pl.BlockSpec((1, 6, d), lambda i, j: (i, 0, 0)), const(g2), const(w),
                  const(qg), const(kg), const(dtb)],
        out_specs=[tile(n) for n in widths] + [pl.BlockSpec((1, nh, tb), lambda i, j: (i, 0, j))],
        out_shape=[jax.ShapeDtypeStruct((b, t, n), F32) for n in widths]
        + [jax.ShapeDtypeStruct((b, nh, t), F32)],
        compiler_params=_params(2),
        name="od_in",
    )(x, mod, g2, w, qg, kg, dtb)


def _swa_kernel(sink_ref, q_ref, k_ref, v_ref, k0_ref, v0_ref, bias_ref, o_ref, kwin, vwin,
                *, n_chunks, mask_start):
    t = pl.program_id(1)
    tb = q_ref.shape[1]
    n_heads = q_ref.shape[2] // HEAD_DIM
    group = n_heads // (k_ref.shape[2] // HEAD_DIM)
    span = WINDOW + CHUNK

    @pl.when(t == 0)
    def _():
        kwin[0:WINDOW, :] = k0_ref[0]
        vwin[0:WINDOW, :] = v0_ref[0]

    kwin[WINDOW:WINDOW + tb, :] = k_ref[0]
    vwin[WINDOW:WINDOW + tb, :] = v_ref[0]

    def chunk(c, carry):
        r0 = pl.multiple_of(c * CHUNK, CHUNK)
        q = q_ref[0, pl.ds(r0, CHUNK), :]
        kk = kwin[pl.ds(r0, span), :].astype(BF16)
        vv = vwin[pl.ds(r0, span), :].astype(BF16)
        if mask_start:
            pos = t * tb + r0 - WINDOW + lax.broadcasted_iota(jnp.int32, (1, span), 1)
            valid = pos >= 0
        outs = []
        for h in range(n_heads):
            kv = slice((h // group) * HEAD_DIM, (h // group + 1) * HEAD_DIM)
            qh = q[:, h * HEAD_DIM:(h + 1) * HEAD_DIM].astype(BF16)
            s = lax.dot_general(qh, kk[:, kv], _NT, preferred_element_type=F32) * HEAD_DIM ** -0.5
            s = s + bias_ref[h]
            if mask_start:
                s = jnp.where(valid, s, NEG_INF)
            sink = sink_ref[h]
            m = jnp.maximum(jnp.max(s, axis=-1, keepdims=True), sink)
            p = jnp.exp(s - m)
            den = jnp.sum(p, axis=-1, keepdims=True) + jnp.exp(sink - m)
            outs.append(jnp.dot(p.astype(BF16), vv[:, kv], preferred_element_type=F32) / den)
        o_ref[0, pl.ds(r0, CHUNK), :] = jnp.concatenate(outs, axis=1)
        return carry

    lax.fori_loop(0, n_chunks, chunk, 0)
    kwin[0:WINDOW, :] = kwin[tb:tb + WINDOW, :]
    vwin[0:WINDOW, :] = vwin[tb:tb + WINDOW, :]


def _t5_bucket(rel):
    nb = T5_BUCKETS // 2
    max_exact = nb // 2
    ret = (rel > 0).astype(jnp.int32) * nb
    n = jnp.abs(rel)
    nf = jnp.maximum(n, 1).astype(F32)
    large = max_exact + (jnp.log(nf / max_exact) / math.log(T5_MAX_DIST / max_exact)
                         * (nb - max_exact)).astype(jnp.int32)
    large = jnp.minimum(large, nb - 1)
    return ret + jnp.where(n < max_exact, n, large)


def _rel_bias(table):
    rel = (jnp.arange(WINDOW + CHUNK)[None, :] - WINDOW) - jnp.arange(CHUNK)[:, None]
    return jnp.transpose(table[_t5_bucket(rel)], (2, 0, 1))


def _swa(q, k, v, k0, v0, bias, sink, tb, mask_start):
    b, t, nq = q.shape
    nkv = k.shape[2]
    tile = lambda n: pl.BlockSpec((1, tb, n), lambda i, j: (i, j, 0))
    wspec = pl.BlockSpec((1, WINDOW, nkv), lambda i, j: (i, 0, 0))
    return pl.pallas_call(
        functools.partial(_swa_kernel, n_chunks=tb // CHUNK, mask_start=mask_start),
        grid=(b, t // tb),
        in_specs=[pl.BlockSpec(memory_space=pltpu.SMEM), tile(nq), tile(nkv), tile(nkv), wspec, wspec,
                  pl.BlockSpec(bias.shape, lambda i, j: (0, 0, 0))],
        out_specs=tile(nq),
        out_shape=jax.ShapeDtypeStruct((b, t, nq), F32),
        scratch_shapes=[pltpu.VMEM((WINDOW + max(tb, WINDOW), nkv), F32),
                        pltpu.VMEM((WINDOW + max(tb, WINDOW), nkv), F32)],
        compiler_params=_params(2),
        name="swa",
    )(sink, q, k, v, k0, v0, bias)


def _ssd_kernel(xbc_ref, zg_ref, dt_ref, dtt_ref, pre_ref, cw_ref, cb_ref, aexp_ref, acol_ref, dexp_ref,
                e_ref, ng_ref, s0_ref, y_ref, sfin_ref, clast_ref, st_scr, carry_scr, xc_scr, dte_scr,
                *, n_chunks):
    t = pl.program_id(1)
    tb = xbc_ref.shape[1]
    inner = zg_ref.shape[2]
    n_heads = inner // SSD_HEAD_DIM
    hpg = n_heads // SSD_GROUPS
    gw = hpg * SSD_HEAD_DIM
    gn = SSD_GROUPS * SSD_STATE

    @pl.when(t == 0)
    def _():
        st_scr[...] = s0_ref[0]
        carry_scr[...] = pre_ref[0]

    xbc = xbc_ref[0]
    xc_scr[...] = _silu(_causal_conv(xbc, carry_scr[...], cw_ref[...], cb_ref[...]))
    carry_scr[...] = xbc[tb - SUBLANES:]
    dte_scr[...] = _dot3_right(dt_ref[0], e_ref[...])

    tril = _tri(CHUNK, True)
    tril_b = jnp.where(tril, 1.0, 0.0).astype(BF16)
    triu_b = jnp.where(_tri(CHUNK, False), 1.0, 0.0).astype(BF16)
    aexp = aexp_ref[...]

    def chunk(c, carry):
        rows = pl.ds(pl.multiple_of(c * CHUNK, CHUNK), CHUNK)
        xs = xc_scr[rows, 0:inner]
        bm = xc_scr[rows, inner:inner + gn]
        cm = xc_scr[rows, inner + gn:inner + 2 * gn]
        dte = dte_scr[rows, :]
        cum = _dot3_left(tril_b, dte * aexp)
        cum_last = cum[CHUNK - 1:CHUNK, :]
        cum_t = _dot3_right(dtt_ref[0, c] * acol_ref[...], triu_b)
        xdt = (xs * dte).astype(BF16)
        st = st_scr[...]
        ys = []
        inters = []
        upds = []
        xw = xs * (jnp.exp(cum_last - cum) * dte)
        for g in range(SSD_GROUPS):
            ncols = slice(g * SSD_STATE, (g + 1) * SSD_STATE)
            gcols = slice(g * gw, (g + 1) * gw)
            cmg = cm[:, ncols].astype(BF16)
            bmg = bm[:, ncols]
            inters.append(jnp.dot(cmg, st[:, gcols].astype(BF16), preferred_element_type=F32))
            cb = lax.dot_general(cmg, bmg.astype(BF16), _NT, preferred_element_type=F32)
            for hh in range(hpg):
                h = g * hpg + hh
                hcols = slice(h * SSD_HEAD_DIM, (h + 1) * SSD_HEAD_DIM)
                seg = cum[:, hcols] - cum_t[h:h + 1, :]
                w = cb * jnp.exp(jnp.where(tril, seg, NEG_INF))
                ys.append(jnp.dot(w.astype(BF16), xdt[:, hcols], preferred_element_type=F32))
            upds.append(_bdot(bmg.T, xw[:, gcols]))
        y = jnp.concatenate(ys, axis=1) + jnp.exp(cum) * jnp.concatenate(inters, axis=1)
        st_scr[...] = st * jnp.exp(cum_last) + jnp.concatenate(upds, axis=1)
        yd = (y + dexp_ref[...] * xs) * _silu(zg_ref[0, rows, :])
        ms = jnp.mean(yd * yd, axis=-1, keepdims=True)
        y_ref[0, rows, :] = yd * lax.rsqrt(ms + RMS_EPS) * ng_ref[...]
        return carry

    lax.fori_loop(0, n_chunks, chunk, 0)

    @pl.when(t == pl.num_programs(1) - 1)
    def _():
        sfin_ref[0] = st_scr[...]
        clast_ref[0] = carry_scr[...]


def _ssd(xbc, zg, dt, dtt, conv_prefix, conv_w, conv_b, a_log, d_skip, norm_g, s0, tb):
    b, t, nxbc = xbc.shape
    inner = zg.shape[2]
    nh = dtt.shape[1]
    width = conv_w.shape[0]
    nc = t // CHUNK
    dtt4 = jnp.transpose(dtt.reshape(b, nh, nc, CHUNK), (0, 2, 1, 3))
    pre8 = jnp.concatenate([jnp.zeros((b, SUBLANES - (width - 1), nxbc), F32), conv_prefix], axis=1)
    a = -jnp.exp(a_log)
    aexp = jnp.repeat(a, SSD_HEAD_DIM).reshape(1, inner)
    acol = a.reshape(nh, 1)
    dexp = jnp.repeat(d_skip, SSD_HEAD_DIM).reshape(1, inner)
    expand = (np.arange(LANES)[:, None] == np.arange(inner)[None, :] // SSD_HEAD_DIM)
    expand = jnp.asarray(expand, BF16)
    s0t = jnp.transpose(s0, (0, 3, 1, 2)).reshape(b, SSD_STATE, inner)
    ng = norm_g.reshape(1, inner)
    cb2 = conv_b.reshape(1, nxbc)
    tile = lambda n: pl.BlockSpec((1, tb, n), lambda i, j: (i, j, 0))
    const = lambda arr: pl.BlockSpec(arr.shape, lambda i, j: (0,) * arr.ndim)
    bspec = lambda arr: pl.BlockSpec((1,) + arr.shape[1:], lambda i, j: (i,) + (0,) * (arr.ndim - 1))
    y, sfin, clast = pl.pallas_call(
        functools.partial(_ssd_kernel, n_chunks=tb // CHUNK),
        grid=(b, t // tb),
        in_specs=[tile(nxbc), tile(inner), tile(LANES),
                  pl.BlockSpec((1, tb // CHUNK, nh, CHUNK), lambda i, j: (i, j, 0, 0)),
                  bspec(pre8), const(conv_w), const(cb2), const(aexp), const(acol), const(dexp),
                  const(expand), const(ng), bspec(s0t)],
        out_specs=[tile(inner), bspec(s0t), bspec(pre8)],
        out_shape=[jax.ShapeDtypeStruct((b, t, inner), F32), jax.ShapeDtypeStruct(s0t.shape, F32),
                   jax.ShapeDtypeStruct(pre8.shape, F32)],
        scratch_shapes=[pltpu.VMEM((SSD_STATE, inner), F32), pltpu.VMEM((SUBLANES, nxbc), F32),
                        pltpu.VMEM((tb, nxbc), F32), pltpu.VMEM((tb, inner), F32)],
        compiler_params=_params(2),
        name="ssd",
    )(xbc, zg, dt, dtt4, pre8, conv_w, cb2, aexp, acol, dexp, expand, ng, s0t)
    sfin = jnp.transpose(sfin.reshape(b, SSD_STATE, nh, SSD_HEAD_DIM), (0, 2, 3, 1))
    return y, sfin, clast[:, SUBLANES - (width - 1):]


def _trunk(x, mods, P, st, sample):
    b, t, d = x.shape
    tb = min(MAX_TILE, t)
    depth = P['w_mod'].shape[0]
    new = {name: [] for name in ('s5_re', 's5_im', 'gla', 'swa_k', 'swa_v', 'ssd', 'ssd_conv', 'ffn_conv')}
    for layer in range(depth):
        i = layer // 2
        mod = mods[layer].reshape(b, 6, d)
        ffn = (P['norm2_g'][layer], P['ffn_w_up'][layer], P['ffn_conv_w'][layer], P['ffn_conv_b'][layer],
               P['ffn_w_down'][layer], st['ffn_conv'][layer], tb)
        if layer % 2 == 0:
            u, qk, v, r, gate = _ev_in(x, mod, P['norm1_g'][layer], P['ev_w_in'][i], P['gla_w_gate2'][i],
                                       P['gla_b_gate'][i], tb)
            mats = _s5_matrices(P['s5_a_re'][i], P['s5_a_im'][i], P['s5_log_dt'][i], P['s5_b_re'][i],
                                P['s5_b_im'][i], P['s5_c_re'][i], P['s5_c_im'][i])
            ya, sr, si = _s5(u, mats, st['s5_re'][i], st['s5_im'][i])
            ob, sg = _gla(qk, v, gate, r, st['gla'][i], P['gla_norm_g'][i], tb)
            new['s5_re'].append(sr)
            new['s5_im'].append(si)
            new['gla'].append(sg)
            x, fp = _out_ffn(x, mod, ya, ob, P['ev_w_out'][i], *ffn,
                             s5_extra=(u, P['s5_d'][i], P['s5_w_glu'][i], P['s5_b_glu'][i]))
        else:
            nq = P['swa_sink'].shape[1] * HEAD_DIM
            nkv = SWA_KV_HEADS * HEAD_DIM
            inner = P['ssd_norm_g'].shape[1]
            nxbc = P['ssd_conv_w'].shape[2]
            nh = P['ssd_a_log'].shape[1]
            q, k, v, zg, xbc, dt, dtt = _od_in(x, mod, P['norm1_g'][layer], P['od_w_in'][i], P['swa_q_norm'][i],
                                               P['swa_k_norm'][i], P['ssd_dt_bias'][i],
                                               (nq, nkv, inner, nxbc, nh), tb)
            bias = _rel_bias(P['t5_bias'])
            if sample:
                k0 = st['swa_k'][i].reshape(b, WINDOW, nkv)
                v0 = st['swa_v'][i].reshape(b, WINDOW, nkv)
            else:
                k0 = v0 = jnp.zeros((b, WINDOW, nkv), F32)
            oc = _swa(q, k, v, k0, v0, bias, P['swa_sink'][i], tb, mask_start=not sample)
            yd, ss, sc = _ssd(xbc, zg, dt, dtt, st['ssd_conv'][i], P['ssd_conv_w'][i], P['ssd_conv_b'][i],
                              P['ssd_a_log'][i], P['ssd_d'][i], P['ssd_norm_g'][i], st['ssd'][i], tb)
            keep = slice(None) if sample else slice(t - WINDOW, t)
            new['swa_k'].append(k[:, keep].reshape(b, -1, SWA_KV_HEADS, HEAD_DIM))
            new['swa_v'].append(v[:, keep].reshape(b, -1, SWA_KV_HEADS, HEAD_DIM))
            new['ssd'].append(ss)
            new['ssd_conv'].append(sc)
            x, fp = _out_ffn(x, mod, oc, yd, P['od_w_out'][i], *ffn)
        new['ffn_conv'].append(fp)
    return x, {name: jnp.stack(vals) for name, vals in new.items()}


def kernel(x_prompt, x_sample, state_s5_re, state_s5_im, state_gla, cache_swa_k, cache_swa_v, state_ssd, state_ssd_conv, state_ffn_conv, c_prompt, c_sample, t5_bias, norm1_g, norm2_g, w_mod, b_mod, ffn_w_up, ffn_conv_w, ffn_conv_b, ffn_w_down, ev_w_in, ev_w_out, s5_a_re, s5_a_im, s5_log_dt, s5_b_re, s5_b_im, s5_c_re, s5_c_im, s5_d, s5_w_glu, s5_b_glu, gla_w_gate2, gla_b_gate, gla_norm_g, od_w_in, od_w_out, swa_q_norm, swa_k_norm, swa_sink, ssd_conv_w, ssd_conv_b, ssd_dt_bias, ssd_a_log, ssd_d, ssd_norm_g):
    P = dict(t5_bias=t5_bias, norm1_g=norm1_g, norm2_g=norm2_g, w_mod=w_mod, b_mod=b_mod,
             ffn_w_up=ffn_w_up, ffn_conv_w=ffn_conv_w, ffn_conv_b=ffn_conv_b, ffn_w_down=ffn_w_down,
             ev_w_in=ev_w_in, ev_w_out=ev_w_out, s5_a_re=s5_a_re, s5_a_im=s5_a_im, s5_log_dt=s5_log_dt,
             s5_b_re=s5_b_re, s5_b_im=s5_b_im, s5_c_re=s5_c_re, s5_c_im=s5_c_im, s5_d=s5_d,
             s5_w_glu=s5_w_glu, s5_b_glu=s5_b_glu, gla_w_gate2=gla_w_gate2, gla_b_gate=gla_b_gate,
             gla_norm_g=gla_norm_g, od_w_in=od_w_in, od_w_out=od_w_out, swa_q_norm=swa_q_norm,
             swa_k_norm=swa_k_norm, swa_sink=swa_sink, ssd_conv_w=ssd_conv_w, ssd_conv_b=ssd_conv_b,
             ssd_dt_bias=ssd_dt_bias, ssd_a_log=ssd_a_log, ssd_d=ssd_d, ssd_norm_g=ssd_norm_g)
    bp = x_prompt.shape[0]
    n_even, n_odd = state_s5_re.shape[0], state_ssd.shape[0]
    depth = w_mod.shape[0]
    zeros_like_b = lambda a: jnp.zeros((a.shape[0], bp) + a.shape[2:], F32)
    zero_st = dict(s5_re=zeros_like_b(state_s5_re), s5_im=zeros_like_b(state_s5_im), gla=zeros_like_b(state_gla),
                   ssd=zeros_like_b(state_ssd), ssd_conv=zeros_like_b(state_ssd_conv),
                   ffn_conv=zeros_like_b(state_ffn_conv))
    sample_st = dict(s5_re=state_s5_re, s5_im=state_s5_im, gla=state_gla, swa_k=cache_swa_k,
                     swa_v=cache_swa_v, ssd=state_ssd, ssd_conv=state_ssd_conv, ffn_conv=state_ffn_conv)
    mods = _modulation(jnp.concatenate([c_prompt, c_sample], axis=0), w_mod, b_mod)
    y_prompt, stp = _trunk(x_prompt, mods[:, :bp], P, zero_st, False)
    y_sample, sts = _trunk(x_sample, mods[:, bp:], P, sample_st, True)
    names = ('s5_re', 's5_im', 'gla', 'swa_k', 'swa_v', 'ssd', 'ssd_conv', 'ffn_conv')
    return (y_prompt, y_sample) + tuple(stp[n] for n in names) + tuple(sts[n] for n in names)
```

```python
import functools
import math

import jax
import jax.numpy as jnp
import numpy as np
from jax import lax
from jax.experimental import pallas as pl
from jax.experimental.pallas import tpu as pltpu

F32 = jnp.float32
BF16 = jnp.bfloat16

CHUNK = 64
WINDOW = 128
S5_GROUP = 16
S5_STATE = 64
S5_SUB = 16
GLA_HEADS = 4
GLA_GATE_NORM = 16.0
HEAD_DIM = 64
SWA_KV_HEADS = 2
SWA_KEYS = 256
SSD_HEAD_DIM = 64
SSD_STATE = 128
SSD_GROUPS = 2
T5_BUCKETS = 32
T5_MAX_DIST = 128
RMS_EPS = 1e-6
NEG_INF = -1e30
LANES = 128
SUBLANES = 8
MAX_TILE = 512
FFN_COL_BLOCK = 256
VMEM_LIMIT = 56 * 1024 * 1024

_NT = (((1,), (1,)), ((), ()))
_HI = lax.Precision.HIGHEST


def _params(n_axes=2):
    sem = ("parallel",) + ("arbitrary",) * (n_axes - 1)
    return pltpu.CompilerParams(dimension_semantics=sem, vmem_limit_bytes=VMEM_LIMIT)


def _bdot(a, b):
    return jnp.dot(a.astype(BF16), b.astype(BF16), preferred_element_type=F32)


def _bdot_nt(a, b):
    return lax.dot_general(a.astype(BF16), b.astype(BF16), _NT, preferred_element_type=F32)


def _split3(x):
    hi = x.astype(BF16)
    r1 = x - hi.astype(F32)
    mid = r1.astype(BF16)
    lo = (r1 - mid.astype(F32)).astype(BF16)
    return hi, mid, lo


def _dot3_left(c, x):
    hi, mid, lo = _split3(x)
    d = lambda p: jnp.dot(c, p, preferred_element_type=F32)
    return d(hi) + d(mid) + d(lo)


def _dot3_right(x, c):
    hi, mid, lo = _split3(x)
    d = lambda p: jnp.dot(p, c, preferred_element_type=F32)
    return d(hi) + d(mid) + d(lo)


def _silu(x):
    return x * jax.nn.sigmoid(x)


def _softplus(x):
    return jnp.maximum(x, 0.0) + jnp.log1p(jnp.exp(-jnp.abs(x)))


def _log_sigmoid(x):
    return jnp.minimum(x, 0.0) - jnp.log1p(jnp.exp(-jnp.abs(x)))


def _norm_mod(x, g, scale, shift):
    ms = jnp.mean(x * x, axis=-1, keepdims=True)
    return (x * lax.rsqrt(ms + RMS_EPS) * g) * (1.0 + scale) + shift


def _tri(n, lower):
    r = lax.broadcasted_iota(jnp.int32, (n, n), 0)
    c = lax.broadcasted_iota(jnp.int32, (n, n), 1)
    return (r >= c) if lower else (r <= c)


def _causal_conv(u, prev8, w, b):
    width = w.shape[0]
    n = u.shape[0]
    ext = jnp.concatenate([prev8, u[0:SUBLANES]], axis=0)
    full = b
    head = b
    for j in range(width):
        sh = width - 1 - j
        if sh == 0:
            full = full + u * w[j:j + 1]
            head = head + u[0:SUBLANES] * w[j:j + 1]
        else:
            full = full + pltpu.roll(u, sh, axis=0) * w[j:j + 1]
            head = head + pltpu.roll(ext, sh, axis=0)[SUBLANES:2 * SUBLANES] * w[j:j + 1]
    if n == SUBLANES:
        return head
    return jnp.concatenate([head, full[SUBLANES:]], axis=0)


def _mod_kernel(c_ref, w_ref, b_ref, o_ref):
    o_ref[0] = _bdot(_silu(c_ref[...]), w_ref[0]) + b_ref[0]


def _modulation(c, w_mod, b_mod):
    depth, d, n = w_mod.shape
    bc = c.shape[0]
    tn = n // 4
    return pl.pallas_call(
        _mod_kernel,
        grid=(depth, n // tn),
        in_specs=[pl.BlockSpec((bc, d), lambda l, j: (0, 0)),
                  pl.BlockSpec((1, d, tn), lambda l, j: (l, 0, j)),
                  pl.BlockSpec((1, 1, tn), lambda l, j: (l, 0, j))],
        out_specs=pl.BlockSpec((1, bc, tn), lambda l, j: (l, 0, j)),
        out_shape=jax.ShapeDtypeStruct((depth, bc, n), F32),
        compiler_params=_params(2),
        name="modulation",
    )(c, w_mod, b_mod.reshape(depth, 1, n))


def _ev_in_kernel(x_ref, mod_ref, g_ref, w_ref, wg2_ref, bg_ref,
                  u_ref, qk_ref, v_ref, r_ref, gate_ref):
    mod = mod_ref[0]
    hn = _norm_mod(x_ref[0], g_ref[...], mod[1:2], mod[0:1]).astype(BF16)
    dot = lambda lo, hi: jnp.dot(hn, w_ref[:, lo:hi], preferred_element_type=F32)
    u_ref[0] = dot(0, 512)
    qk_ref[0] = dot(512, 1024)
    v_ref[0] = dot(1024, 1536)
    r_ref[0] = dot(1536, 2048)
    gl = dot(2048, 2048 + LANES)
    gate_ref[0] = _log_sigmoid(_bdot(gl, wg2_ref[...]) + bg_ref[...]) * (1.0 / GLA_GATE_NORM)


def _ev_in(x, mod, norm_g, w_in, w_gate2, b_gate, tb):
    b, t, d = x.shape
    rank = w_gate2.shape[0]
    nk = w_gate2.shape[1]
    wu, wq, wk, wv, wgl, wr = jnp.split(w_in, [512, 768, 1024, 1536, 1536 + rank], axis=1)
    w = jnp.concatenate([wu, wq, wk, wv, wr, wgl, jnp.zeros((d, LANES - rank), F32)], axis=1).astype(BF16)
    wg2 = jnp.concatenate([w_gate2, jnp.zeros((LANES - rank, nk), F32)], axis=0).astype(BF16)
    tile = lambda n: pl.BlockSpec((1, tb, n), lambda i, j: (i, j, 0))
    const = lambda a: pl.BlockSpec(a.shape, lambda i, j: (0,) * a.ndim)
    g2 = norm_g.reshape(1, d)
    bg = b_gate.reshape(1, nk)
    return pl.pallas_call(
        _ev_in_kernel,
        grid=(b, t // tb),
        in_specs=[tile(d), pl.BlockSpec((1, 6, d), lambda i, j: (i, 0, 0)), const(g2), const(w),
                  const(wg2), const(bg)],
        out_specs=[tile(512), tile(512), tile(512), tile(512), tile(nk)],
        out_shape=[jax.ShapeDtypeStruct((b, t, n), F32) for n in (512, 512, 512, 512, nk)],
        compiler_params=_params(2),
        name="ev_in",
    )(x, mod, g2, w, wg2, bg)


def _gla_kernel(qk_ref, v_ref, gate_ref, r_ref, s0_ref, ng_ref, o_ref, sfin_ref, st_scr, *, n_chunks):
    t = pl.program_id(1)
    dk = qk_ref.shape[2] // 2 // GLA_HEADS
    dv = v_ref.shape[2] // GLA_HEADS
    nk = GLA_HEADS * dk
    nv = GLA_HEADS * dv
    blk = (lax.broadcasted_iota(jnp.int32, (nv, nk), 0) // dv
           == lax.broadcasted_iota(jnp.int32, (nv, nk), 1) // dk)

    @pl.when(t == 0)
    def _():
        s0 = jnp.concatenate([s0_ref[0, h] for h in range(GLA_HEADS)], axis=0)
        st_scr[...] = jnp.where(blk, jnp.concatenate([s0] * GLA_HEADS, axis=1), 0.0)

    tril = _tri(CHUNK, True)
    tril_b = jnp.where(tril, 1.0, 0.0).astype(BF16)
    lane_head = lax.broadcasted_iota(jnp.int32, (CHUNK, nk), 1) // dk
    ng = ng_ref[...]

    def chunk(c, carry):
        rows = pl.ds(pl.multiple_of(c * CHUNK, CHUNK), CHUNK)
        q = qk_ref[0, rows, 0:nk] * dk ** -0.5
        k = qk_ref[0, rows, nk:2 * nk]
        v = v_ref[0, rows, :]
        cum = _dot3_left(tril_b, gate_ref[0, rows, :])
        cum_last = cum[CHUNK - 1:CHUNK, :]
        qe = q * jnp.exp(cum)
        ke = (k * jnp.exp(-cum)).astype(BF16)
        kd = k * jnp.exp(cum_last - cum)
        st = st_scr[...]
        inter = _bdot_nt(qe, st)
        v_b = v.astype(BF16)
        for h in range(GLA_HEADS):
            qm = jnp.where(lane_head == h, qe, 0.0).astype(BF16)
            att = lax.dot_general(qm, ke, _NT, preferred_element_type=F32)
            att = jnp.where(tril, att, 0.0).astype(BF16)
            cols = slice(h * dv, (h + 1) * dv)
            oh = jnp.dot(att, v_b[:, cols], preferred_element_type=F32) + inter[:, cols]
            ms = jnp.mean(oh * oh, axis=-1, keepdims=True)
            oh = oh * lax.rsqrt(ms + RMS_EPS) * ng
            o_ref[0, rows, cols] = oh * _silu(r_ref[0, rows, cols])
        upd = _bdot(v.T, kd)
        st_scr[...] = jnp.where(blk, st * jnp.exp(cum_last) + upd, 0.0)
        return carry

    lax.fori_loop(0, n_chunks, chunk, 0)

    @pl.when(t == pl.num_programs(1) - 1)
    def _():
        for h in range(GLA_HEADS):
            sfin_ref[0, h] = st_scr[h * dv:(h + 1) * dv, h * dk:(h + 1) * dk]


def _gla(qk, v, gate, r, s0, norm_g, tb):
    b, t, nv = v.shape
    nk = gate.shape[2]
    dk, dv = nk // GLA_HEADS, nv // GLA_HEADS
    s0t = jnp.swapaxes(s0, 2, 3)
    tile = lambda n: pl.BlockSpec((1, tb, n), lambda i, j: (i, j, 0))
    sspec = pl.BlockSpec((1, GLA_HEADS, dv, dk), lambda i, j: (i, 0, 0, 0))
    ng = norm_g.reshape(1, dv)
    o, sfin = pl.pallas_call(
        functools.partial(_gla_kernel, n_chunks=tb // CHUNK),
        grid=(b, t // tb),
        in_specs=[tile(2 * nk), tile(nv), tile(nk), tile(nv), sspec,
                  pl.BlockSpec((1, dv), lambda i, j: (0, 0))],
        out_specs=[tile(nv), sspec],
        out_shape=[jax.ShapeDtypeStruct((b, t, nv), F32),
                   jax.ShapeDtypeStruct((b, GLA_HEADS, dv, dk), F32)],
        scratch_shapes=[pltpu.VMEM((nv, nk), F32)],
        compiler_params=_params(2),
        name="gla",
    )(qk, v, gate, r, s0t, ng)
    return o, jnp.swapaxes(sfin, 2, 3)


def _s5_kernel(u_ref, m_ref, p_ref, n_ref, a1_ref, a2_ref, x0_ref, y_ref, xf_ref, upd_scr, xs_scr,
               *, n_sub, bsz):
    ub = u_ref[0].astype(BF16)
    upd_scr[...] = jnp.dot(ub, p_ref[0], preferred_element_type=F32)
    a1 = a1_ref[0]
    a2 = a2_ref[0]

    def step(j, st):
        rows = pl.ds(pl.multiple_of(j * bsz, bsz), bsz)
        xs_scr[rows, :] = st
        return a1 * st + a2 * pltpu.roll(st, S5_STATE, axis=1) + upd_scr[rows, :]

    xf_ref[0] = lax.fori_loop(0, n_sub, step, x0_ref[0])
    y_ref[0] = (jnp.dot(ub, m_ref[0], preferred_element_type=F32)
                + jnp.dot(xs_scr[...].astype(BF16), n_ref[0], preferred_element_type=F32))


def _cmul(ar, ai, br, bi):
    return ar * br - ai * bi, ar * bi + ai * br


def _s5_matrices(a_re, a_im, log_dt, b_re, b_im, c_re, c_im):
    g, p = a_re.shape
    hdim = b_re.shape[-1]
    dt = jnp.exp(log_dt)[:, None]
    mag = jnp.exp(a_re * dt)
    ab_re, ab_im = mag * jnp.cos(a_im * dt), mag * jnp.sin(a_im * dt)
    den = a_re * a_re + a_im * a_im
    num_re, num_im = ab_re - 1.0, ab_im
    g_re = (num_re * a_re + num_im * a_im) / den
    g_im = (num_im * a_re - num_re * a_im) / den
    bb_re = g_re[..., None] * b_re - g_im[..., None] * b_im
    bb_im = g_re[..., None] * b_im + g_im[..., None] * b_re
    pw_re, pw_im = jnp.ones((1, g, p), F32), jnp.zeros((1, g, p), F32)
    sq_re, sq_im = ab_re, ab_im
    while pw_re.shape[0] < S5_SUB + 1:
        nr, ni = _cmul(pw_re, pw_im, sq_re, sq_im)
        pw_re, pw_im = jnp.concatenate([pw_re, nr]), jnp.concatenate([pw_im, ni])
        sq_re, sq_im = _cmul(sq_re, sq_im, sq_re, sq_im)
    pw_re, pw_im = pw_re[:S5_SUB + 1], pw_im[:S5_SUB + 1]
    ca_re, ca_im = _cmul(c_re[None], c_im[None], pw_re[:, :, None, :], pw_im[:, :, None, :])
    kern = (jnp.einsum('tghp,gpk->tghk', ca_re[:S5_SUB], bb_re, precision=_HI)
            - jnp.einsum('tghp,gpk->tghk', ca_im[:S5_SUB], bb_im, precision=_HI))
    kern = jnp.concatenate([kern, jnp.zeros((1,) + kern.shape[1:], F32)])
    s_in = np.arange(S5_SUB)[:, None]
    s_out = np.arange(S5_SUB)[None, :]
    lag = np.where(s_out >= s_in, s_out - s_in, S5_SUB)
    m = kern[lag]
    m = jnp.transpose(m, (2, 0, 4, 1, 3)).reshape(g, S5_SUB * hdim, S5_SUB * hdim)
    pr, pi = _cmul(pw_re[S5_SUB - 1::-1][..., None], pw_im[S5_SUB - 1::-1][..., None], bb_re[None], bb_im[None])
    pm = jnp.concatenate([pr, pi], axis=2)
    pm = jnp.transpose(pm, (1, 0, 3, 2)).reshape(g, S5_SUB * hdim, 2 * p)
    nm = jnp.concatenate([ca_re[1:], -ca_im[1:]], axis=3)
    nm = jnp.transpose(nm, (1, 3, 0, 2)).reshape(g, 2 * p, S5_SUB * hdim)
    a_r, a_i = pw_re[S5_SUB], pw_im[S5_SUB]
    a1 = jnp.concatenate([a_r, a_r], axis=1)[:, None, :]
    a2 = jnp.concatenate([-a_i, a_i], axis=1)[:, None, :]
    return m.astype(BF16), pm.astype(BF16), nm.astype(BF16), a1, a2


def _s5(u, mats, x0_re, x0_im):
    m, pm, nm, a1, a2 = mats
    b, t, width = u.shape
    g = m.shape[0]
    hdim = width // g
    n_sub = t // S5_SUB
    cols = S5_SUB * hdim
    p2 = pm.shape[2]
    rows = n_sub * b
    ug = jnp.transpose(u.reshape(b, n_sub, S5_SUB, g, hdim), (3, 1, 0, 2, 4)).reshape(g, rows, cols)
    x0 = jnp.transpose(jnp.concatenate([x0_re, x0_im], axis=2), (1, 0, 2))
    gspec = lambda a: pl.BlockSpec((1,) + a.shape[1:], lambda i: (i, 0, 0))
    y, xf = pl.pallas_call(
        functools.partial(_s5_kernel, n_sub=n_sub, bsz=b),
        grid=(g,),
        in_specs=[gspec(ug), gspec(m), gspec(pm), gspec(nm), gspec(a1), gspec(a2), gspec(x0)],
        out_specs=[gspec(ug), gspec(x0)],
        out_shape=[jax.ShapeDtypeStruct(ug.shape, F32), jax.ShapeDtypeStruct(x0.shape, F32)],
        scratch_shapes=[pltpu.VMEM((rows, p2), F32), pltpu.VMEM((rows, p2), F32)],
        compiler_params=_params(1),
        name="s5",
    )(ug, m, pm, nm, a1, a2, x0)
    y = jnp.transpose(y.reshape(g, n_sub, b, S5_SUB, hdim), (2, 1, 3, 0, 4)).reshape(b, t, width)
    xf = jnp.transpose(xf, (1, 0, 2))
    return y, xf[:, :, :p2 // 2], xf[:, :, p2 // 2:]


def _out_ffn_kernel(*refs, with_s5, n_cb):
    if with_s5:
        (x_ref, mod_ref, u_ref, m1_ref, m2_ref, d_ref, wglu_ref, bglu_ref, wout_ref, n2g_ref,
         wa_ref, wg_ref, cwa_ref, cwg_ref, cba_ref, cbg_ref, wd_ref, pre_ref,
         xo_ref, ulast_ref, carry_scr, h_scr) = refs
    else:
        (x_ref, mod_ref, m1_ref, m2_ref, wout_ref, n2g_ref,
         wa_ref, wg_ref, cwa_ref, cwg_ref, cba_ref, cbg_ref, wd_ref, pre_ref,
         xo_ref, ulast_ref, carry_scr, h_scr) = refs
    t = pl.program_id(1)

    @pl.when(t == 0)
    def _():
        carry_scr[...] = pre_ref[0]

    mod = mod_ref[0]
    m1 = m1_ref[0]
    if with_s5:
        ya = jax.nn.gelu(m1 + d_ref[...] * u_ref[0])
        m1 = ya * jax.nn.sigmoid(_bdot(ya, wglu_ref[...]) + bglu_ref[...])
    half = m1.shape[1]
    out = (jnp.dot(m1.astype(BF16), wout_ref[0:half, :], preferred_element_type=F32)
           + jnp.dot(m2_ref[0].astype(BF16), wout_ref[half:, :], preferred_element_type=F32))
    x1 = x_ref[0] + mod[2:3] * out
    hn = _norm_mod(x1, n2g_ref[...], mod[4:5], mod[3:4]).astype(BF16)
    tb = x1.shape[0]
    cb = wa_ref.shape[2]

    def up(j):
        return (jnp.dot(hn, wa_ref[j], preferred_element_type=F32),
                jnp.dot(hn, wg_ref[j], preferred_element_type=F32))

    nxt = up(0)
    for j in range(n_cb):
        ua, ug = nxt
        if j + 1 < n_cb:
            nxt = up(j + 1)
        a = _causal_conv(ua, carry_scr[0, j], cwa_ref[j], cba_ref[j])
        gt = _causal_conv(ug, carry_scr[1, j], cwg_ref[j], cbg_ref[j])
        carry_scr[0, j] = ua[tb - SUBLANES:]
        carry_scr[1, j] = ug[tb - SUBLANES:]
        h_scr[:, j * cb:(j + 1) * cb] = (_silu(a) * gt).astype(BF16)
    f = jnp.dot(h_scr[...], wd_ref[...], preferred_element_type=F32)
    xo_ref[0] = x1 + mod[5:6] * f

    @pl.when(t == pl.num_programs(1) - 1)
    def _():
        ulast_ref[0] = carry_scr[...]


def _out_ffn(x, mod, m1, m2, w_out, norm2_g, w_up, conv_w, conv_b, w_down, prefix, tb, s5_extra=None):
    b, t, d = x.shape
    dff = w_down.shape[0]
    cb = FFN_COL_BLOCK
    n_cb = dff // cb
    width = conv_w.shape[0]
    half = m1.shape[2]
    blocks = lambda a: a.reshape(a.shape[0], 2, n_cb, cb)
    w_up_b = jnp.transpose(blocks(w_up.astype(BF16)), (1, 2, 0, 3))
    cw = jnp.transpose(blocks(conv_w), (1, 2, 0, 3))
    cbias = jnp.transpose(blocks(conv_b.reshape(1, 2 * dff)), (1, 2, 0, 3))
    wd = w_down.astype(BF16)
    pre8 = jnp.concatenate([jnp.zeros((b, SUBLANES - (width - 1), 2 * dff), F32), prefix], axis=1)
    pre8 = jnp.transpose(pre8.reshape(b, SUBLANES, 2, n_cb, cb), (0, 2, 3, 1, 4))
    tile = lambda n: pl.BlockSpec((1, tb, n), lambda i, j: (i, j, 0))
    const = lambda a: pl.BlockSpec(a.shape, lambda i, j: (0,) * a.ndim, pipeline_mode=pl.Buffered(1))
    bspec = lambda a: pl.BlockSpec((1,) + a.shape[1:], lambda i, j: (i,) + (0,) * (a.ndim - 1))
    wo = w_out.astype(BF16)
    n2g = norm2_g.reshape(1, d)
    args = [x, mod]
    specs = [tile(d), bspec(mod)]
    if s5_extra is not None:
        u, s5_d, w_glu, b_glu = s5_extra
        s5_d, w_glu, b_glu = s5_d.reshape(1, half), w_glu.astype(BF16), b_glu.reshape(1, half)
        args += [u, m1, m2, s5_d, w_glu, b_glu]
        specs += [tile(half), tile(half), tile(m2.shape[2]), const(s5_d), const(w_glu), const(b_glu)]
    else:
        args += [m1, m2]
        specs += [tile(half), tile(m2.shape[2])]
    weights = [wo, n2g, w_up_b[0], w_up_b[1], cw[0], cw[1], cbias[0], cbias[1], wd]
    args += weights + [pre8]
    specs += [const(a) for a in weights] + [bspec(pre8)]
    xo, ulast = pl.pallas_call(
        functools.partial(_out_ffn_kernel, with_s5=s5_extra is not None, n_cb=n_cb),
        grid=(b, t // tb),
        in_specs=specs,
        out_specs=[tile(d), bspec(pre8)],
        out_shape=[jax.ShapeDtypeStruct((b, t, d), F32), jax.ShapeDtypeStruct(pre8.shape, F32)],
        scratch_shapes=[pltpu.VMEM(pre8.shape[1:], F32), pltpu.VMEM((tb, dff), BF16)],
        compiler_params=_params(2),
        name="out_ffn_s5" if s5_extra is not None else "out_ffn",
    )(*args)
    ulast = jnp.transpose(ulast, (0, 3, 1, 2, 4)).reshape(b, SUBLANES, 2 * dff)
    return xo, ulast[:, SUBLANES - (width - 1):]


def _head_rms(x, ones_blk, g):
    ss = _dot3_right(x * x, ones_blk) * (1.0 / HEAD_DIM)
    return x * lax.rsqrt(ss + RMS_EPS) * g


def _od_in_kernel(x_ref, mod_ref, g_ref, w_ref, qg_ref, kg_ref, dtb_ref,
                  q_ref, k_ref, v_ref, zg_ref, xbc_ref, dt_ref, dtt_ref):
    mod = mod_ref[0]
    hn = _norm_mod(x_ref[0], g_ref[...], mod[1:2], mod[0:1]).astype(BF16)
    dot = lambda lo, hi: jnp.dot(hn, w_ref[:, lo:hi], preferred_element_type=F32)
    nq = q_ref.shape[2]
    nkv = k_ref.shape[2]
    hr = lax.broadcasted_iota(jnp.int32, (nq, nq), 0) // HEAD_DIM
    hc = lax.broadcasted_iota(jnp.int32, (nq, nq), 1) // HEAD_DIM
    ones_blk = jnp.where(hr == hc, 1.0, 0.0).astype(BF16)
    q_ref[0] = _head_rms(dot(0, nq), ones_blk, qg_ref[...])
    k_ref[0] = _head_rms(dot(nq, nq + nkv), ones_blk[0:nkv, 0:nkv], kg_ref[...])
    o = nq + nkv
    v_ref[0] = dot(o, o + nkv)
    o += nkv
    zg_ref[0] = dot(o, o + zg_ref.shape[2])
    o += zg_ref.shape[2]
    xbc_ref[0] = dot(o, o + xbc_ref.shape[2])
    o += xbc_ref.shape[2]
    dt = _softplus(dot(o, o + LANES) + dtb_ref[...])
    dt_ref[0] = dt
    dtt_ref[0] = dt.T[0:dtt_ref.shape[1], :]


def _od_in(x, mod, norm_g, w_in, q_norm, k_norm, dt_bias, dims, tb):
    b, t, d = x.shape
    nq, nkv, nz, nxbc, nh = dims
    w = jnp.concatenate([w_in, jnp.zeros((d, LANES - nh), F32)], axis=1).astype(BF16)
    qg = jnp.tile(q_norm, nq // HEAD_DIM).reshape(1, nq)
    kg = jnp.tile(k_norm, nkv // HEAD_DIM).reshape(1, nkv)
    dtb = jnp.concatenate([dt_bias, jnp.zeros((LANES - nh,), F32)]).reshape(1, LANES)
    g2 = norm_g.reshape(1, d)
    tile = lambda n: pl.BlockSpec((1, tb, n), lambda i, j: (i, j, 0))
    const = lambda a: pl.BlockSpec(a.shape, lambda i, j: (0,) * a.ndim)
    widths = (nq, nkv, nkv, nz, nxbc, LANES)
    return pl.pallas_call(
        _od_in_kernel,
        grid=(b, t // tb),
        in_specs=[tile(d), pl.BlockSpec((1, 6, d), lambda i, j: (i, 0, 0)), const(g2), const(w),
                  const(qg), const(kg), const(dtb)],
        out_specs=[tile(n) for n in widths] + [pl.BlockSpec((1, nh, tb), lambda i, j: (i, 0, j))],
        out_shape=[jax.ShapeDtypeStruct((b, t, n), F32) for n in widths]
        + [jax.ShapeDtypeStruct((b, nh, t), F32)],
        compiler_params=_params(2),
        name="od_in",
    )(x, mod, g2, w, qg, kg, dtb)


def _swa_kernel(q_ref, k_ref, v_ref, k0_ref, v0_ref, bias_ref, sink_ref, o_ref, kx, vx,
                *, n_blocks, nq, mask_start):
    t = pl.program_id(1)
    tb = q_ref.shape[1]
    n_kv = k_ref.shape[2] // HEAD_DIM
    rows_q = nq * CHUNK
    tail = kx.shape[1] - WINDOW - tb
    left = lax.broadcasted_iota(jnp.int32, (1, LANES), 1) < HEAD_DIM
    ones = jnp.ones((SWA_KEYS, LANES), BF16)
    col = lax.broadcasted_iota(jnp.int32, (1, SWA_KEYS), 1)

    def place(dst, rows, x):
        rolled = pltpu.roll(x, HEAD_DIM, axis=1)
        dst[0, rows, :] = jnp.where(left, x, 0.0).astype(BF16)
        dst[1, rows, :] = jnp.where(left, 0.0, rolled).astype(BF16)
        dst[2, rows, :] = jnp.where(left, rolled, 0.0).astype(BF16)
        dst[3, rows, :] = jnp.where(left, 0.0, x).astype(BF16)

    @pl.when(t == 0)
    def _():
        place(kx, slice(0, WINDOW), k0_ref[0])
        place(vx, slice(0, WINDOW), v0_ref[0])
        for i in range(2 * n_kv if tail else 0):
            kx[i, WINDOW + tb:, :] = jnp.zeros((tail, LANES), BF16)
            vx[i, WINDOW + tb:, :] = jnp.zeros((tail, LANES), BF16)

    place(kx, slice(WINDOW, WINDOW + tb), k_ref[0])
    place(vx, slice(WINDOW, WINDOW + tb), v_ref[0])

    def block(blk, carry):
        r0 = pl.multiple_of(blk * rows_q, rows_q)
        keys = pl.ds(r0, SWA_KEYS)
        if mask_start:
            valid = t * tb + r0 - WINDOW + col >= 0
        probs = []
        sinks = []
        for j in range(n_kv):
            lo = 2 * j * LANES
            qg = jnp.concatenate([q_ref[0, pl.ds(r0 + cq * CHUNK, CHUNK), lo + r * LANES:lo + (r + 1) * LANES]
                                  for cq in range(nq) for r in range(2)], axis=0).astype(BF16)
            kcat = jnp.concatenate([kx[2 * j, keys, :], kx[2 * j + 1, keys, :]], axis=0)
            s_both = lax.dot_general(qg, kcat, _NT, preferred_element_type=F32) * HEAD_DIM ** -0.5
            for side in range(2):
                s = s_both[:, side * SWA_KEYS:(side + 1) * SWA_KEYS] + bias_ref[j, side]
                if mask_start:
                    s = jnp.where(valid, s, NEG_INF)
                m = jnp.max(s, axis=-1, keepdims=True)
                probs.append(jnp.exp(s - m).astype(BF16))
                sinks.append(jnp.exp(sink_ref[j, side] - m))
        den = jnp.dot(jnp.concatenate(probs, axis=0), ones, preferred_element_type=F32)
        nr = 2 * rows_q
        for j in range(n_kv):
            lo = 2 * j * LANES
            vcat = jnp.concatenate([vx[2 * j, keys, :], vx[2 * j + 1, keys, :]], axis=0)
            pv = jnp.dot(jnp.concatenate(probs[2 * j:2 * j + 2], axis=1), vcat, preferred_element_type=F32)
            d_l = den[(2 * j) * nr:(2 * j + 1) * nr] + sinks[2 * j]
            d_r = den[(2 * j + 1) * nr:(2 * j + 2) * nr] + sinks[2 * j + 1]
            out = pv * jnp.where(left, 1.0 / d_l, 1.0 / d_r)
            for cq in range(nq):
                for r in range(2):
                    o_ref[0, pl.ds(r0 + cq * CHUNK, CHUNK), lo + r * LANES:lo + (r + 1) * LANES] = (
                        out[(2 * cq + r) * CHUNK:(2 * cq + r + 1) * CHUNK])
        return carry

    lax.fori_loop(0, n_blocks, block, 0, unroll=math.gcd(n_blocks, 2))
    for i in range(2 * n_kv):
        kx[i, 0:WINDOW, :] = kx[i, tb:tb + WINDOW, :]
        vx[i, 0:WINDOW, :] = vx[i, tb:tb + WINDOW, :]


def _t5_bucket(rel):
    nb = T5_BUCKETS // 2
    max_exact = nb // 2
    ret = (rel > 0).astype(jnp.int32) * nb
    n = jnp.abs(rel)
    nf = jnp.maximum(n, 1).astype(F32)
    large = max_exact + (jnp.log(nf / max_exact) / math.log(T5_MAX_DIST / max_exact)
                         * (nb - max_exact)).astype(jnp.int32)
    large = jnp.minimum(large, nb - 1)
    return ret + jnp.where(n < max_exact, n, large)


def _rel_bias(table):
    rel = (jnp.arange(WINDOW + CHUNK)[None, :] - WINDOW) - jnp.arange(CHUNK)[:, None]
    return jnp.transpose(table[_t5_bucket(rel)], (2, 0, 1))


def _swa(q, k, v, k0, v0, bias, sink, tb, mask_start):
    b, t, nq = q.shape
    nkv = k.shape[2]
    n_kv = nkv // HEAD_DIM
    assert nkv == LANES and nq == 2 * n_kv * LANES
    by_side = lambda a: jnp.transpose(a.reshape((n_kv, 2, 2) + a.shape[1:]), (0, 2, 1) + tuple(range(3, a.ndim + 2)))
    span = WINDOW + CHUNK
    n_chunks = tb // CHUNK
    cpb = 2 if n_chunks % 2 == 0 else 1
    assert WINDOW + cpb * CHUNK <= SWA_KEYS
    bias2 = by_side(bias).reshape(n_kv, 2, 2 * CHUNK, span)
    bias3 = jnp.concatenate(
        [jnp.pad(bias2, ((0, 0), (0, 0), (0, 0), (cq * CHUNK, SWA_KEYS - span - cq * CHUNK)),
                 constant_values=NEG_INF) for cq in range(cpb)], axis=2)
    sink3 = jnp.tile(jnp.repeat(by_side(sink), CHUNK, axis=2), (1, 1, cpb))
    sink3 = jnp.broadcast_to(sink3[..., None], sink3.shape + (LANES,))
    tile = lambda n: pl.BlockSpec((1, tb, n), lambda i, j: (i, j, 0))
    wspec = pl.BlockSpec((1, WINDOW, nkv), lambda i, j: (i, 0, 0))
    const = lambda a: pl.BlockSpec(a.shape, lambda i, j: (0,) * a.ndim)
    rows = WINDOW + max(tb, WINDOW) + SWA_KEYS - WINDOW - cpb * CHUNK
    return pl.pallas_call(
        functools.partial(_swa_kernel, n_blocks=n_chunks // cpb, nq=cpb, mask_start=mask_start),
        grid=(b, t // tb),
        in_specs=[tile(nq), tile(nkv), tile(nkv), wspec, wspec, const(bias3), const(sink3)],
        out_specs=tile(nq),
        out_shape=jax.ShapeDtypeStruct((b, t, nq), F32),
        scratch_shapes=[pltpu.VMEM((2 * n_kv, rows, LANES), BF16), pltpu.VMEM((2 * n_kv, rows, LANES), BF16)],
        compiler_params=_params(2),
        name="swa",
    )(q, k, v, k0, v0, bias3, sink3)


def _ssd_kernel(xbc_ref, zg_ref, dt_ref, dtt_ref, pre_ref, cw_ref, cb_ref, aexp_ref, acol_ref, dexp_ref,
                e_ref, ng_ref, s0_ref, y_ref, sfin_ref, clast_ref, st_scr, carry_scr, xc_scr, dte_scr,
                *, n_chunks):
    t = pl.program_id(1)
    tb = xbc_ref.shape[1]
    inner = zg_ref.shape[2]
    n_heads = inner // SSD_HEAD_DIM
    hpg = n_heads // SSD_GROUPS
    gw = hpg * SSD_HEAD_DIM
    gn = SSD_GROUPS * SSD_STATE

    @pl.when(t == 0)
    def _():
        st_scr[...] = s0_ref[0]
        carry_scr[...] = pre_ref[0]

    xbc = xbc_ref[0]
    xc_scr[...] = _silu(_causal_conv(xbc, carry_scr[...], cw_ref[...], cb_ref[...]))
    carry_scr[...] = xbc[tb - SUBLANES:]
    dte_scr[...] = _dot3_right(dt_ref[0], e_ref[...])

    tril = _tri(CHUNK, True)
    tril_b = jnp.where(tril, 1.0, 0.0).astype(BF16)
    triu_b = jnp.where(_tri(CHUNK, False), 1.0, 0.0).astype(BF16)
    aexp = aexp_ref[...]

    def chunk(c, carry):
        rows = pl.ds(pl.multiple_of(c * CHUNK, CHUNK), CHUNK)
        xs = xc_scr[rows, 0:inner]
        bm = xc_scr[rows, inner:inner + gn]
        cm = xc_scr[rows, inner + gn:inner + 2 * gn]
        dte = dte_scr[rows, :]
        cum = _dot3_left(tril_b, dte * aexp)
        cum_last = cum[CHUNK - 1:CHUNK, :]
        cum_t = _dot3_right(dtt_ref[0, c] * acol_ref[...], triu_b)
        xdt = (xs * dte).astype(BF16)
        st = st_scr[...]
        ys = []
        inters = []
        upds = []
        xw = xs * (jnp.exp(cum_last - cum) * dte)
        for g in range(SSD_GROUPS):
            ncols = slice(g * SSD_STATE, (g + 1) * SSD_STATE)
            gcols = slice(g * gw, (g + 1) * gw)
            cmg = cm[:, ncols].astype(BF16)
            bmg = bm[:, ncols]
            inters.append(jnp.dot(cmg, st[:, gcols].astype(BF16), preferred_element_type=F32))
            cb = lax.dot_general(cmg, bmg.astype(BF16), _NT, preferred_element_type=F32)
            for hh in range(hpg):
                h = g * hpg + hh
                hcols = slice(h * SSD_HEAD_DIM, (h + 1) * SSD_HEAD_DIM)
                seg = cum[:, hcols] - cum_t[h:h + 1, :]
                w = cb * jnp.exp(jnp.where(tril, seg, NEG_INF))
                ys.append(jnp.dot(w.astype(BF16), xdt[:, hcols], preferred_element_type=F32))
            upds.append(_bdot(bmg.T, xw[:, gcols]))
        y = jnp.concatenate(ys, axis=1) + jnp.exp(cum) * jnp.concatenate(inters, axis=1)
        st_scr[...] = st * jnp.exp(cum_last) + jnp.concatenate(upds, axis=1)
        yd = (y + dexp_ref[...] * xs) * _silu(zg_ref[0, rows, :])
        ms = jnp.mean(yd * yd, axis=-1, keepdims=True)
        y_ref[0, rows, :] = yd * lax.rsqrt(ms + RMS_EPS) * ng_ref[...]
        return carry

    lax.fori_loop(0, n_chunks, chunk, 0)

    @pl.when(t == pl.num_programs(1) - 1)
    def _():
        sfin_ref[0] = st_scr[...]
        clast_ref[0] = carry_scr[...]


def _ssd(xbc, zg, dt, dtt, conv_prefix, conv_w, conv_b, a_log, d_skip, norm_g, s0, tb):
    b, t, nxbc = xbc.shape
    inner = zg.shape[2]
    nh = dtt.shape[1]
    width = conv_w.shape[0]
    nc = t // CHUNK
    dtt4 = jnp.transpose(dtt.reshape(b, nh, nc, CHUNK), (0, 2, 1, 3))
    pre8 = jnp.concatenate([jnp.zeros((b, SUBLANES - (width - 1), nxbc), F32), conv_prefix], axis=1)
    a = -jnp.exp(a_log)
    aexp = jnp.repeat(a, SSD_HEAD_DIM).reshape(1, inner)
    acol = a.reshape(nh, 1)
    dexp = jnp.repeat(d_skip, SSD_HEAD_DIM).reshape(1, inner)
    expand = (np.arange(LANES)[:, None] == np.arange(inner)[None, :] // SSD_HEAD_DIM)
    expand = jnp.asarray(expand, BF16)
    s0t = jnp.transpose(s0, (0, 3, 1, 2)).reshape(b, SSD_STATE, inner)
    ng = norm_g.reshape(1, inner)
    cb2 = conv_b.reshape(1, nxbc)
    tile = lambda n: pl.BlockSpec((1, tb, n), lambda i, j: (i, j, 0))
    const = lambda arr: pl.BlockSpec(arr.shape, lambda i, j: (0,) * arr.ndim)
    bspec = lambda arr: pl.BlockSpec((1,) + arr.shape[1:], lambda i, j: (i,) + (0,) * (arr.ndim - 1))
    y, sfin, clast = pl.pallas_call(
        functools.partial(_ssd_kernel, n_chunks=tb // CHUNK),
        grid=(b, t // tb),
        in_specs=[tile(nxbc), tile(inner), tile(LANES),
                  pl.BlockSpec((1, tb // CHUNK, nh, CHUNK), lambda i, j: (i, j, 0, 0)),
                  bspec(pre8), const(conv_w), const(cb2), const(aexp), const(acol), const(dexp),
                  const(expand), const(ng), bspec(s0t)],
        out_specs=[tile(inner), bspec(s0t), bspec(pre8)],
        out_shape=[jax.ShapeDtypeStruct((b, t, inner), F32), jax.ShapeDtypeStruct(s0t.shape, F32),
                   jax.ShapeDtypeStruct(pre8.shape, F32)],
        scratch_shapes=[pltpu.VMEM((SSD_STATE, inner), F32), pltpu.VMEM((SUBLANES, nxbc), F32),
                        pltpu.VMEM((tb, nxbc), F32), pltpu.VMEM((tb, inner), F32)],
        compiler_params=_params(2),
        name="ssd",
    )(xbc, zg, dt, dtt4, pre8, conv_w, cb2, aexp, acol, dexp, expand, ng, s0t)
    sfin = jnp.transpose(sfin.reshape(b, SSD_STATE, nh, SSD_HEAD_DIM), (0, 2, 3, 1))
    return y, sfin, clast[:, SUBLANES - (width - 1):]


def _trunk(x, mods, P, st, sample):
    b, t, d = x.shape
    tb = min(MAX_TILE, t)
    depth = P['w_mod'].shape[0]
    new = {name: [] for name in ('s5_re', 's5_im', 'gla', 'swa_k', 'swa_v', 'ssd', 'ssd_conv', 'ffn_conv')}
    for layer in range(depth):
        i = layer // 2
        mod = mods[layer].reshape(b, 6, d)
        ffn = (P['norm2_g'][layer], P['ffn_w_up'][layer], P['ffn_conv_w'][layer], P['ffn_conv_b'][layer],
               P['ffn_w_down'][layer], st['ffn_conv'][layer], tb)
        if layer % 2 == 0:
            u, qk, v, r, gate = _ev_in(x, mod, P['norm1_g'][layer], P['ev_w_in'][i], P['gla_w_gate2'][i],
                                       P['gla_b_gate'][i], tb)
            mats = _s5_matrices(P['s5_a_re'][i], P['s5_a_im'][i], P['s5_log_dt'][i], P['s5_b_re'][i],
                                P['s5_b_im'][i], P['s5_c_re'][i], P['s5_c_im'][i])
            ya, sr, si = _s5(u, mats, st['s5_re'][i], st['s5_im'][i])
            ob, sg = _gla(qk, v, gate, r, st['gla'][i], P['gla_norm_g'][i], tb)
            new['s5_re'].append(sr)
            new['s5_im'].append(si)
            new['gla'].append(sg)
            x, fp = _out_ffn(x, mod, ya, ob, P['ev_w_out'][i], *ffn,
                             s5_extra=(u, P['s5_d'][i], P['s5_w_glu'][i], P['s5_b_glu'][i]))
        else:
            nq = P['swa_sink'].shape[1] * HEAD_DIM
            nkv = SWA_KV_HEADS * HEAD_DIM
            inner = P['ssd_norm_g'].shape[1]
            nxbc = P['ssd_conv_w'].shape[2]
            nh = P['ssd_a_log'].shape[1]
            q, k, v, zg, xbc, dt, dtt = _od_in(x, mod, P['norm1_g'][layer], P['od_w_in'][i], P['swa_q_norm'][i],
                                               P['swa_k_norm'][i], P['ssd_dt_bias'][i],
                                               (nq, nkv, inner, nxbc, nh), tb)
            bias = _rel_bias(P['t5_bias'])
            if sample:
                k0 = st['swa_k'][i].reshape(b, WINDOW, nkv)
                v0 = st['swa_v'][i].reshape(b, WINDOW, nkv)
            else:
                k0 = v0 = jnp.zeros((b, WINDOW, nkv), F32)
            oc = _swa(q, k, v, k0, v0, bias, P['swa_sink'][i], tb, mask_start=not sample)
            yd, ss, sc = _ssd(xbc, zg, dt, dtt, st['ssd_conv'][i], P['ssd_conv_w'][i], P['ssd_conv_b'][i],
                              P['ssd_a_log'][i], P['ssd_d'][i], P['ssd_norm_g'][i], st['ssd'][i], tb)
            keep = slice(None) if sample else slice(t - WINDOW, t)
            new['swa_k'].append(k[:, keep].reshape(b, -1, SWA_KV_HEADS, HEAD_DIM))
            new['swa_v'].append(v[:, keep].reshape(b, -1, SWA_KV_HEADS, HEAD_DIM))
            new['ssd'].append(ss)
            new['ssd_conv'].append(sc)
            x, fp = _out_ffn(x, mod, oc, yd, P['od_w_out'][i], *ffn)
        new['ffn_conv'].append(fp)
    return x, {name: jnp.stack(vals) for name, vals in new.items()}


def kernel(x_prompt, x_sample, state_s5_re, state_s5_im, state_gla, cache_swa_k, cache_swa_v, state_ssd, state_ssd_conv, state_ffn_conv, c_prompt, c_sample, t5_bias, norm1_g, norm2_g, w_mod, b_mod, ffn_w_up, ffn_conv_w, ffn_conv_b, ffn_w_down, ev_w_in, ev_w_out, s5_a_re, s5_a_im, s5_log_dt, s5_b_re, s5_b_im, s5_c_re, s5_c_im, s5_d, s5_w_glu, s5_b_glu, gla_w_gate2, gla_b_gate, gla_norm_g, od_w_in, od_w_out, swa_q_norm, swa_k_norm, swa_sink, ssd_conv_w, ssd_conv_b, ssd_dt_bias, ssd_a_log, ssd_d, ssd_norm_g):
    P = dict(t5_bias=t5_bias, norm1_g=norm1_g, norm2_g=norm2_g, w_mod=w_mod, b_mod=b_mod,
             ffn_w_up=ffn_w_up, ffn_conv_w=ffn_conv_w, ffn_conv_b=ffn_conv_b, ffn_w_down=ffn_w_down,
             ev_w_in=ev_w_in, ev_w_out=ev_w_out, s5_a_re=s5_a_re, s5_a_im=s5_a_im, s5_log_dt=s5_log_dt,
             s5_b_re=s5_b_re, s5_b_im=s5_b_im, s5_c_re=s5_c_re, s5_c_im=s5_c_im, s5_d=s5_d,
             s5_w_glu=s5_w_glu, s5_b_glu=s5_b_glu, gla_w_gate2=gla_w_gate2, gla_b_gate=gla_b_gate,
             gla_norm_g=gla_norm_g, od_w_in=od_w_in, od_w_out=od_w_out, swa_q_norm=swa_q_norm,
             swa_k_norm=swa_k_norm, swa_sink=swa_sink, ssd_conv_w=ssd_conv_w, ssd_conv_b=ssd_conv_b,
             ssd_dt_bias=ssd_dt_bias, ssd_a_log=ssd_a_log, ssd_d=ssd_d, ssd_norm_g=ssd_norm_g)
    bp = x_prompt.shape[0]
    n_even, n_odd = state_s5_re.shape[0], state_ssd.shape[0]
    depth = w_mod.shape[0]
    zeros_like_b = lambda a: jnp.zeros((a.shape[0], bp) + a.shape[2:], F32)
    zero_st = dict(s5_re=zeros_like_b(state_s5_re), s5_im=zeros_like_b(state_s5_im), gla=zeros_like_b(state_gla),
                   ssd=zeros_like_b(state_ssd), ssd_conv=zeros_like_b(state_ssd_conv),
                   ffn_conv=zeros_like_b(state_ffn_conv))
    sample_st = dict(s5_re=state_s5_re, s5_im=state_s5_im, gla=state_gla, swa_k=cache_swa_k,
                     swa_v=cache_swa_v, ssd=state_ssd, ssd_conv=state_ssd_conv, ffn_conv=state_ffn_conv)
    mods = _modulation(jnp.concatenate([c_prompt, c_sample], axis=0), w_mod, b_mod)
    y_prompt, stp = _trunk(x_prompt, mods[:, :bp], P, zero_st, False)
    y_sample, sts = _trunk(x_sample, mods[:, bp:], P, sample_st, True)
    names = ('s5_re', 's5_im', 'gla', 'swa_k', 'swa_v', 'ssd', 'ssd_conv', 'ffn_conv')
    return (y_prompt, y_sample) + tuple(stp[n] for n in names) + tuple(sts[n] for n in names)
```

```python
import functools
import math

import jax
import jax.numpy as jnp
import numpy as np
from jax import lax
from jax.experimental import pallas as pl
from jax.experimental.pallas import tpu as pltpu

F32 = jnp.float32
BF16 = jnp.bfloat16

CHUNK = 64
WINDOW = 128
S5_GROUP = 16
S5_STATE = 64
S5_SUB = 8
S5_MAX_SUBS = 64
GLA_HEADS = 4
GLA_GATE_NORM = 16.0
HEAD_DIM = 64
SWA_KV_HEADS = 2
SWA_KEYS = 256
SSD_HEAD_DIM = 64
SSD_STATE = 128
SSD_GROUPS = 2
T5_BUCKETS = 32
T5_MAX_DIST = 128
RMS_EPS = 1e-6
NEG_INF = -1e30
LANES = 128
SUBLANES = 8
assert S5_SUB == SUBLANES and S5_GROUP * SUBLANES == LANES
MAX_TILE = 512
CHUNK_UNROLL = 4
FFN_COL_BLOCK = 256
VMEM_LIMIT = 56 * 1024 * 1024

_NT = (((1,), (1,)), ((), ()))
_HI = lax.Precision.HIGHEST


def _params(n_axes=2):
    sem = ("parallel",) + ("arbitrary",) * (n_axes - 1)
    return pltpu.CompilerParams(dimension_semantics=sem, vmem_limit_bytes=VMEM_LIMIT)


def _bdot(a, b):
    return jnp.dot(a.astype(BF16), b.astype(BF16), preferred_element_type=F32)


def _bdot_nt(a, b):
    return lax.dot_general(a.astype(BF16), b.astype(BF16), _NT, preferred_element_type=F32)


def _split3(x):
    hi = x.astype(BF16)
    r1 = x - hi.astype(F32)
    mid = r1.astype(BF16)
    lo = (r1 - mid.astype(F32)).astype(BF16)
    return hi, mid, lo


def _dot3_left(c, x):
    hi, mid, lo = _split3(x)
    d = lambda p: jnp.dot(c, p, preferred_element_type=F32)
    return d(hi) + d(mid) + d(lo)


def _dot3_right(x, c):
    hi, mid, lo = _split3(x)
    d = lambda p: jnp.dot(p, c, preferred_element_type=F32)
    return d(hi) + d(mid) + d(lo)


def _silu(x):
    return x * jax.nn.sigmoid(x)


def _softplus(x):
    return jnp.maximum(x, 0.0) + jnp.log1p(jnp.exp(-jnp.abs(x)))


def _log_sigmoid(x):
    return jnp.minimum(x, 0.0) - jnp.log1p(jnp.exp(-jnp.abs(x)))


def _norm_mod(x, g, scale, shift):
    ms = jnp.mean(x * x, axis=-1, keepdims=True)
    return (x * lax.rsqrt(ms + RMS_EPS) * g) * (1.0 + scale) + shift


def _tri(n, lower):
    r = lax.broadcasted_iota(jnp.int32, (n, n), 0)
    c = lax.broadcasted_iota(jnp.int32, (n, n), 1)
    return (r >= c) if lower else (r <= c)


def _causal_conv(u, prev8, w, b):
    width = w.shape[0]
    n = u.shape[0]
    ext = jnp.concatenate([prev8, u[0:SUBLANES]], axis=0)
    full = b
    head = b
    for j in range(width):
        sh = width - 1 - j
        if sh == 0:
            full = full + u * w[j:j + 1]
            head = head + u[0:SUBLANES] * w[j:j + 1]
        else:
            full = full + pltpu.roll(u, sh, axis=0) * w[j:j + 1]
            head = head + pltpu.roll(ext, sh, axis=0)[SUBLANES:2 * SUBLANES] * w[j:j + 1]
    if n == SUBLANES:
        return head
    return jnp.concatenate([head, full[SUBLANES:]], axis=0)


def _mod_kernel(c_ref, w_ref, b_ref, o_ref):
    o_ref[0] = _bdot(_silu(c_ref[...]), w_ref[0]) + b_ref[0]


def _modulation(c, w_mod, b_mod):
    depth, d, n = w_mod.shape
    bc = c.shape[0]
    tn = n // 4
    return pl.pallas_call(
        _mod_kernel,
        grid=(depth, n // tn),
        in_specs=[pl.BlockSpec((bc, d), lambda l, j: (0, 0)),
                  pl.BlockSpec((1, d, tn), lambda l, j: (l, 0, j)),
                  pl.BlockSpec((1, 1, tn), lambda l, j: (l, 0, j))],
        out_specs=pl.BlockSpec((1, bc, tn), lambda l, j: (l, 0, j)),
        out_shape=jax.ShapeDtypeStruct((depth, bc, n), F32),
        compiler_params=_params(2),
        name="modulation",
    )(c, w_mod, b_mod.reshape(depth, 1, n))


def _ev_in_kernel(x_ref, mod_ref, g_ref, w_ref, wg2_ref, bg_ref,
                  u_ref, qk_ref, v_ref, r_ref, gate_ref):
    mod = mod_ref[0]
    hn = _norm_mod(x_ref[0], g_ref[...], mod[1:2], mod[0:1]).astype(BF16)
    dot = lambda lo, hi: jnp.dot(hn, w_ref[:, lo:hi], preferred_element_type=F32)
    u = dot(0, 512)
    for c in range(u_ref.shape[0]):
        u_ref[c] = _chunk_transpose(u[:, c * LANES:(c + 1) * LANES].reshape(-1, SUBLANES, LANES))
    qk_ref[0] = dot(512, 1024)
    v_ref[0] = dot(1024, 1536)
    r_ref[0] = dot(1536, 2048)
    gl = dot(2048, 2048 + LANES)
    gate_ref[0] = _log_sigmoid(_bdot(gl, wg2_ref[...]) + bg_ref[...]) * (1.0 / GLA_GATE_NORM)


def _ev_in(x, mod, norm_g, w_in, w_gate2, b_gate, tb):
    b, t, d = x.shape
    rank = w_gate2.shape[0]
    nk = w_gate2.shape[1]
    wu, wq, wk, wv, wgl, wr = jnp.split(w_in, [512, 768, 1024, 1536, 1536 + rank], axis=1)
    w = jnp.concatenate([wu, wq, wk, wv, wr, wgl, jnp.zeros((d, LANES - rank), F32)], axis=1).astype(BF16)
    wg2 = jnp.concatenate([w_gate2, jnp.zeros((LANES - rank, nk), F32)], axis=0).astype(BF16)
    tile = lambda n: pl.BlockSpec((1, tb, n), lambda i, j: (i, j, 0))
    const = lambda a: pl.BlockSpec(a.shape, lambda i, j: (0,) * a.ndim)
    g2 = norm_g.reshape(1, d)
    bg = b_gate.reshape(1, nk)
    return pl.pallas_call(
        _ev_in_kernel,
        grid=(b, t // tb),
        in_specs=[tile(d), pl.BlockSpec((1, 6, d), lambda i, j: (i, 0, 0)), const(g2), const(w),
                  const(wg2), const(bg)],
        out_specs=[pl.BlockSpec((512 // LANES, tb // S5_SUB, None, SUBLANES, LANES), lambda i, j: (0, j, i, 0, 0)),
                   tile(512), tile(512), tile(512), tile(nk)],
        out_shape=[jax.ShapeDtypeStruct((512 // LANES, t // S5_SUB, b, SUBLANES, LANES), F32)]
        + [jax.ShapeDtypeStruct((b, t, n), F32) for n in (512, 512, 512, nk)],
        compiler_params=_params(2),
        name="ev_in",
    )(x, mod, g2, w, wg2, bg)


def _gla_kernel(qk_ref, v_ref, gate_ref, r_ref, s0_ref, ng_ref, o_ref, sfin_ref, st_scr, *, n_chunks):
    t = pl.program_id(1)
    dk = qk_ref.shape[2] // 2 // GLA_HEADS
    dv = v_ref.shape[2] // GLA_HEADS
    nk = GLA_HEADS * dk
    nv = GLA_HEADS * dv
    blk = (lax.broadcasted_iota(jnp.int32, (nv, nk), 0) // dv
           == lax.broadcasted_iota(jnp.int32, (nv, nk), 1) // dk)

    @pl.when(t == 0)
    def _():
        s0 = jnp.concatenate([s0_ref[0, h] for h in range(GLA_HEADS)], axis=0)
        st_scr[...] = jnp.where(blk, jnp.concatenate([s0] * GLA_HEADS, axis=1), 0.0)

    tril = _tri(CHUNK, True)
    tril_b = jnp.where(tril, 1.0, 0.0).astype(BF16)
    lane_head = lax.broadcasted_iota(jnp.int32, (CHUNK, nk), 1) // dk
    ng = ng_ref[...]

    def chunk(c, carry):
        rows = pl.ds(pl.multiple_of(c * CHUNK, CHUNK), CHUNK)
        q = qk_ref[0, rows, 0:nk] * dk ** -0.5
        k = qk_ref[0, rows, nk:2 * nk]
        v = v_ref[0, rows, :]
        cum = _dot3_left(tril_b, gate_ref[0, rows, :])
        cum_last = cum[CHUNK - 1:CHUNK, :]
        qe = q * jnp.exp(cum)
        ke = (k * jnp.exp(-cum)).astype(BF16)
        kd = k * jnp.exp(cum_last - cum)
        st = st_scr[...]
        inter = _bdot_nt(qe, st)
        v_b = v.astype(BF16)
        for h in range(GLA_HEADS):
            qm = jnp.where(lane_head == h, qe, 0.0).astype(BF16)
            att = lax.dot_general(qm, ke, _NT, preferred_element_type=F32)
            att = jnp.where(tril, att, 0.0).astype(BF16)
            cols = slice(h * dv, (h + 1) * dv)
            oh = jnp.dot(att, v_b[:, cols], preferred_element_type=F32) + inter[:, cols]
            ms = jnp.mean(oh * oh, axis=-1, keepdims=True)
            oh = oh * lax.rsqrt(ms + RMS_EPS) * ng
            o_ref[0, rows, cols] = oh * _silu(r_ref[0, rows, cols])
        upd = _bdot(v.T, kd)
        st_scr[...] = jnp.where(blk, st * jnp.exp(cum_last) + upd, 0.0)
        return carry

    lax.fori_loop(0, n_chunks, chunk, 0, unroll=math.gcd(n_chunks, CHUNK_UNROLL))

    @pl.when(t == pl.num_programs(1) - 1)
    def _():
        for h in range(GLA_HEADS):
            sfin_ref[0, h] = st_scr[h * dv:(h + 1) * dv, h * dk:(h + 1) * dk]


def _gla(qk, v, gate, r, s0, norm_g, tb):
    b, t, nv = v.shape
    nk = gate.shape[2]
    dk, dv = nk // GLA_HEADS, nv // GLA_HEADS
    s0t = jnp.swapaxes(s0, 2, 3)
    tile = lambda n: pl.BlockSpec((1, tb, n), lambda i, j: (i, j, 0))
    sspec = pl.BlockSpec((1, GLA_HEADS, dv, dk), lambda i, j: (i, 0, 0, 0))
    ng = norm_g.reshape(1, dv)
    o, sfin = pl.pallas_call(
        functools.partial(_gla_kernel, n_chunks=tb // CHUNK),
        grid=(b, t // tb),
        in_specs=[tile(2 * nk), tile(nv), tile(nk), tile(nv), sspec,
                  pl.BlockSpec((1, dv), lambda i, j: (0, 0))],
        out_specs=[tile(nv), sspec],
        out_shape=[jax.ShapeDtypeStruct((b, t, nv), F32),
                   jax.ShapeDtypeStruct((b, GLA_HEADS, dv, dk), F32)],
        scratch_shapes=[pltpu.VMEM((nv, nk), F32)],
        compiler_params=_params(2),
        name="gla",
    )(qk, v, gate, r, s0t, ng)
    return o, jnp.swapaxes(sfin, 2, 3)


def _chunk_transpose(x):
    s = lax.broadcasted_iota(jnp.int32, (1, SUBLANES, LANES), 1)
    c = lax.broadcasted_iota(jnp.int32, (1, SUBLANES, LANES), 2) // S5_GROUP
    for d in (4, 2, 1):
        sb = (s & d) != 0
        cb = (c & d) != 0
        if 2 * d == SUBLANES:
            t = pltpu.roll(pltpu.roll(x, d, axis=1), S5_GROUP * d, axis=2)
        else:
            xs = jnp.where(sb, pltpu.roll(x, d, axis=1), pltpu.roll(x, SUBLANES - d, axis=1))
            t = jnp.where(cb, pltpu.roll(xs, S5_GROUP * d, axis=2), pltpu.roll(xs, LANES - S5_GROUP * d, axis=2))
        x = jnp.where(sb != cb, t, x)
    return x


def _s5_kernel(u_ref, w1_ref, n_ref, a_ref, d_ref, x0_ref, y_ref, xf_ref, mm_scr, xs_scr, st_scr, *, tk, bsz):
    j = pl.program_id(1)
    n_g = st_scr.shape[0]
    rows = tk * bsz
    of_group = lambda g: pl.ds(g, rows, stride=n_g)

    @pl.when(j == 0)
    def _():
        st_scr[...] = x0_ref[0]

    for g in range(n_g):
        ub = u_ref[of_group(g), :].astype(BF16)
        mm_scr[g] = jnp.dot(ub, w1_ref[0, g], preferred_element_type=F32)

    half = n_g // 2
    for g0 in (0, half):
        coef = [a_ref[0, g] for g in range(g0, g0 + half)]

        def step(k, carry, g0=g0, coef=coef):
            r = pl.ds(pl.multiple_of(k * bsz, bsz), bsz)
            out = []
            for i in range(half):
                x, xsw = carry[2 * i], carry[2 * i + 1]
                a1, a2, a2s = coef[i][0:1], coef[i][1:2], coef[i][2:3]
                xs_scr[g0 + i, r, :] = x
                out.append(a1 * x + a2 * xsw + mm_scr[g0 + i, r, LANES:2 * LANES])
                out.append(a1 * xsw + a2s * x + mm_scr[g0 + i, r, 2 * LANES:3 * LANES])
            return tuple(out)

        fin = lax.fori_loop(0, tk, step, tuple(st_scr[g0 + i, v] for i in range(half) for v in range(2)))
        for i in range(half):
            st_scr[g0 + i, 0] = fin[2 * i]
            st_scr[g0 + i, 1] = fin[2 * i + 1]

    for g in range(n_g):
        y = (mm_scr[g, :, 0:LANES] + jnp.dot(xs_scr[g].astype(BF16), n_ref[0, g], preferred_element_type=F32)
             + d_ref[0, g] * u_ref[of_group(g), :])
        y_ref[of_group(g), :] = jax.nn.gelu(y)

    @pl.when(j == pl.num_programs(1) - 1)
    def _():
        xf_ref[0] = st_scr[:, 0]


def _cmul(ar, ai, br, bi):
    return ar * br - ai * bi, ar * bi + ai * br


def _s5_matrices(a_re, a_im, log_dt, b_re, b_im, c_re, c_im):
    g, p = a_re.shape
    hdim = b_re.shape[-1]
    dt = jnp.exp(log_dt)[:, None]
    mag = jnp.exp(a_re * dt)
    ab_re, ab_im = mag * jnp.cos(a_im * dt), mag * jnp.sin(a_im * dt)
    den = a_re * a_re + a_im * a_im
    num_re, num_im = ab_re - 1.0, ab_im
    g_re = (num_re * a_re + num_im * a_im) / den
    g_im = (num_im * a_re - num_re * a_im) / den
    bb_re = g_re[..., None] * b_re - g_im[..., None] * b_im
    bb_im = g_re[..., None] * b_im + g_im[..., None] * b_re
    pw_re, pw_im = jnp.ones((1, g, p), F32), jnp.zeros((1, g, p), F32)
    sq_re, sq_im = ab_re, ab_im
    while pw_re.shape[0] < S5_SUB + 1:
        nr, ni = _cmul(pw_re, pw_im, sq_re, sq_im)
        pw_re, pw_im = jnp.concatenate([pw_re, nr]), jnp.concatenate([pw_im, ni])
        sq_re, sq_im = _cmul(sq_re, sq_im, sq_re, sq_im)
    pw_re, pw_im = pw_re[:S5_SUB + 1], pw_im[:S5_SUB + 1]
    ca_re, ca_im = _cmul(c_re[None], c_im[None], pw_re[:, :, None, :], pw_im[:, :, None, :])
    kern = (jnp.einsum('tghp,gpk->tghk', ca_re[:S5_SUB], bb_re, precision=_HI)
            - jnp.einsum('tghp,gpk->tghk', ca_im[:S5_SUB], bb_im, precision=_HI))
    kern = jnp.concatenate([kern, jnp.zeros((1,) + kern.shape[1:], F32)])
    s_in = np.arange(S5_SUB)[:, None]
    s_out = np.arange(S5_SUB)[None, :]
    lag = np.where(s_out >= s_in, s_out - s_in, S5_SUB)
    m = kern[lag]
    m = jnp.transpose(m, (2, 0, 4, 1, 3)).reshape(g, S5_SUB * hdim, S5_SUB * hdim)
    pr, pi = _cmul(pw_re[S5_SUB - 1::-1][..., None], pw_im[S5_SUB - 1::-1][..., None], bb_re[None], bb_im[None])
    flat = lambda a: jnp.transpose(a, (1, 0, 3, 2)).reshape(g, S5_SUB * hdim, p)
    nm = jnp.concatenate([ca_re[1:], -ca_im[1:]], axis=3)
    nm = jnp.transpose(nm, (1, 3, 0, 2)).reshape(g, 2 * p, S5_SUB * hdim)
    w1 = jnp.concatenate([m, flat(pr), flat(pi), flat(pi), flat(pr)], axis=2)
    a_r, a_i = pw_re[S5_SUB], pw_im[S5_SUB]
    coef = jnp.stack([jnp.concatenate([a_r, a_r], axis=1), jnp.concatenate([-a_i, a_i], axis=1),
                      jnp.concatenate([a_i, -a_i], axis=1)], axis=1)
    return w1.astype(BF16), nm.astype(BF16), coef


def _s5(u2, mats, d_skip, x0_re, x0_im):
    w1, nm, coef = mats
    n_ct, n_sub, b, n_g, lanes = u2.shape
    g, p2 = nm.shape[0], nm.shape[1]
    hdim = lanes // S5_SUB
    tk = min(n_sub, S5_MAX_SUBS)
    x0 = jnp.transpose(jnp.concatenate([x0_re, x0_im], axis=2), (1, 0, 2))
    x0 = jnp.stack([x0, jnp.roll(x0, p2 // 2, axis=2)], axis=1)
    dt = jnp.tile(d_skip.reshape(g, 1, hdim), (1, 1, S5_SUB))
    by_tile = lambda a: a.reshape((n_ct, n_g) + a.shape[1:])
    pspec = lambda a: pl.BlockSpec((1,) + a.shape[1:], lambda c, j: (c,) + (0,) * (a.ndim - 1))
    rows = tk * b
    uspec = pl.BlockSpec((None, rows * n_g, lanes), lambda c, j: (c, j, 0))
    params = [by_tile(a) for a in (w1, nm, coef, dt, x0)]
    y, xf = pl.pallas_call(
        functools.partial(_s5_kernel, tk=tk, bsz=b),
        grid=(n_ct, n_sub // tk),
        in_specs=[uspec] + [pspec(a) for a in params],
        out_specs=[uspec, pl.BlockSpec((1, n_g, b, p2), lambda c, j: (c, 0, 0, 0))],
        out_shape=[jax.ShapeDtypeStruct((n_ct, n_sub * b * n_g, lanes), F32),
                   jax.ShapeDtypeStruct((n_ct, n_g, b, p2), F32)],
        scratch_shapes=[pltpu.VMEM((n_g, rows, 3 * lanes), F32), pltpu.VMEM((n_g, rows, lanes), F32),
                        pltpu.VMEM((n_g, 2, b, p2), F32)],
        compiler_params=_params(2),
        name="s5",
    )(u2.reshape(n_ct, n_sub * b * n_g, lanes), *params)
    xf = jnp.transpose(xf.reshape(g, b, p2), (1, 0, 2))
    return y.reshape(u2.shape), xf[:, :, :p2 // 2], xf[:, :, p2 // 2:]


def _out_ffn_kernel(*refs, with_s5, n_cb):
    if with_s5:
        (x_ref, mod_ref, m1_ref, m2_ref, wglu_ref, bglu_ref, wout_ref, n2g_ref,
         wa_ref, wg_ref, cwa_ref, cwg_ref, cba_ref, cbg_ref, wd_ref, pre_ref,
         xo_ref, ulast_ref, carry_scr, h_scr) = refs
    else:
        (x_ref, mod_ref, m1_ref, m2_ref, wout_ref, n2g_ref,
         wa_ref, wg_ref, cwa_ref, cwg_ref, cba_ref, cbg_ref, wd_ref, pre_ref,
         xo_ref, ulast_ref, carry_scr, h_scr) = refs
    t = pl.program_id(1)

    @pl.when(t == 0)
    def _():
        carry_scr[...] = pre_ref[0]

    mod = mod_ref[0]
    if with_s5:
        ya = jnp.concatenate([_chunk_transpose(m1_ref[c]).reshape(-1, LANES) for c in range(m1_ref.shape[0])],
                             axis=1)
        m1 = ya * jax.nn.sigmoid(_bdot(ya, wglu_ref[...]) + bglu_ref[...])
    else:
        m1 = m1_ref[0]
    half = m1.shape[1]
    out = (jnp.dot(m1.astype(BF16), wout_ref[0:half, :], preferred_element_type=F32)
           + jnp.dot(m2_ref[0].astype(BF16), wout_ref[half:, :], preferred_element_type=F32))
    x1 = x_ref[0] + mod[2:3] * out
    hn = _norm_mod(x1, n2g_ref[...], mod[4:5], mod[3:4]).astype(BF16)
    tb = x1.shape[0]
    cb = wa_ref.shape[2]

    def up(j):
        return (jnp.dot(hn, wa_ref[j], preferred_element_type=F32),
                jnp.dot(hn, wg_ref[j], preferred_element_type=F32))

    nxt = up(0)
    for j in range(n_cb):
        ua, ug = nxt
        if j + 1 < n_cb:
            nxt = up(j + 1)
        a = _causal_conv(ua, carry_scr[0, j], cwa_ref[j], cba_ref[j])
        gt = _causal_conv(ug, carry_scr[1, j], cwg_ref[j], cbg_ref[j])
        carry_scr[0, j] = ua[tb - SUBLANES:]
        carry_scr[1, j] = ug[tb - SUBLANES:]
        h_scr[:, j * cb:(j + 1) * cb] = (_silu(a) * gt).astype(BF16)
    f = jnp.dot(h_scr[...], wd_ref[...], preferred_element_type=F32)
    xo_ref[0] = x1 + mod[5:6] * f

    @pl.when(t == pl.num_programs(1) - 1)
    def _():
        ulast_ref[0] = carry_scr[...]


def _out_ffn(x, mod, m1, m2, w_out, norm2_g, w_up, conv_w, conv_b, w_down, prefix, tb, s5_extra=None):
    b, t, d = x.shape
    dff = w_down.shape[0]
    cb = FFN_COL_BLOCK
    n_cb = dff // cb
    width = conv_w.shape[0]
    half = m2.shape[2]
    blocks = lambda a: a.reshape(a.shape[0], 2, n_cb, cb)
    w_up_b = jnp.transpose(blocks(w_up.astype(BF16)), (1, 2, 0, 3))
    cw = jnp.transpose(blocks(conv_w), (1, 2, 0, 3))
    cbias = jnp.transpose(blocks(conv_b.reshape(1, 2 * dff)), (1, 2, 0, 3))
    wd = w_down.astype(BF16)
    pre8 = jnp.concatenate([jnp.zeros((b, SUBLANES - (width - 1), 2 * dff), F32), prefix], axis=1)
    pre8 = jnp.transpose(pre8.reshape(b, SUBLANES, 2, n_cb, cb), (0, 2, 3, 1, 4))
    tile = lambda n: pl.BlockSpec((1, tb, n), lambda i, j: (i, j, 0))
    const = lambda a: pl.BlockSpec(a.shape, lambda i, j: (0,) * a.ndim, pipeline_mode=pl.Buffered(1))
    bspec = lambda a: pl.BlockSpec((1,) + a.shape[1:], lambda i, j: (i,) + (0,) * (a.ndim - 1))
    wo = w_out.astype(BF16)
    n2g = norm2_g.reshape(1, d)
    args = [x, mod]
    specs = [tile(d), bspec(mod)]
    if s5_extra is not None:
        w_glu, b_glu = s5_extra
        w_glu, b_glu = w_glu.astype(BF16), b_glu.reshape(1, half)
        args += [m1, m2, w_glu, b_glu]
        specs += [pl.BlockSpec((m1.shape[0], tb // S5_SUB, None) + m1.shape[3:], lambda i, j: (0, j, i, 0, 0)),
                  tile(m2.shape[2]), const(w_glu), const(b_glu)]
    else:
        args += [m1, m2]
        specs += [tile(half), tile(m2.shape[2])]
    weights = [wo, n2g, w_up_b[0], w_up_b[1], cw[0], cw[1], cbias[0], cbias[1], wd]
    args += weights + [pre8]
    specs += [const(a) for a in weights] + [bspec(pre8)]
    xo, ulast = pl.pallas_call(
        functools.partial(_out_ffn_kernel, with_s5=s5_extra is not None, n_cb=n_cb),
        grid=(b, t // tb),
        in_specs=specs,
        out_specs=[tile(d), bspec(pre8)],
        out_shape=[jax.ShapeDtypeStruct((b, t, d), F32), jax.ShapeDtypeStruct(pre8.shape, F32)],
        scratch_shapes=[pltpu.VMEM(pre8.shape[1:], F32), pltpu.VMEM((tb, dff), BF16)],
        compiler_params=_params(2),
        name="out_ffn_s5" if s5_extra is not None else "out_ffn",
    )(*args)
    ulast = jnp.transpose(ulast, (0, 3, 1, 2, 4)).reshape(b, SUBLANES, 2 * dff)
    return xo, ulast[:, SUBLANES - (width - 1):]


def _head_rms(x, ones_blk, g):
    ss = _dot3_right(x * x, ones_blk) * (1.0 / HEAD_DIM)
    return x * lax.rsqrt(ss + RMS_EPS) * g


def _od_in_kernel(x_ref, mod_ref, g_ref, w_ref, qg_ref, kg_ref, dtb_ref,
                  q_ref, k_ref, v_ref, zg_ref, xbc_ref, dt_ref, dtt_ref):
    mod = mod_ref[0]
    hn = _norm_mod(x_ref[0], g_ref[...], mod[1:2], mod[0:1]).astype(BF16)
    dot = lambda lo, hi: jnp.dot(hn, w_ref[:, lo:hi], preferred_element_type=F32)
    nq = q_ref.shape[2]
    nkv = k_ref.shape[2]
    hr = lax.broadcasted_iota(jnp.int32, (nq, nq), 0) // HEAD_DIM
    hc = lax.broadcasted_iota(jnp.int32, (nq, nq), 1) // HEAD_DIM
    ones_blk = jnp.where(hr == hc, 1.0, 0.0).astype(BF16)
    q_ref[0] = _head_rms(dot(0, nq), ones_blk, qg_ref[...])
    k_ref[0] = _head_rms(dot(nq, nq + nkv), ones_blk[0:nkv, 0:nkv], kg_ref[...])
    o = nq + nkv
    v_ref[0] = dot(o, o + nkv)
    o += nkv
    zg_ref[0] = dot(o, o + zg_ref.shape[2])
    o += zg_ref.shape[2]
    xbc_ref[0] = dot(o, o + xbc_ref.shape[2])
    o += xbc_ref.shape[2]
    dt = _softplus(dot(o, o + LANES) + dtb_ref[...])
    dt_ref[0] = dt
    dtt_ref[0] = dt.T[0:dtt_ref.shape[1], :]


def _od_in(x, mod, norm_g, w_in, q_norm, k_norm, dt_bias, dims, tb):
    b, t, d = x.shape
    nq, nkv, nz, nxbc, nh = dims
    w = jnp.concatenate([w_in, jnp.zeros((d, LANES - nh), F32)], axis=1).astype(BF16)
    qg = jnp.tile(q_norm, nq // HEAD_DIM).reshape(1, nq)
    kg = jnp.tile(k_norm, nkv // HEAD_DIM).reshape(1, nkv)
    dtb = jnp.concatenate([dt_bias, jnp.zeros((LANES - nh,), F32)]).reshape(1, LANES)
    g2 = norm_g.reshape(1, d)
    tile = lambda n: pl.BlockSpec((1, tb, n), lambda i, j: (i, j, 0))
    const = lambda a: pl.BlockSpec(a.shape, lambda i, j: (0,) * a.ndim)
    widths = (nq, nkv, nkv, nz, nxbc, LANES)
    return pl.pallas_call(
        _od_in_kernel,
        grid=(b, t // tb),
        in_specs=[tile(d), pl.BlockSpec((1, 6, d), lambda i, j: (i, 0, 0)), const(g2), const(w),
                  const(qg), const(kg), const(dtb)],
        out_specs=[tile(n) for n in widths] + [pl.BlockSpec((1, nh, tb), lambda i, j: (i, 0, j))],
        out_shape=[jax.ShapeDtypeStruct((b, t, n), F32) for n in widths]
        + [jax.ShapeDtypeStruct((b, nh, t), F32)],
        compiler_params=_params(2),
        name="od_in",
    )(x, mod, g2, w, qg, kg, dtb)


def _swa_kernel(q_ref, k_ref, v_ref, k0_ref, v0_ref, bias_ref, sink_ref, o_ref, kx, vx,
                *, n_blocks, nq, mask_start):
    t = pl.program_id(1)
    tb = q_ref.shape[1]
    n_kv = k_ref.shape[2] // HEAD_DIM
    rows_q = nq * CHUNK
    tail = kx.shape[1] - WINDOW - tb
    left = lax.broadcasted_iota(jnp.int32, (1, LANES), 1) < HEAD_DIM
    ones = jnp.ones((SWA_KEYS, LANES), BF16)
    col = lax.broadcasted_iota(jnp.int32, (1, SWA_KEYS), 1)

    def place(dst, rows, x):
        rolled = pltpu.roll(x, HEAD_DIM, axis=1)
        dst[0, rows, :] = jnp.where(left, x, 0.0).astype(BF16)
        dst[1, rows, :] = jnp.where(left, 0.0, rolled).astype(BF16)
        dst[2, rows, :] = jnp.where(left, rolled, 0.0).astype(BF16)
        dst[3, rows, :] = jnp.where(left, 0.0, x).astype(BF16)

    @pl.when(t == 0)
    def _():
        place(kx, slice(0, WINDOW), k0_ref[0])
        place(vx, slice(0, WINDOW), v0_ref[0])
        for i in range(2 * n_kv if tail else 0):
            kx[i, WINDOW + tb:, :] = jnp.zeros((tail, LANES), BF16)
            vx[i, WINDOW + tb:, :] = jnp.zeros((tail, LANES), BF16)

    place(kx, slice(WINDOW, WINDOW + tb), k_ref[0])
    place(vx, slice(WINDOW, WINDOW + tb), v_ref[0])

    def block(blk, carry):
        r0 = pl.multiple_of(blk * rows_q, rows_q)
        keys = pl.ds(r0, SWA_KEYS)
        if mask_start:
            valid = t * tb + r0 - WINDOW + col >= 0
        probs = []
        sinks = []
        for j in range(n_kv):
            lo = 2 * j * LANES
            qg = jnp.concatenate([q_ref[0, pl.ds(r0 + cq * CHUNK, CHUNK), lo + r * LANES:lo + (r + 1) * LANES]
                                  for cq in range(nq) for r in range(2)], axis=0).astype(BF16)
            kcat = jnp.concatenate([kx[2 * j, keys, :], kx[2 * j + 1, keys, :]], axis=0)
            s_both = lax.dot_general(qg, kcat, _NT, preferred_element_type=F32) * HEAD_DIM ** -0.5
            for side in range(2):
                s = s_both[:, side * SWA_KEYS:(side + 1) * SWA_KEYS] + bias_ref[j, side]
                if mask_start:
                    s = jnp.where(valid, s, NEG_INF)
                m = jnp.max(s, axis=-1, keepdims=True)
                probs.append(jnp.exp(s - m).astype(BF16))
                sinks.append(jnp.exp(sink_ref[j, side] - m))
        den = jnp.dot(jnp.concatenate(probs, axis=0), ones, preferred_element_type=F32)
        nr = 2 * rows_q
        for j in range(n_kv):
            lo = 2 * j * LANES
            vcat = jnp.concatenate([vx[2 * j, keys, :], vx[2 * j + 1, keys, :]], axis=0)
            pv = jnp.dot(jnp.concatenate(probs[2 * j:2 * j + 2], axis=1), vcat, preferred_element_type=F32)
            d_l = den[(2 * j) * nr:(2 * j + 1) * nr] + sinks[2 * j]
            d_r = den[(2 * j + 1) * nr:(2 * j + 2) * nr] + sinks[2 * j + 1]
            out = pv * jnp.where(left, 1.0 / d_l, 1.0 / d_r)
            for cq in range(nq):
                for r in range(2):
                    o_ref[0, pl.ds(r0 + cq * CHUNK, CHUNK), lo + r * LANES:lo + (r + 1) * LANES] = (
                        out[(2 * cq + r) * CHUNK:(2 * cq + r + 1) * CHUNK])
        return carry

    lax.fori_loop(0, n_blocks, block, 0, unroll=math.gcd(n_blocks, 2))
    for i in range(2 * n_kv):
        kx[i, 0:WINDOW, :] = kx[i, tb:tb + WINDOW, :]
        vx[i, 0:WINDOW, :] = vx[i, tb:tb + WINDOW, :]


def _t5_bucket(rel):
    nb = T5_BUCKETS // 2
    max_exact = nb // 2
    ret = (rel > 0).astype(jnp.int32) * nb
    n = jnp.abs(rel)
    nf = jnp.maximum(n, 1).astype(F32)
    large = max_exact + (jnp.log(nf / max_exact) / math.log(T5_MAX_DIST / max_exact)
                         * (nb - max_exact)).astype(jnp.int32)
    large = jnp.minimum(large, nb - 1)
    return ret + jnp.where(n < max_exact, n, large)


def _rel_bias(table):
    rel = (jnp.arange(WINDOW + CHUNK)[None, :] - WINDOW) - jnp.arange(CHUNK)[:, None]
    return jnp.transpose(table[_t5_bucket(rel)], (2, 0, 1))


def _swa(q, k, v, k0, v0, bias, sink, tb, mask_start):
    b, t, nq = q.shape
    nkv = k.shape[2]
    n_kv = nkv // HEAD_DIM
    assert nkv == LANES and nq == 2 * n_kv * LANES
    by_side = lambda a: jnp.transpose(a.reshape((n_kv, 2, 2) + a.shape[1:]), (0, 2, 1) + tuple(range(3, a.ndim + 2)))
    span = WINDOW + CHUNK
    n_chunks = tb // CHUNK
    cpb = 2 if n_chunks % 2 == 0 else 1
    assert WINDOW + cpb * CHUNK <= SWA_KEYS
    bias2 = by_side(bias).reshape(n_kv, 2, 2 * CHUNK, span)
    bias3 = jnp.concatenate(
        [jnp.pad(bias2, ((0, 0), (0, 0), (0, 0), (cq * CHUNK, SWA_KEYS - span - cq * CHUNK)),
                 constant_values=NEG_INF) for cq in range(cpb)], axis=2)
    sink3 = jnp.tile(jnp.repeat(by_side(sink), CHUNK, axis=2), (1, 1, cpb))
    sink3 = jnp.broadcast_to(sink3[..., None], sink3.shape + (LANES,))
    tile = lambda n: pl.BlockSpec((1, tb, n), lambda i, j: (i, j, 0))
    wspec = pl.BlockSpec((1, WINDOW, nkv), lambda i, j: (i, 0, 0))
    const = lambda a: pl.BlockSpec(a.shape, lambda i, j: (0,) * a.ndim)
    rows = WINDOW + max(tb, WINDOW) + SWA_KEYS - WINDOW - cpb * CHUNK
    return pl.pallas_call(
        functools.partial(_swa_kernel, n_blocks=n_chunks // cpb, nq=cpb, mask_start=mask_start),
        grid=(b, t // tb),
        in_specs=[tile(nq), tile(nkv), tile(nkv), wspec, wspec, const(bias3), const(sink3)],
        out_specs=tile(nq),
        out_shape=jax.ShapeDtypeStruct((b, t, nq), F32),
        scratch_shapes=[pltpu.VMEM((2 * n_kv, rows, LANES), BF16), pltpu.VMEM((2 * n_kv, rows, LANES), BF16)],
        compiler_params=_params(2),
        name="swa",
    )(q, k, v, k0, v0, bias3, sink3)


def _ssd_kernel(xbc_ref, zg_ref, dt_ref, dtt_ref, pre_ref, cw_ref, cb_ref, aexp_ref, acol_ref, dexp_ref,
                e_ref, ng_ref, s0_ref, y_ref, sfin_ref, clast_ref, st_scr, carry_scr, xc_scr, dte_scr,
                *, n_chunks):
    t = pl.program_id(1)
    tb = xbc_ref.shape[1]
    inner = zg_ref.shape[2]
    n_heads = inner // SSD_HEAD_DIM
    hpg = n_heads // SSD_GROUPS
    gw = hpg * SSD_HEAD_DIM
    gn = SSD_GROUPS * SSD_STATE

    @pl.when(t == 0)
    def _():
        st_scr[...] = s0_ref[0]
        carry_scr[...] = pre_ref[0]

    xbc = xbc_ref[0]
    xc_scr[...] = _silu(_causal_conv(xbc, carry_scr[...], cw_ref[...], cb_ref[...]))
    carry_scr[...] = xbc[tb - SUBLANES:]
    dte_scr[...] = _dot3_right(dt_ref[0], e_ref[...])

    tril = _tri(CHUNK, True)
    tril_b = jnp.where(tril, 1.0, 0.0).astype(BF16)
    triu_b = jnp.where(_tri(CHUNK, False), 1.0, 0.0).astype(BF16)
    aexp = aexp_ref[...]

    def chunk(c, carry):
        rows = pl.ds(pl.multiple_of(c * CHUNK, CHUNK), CHUNK)
        xs = xc_scr[rows, 0:inner]
        bm = xc_scr[rows, inner:inner + gn]
        cm = xc_scr[rows, inner + gn:inner + 2 * gn]
        dte = dte_scr[rows, :]
        cum = _dot3_left(tril_b, dte * aexp)
        cum_last = cum[CHUNK - 1:CHUNK, :]
        cum_t = _dot3_right(dtt_ref[0, c] * acol_ref[...], triu_b)
        xdt = (xs * dte).astype(BF16)
        st = st_scr[...]
        ys = []
        inters = []
        upds = []
        xw = xs * (jnp.exp(cum_last - cum) * dte)
        for g in range(SSD_GROUPS):
            ncols = slice(g * SSD_STATE, (g + 1) * SSD_STATE)
            gcols = slice(g * gw, (g + 1) * gw)
            cmg = cm[:, ncols].astype(BF16)
            bmg = bm[:, ncols]
            inters.append(jnp.dot(cmg, st[:, gcols].astype(BF16), preferred_element_type=F32))
            cb = lax.dot_general(cmg, bmg.astype(BF16), _NT, preferred_element_type=F32)
            for hh in range(hpg):
                h = g * hpg + hh
                hcols = slice(h * SSD_HEAD_DIM, (h + 1) * SSD_HEAD_DIM)
                seg = cum[:, hcols] - cum_t[h:h + 1, :]
                w = cb * jnp.exp(jnp.where(tril, seg, NEG_INF))
                ys.append(jnp.dot(w.astype(BF16), xdt[:, hcols], preferred_element_type=F32))
            upds.append(_bdot(bmg.T, xw[:, gcols]))
        y = jnp.concatenate(ys, axis=1) + jnp.exp(cum) * jnp.concatenate(inters, axis=1)
        st_scr[...] = st * jnp.exp(cum_last) + jnp.concatenate(upds, axis=1)
        yd = (y + dexp_ref[...] * xs) * _silu(zg_ref[0, rows, :])
        ms = jnp.mean(yd * yd, axis=-1, keepdims=True)
        y_ref[0, rows, :] = yd * lax.rsqrt(ms + RMS_EPS) * ng_ref[...]
        return carry

    lax.fori_loop(0, n_chunks, chunk, 0, unroll=math.gcd(n_chunks, CHUNK_UNROLL))

    @pl.when(t == pl.num_programs(1) - 1)
    def _():
        sfin_ref[0] = st_scr[...]
        clast_ref[0] = carry_scr[...]


def _ssd(xbc, zg, dt, dtt, conv_prefix, conv_w, conv_b, a_log, d_skip, norm_g, s0, tb):
    b, t, nxbc = xbc.shape
    inner = zg.shape[2]
    nh = dtt.shape[1]
    width = conv_w.shape[0]
    nc = t // CHUNK
    dtt4 = jnp.transpose(dtt.reshape(b, nh, nc, CHUNK), (0, 2, 1, 3))
    pre8 = jnp.concatenate([jnp.zeros((b, SUBLANES - (width - 1), nxbc), F32), conv_prefix], axis=1)
    a = -jnp.exp(a_log)
    aexp = jnp.repeat(a, SSD_HEAD_DIM).reshape(1, inner)
    acol = a.reshape(nh, 1)
    dexp = jnp.repeat(d_skip, SSD_HEAD_DIM).reshape(1, inner)
    expand = (np.arange(LANES)[:, None] == np.arange(inner)[None, :] // SSD_HEAD_DIM)
    expand = jnp.asarray(expand, BF16)
    s0t = jnp.transpose(s0, (0, 3, 1, 2)).reshape(b, SSD_STATE, inner)
    ng = norm_g.reshape(1, inner)
    cb2 = conv_b.reshape(1, nxbc)
    tile = lambda n: pl.BlockSpec((1, tb, n), lambda i, j: (i, j, 0))
    const = lambda arr: pl.BlockSpec(arr.shape, lambda i, j: (0,) * arr.ndim)
    bspec = lambda arr: pl.BlockSpec((1,) + arr.shape[1:], lambda i, j: (i,) + (0,) * (arr.ndim - 1))
    y, sfin, clast = pl.pallas_call(
        functools.partial(_ssd_kernel, n_chunks=tb // CHUNK),
        grid=(b, t // tb),
        in_specs=[tile(nxbc), tile(inner), tile(LANES),
                  pl.BlockSpec((1, tb // CHUNK, nh, CHUNK), lambda i, j: (i, j, 0, 0)),
                  bspec(pre8), const(conv_w), const(cb2), const(aexp), const(acol), const(dexp),
                  const(expand), const(ng), bspec(s0t)],
        out_specs=[tile(inner), bspec(s0t), bspec(pre8)],
        out_shape=[jax.ShapeDtypeStruct((b, t, inner), F32), jax.ShapeDtypeStruct(s0t.shape, F32),
                   jax.ShapeDtypeStruct(pre8.shape, F32)],
        scratch_shapes=[pltpu.VMEM((SSD_STATE, inner), F32), pltpu.VMEM((SUBLANES, nxbc), F32),
                        pltpu.VMEM((tb, nxbc), F32), pltpu.VMEM((tb, inner), F32)],
        compiler_params=_params(2),
        name="ssd",
    )(xbc, zg, dt, dtt4, pre8, conv_w, cb2, aexp, acol, dexp, expand, ng, s0t)
    sfin = jnp.transpose(sfin.reshape(b, SSD_STATE, nh, SSD_HEAD_DIM), (0, 2, 3, 1))
    return y, sfin, clast[:, SUBLANES - (width - 1):]


def _trunk(x, mods, P, st, sample):
    b, t, d = x.shape
    tb = min(MAX_TILE, t)
    depth = P['w_mod'].shape[0]
    new = {name: [] for name in ('s5_re', 's5_im', 'gla', 'swa_k', 'swa_v', 'ssd', 'ssd_conv', 'ffn_conv')}
    for layer in range(depth):
        i = layer // 2
        mod = mods[layer].reshape(b, 6, d)
        ffn = (P['norm2_g'][layer], P['ffn_w_up'][layer], P['ffn_conv_w'][layer], P['ffn_conv_b'][layer],
               P['ffn_w_down'][layer], st['ffn_conv'][layer], tb)
        if layer % 2 == 0:
            u, qk, v, r, gate = _ev_in(x, mod, P['norm1_g'][layer], P['ev_w_in'][i], P['gla_w_gate2'][i],
                                       P['gla_b_gate'][i], tb)
            mats = _s5_matrices(P['s5_a_re'][i], P['s5_a_im'][i], P['s5_log_dt'][i], P['s5_b_re'][i],
                                P['s5_b_im'][i], P['s5_c_re'][i], P['s5_c_im'][i])
            ya, sr, si = _s5(u, mats, P['s5_d'][i], st['s5_re'][i], st['s5_im'][i])
            ob, sg = _gla(qk, v, gate, r, st['gla'][i], P['gla_norm_g'][i], tb)
            new['s5_re'].append(sr)
            new['s5_im'].append(si)
            new['gla'].append(sg)
            x, fp = _out_ffn(x, mod, ya, ob, P['ev_w_out'][i], *ffn,
                             s5_extra=(P['s5_w_glu'][i], P['s5_b_glu'][i]))
        else:
            nq = P['swa_sink'].shape[1] * HEAD_DIM
            nkv = SWA_KV_HEADS * HEAD_DIM
            inner = P['ssd_norm_g'].shape[1]
            nxbc = P['ssd_conv_w'].shape[2]
            nh = P['ssd_a_log'].shape[1]
            q, k, v, zg, xbc, dt, dtt = _od_in(x, mod, P['norm1_g'][layer], P['od_w_in'][i], P['swa_q_norm'][i],
                                               P['swa_k_norm'][i], P['ssd_dt_bias'][i],
                                               (nq, nkv, inner, nxbc, nh), tb)
            bias = _rel_bias(P['t5_bias'])
            if sample:
                k0 = st['swa_k'][i].reshape(b, WINDOW, nkv)
                v0 = st['swa_v'][i].reshape(b, WINDOW, nkv)
            else:
                k0 = v0 = jnp.zeros((b, WINDOW, nkv), F32)
            oc = _swa(q, k, v, k0, v0, bias, P['swa_sink'][i], tb, mask_start=not sample)
            yd, ss, sc = _ssd(xbc, zg, dt, dtt, st['ssd_conv'][i], P['ssd_conv_w'][i], P['ssd_conv_b'][i],
                              P['ssd_a_log'][i], P['ssd_d'][i], P['ssd_norm_g'][i], st['ssd'][i], tb)
            keep = slice(None) if sample else slice(t - WINDOW, t)
            new['swa_k'].append(k[:, keep].reshape(b, -1, SWA_KV_HEADS, HEAD_DIM))
            new['swa_v'].append(v[:, keep].reshape(b, -1, SWA_KV_HEADS, HEAD_DIM))
            new['ssd'].append(ss)
            new['ssd_conv'].append(sc)
            x, fp = _out_ffn(x, mod, oc, yd, P['od_w_out'][i], *ffn)
        new['ffn_conv'].append(fp)
    return x, {name: jnp.stack(vals) for name, vals in new.items()}


def kernel(x_prompt, x_sample, state_s5_re, state_s5_im, state_gla, cache_swa_k, cache_swa_v, state_ssd, state_ssd_conv, state_ffn_conv, c_prompt, c_sample, t5_bias, norm1_g, norm2_g, w_mod, b_mod, ffn_w_up, ffn_conv_w, ffn_conv_b, ffn_w_down, ev_w_in, ev_w_out, s5_a_re, s5_a_im, s5_log_dt, s5_b_re, s5_b_im, s5_c_re, s5_c_im, s5_d, s5_w_glu, s5_b_glu, gla_w_gate2, gla_b_gate, gla_norm_g, od_w_in, od_w_out, swa_q_norm, swa_k_norm, swa_sink, ssd_conv_w, ssd_conv_b, ssd_dt_bias, ssd_a_log, ssd_d, ssd_norm_g):
    P = dict(t5_bias=t5_bias, norm1_g=norm1_g, norm2_g=norm2_g, w_mod=w_mod, b_mod=b_mod,
             ffn_w_up=ffn_w_up, ffn_conv_w=ffn_conv_w, ffn_conv_b=ffn_conv_b, ffn_w_down=ffn_w_down,
             ev_w_in=ev_w_in, ev_w_out=ev_w_out, s5_a_re=s5_a_re, s5_a_im=s5_a_im, s5_log_dt=s5_log_dt,
             s5_b_re=s5_b_re, s5_b_im=s5_b_im, s5_c_re=s5_c_re, s5_c_im=s5_c_im, s5_d=s5_d,
             s5_w_glu=s5_w_glu, s5_b_glu=s5_b_glu, gla_w_gate2=gla_w_gate2, gla_b_gate=gla_b_gate,
             gla_norm_g=gla_norm_g, od_w_in=od_w_in, od_w_out=od_w_out, swa_q_norm=swa_q_norm,
             swa_k_norm=swa_k_norm, swa_sink=swa_sink, ssd_conv_w=ssd_conv_w, ssd_conv_b=ssd_conv_b,
             ssd_dt_bias=ssd_dt_bias, ssd_a_log=ssd_a_log, ssd_d=ssd_d, ssd_norm_g=ssd_norm_g)
    bp = x_prompt.shape[0]
    n_even, n_odd = state_s5_re.shape[0], state_ssd.shape[0]
    depth = w_mod.shape[0]
    zeros_like_b = lambda a: jnp.zeros((a.shape[0], bp) + a.shape[2:], F32)
    zero_st = dict(s5_re=zeros_like_b(state_s5_re), s5_im=zeros_like_b(state_s5_im), gla=zeros_like_b(state_gla),
                   ssd=zeros_like_b(state_ssd), ssd_conv=zeros_like_b(state_ssd_conv),
                   ffn_conv=zeros_like_b(state_ffn_conv))
    sample_st = dict(s5_re=state_s5_re, s5_im=state_s5_im, gla=state_gla, swa_k=cache_swa_k,
                     swa_v=cache_swa_v, ssd=state_ssd, ssd_conv=state_ssd_conv, ffn_conv=state_ffn_conv)
    mods = _modulation(jnp.concatenate([c_prompt, c_sample], axis=0), w_mod, b_mod)
    y_prompt, stp = _trunk(x_prompt, mods[:, :bp], P, zero_st, False)
    y_sample, sts = _trunk(x_sample, mods[:, bp:], P, sample_st, True)
    names = ('s5_re', 's5_im', 'gla', 'swa_k', 'swa_v', 'ssd', 'ssd_conv', 'ffn_conv')
    return (y_prompt, y_sample) + tuple(stp[n] for n in names) + tuple(sts[n] for n in names)
```

```python
import functools
import math

import jax
import jax.numpy as jnp
import numpy as np
from jax import lax
from jax.experimental import pallas as pl
from jax.experimental.pallas import tpu as pltpu

F32 = jnp.float32
BF16 = jnp.bfloat16

CHUNK = 64
WINDOW = 128
S5_GROUP = 16
S5_STATE = 64
S5_SUB = 8
S5_MAX_SUBS = 64
GLA_HEADS = 4
GLA_GATE_NORM = 16.0
HEAD_DIM = 64
SWA_KV_HEADS = 2
SWA_KEYS = 256
SSD_HEAD_DIM = 64
SSD_STATE = 128
SSD_GROUPS = 2
T5_BUCKETS = 32
T5_MAX_DIST = 128
RMS_EPS = 1e-6
NEG_INF = -1e30
LANES = 128
SUBLANES = 8
assert S5_SUB == SUBLANES and S5_GROUP * SUBLANES == LANES
MAX_TILE = 512
CHUNK_UNROLL = 4
FFN_COL_BLOCK = 256
VMEM_LIMIT = 56 * 1024 * 1024

_NT = (((1,), (1,)), ((), ()))
_HI = lax.Precision.HIGHEST


def _params(n_axes=2):
    sem = ("parallel",) + ("arbitrary",) * (n_axes - 1)
    return pltpu.CompilerParams(dimension_semantics=sem, vmem_limit_bytes=VMEM_LIMIT)


def _bdot(a, b):
    return jnp.dot(a.astype(BF16), b.astype(BF16), preferred_element_type=F32)


def _dot3_right(x, c):
    hi = x.astype(BF16)
    r1 = x - hi.astype(F32)
    mid = r1.astype(BF16)
    lo = (r1 - mid.astype(F32)).astype(BF16)
    n = x.shape[0]
    d = jnp.dot(jnp.concatenate([hi, mid, lo], axis=0), c, preferred_element_type=F32)
    return d[0:n] + d[n:2 * n] + d[2 * n:3 * n]


def _silu(x):
    return x * jax.nn.sigmoid(x)


def _softplus(x):
    return jnp.maximum(x, 0.0) + jnp.log1p(jnp.exp(-jnp.abs(x)))


def _log_sigmoid(x):
    return jnp.minimum(x, 0.0) - jnp.log1p(jnp.exp(-jnp.abs(x)))


def _norm_mod(x, g, scale, shift):
    ms = jnp.mean(x * x, axis=-1, keepdims=True)
    return (x * lax.rsqrt(ms + RMS_EPS) * g) * (1.0 + scale) + shift


def _cumsum_rows(x):
    n = x.shape[0]
    row = lax.broadcasted_iota(jnp.int32, (n, 1), 0)
    d = 1
    while d < n:
        x = x + jnp.where(row >= d, pltpu.roll(x, d, axis=0), 0.0)
        d *= 2
    return x


def _causal_conv(u, prev8, w, b):
    width = w.shape[0]
    n = u.shape[0]
    ext = jnp.concatenate([prev8, u[0:SUBLANES]], axis=0)
    full = b
    head = b
    for j in range(width):
        sh = width - 1 - j
        if sh == 0:
            full = full + u * w[j:j + 1]
            head = head + u[0:SUBLANES] * w[j:j + 1]
        else:
            full = full + pltpu.roll(u, sh, axis=0) * w[j:j + 1]
            head = head + pltpu.roll(ext, sh, axis=0)[SUBLANES:2 * SUBLANES] * w[j:j + 1]
    if n == SUBLANES:
        return head
    return jnp.concatenate([head, full[SUBLANES:]], axis=0)


def _mod_kernel(c_ref, w_ref, b_ref, o_ref):
    o_ref[0] = _bdot(_silu(c_ref[...]), w_ref[0]) + b_ref[0]


def _modulation(c, w_mod, b_mod):
    depth, d, n = w_mod.shape
    bc = c.shape[0]
    tn = n // 4
    return pl.pallas_call(
        _mod_kernel,
        grid=(depth, n // tn),
        in_specs=[pl.BlockSpec((bc, d), lambda l, j: (0, 0)),
                  pl.BlockSpec((1, d, tn), lambda l, j: (l, 0, j)),
                  pl.BlockSpec((1, 1, tn), lambda l, j: (l, 0, j))],
        out_specs=pl.BlockSpec((1, bc, tn), lambda l, j: (l, 0, j)),
        out_shape=jax.ShapeDtypeStruct((depth, bc, n), F32),
        compiler_params=_params(2),
        name="modulation",
    )(c, w_mod, b_mod.reshape(depth, 1, n))


def _ev_in_kernel(x_ref, mod_ref, g_ref, w_ref, wg2_ref, bg_ref,
                  u_ref, qk_ref, v_ref, r_ref, gate_ref):
    mod = mod_ref[0]
    hn = _norm_mod(x_ref[0], g_ref[...], mod[1:2], mod[0:1]).astype(BF16)
    dot = lambda lo, hi: jnp.dot(hn, w_ref[:, lo:hi], preferred_element_type=F32)
    u = dot(0, 512)
    for c in range(u_ref.shape[0]):
        u_ref[c] = _chunk_transpose(u[:, c * LANES:(c + 1) * LANES].reshape(-1, SUBLANES, LANES))
    qk_ref[0] = dot(512, 1024)
    v_ref[0] = dot(1024, 1536)
    r_ref[0] = dot(1536, 2048)
    gl = dot(2048, 2048 + LANES)
    gate_ref[0] = _log_sigmoid(_bdot(gl, wg2_ref[...]) + bg_ref[...]) * (1.0 / GLA_GATE_NORM)


def _ev_in(x, mod, norm_g, w_in, w_gate2, b_gate, tb):
    b, t, d = x.shape
    rank = w_gate2.shape[0]
    nk = w_gate2.shape[1]
    wu, wq, wk, wv, wgl, wr = jnp.split(w_in, [512, 768, 1024, 1536, 1536 + rank], axis=1)
    w = jnp.concatenate([wu, wq, wk, wv, wr, wgl, jnp.zeros((d, LANES - rank), F32)], axis=1).astype(BF16)
    wg2 = jnp.concatenate([w_gate2, jnp.zeros((LANES - rank, nk), F32)], axis=0).astype(BF16)
    tile = lambda n: pl.BlockSpec((1, tb, n), lambda i, j: (i, j, 0))
    const = lambda a: pl.BlockSpec(a.shape, lambda i, j: (0,) * a.ndim)
    g2 = norm_g.reshape(1, d)
    bg = b_gate.reshape(1, nk)
    return pl.pallas_call(
        _ev_in_kernel,
        grid=(b, t // tb),
        in_specs=[tile(d), pl.BlockSpec((1, 6, d), lambda i, j: (i, 0, 0)), const(g2), const(w),
                  const(wg2), const(bg)],
        out_specs=[pl.BlockSpec((512 // LANES, tb // S5_SUB, None, SUBLANES, LANES), lambda i, j: (0, j, i, 0, 0)),
                   tile(512), tile(512), tile(512), tile(nk)],
        out_shape=[jax.ShapeDtypeStruct((512 // LANES, t // S5_SUB, b, SUBLANES, LANES), F32)]
        + [jax.ShapeDtypeStruct((b, t, n), F32) for n in (512, 512, 512, nk)],
        compiler_params=_params(2),
        name="ev_in",
    )(x, mod, g2, w, wg2, bg)


def _gla_kernel(qk_ref, v_ref, gate_ref, r_ref, s0_ref, ng_ref, o_ref, sfin_ref, st_scr, *, n_chunks):
    t = pl.program_id(1)
    dk = qk_ref.shape[2] // 2 // GLA_HEADS
    dv = v_ref.shape[2] // GLA_HEADS
    nk = GLA_HEADS * dk
    nv = GLA_HEADS * dv
    blk = (lax.broadcasted_iota(jnp.int32, (nv, nk), 0) // dv
           == lax.broadcasted_iota(jnp.int32, (nv, nk), 1) // dk)

    @pl.when(t == 0)
    def _():
        s0 = jnp.concatenate([s0_ref[0, h] for h in range(GLA_HEADS)], axis=0)
        st_scr[...] = jnp.where(blk, jnp.concatenate([s0] * GLA_HEADS, axis=1), 0.0)

    n_hs = GLA_HEADS * CHUNK
    row_head = lax.broadcasted_iota(jnp.int32, (n_hs, 1), 0) // CHUNK
    own_k = row_head == lax.broadcasted_iota(jnp.int32, (1, nk), 1) // dk
    own_v = row_head == lax.broadcasted_iota(jnp.int32, (1, nv), 1) // dv
    causal = (lax.broadcasted_iota(jnp.int32, (CHUNK, 1), 0)
              >= lax.broadcasted_iota(jnp.int32, (1, n_hs), 1) % CHUNK)
    ng = ng_ref[...]

    def chunk(c, carry):
        rows = pl.ds(pl.multiple_of(c * CHUNK, CHUNK), CHUNK)
        k = qk_ref[0, rows, nk:2 * nk]
        v = v_ref[0, rows, :]
        cum = _cumsum_rows(gate_ref[0, rows, :])
        cum_last = cum[CHUNK - 1:CHUNK, :]
        qe = (qk_ref[0, rows, 0:nk] * dk ** -0.5 * jnp.exp(cum)).astype(BF16)
        ke = (k * jnp.exp(-cum)).astype(BF16)
        kd = (k * jnp.exp(cum_last - cum)).astype(BF16)
        v_b = v.astype(BF16)
        st = st_scr[...]
        ke_bd = jnp.where(own_k, jnp.concatenate([ke] * GLA_HEADS, axis=0), jnp.zeros((), BF16))
        v_bd = jnp.where(own_v, jnp.concatenate([v_b] * GLA_HEADS, axis=0), jnp.zeros((), BF16))
        att = lax.dot_general(qe, ke_bd, _NT, preferred_element_type=F32)
        att = jnp.where(causal, att, 0.0).astype(BF16)
        o = (jnp.dot(att, v_bd, preferred_element_type=F32)
             + lax.dot_general(qe, st.astype(BF16), _NT, preferred_element_type=F32))
        for h in range(GLA_HEADS):
            cols = slice(h * dv, (h + 1) * dv)
            oh = o[:, cols]
            ms = jnp.mean(oh * oh, axis=-1, keepdims=True)
            oh = oh * lax.rsqrt(ms + RMS_EPS) * ng
            o_ref[0, rows, cols] = oh * _silu(r_ref[0, rows, cols])
        upd = jnp.dot(v.T.astype(BF16), kd, preferred_element_type=F32)
        st_scr[...] = jnp.where(blk, st * jnp.exp(cum_last) + upd, 0.0)
        return carry

    lax.fori_loop(0, n_chunks, chunk, 0, unroll=math.gcd(n_chunks, CHUNK_UNROLL))

    @pl.when(t == pl.num_programs(1) - 1)
    def _():
        for h in range(GLA_HEADS):
            sfin_ref[0, h] = st_scr[h * dv:(h + 1) * dv, h * dk:(h + 1) * dk]


def _gla(qk, v, gate, r, s0, norm_g, tb):
    b, t, nv = v.shape
    nk = gate.shape[2]
    dk, dv = nk // GLA_HEADS, nv // GLA_HEADS
    s0t = jnp.swapaxes(s0, 2, 3)
    tile = lambda n: pl.BlockSpec((1, tb, n), lambda i, j: (i, j, 0))
    sspec = pl.BlockSpec((1, GLA_HEADS, dv, dk), lambda i, j: (i, 0, 0, 0))
    ng = norm_g.reshape(1, dv)
    o, sfin = pl.pallas_call(
        functools.partial(_gla_kernel, n_chunks=tb // CHUNK),
        grid=(b, t // tb),
        in_specs=[tile(2 * nk), tile(nv), tile(nk), tile(nv), sspec,
                  pl.BlockSpec((1, dv), lambda i, j: (0, 0))],
        out_specs=[tile(nv), sspec],
        out_shape=[jax.ShapeDtypeStruct((b, t, nv), F32),
                   jax.ShapeDtypeStruct((b, GLA_HEADS, dv, dk), F32)],
        scratch_shapes=[pltpu.VMEM((nv, nk), F32)],
        compiler_params=_params(2),
        name="gla",
    )(qk, v, gate, r, s0t, ng)
    return o, jnp.swapaxes(sfin, 2, 3)


def _chunk_transpose(x):
    s = lax.broadcasted_iota(jnp.int32, (1, SUBLANES, LANES), 1)
    c = lax.broadcasted_iota(jnp.int32, (1, SUBLANES, LANES), 2) // S5_GROUP
    for d in (4, 2, 1):
        sb = (s & d) != 0
        cb = (c & d) != 0
        if 2 * d == SUBLANES:
            t = pltpu.roll(pltpu.roll(x, d, axis=1), S5_GROUP * d, axis=2)
        else:
            xs = jnp.where(sb, pltpu.roll(x, d, axis=1), pltpu.roll(x, SUBLANES - d, axis=1))
            t = jnp.where(cb, pltpu.roll(xs, S5_GROUP * d, axis=2), pltpu.roll(xs, LANES - S5_GROUP * d, axis=2))
        x = jnp.where(sb != cb, t, x)
    return x


def _s5_kernel(u_ref, w1_ref, n_ref, a_ref, d_ref, x0_ref, y_ref, xf_ref, mm_scr, xs_scr, st_scr, *, tk, bsz):
    j = pl.program_id(1)
    n_g = st_scr.shape[0]
    rows = tk * bsz
    of_group = lambda g: pl.ds(g, rows, stride=n_g)

    @pl.when(j == 0)
    def _():
        st_scr[...] = x0_ref[0]

    for g in range(n_g):
        ub = u_ref[of_group(g), :].astype(BF16)
        mm_scr[g] = jnp.dot(ub, w1_ref[0, g], preferred_element_type=F32)

    half = n_g // 2
    for g0 in (0, half):
        coef = [a_ref[0, g] for g in range(g0, g0 + half)]

        def step(k, carry, g0=g0, coef=coef):
            r = pl.ds(pl.multiple_of(k * bsz, bsz), bsz)
            out = []
            for i in range(half):
                x, xsw = carry[2 * i], carry[2 * i + 1]
                a1, a2, a2s = coef[i][0:1], coef[i][1:2], coef[i][2:3]
                xs_scr[g0 + i, r, :] = x
                out.append(a1 * x + a2 * xsw + mm_scr[g0 + i, r, LANES:2 * LANES])
                out.append(a1 * xsw + a2s * x + mm_scr[g0 + i, r, 2 * LANES:3 * LANES])
            return tuple(out)

        fin = lax.fori_loop(0, tk, step, tuple(st_scr[g0 + i, v] for i in range(half) for v in range(2)))
        for i in range(half):
            st_scr[g0 + i, 0] = fin[2 * i]
            st_scr[g0 + i, 1] = fin[2 * i + 1]

    for g in range(n_g):
        y = (mm_scr[g, :, 0:LANES] + jnp.dot(xs_scr[g].astype(BF16), n_ref[0, g], preferred_element_type=F32)
             + d_ref[0, g] * u_ref[of_group(g), :])
        y_ref[of_group(g), :] = jax.nn.gelu(y)

    @pl.when(j == pl.num_programs(1) - 1)
    def _():
        xf_ref[0] = st_scr[:, 0]


def _cmul(ar, ai, br, bi):
    return ar * br - ai * bi, ar * bi + ai * br


def _s5_matrices(a_re, a_im, log_dt, b_re, b_im, c_re, c_im):
    g, p = a_re.shape
    hdim = b_re.shape[-1]
    dt = jnp.exp(log_dt)[:, None]
    mag = jnp.exp(a_re * dt)
    ab_re, ab_im = mag * jnp.cos(a_im * dt), mag * jnp.sin(a_im * dt)
    den = a_re * a_re + a_im * a_im
    num_re, num_im = ab_re - 1.0, ab_im
    g_re = (num_re * a_re + num_im * a_im) / den
    g_im = (num_im * a_re - num_re * a_im) / den
    bb_re = g_re[..., None] * b_re - g_im[..., None] * b_im
    bb_im = g_re[..., None] * b_im + g_im[..., None] * b_re
    pw_re, pw_im = jnp.ones((1, g, p), F32), jnp.zeros((1, g, p), F32)
    sq_re, sq_im = ab_re, ab_im
    while pw_re.shape[0] < S5_SUB + 1:
        nr, ni = _cmul(pw_re, pw_im, sq_re, sq_im)
        pw_re, pw_im = jnp.concatenate([pw_re, nr]), jnp.concatenate([pw_im, ni])
        sq_re, sq_im = _cmul(sq_re, sq_im, sq_re, sq_im)
    pw_re, pw_im = pw_re[:S5_SUB + 1], pw_im[:S5_SUB + 1]
    ca_re, ca_im = _cmul(c_re[None], c_im[None], pw_re[:, :, None, :], pw_im[:, :, None, :])
    kern = (jnp.einsum('tghp,gpk->tghk', ca_re[:S5_SUB], bb_re, precision=_HI)
            - jnp.einsum('tghp,gpk->tghk', ca_im[:S5_SUB], bb_im, precision=_HI))
    kern = jnp.concatenate([kern, jnp.zeros((1,) + kern.shape[1:], F32)])
    s_in = np.arange(S5_SUB)[:, None]
    s_out = np.arange(S5_SUB)[None, :]
    lag = np.where(s_out >= s_in, s_out - s_in, S5_SUB)
    m = kern[lag]
    m = jnp.transpose(m, (2, 0, 4, 1, 3)).reshape(g, S5_SUB * hdim, S5_SUB * hdim)
    pr, pi = _cmul(pw_re[S5_SUB - 1::-1][..., None], pw_im[S5_SUB - 1::-1][..., None], bb_re[None], bb_im[None])
    flat = lambda a: jnp.transpose(a, (1, 0, 3, 2)).reshape(g, S5_SUB * hdim, p)
    nm = jnp.concatenate([ca_re[1:], -ca_im[1:]], axis=3)
    nm = jnp.transpose(nm, (1, 3, 0, 2)).reshape(g, 2 * p, S5_SUB * hdim)
    w1 = jnp.concatenate([m, flat(pr), flat(pi), flat(pi), flat(pr)], axis=2)
    a_r, a_i = pw_re[S5_SUB], pw_im[S5_SUB]
    coef = jnp.stack([jnp.concatenate([a_r, a_r], axis=1), jnp.concatenate([-a_i, a_i], axis=1),
                      jnp.concatenate([a_i, -a_i], axis=1)], axis=1)
    return w1.astype(BF16), nm.astype(BF16), coef


def _s5(u2, mats, d_skip, x0_re, x0_im):
    w1, nm, coef = mats
    n_ct, n_sub, b, n_g, lanes = u2.shape
    g, p2 = nm.shape[0], nm.shape[1]
    hdim = lanes // S5_SUB
    tk = min(n_sub, S5_MAX_SUBS)
    x0 = jnp.transpose(jnp.concatenate([x0_re, x0_im], axis=2), (1, 0, 2))
    x0 = jnp.stack([x0, jnp.roll(x0, p2 // 2, axis=2)], axis=1)
    dt = jnp.tile(d_skip.reshape(g, 1, hdim), (1, 1, S5_SUB))
    by_tile = lambda a: a.reshape((n_ct, n_g) + a.shape[1:])
    pspec = lambda a: pl.BlockSpec((1,) + a.shape[1:], lambda c, j: (c,) + (0,) * (a.ndim - 1))
    rows = tk * b
    uspec = pl.BlockSpec((None, rows * n_g, lanes), lambda c, j: (c, j, 0))
    params = [by_tile(a) for a in (w1, nm, coef, dt, x0)]
    y, xf = pl.pallas_call(
        functools.partial(_s5_kernel, tk=tk, bsz=b),
        grid=(n_ct, n_sub // tk),
        in_specs=[uspec] + [pspec(a) for a in params],
        out_specs=[uspec, pl.BlockSpec((1, n_g, b, p2), lambda c, j: (c, 0, 0, 0))],
        out_shape=[jax.ShapeDtypeStruct((n_ct, n_sub * b * n_g, lanes), F32),
                   jax.ShapeDtypeStruct((n_ct, n_g, b, p2), F32)],
        scratch_shapes=[pltpu.VMEM((n_g, rows, 3 * lanes), F32), pltpu.VMEM((n_g, rows, lanes), F32),
                        pltpu.VMEM((n_g, 2, b, p2), F32)],
        compiler_params=_params(2),
        name="s5",
    )(u2.reshape(n_ct, n_sub * b * n_g, lanes), *params)
    xf = jnp.transpose(xf.reshape(g, b, p2), (1, 0, 2))
    return y.reshape(u2.shape), xf[:, :, :p2 // 2], xf[:, :, p2 // 2:]


def _out_ffn_kernel(*refs, with_s5, n_cb):
    if with_s5:
        (x_ref, mod_ref, m1_ref, m2_ref, wglu_ref, bglu_ref, wout_ref, n2g_ref,
         wa_ref, wg_ref, cwa_ref, cwg_ref, cba_ref, cbg_ref, wd_ref, pre_ref,
         xo_ref, ulast_ref, carry_scr, h_scr) = refs
    else:
        (x_ref, mod_ref, m1_ref, m2_ref, wout_ref, n2g_ref,
         wa_ref, wg_ref, cwa_ref, cwg_ref, cba_ref, cbg_ref, wd_ref, pre_ref,
         xo_ref, ulast_ref, carry_scr, h_scr) = refs
    t = pl.program_id(1)

    @pl.when(t == 0)
    def _():
        carry_scr[...] = pre_ref[0]

    mod = mod_ref[0]
    if with_s5:
        ya = jnp.concatenate([_chunk_transpose(m1_ref[c]).reshape(-1, LANES) for c in range(m1_ref.shape[0])],
                             axis=1)
        m1 = ya * jax.nn.sigmoid(_bdot(ya, wglu_ref[...]) + bglu_ref[...])
    else:
        m1 = m1_ref[0]
    half = m1.shape[1]
    out = (jnp.dot(m1.astype(BF16), wout_ref[0:half, :], preferred_element_type=F32)
           + jnp.dot(m2_ref[0].astype(BF16), wout_ref[half:, :], preferred_element_type=F32))
    x1 = x_ref[0] + mod[2:3] * out
    hn = _norm_mod(x1, n2g_ref[...], mod[4:5], mod[3:4]).astype(BF16)
    tb = x1.shape[0]
    cb = wa_ref.shape[2]

    def up(j):
        return (jnp.dot(hn, wa_ref[j], preferred_element_type=F32),
                jnp.dot(hn, wg_ref[j], preferred_element_type=F32))

    nxt = up(0)
    for j in range(n_cb):
        ua, ug = nxt
        if j + 1 < n_cb:
            nxt = up(j + 1)
        a = _causal_conv(ua, carry_scr[0, j], cwa_ref[j], cba_ref[j])
        gt = _causal_conv(ug, carry_scr[1, j], cwg_ref[j], cbg_ref[j])
        carry_scr[0, j] = ua[tb - SUBLANES:]
        carry_scr[1, j] = ug[tb - SUBLANES:]
        h_scr[:, j * cb:(j + 1) * cb] = (_silu(a) * gt).astype(BF16)
    f = jnp.dot(h_scr[...], wd_ref[...], preferred_element_type=F32)
    xo_ref[0] = x1 + mod[5:6] * f

    @pl.when(t == pl.num_programs(1) - 1)
    def _():
        ulast_ref[0] = carry_scr[...]


def _out_ffn(x, mod, m1, m2, w_out, norm2_g, w_up, conv_w, conv_b, w_down, prefix, tb, s5_extra=None):
    b, t, d = x.shape
    dff = w_down.shape[0]
    cb = FFN_COL_BLOCK
    n_cb = dff // cb
    width = conv_w.shape[0]
    half = m2.shape[2]
    blocks = lambda a: a.reshape(a.shape[0], 2, n_cb, cb)
    w_up_b = jnp.transpose(blocks(w_up.astype(BF16)), (1, 2, 0, 3))
    cw = jnp.transpose(blocks(conv_w), (1, 2, 0, 3))
    cbias = jnp.transpose(blocks(conv_b.reshape(1, 2 * dff)), (1, 2, 0, 3))
    wd = w_down.astype(BF16)
    pre8 = jnp.concatenate([jnp.zeros((b, SUBLANES - (width - 1), 2 * dff), F32), prefix], axis=1)
    pre8 = jnp.transpose(pre8.reshape(b, SUBLANES, 2, n_cb, cb), (0, 2, 3, 1, 4))
    tile = lambda n: pl.BlockSpec((1, tb, n), lambda i, j: (i, j, 0))
    const = lambda a: pl.BlockSpec(a.shape, lambda i, j: (0,) * a.ndim, pipeline_mode=pl.Buffered(1))
    bspec = lambda a: pl.BlockSpec((1,) + a.shape[1:], lambda i, j: (i,) + (0,) * (a.ndim - 1))
    wo = w_out.astype(BF16)
    n2g = norm2_g.reshape(1, d)
    args = [x, mod]
    specs = [tile(d), bspec(mod)]
    if s5_extra is not None:
        w_glu, b_glu = s5_extra
        w_glu, b_glu = w_glu.astype(BF16), b_glu.reshape(1, half)
        args += [m1, m2, w_glu, b_glu]
        specs += [pl.BlockSpec((m1.shape[0], tb // S5_SUB, None) + m1.shape[3:], lambda i, j: (0, j, i, 0, 0)),
                  tile(m2.shape[2]), const(w_glu), const(b_glu)]
    else:
        args += [m1, m2]
        specs += [tile(half), tile(m2.shape[2])]
    weights = [wo, n2g, w_up_b[0], w_up_b[1], cw[0], cw[1], cbias[0], cbias[1], wd]
    args += weights + [pre8]
    specs += [const(a) for a in weights] + [bspec(pre8)]
    xo, ulast = pl.pallas_call(
        functools.partial(_out_ffn_kernel, with_s5=s5_extra is not None, n_cb=n_cb),
        grid=(b, t // tb),
        in_specs=specs,
        out_specs=[tile(d), bspec(pre8)],
        out_shape=[jax.ShapeDtypeStruct((b, t, d), F32), jax.ShapeDtypeStruct(pre8.shape, F32)],
        scratch_shapes=[pltpu.VMEM(pre8.shape[1:], F32), pltpu.VMEM((tb, dff), BF16)],
        compiler_params=_params(2),
        name="out_ffn_s5" if s5_extra is not None else "out_ffn",
    )(*args)
    ulast = jnp.transpose(ulast, (0, 3, 1, 2, 4)).reshape(b, SUBLANES, 2 * dff)
    return xo, ulast[:, SUBLANES - (width - 1):]


def _head_rms(x, ones_blk, g):
    ss = _dot3_right(x * x, ones_blk) * (1.0 / HEAD_DIM)
    return x * lax.rsqrt(ss + RMS_EPS) * g


def _od_in_kernel(x_ref, mod_ref, g_ref, w_ref, qg_ref, kg_ref, dtb_ref,
                  q_ref, k_ref, v_ref, zg_ref, xbc_ref, dt_ref):
    mod = mod_ref[0]
    hn = _norm_mod(x_ref[0], g_ref[...], mod[1:2], mod[0:1]).astype(BF16)
    dot = lambda lo, hi: jnp.dot(hn, w_ref[:, lo:hi], preferred_element_type=F32)
    nq = q_ref.shape[2]
    nkv = k_ref.shape[2]
    hr = lax.broadcasted_iota(jnp.int32, (nq, nq), 0) // HEAD_DIM
    hc = lax.broadcasted_iota(jnp.int32, (nq, nq), 1) // HEAD_DIM
    ones_blk = jnp.where(hr == hc, 1.0, 0.0).astype(BF16)
    q_ref[0] = _head_rms(dot(0, nq), ones_blk, qg_ref[...])
    k_ref[0] = _head_rms(dot(nq, nq + nkv), ones_blk[0:nkv, 0:nkv], kg_ref[...])
    o = nq + nkv
    v_ref[0] = dot(o, o + nkv)
    o += nkv
    zg_ref[0] = dot(o, o + zg_ref.shape[2])
    o += zg_ref.shape[2]
    xbc_ref[0] = dot(o, o + xbc_ref.shape[2])
    o += xbc_ref.shape[2]
    dt_ref[0] = _softplus(dot(o, o + LANES) + dtb_ref[...])


def _od_in(x, mod, norm_g, w_in, q_norm, k_norm, dt_bias, dims, tb):
    b, t, d = x.shape
    nq, nkv, nz, nxbc, nh = dims
    w = jnp.concatenate([w_in, jnp.zeros((d, LANES - nh), F32)], axis=1).astype(BF16)
    qg = jnp.tile(q_norm, nq // HEAD_DIM).reshape(1, nq)
    kg = jnp.tile(k_norm, nkv // HEAD_DIM).reshape(1, nkv)
    dtb = jnp.concatenate([dt_bias, jnp.zeros((LANES - nh,), F32)]).reshape(1, LANES)
    g2 = norm_g.reshape(1, d)
    tile = lambda n: pl.BlockSpec((1, tb, n), lambda i, j: (i, j, 0))
    const = lambda a: pl.BlockSpec(a.shape, lambda i, j: (0,) * a.ndim)
    widths = (nq, nkv, nkv, nz, nxbc, LANES)
    return pl.pallas_call(
        _od_in_kernel,
        grid=(b, t // tb),
        in_specs=[tile(d), pl.BlockSpec((1, 6, d), lambda i, j: (i, 0, 0)), const(g2), const(w),
                  const(qg), const(kg), const(dtb)],
        out_specs=[tile(n) for n in widths],
        out_shape=[jax.ShapeDtypeStruct((b, t, n), F32) for n in widths],
        compiler_params=_params(2),
        name="od_in",
    )(x, mod, g2, w, qg, kg, dtb)


def _swa_kernel(q_ref, k_ref, v_ref, k0_ref, v0_ref, bias_ref, sink_ref, o_ref, kx, vx,
                *, n_blocks, nq, mask_start):
    t = pl.program_id(1)
    tb = q_ref.shape[1]
    n_kv = k_ref.shape[2] // HEAD_DIM
    rows_q = nq * CHUNK
    tail = kx.shape[1] - WINDOW - tb
    left = lax.broadcasted_iota(jnp.int32, (1, LANES), 1) < HEAD_DIM
    ones = jnp.ones((SWA_KEYS, LANES), BF16)
    col = lax.broadcasted_iota(jnp.int32, (1, SWA_KEYS), 1)

    def place(dst, rows, x):
        rolled = pltpu.roll(x, HEAD_DIM, axis=1)
        dst[0, rows, :] = jnp.where(left, x, 0.0).astype(BF16)
        dst[1, rows, :] = jnp.where(left, 0.0, rolled).astype(BF16)
        dst[2, rows, :] = jnp.where(left, rolled, 0.0).astype(BF16)
        dst[3, rows, :] = jnp.where(left, 0.0, x).astype(BF16)

    @pl.when(t == 0)
    def _():
        place(kx, slice(0, WINDOW), k0_ref[0])
        place(vx, slice(0, WINDOW), v0_ref[0])
        for i in range(2 * n_kv if tail else 0):
            kx[i, WINDOW + tb:, :] = jnp.zeros((tail, LANES), BF16)
            vx[i, WINDOW + tb:, :] = jnp.zeros((tail, LANES), BF16)

    place(kx, slice(WINDOW, WINDOW + tb), k_ref[0])
    place(vx, slice(WINDOW, WINDOW + tb), v_ref[0])

    def block(blk, carry):
        r0 = pl.multiple_of(blk * rows_q, rows_q)
        keys = pl.ds(r0, SWA_KEYS)
        if mask_start:
            valid = t * tb + r0 - WINDOW + col >= 0
        probs = []
        sinks = []
        for j in range(n_kv):
            lo = 2 * j * LANES
            qg = jnp.concatenate([q_ref[0, pl.ds(r0 + cq * CHUNK, CHUNK), lo + r * LANES:lo + (r + 1) * LANES]
                                  for cq in range(nq) for r in range(2)], axis=0).astype(BF16)
            kcat = jnp.concatenate([kx[2 * j, keys, :], kx[2 * j + 1, keys, :]], axis=0)
            s_both = lax.dot_general(qg, kcat, _NT, preferred_element_type=F32) * HEAD_DIM ** -0.5
            for side in range(2):
                s = s_both[:, side * SWA_KEYS:(side + 1) * SWA_KEYS] + bias_ref[j, side]
                if mask_start:
                    s = jnp.where(valid, s, NEG_INF)
                m = jnp.max(s, axis=-1, keepdims=True)
                probs.append(jnp.exp(s - m).astype(BF16))
                sinks.append(jnp.exp(sink_ref[j, side] - m))
        den = jnp.dot(jnp.concatenate(probs, axis=0), ones, preferred_element_type=F32)
        nr = 2 * rows_q
        for j in range(n_kv):
            lo = 2 * j * LANES
            vcat = jnp.concatenate([vx[2 * j, keys, :], vx[2 * j + 1, keys, :]], axis=0)
            pv = jnp.dot(jnp.concatenate(probs[2 * j:2 * j + 2], axis=1), vcat, preferred_element_type=F32)
            d_l = den[(2 * j) * nr:(2 * j + 1) * nr] + sinks[2 * j]
            d_r = den[(2 * j + 1) * nr:(2 * j + 2) * nr] + sinks[2 * j + 1]
            out = pv * jnp.where(left, 1.0 / d_l, 1.0 / d_r)
            for cq in range(nq):
                for r in range(2):
                    o_ref[0, pl.ds(r0 + cq * CHUNK, CHUNK), lo + r * LANES:lo + (r + 1) * LANES] = (
                        out[(2 * cq + r) * CHUNK:(2 * cq + r + 1) * CHUNK])
        return carry

    lax.fori_loop(0, n_blocks, block, 0, unroll=math.gcd(n_blocks, 2))
    for i in range(2 * n_kv):
        kx[i, 0:WINDOW, :] = kx[i, tb:tb + WINDOW, :]
        vx[i, 0:WINDOW, :] = vx[i, tb:tb + WINDOW, :]


def _t5_bucket(rel):
    nb = T5_BUCKETS // 2
    max_exact = nb // 2
    ret = (rel > 0).astype(jnp.int32) * nb
    n = jnp.abs(rel)
    nf = jnp.maximum(n, 1).astype(F32)
    large = max_exact + (jnp.log(nf / max_exact) / math.log(T5_MAX_DIST / max_exact)
                         * (nb - max_exact)).astype(jnp.int32)
    large = jnp.minimum(large, nb - 1)
    return ret + jnp.where(n < max_exact, n, large)


def _rel_bias(table):
    rel = (jnp.arange(WINDOW + CHUNK)[None, :] - WINDOW) - jnp.arange(CHUNK)[:, None]
    return jnp.transpose(table[_t5_bucket(rel)], (2, 0, 1))


def _swa(q, k, v, k0, v0, bias, sink, tb, mask_start):
    b, t, nq = q.shape
    nkv = k.shape[2]
    n_kv = nkv // HEAD_DIM
    assert nkv == LANES and nq == 2 * n_kv * LANES
    by_side = lambda a: jnp.transpose(a.reshape((n_kv, 2, 2) + a.shape[1:]), (0, 2, 1) + tuple(range(3, a.ndim + 2)))
    span = WINDOW + CHUNK
    n_chunks = tb // CHUNK
    cpb = 2 if n_chunks % 2 == 0 else 1
    assert WINDOW + cpb * CHUNK <= SWA_KEYS
    bias2 = by_side(bias).reshape(n_kv, 2, 2 * CHUNK, span)
    bias3 = jnp.concatenate(
        [jnp.pad(bias2, ((0, 0), (0, 0), (0, 0), (cq * CHUNK, SWA_KEYS - span - cq * CHUNK)),
                 constant_values=NEG_INF) for cq in range(cpb)], axis=2)
    sink3 = jnp.tile(jnp.repeat(by_side(sink), CHUNK, axis=2), (1, 1, cpb))
    sink3 = jnp.broadcast_to(sink3[..., None], sink3.shape + (LANES,))
    tile = lambda n: pl.BlockSpec((1, tb, n), lambda i, j: (i, j, 0))
    wspec = pl.BlockSpec((1, WINDOW, nkv), lambda i, j: (i, 0, 0))
    const = lambda a: pl.BlockSpec(a.shape, lambda i, j: (0,) * a.ndim)
    rows = WINDOW + max(tb, WINDOW) + SWA_KEYS - WINDOW - cpb * CHUNK
    return pl.pallas_call(
        functools.partial(_swa_kernel, n_blocks=n_chunks // cpb, nq=cpb, mask_start=mask_start),
        grid=(b, t // tb),
        in_specs=[tile(nq), tile(nkv), tile(nkv), wspec, wspec, const(bias3), const(sink3)],
        out_specs=tile(nq),
        out_shape=jax.ShapeDtypeStruct((b, t, nq), F32),
        scratch_shapes=[pltpu.VMEM((2 * n_kv, rows, LANES), BF16), pltpu.VMEM((2 * n_kv, rows, LANES), BF16)],
        compiler_params=_params(2),
        name="swa",
    )(q, k, v, k0, v0, bias3, sink3)


def _ssd_kernel(xbc_ref, zg_ref, dt_ref, pre_ref, cw_ref, cb_ref, aexp_ref, dexp_ref,
                e_ref, ng_ref, s0_ref, y_ref, sfin_ref, clast_ref, st_scr, xin_scr, dte_scr,
                *, n_chunks):
    t = pl.program_id(1)
    tb = xbc_ref.shape[1]
    inner = zg_ref.shape[2]
    n_heads = inner // SSD_HEAD_DIM
    hpg = n_heads // SSD_GROUPS
    gn = SSD_GROUPS * SSD_STATE
    n_hs = n_heads * CHUNK

    @pl.when(t == 0)
    def _():
        st_scr[...] = s0_ref[0]
        xin_scr[0:SUBLANES, :] = pre_ref[0]

    xin_scr[SUBLANES:, :] = xbc_ref[0]
    dte_scr[...] = _dot3_right(dt_ref[0], e_ref[...])

    row_hs = lax.broadcasted_iota(jnp.int32, (n_hs, 1), 0)
    own_group = row_hs // (hpg * CHUNK) == lax.broadcasted_iota(jnp.int32, (1, gn), 1) // SSD_STATE
    own_head = row_hs // CHUNK == lax.broadcasted_iota(jnp.int32, (1, inner), 1) // SSD_HEAD_DIM
    state_group = (lax.broadcasted_iota(jnp.int32, (gn, 1), 0) // SSD_STATE
                   == lax.broadcasted_iota(jnp.int32, (1, inner), 1) // (hpg * SSD_HEAD_DIM))
    first_group = lax.broadcasted_iota(jnp.int32, (1, inner), 1) < hpg * SSD_HEAD_DIM
    step_row = lax.broadcasted_iota(jnp.int32, (CHUNK, 1), 0)
    step_lane = lax.broadcasted_iota(jnp.int32, (1, n_hs), 1) % CHUNK
    zero = jnp.zeros((), BF16)
    aexp = aexp_ref[...]

    def chunk(c, carry):
        r0 = pl.multiple_of(c * CHUNK, CHUNK)
        rows = pl.ds(r0, CHUNK)
        xc = _silu(_causal_conv(xin_scr[pl.ds(r0 + SUBLANES, CHUNK), :], xin_scr[pl.ds(r0, SUBLANES), :],
                                cw_ref[...], cb_ref[...]))
        xs = xc[:, 0:inner]
        bm = xc[:, inner:inner + gn]
        cm = xc[:, inner + gn:inner + 2 * gn].astype(BF16)
        dte = dte_scr[rows, :]
        cum = _cumsum_rows(dte * aexp)
        cum_last = cum[CHUNK - 1:CHUNK, :]
        cum_at_step = jnp.sum(jnp.where(step_row == step_lane, cum, 0.0), axis=0, keepdims=True)
        decay = jnp.exp(jnp.where(step_row >= step_lane, cum - cum_at_step, NEG_INF))
        bm_bd = jnp.where(own_group, jnp.concatenate([bm.astype(BF16)] * n_heads, axis=0), zero)
        cb = lax.dot_general(cm, bm_bd, _NT, preferred_element_type=F32)
        xdt = (xs * dte).astype(BF16)
        xdt_bd = jnp.where(own_head, jnp.concatenate([xdt] * n_heads, axis=0), zero)
        st = st_scr[...]
        st_bd = jnp.where(state_group, jnp.concatenate([st.astype(BF16)] * SSD_GROUPS, axis=0), zero)
        y = (jnp.dot((cb * decay).astype(BF16), xdt_bd, preferred_element_type=F32)
             + jnp.exp(cum) * jnp.dot(cm, st_bd, preferred_element_type=F32))
        xw = (xs * (jnp.exp(cum_last - cum) * dte)).astype(BF16)
        upd = jnp.dot(bm.T.astype(BF16), xw, preferred_element_type=F32)
        st_scr[...] = st * jnp.exp(cum_last) + jnp.where(first_group, upd[0:SSD_STATE], upd[SSD_STATE:])
        yd = (y + dexp_ref[...] * xs) * _silu(zg_ref[0, rows, :])
        ms = jnp.mean(yd * yd, axis=-1, keepdims=True)
        y_ref[0, rows, :] = yd * lax.rsqrt(ms + RMS_EPS) * ng_ref[...]
        return carry

    lax.fori_loop(0, n_chunks, chunk, 0, unroll=math.gcd(n_chunks, CHUNK_UNROLL))
    xin_scr[0:SUBLANES, :] = xin_scr[tb:tb + SUBLANES, :]

    @pl.when(t == pl.num_programs(1) - 1)
    def _():
        sfin_ref[0] = st_scr[...]
        clast_ref[0] = xin_scr[0:SUBLANES, :]


def _ssd(xbc, zg, dt, conv_prefix, conv_w, conv_b, a_log, d_skip, norm_g, s0, tb):
    b, t, nxbc = xbc.shape
    inner = zg.shape[2]
    nh = a_log.shape[0]
    width = conv_w.shape[0]
    assert SSD_GROUPS == 2 and SSD_HEAD_DIM == CHUNK
    pre8 = jnp.concatenate([jnp.zeros((b, SUBLANES - (width - 1), nxbc), F32), conv_prefix], axis=1)
    a = -jnp.exp(a_log)
    aexp = jnp.repeat(a, SSD_HEAD_DIM).reshape(1, inner)
    dexp = jnp.repeat(d_skip, SSD_HEAD_DIM).reshape(1, inner)
    expand = (np.arange(LANES)[:, None] == np.arange(inner)[None, :] // SSD_HEAD_DIM)
    expand = jnp.asarray(expand, BF16)
    s0t = jnp.transpose(s0, (0, 3, 1, 2)).reshape(b, SSD_STATE, inner)
    ng = norm_g.reshape(1, inner)
    cb2 = conv_b.reshape(1, nxbc)
    tile = lambda n: pl.BlockSpec((1, tb, n), lambda i, j: (i, j, 0))
    const = lambda arr: pl.BlockSpec(arr.shape, lambda i, j: (0,) * arr.ndim)
    bspec = lambda arr: pl.BlockSpec((1,) + arr.shape[1:], lambda i, j: (i,) + (0,) * (arr.ndim - 1))
    y, sfin, clast = pl.pallas_call(
        functools.partial(_ssd_kernel, n_chunks=tb // CHUNK),
        grid=(b, t // tb),
        in_specs=[tile(nxbc), tile(inner), tile(LANES),
                  bspec(pre8), const(conv_w), const(cb2), const(aexp), const(dexp),
                  const(expand), const(ng), bspec(s0t)],
        out_specs=[tile(inner), bspec(s0t), bspec(pre8)],
        out_shape=[jax.ShapeDtypeStruct((b, t, inner), F32), jax.ShapeDtypeStruct(s0t.shape, F32),
                   jax.ShapeDtypeStruct(pre8.shape, F32)],
        scratch_shapes=[pltpu.VMEM((SSD_STATE, inner), F32), pltpu.VMEM((SUBLANES + tb, nxbc), F32),
                        pltpu.VMEM((tb, inner), F32)],
        compiler_params=_params(2),
        name="ssd",
    )(xbc, zg, dt, pre8, conv_w, cb2, aexp, dexp, expand, ng, s0t)
    sfin = jnp.transpose(sfin.reshape(b, SSD_STATE, nh, SSD_HEAD_DIM), (0, 2, 3, 1))
    return y, sfin, clast[:, SUBLANES - (width - 1):]


def _trunk(x, mods, P, st, sample):
    b, t, d = x.shape
    tb = min(MAX_TILE, t)
    depth = P['w_mod'].shape[0]
    new = {name: [] for name in ('s5_re', 's5_im', 'gla', 'swa_k', 'swa_v', 'ssd', 'ssd_conv', 'ffn_conv')}
    for layer in range(depth):
        i = layer // 2
        mod = mods[layer].reshape(b, 6, d)
        ffn = (P['norm2_g'][layer], P['ffn_w_up'][layer], P['ffn_conv_w'][layer], P['ffn_conv_b'][layer],
               P['ffn_w_down'][layer], st['ffn_conv'][layer], tb)
        if layer % 2 == 0:
            u, qk, v, r, gate = _ev_in(x, mod, P['norm1_g'][layer], P['ev_w_in'][i], P['gla_w_gate2'][i],
                                       P['gla_b_gate'][i], tb)
            mats = _s5_matrices(P['s5_a_re'][i], P['s5_a_im'][i], P['s5_log_dt'][i], P['s5_b_re'][i],
                                P['s5_b_im'][i], P['s5_c_re'][i], P['s5_c_im'][i])
            ya, sr, si = _s5(u, mats, P['s5_d'][i], st['s5_re'][i], st['s5_im'][i])
            ob, sg = _gla(qk, v, gate, r, st['gla'][i], P['gla_norm_g'][i], tb)
            new['s5_re'].append(sr)
            new['s5_im'].append(si)
            new['gla'].append(sg)
            x, fp = _out_ffn(x, mod, ya, ob, P['ev_w_out'][i], *ffn,
                             s5_extra=(P['s5_w_glu'][i], P['s5_b_glu'][i]))
        else:
            nq = P['swa_sink'].shape[1] * HEAD_DIM
            nkv = SWA_KV_HEADS * HEAD_DIM
            inner = P['ssd_norm_g'].shape[1]
            nxbc = P['ssd_conv_w'].shape[2]
            nh = P['ssd_a_log'].shape[1]
            q, k, v, zg, xbc, dt = _od_in(x, mod, P['norm1_g'][layer], P['od_w_in'][i], P['swa_q_norm'][i],
                                               P['swa_k_norm'][i], P['ssd_dt_bias'][i],
                                               (nq, nkv, inner, nxbc, nh), tb)
            bias = _rel_bias(P['t5_bias'])
            if sample:
                k0 = st['swa_k'][i].reshape(b, WINDOW, nkv)
                v0 = st['swa_v'][i].reshape(b, WINDOW, nkv)
            else:
                k0 = v0 = jnp.zeros((b, WINDOW, nkv), F32)
            oc = _swa(q, k, v, k0, v0, bias, P['swa_sink'][i], tb, mask_start=not sample)
            yd, ss, sc = _ssd(xbc, zg, dt, st['ssd_conv'][i], P['ssd_conv_w'][i], P['ssd_conv_b'][i],
                              P['ssd_a_log'][i], P['ssd_d'][i], P['ssd_norm_g'][i], st['ssd'][i], tb)
            keep = slice(None) if sample else slice(t - WINDOW, t)
            new['swa_k'].append(k[:, keep].reshape(b, -1, SWA_KV_HEADS, HEAD_DIM))
            new['swa_v'].append(v[:, keep].reshape(b, -1, SWA_KV_HEADS, HEAD_DIM))
            new['ssd'].append(ss)
            new['ssd_conv'].append(sc)
            x, fp = _out_ffn(x, mod, oc, yd, P['od_w_out'][i], *ffn)
        new['ffn_conv'].append(fp)
    return x, {name: jnp.stack(vals) for name, vals in new.items()}


def kernel(x_prompt, x_sample, state_s5_re, state_s5_im, state_gla, cache_swa_k, cache_swa_v, state_ssd, state_ssd_conv, state_ffn_conv, c_prompt, c_sample, t5_bias, norm1_g, norm2_g, w_mod, b_mod, ffn_w_up, ffn_conv_w, ffn_conv_b, ffn_w_down, ev_w_in, ev_w_out, s5_a_re, s5_a_im, s5_log_dt, s5_b_re, s5_b_im, s5_c_re, s5_c_im, s5_d, s5_w_glu, s5_b_glu, gla_w_gate2, gla_b_gate, gla_norm_g, od_w_in, od_w_out, swa_q_norm, swa_k_norm, swa_sink, ssd_conv_w, ssd_conv_b, ssd_dt_bias, ssd_a_log, ssd_d, ssd_norm_g):
    P = dict(t5_bias=t5_bias, norm1_g=norm1_g, norm2_g=norm2_g, w_mod=w_mod, b_mod=b_mod,
             ffn_w_up=ffn_w_up, ffn_conv_w=ffn_conv_w, ffn_conv_b=ffn_conv_b, ffn_w_down=ffn_w_down,
             ev_w_in=ev_w_in, ev_w_out=ev_w_out, s5_a_re=s5_a_re, s5_a_im=s5_a_im, s5_log_dt=s5_log_dt,
             s5_b_re=s5_b_re, s5_b_im=s5_b_im, s5_c_re=s5_c_re, s5_c_im=s5_c_im, s5_d=s5_d,
             s5_w_glu=s5_w_glu, s5_b_glu=s5_b_glu, gla_w_gate2=gla_w_gate2, gla_b_gate=gla_b_gate,
             gla_norm_g=gla_norm_g, od_w_in=od_w_in, od_w_out=od_w_out, swa_q_norm=swa_q_norm,
             swa_k_norm=swa_k_norm, swa_sink=swa_sink, ssd_conv_w=ssd_conv_w, ssd_conv_b=ssd_conv_b,
             ssd_dt_bias=ssd_dt_bias, ssd_a_log=ssd_a_log, ssd_d=ssd_d, ssd_norm_g=ssd_norm_g)
    bp = x_prompt.shape[0]
    n_even, n_odd = state_s5_re.shape[0], state_ssd.shape[0]
    depth = w_mod.shape[0]
    zeros_like_b = lambda a: jnp.zeros((a.shape[0], bp) + a.shape[2:], F32)
    zero_st = dict(s5_re=zeros_like_b(state_s5_re), s5_im=zeros_like_b(state_s5_im), gla=zeros_like_b(state_gla),
                   ssd=zeros_like_b(state_ssd), ssd_conv=zeros_like_b(state_ssd_conv),
                   ffn_conv=zeros_like_b(state_ffn_conv))
    sample_st = dict(s5_re=state_s5_re, s5_im=state_s5_im, gla=state_gla, swa_k=cache_swa_k,
                     swa_v=cache_swa_v, ssd=state_ssd, ssd_conv=state_ssd_conv, ffn_conv=state_ffn_conv)
    mods = _modulation(jnp.concatenate([c_prompt, c_sample], axis=0), w_mod, b_mod)
    y_prompt, stp = _trunk(x_prompt, mods[:, :bp], P, zero_st, False)
    y_sample, sts = _trunk(x_sample, mods[:, bp:], P, sample_st, True)
    names = ('s5_re', 's5_im', 'gla', 'swa_k', 'swa_v', 'ssd', 'ssd_conv', 'ffn_conv')
    return (y_prompt, y_sample) + tuple(stp[n] for n in names) + tuple(sts[n] for n in names)
```

```python
import functools
import math

import jax
import jax.numpy as jnp
import numpy as np
from jax import lax
from jax.experimental import pallas as pl
from jax.experimental.pallas import tpu as pltpu

F32 = jnp.float32
BF16 = jnp.bfloat16

CHUNK = 64
WINDOW = 128
S5_GROUP = 16
S5_STATE = 64
S5_SUB = 8
S5_MAX_SUBS = 64
GLA_HEADS = 4
GLA_GATE_NORM = 16.0
HEAD_DIM = 64
SWA_KV_HEADS = 2
SWA_KEYS = 256
SSD_HEAD_DIM = 64
SSD_STATE = 128
SSD_GROUPS = 2
T5_BUCKETS = 32
T5_MAX_DIST = 128
RMS_EPS = 1e-6
NEG_INF = -1e30
LANES = 128
SUBLANES = 8
assert S5_SUB == SUBLANES and S5_GROUP * SUBLANES == LANES
MAX_TILE = 512
CHUNK_UNROLL = 4
FFN_COL_BLOCK = 256
VMEM_LIMIT = 56 * 1024 * 1024

_NT = (((1,), (1,)), ((), ()))
_HI = lax.Precision.HIGHEST


def _params(n_axes=2):
    sem = ("parallel",) + ("arbitrary",) * (n_axes - 1)
    return pltpu.CompilerParams(dimension_semantics=sem, vmem_limit_bytes=VMEM_LIMIT)


def _bdot(a, b):
    return jnp.dot(a.astype(BF16), b.astype(BF16), preferred_element_type=F32)


def _split_dot(x, c, parts=3):
    pieces = []
    rest = x
    for _ in range(parts):
        piece = rest.astype(BF16)
        pieces.append(piece)
        rest = rest - piece.astype(F32)
    n = x.shape[0]
    d = jnp.dot(jnp.concatenate(pieces, axis=0), c, preferred_element_type=F32)
    out = d[0:n]
    for i in range(1, parts):
        out = out + d[i * n:(i + 1) * n]
    return out


def _silu(x):
    return x * jax.nn.sigmoid(x)


def _softplus(x):
    return jnp.maximum(x, 0.0) + jnp.log1p(jnp.exp(-jnp.abs(x)))


def _log_sigmoid(x):
    return jnp.minimum(x, 0.0) - jnp.log1p(jnp.exp(-jnp.abs(x)))


def _norm_mod(x, g, scale, shift):
    ms = jnp.mean(x * x, axis=-1, keepdims=True)
    return (x * lax.rsqrt(ms + RMS_EPS) * g) * (1.0 + scale) + shift


def _cumsum_rows(x):
    n, m = x.shape
    tiles = x.reshape(n // SUBLANES, SUBLANES, m)
    sub = lax.broadcasted_iota(jnp.int32, (1, SUBLANES, 1), 1)
    d = 1
    while d < SUBLANES:
        tiles = tiles + jnp.where(sub >= d, pltpu.roll(tiles, d, axis=1), 0.0)
        d *= 2
    out = [tiles[0]]
    for i in range(1, n // SUBLANES):
        out.append(tiles[i] + out[-1][SUBLANES - 1:SUBLANES, :])
    return jnp.concatenate(out, axis=0)


def _causal_conv(u, prev8, w, b):
    width = w.shape[0]
    n = u.shape[0]
    ext = jnp.concatenate([prev8, u[0:SUBLANES]], axis=0)
    full = b
    head = b
    for j in range(width):
        sh = width - 1 - j
        if sh == 0:
            full = full + u * w[j:j + 1]
            head = head + u[0:SUBLANES] * w[j:j + 1]
        else:
            full = full + pltpu.roll(u, sh, axis=0) * w[j:j + 1]
            head = head + pltpu.roll(ext, sh, axis=0)[SUBLANES:2 * SUBLANES] * w[j:j + 1]
    if n == SUBLANES:
        return head
    return jnp.concatenate([head, full[SUBLANES:]], axis=0)


def _segment_pitch(n):
    pitch = n // SUBLANES + SUBLANES
    return pitch if (pitch // SUBLANES) % 2 else pitch + SUBLANES


def _to_segments(ref, val):
    nv = val.shape[0] // SUBLANES
    pitch = ref.shape[1] // SUBLANES
    for c in range(ref.shape[0]):
        for s in range(SUBLANES):
            ref[c, s * pitch:s * pitch + nv, :] = val[s * nv:(s + 1) * nv, c * LANES:(c + 1) * LANES]


def _from_segments(ref, n):
    nv = n // SUBLANES
    pitch = ref.shape[1] // SUBLANES
    return jnp.concatenate(
        [jnp.concatenate([ref[c, s * pitch:s * pitch + nv, :] for s in range(SUBLANES)], axis=0)
         for c in range(ref.shape[0])], axis=1)


def _load_interleaved(ref, n):
    pitch = ref.shape[1] // SUBLANES
    return jnp.concatenate(
        [jnp.concatenate([ref[c, pl.ds(i, SUBLANES, stride=pitch), :] for c in range(ref.shape[0])], axis=1)
         for i in range(n // SUBLANES)], axis=0)


def _store_interleaved(ref, val):
    pitch = ref.shape[1] // SUBLANES
    for i in range(val.shape[0] // SUBLANES):
        for c in range(ref.shape[0]):
            ref[c, pl.ds(i, SUBLANES, stride=pitch), :] = val[i * SUBLANES:(i + 1) * SUBLANES, c * LANES:(c + 1) * LANES]


def _causal_conv_interleaved(u, prev8, w, b):
    width = w.shape[0]
    n, m = u.shape
    nv = n // SUBLANES
    sub = lax.broadcasted_iota(jnp.int32, (SUBLANES, 1), 0)
    tiles = u.reshape(nv, SUBLANES, m)

    def back_one(a, before):
        first = jnp.where(sub == 0, before, pltpu.roll(a[nv - 1], 1, axis=0))
        return jnp.concatenate([first[None], a[:nv - 1]], axis=0)

    delayed = [tiles]
    for k in range(1, width):
        delayed.append(back_one(delayed[-1], prev8[SUBLANES - k:SUBLANES - k + 1]))
    out = b
    for j in range(width):
        out = out + delayed[width - 1 - j] * w[j:j + 1]
    hist = tiles[nv - 1]
    for k in range(2, width):
        hist = jnp.where(sub == SUBLANES - k, pltpu.roll(tiles[nv - k], SUBLANES - k + 1, axis=0), hist)
    return out.reshape(n, m), hist


def _mod_kernel(c_ref, w_ref, b_ref, o_ref):
    o_ref[0] = _bdot(_silu(c_ref[...]), w_ref[0]) + b_ref[0]


def _modulation(c, w_mod, b_mod):
    depth, d, n = w_mod.shape
    bc = c.shape[0]
    tn = n // 4
    return pl.pallas_call(
        _mod_kernel,
        grid=(depth, n // tn),
        in_specs=[pl.BlockSpec((bc, d), lambda l, j: (0, 0)),
                  pl.BlockSpec((1, d, tn), lambda l, j: (l, 0, j)),
                  pl.BlockSpec((1, 1, tn), lambda l, j: (l, 0, j))],
        out_specs=pl.BlockSpec((1, bc, tn), lambda l, j: (l, 0, j)),
        out_shape=jax.ShapeDtypeStruct((depth, bc, n), F32),
        compiler_params=_params(2),
        name="modulation",
    )(c, w_mod, b_mod.reshape(depth, 1, n))


def _ev_in_kernel(x_ref, mod_ref, g_ref, w_ref, wg2_ref, bg_ref,
                  u_ref, qk_ref, v_ref, r_ref, gate_ref):
    mod = mod_ref[0]
    hn = _norm_mod(x_ref[0], g_ref[...], mod[1:2], mod[0:1]).astype(BF16)
    dot = lambda lo, hi: jnp.dot(hn, w_ref[:, lo:hi], preferred_element_type=F32)
    u = dot(0, 512)
    for c in range(u_ref.shape[0]):
        u_ref[c] = _chunk_transpose(u[:, c * LANES:(c + 1) * LANES].reshape(-1, SUBLANES, LANES))
    qk_ref[0] = dot(512, 1024)
    v_ref[0] = dot(1024, 1536)
    r_ref[0] = dot(1536, 2048)
    gl = dot(2048, 2048 + LANES)
    gate_ref[0] = _log_sigmoid(_bdot(gl, wg2_ref[...]) + bg_ref[...]) * (1.0 / GLA_GATE_NORM)


def _ev_in(x, mod, norm_g, w_in, w_gate2, b_gate, tb):
    b, t, d = x.shape
    rank = w_gate2.shape[0]
    nk = w_gate2.shape[1]
    wu, wq, wk, wv, wgl, wr = jnp.split(w_in, [512, 768, 1024, 1536, 1536 + rank], axis=1)
    w = jnp.concatenate([wu, wq, wk, wv, wr, wgl, jnp.zeros((d, LANES - rank), F32)], axis=1).astype(BF16)
    wg2 = jnp.concatenate([w_gate2, jnp.zeros((LANES - rank, nk), F32)], axis=0).astype(BF16)
    tile = lambda n: pl.BlockSpec((1, tb, n), lambda i, j: (i, j, 0))
    const = lambda a: pl.BlockSpec(a.shape, lambda i, j: (0,) * a.ndim)
    g2 = norm_g.reshape(1, d)
    bg = b_gate.reshape(1, nk)
    return pl.pallas_call(
        _ev_in_kernel,
        grid=(b, t // tb),
        in_specs=[tile(d), pl.BlockSpec((1, 6, d), lambda i, j: (i, 0, 0)), const(g2), const(w),
                  const(wg2), const(bg)],
        out_specs=[pl.BlockSpec((512 // LANES, tb // S5_SUB, None, SUBLANES, LANES), lambda i, j: (0, j, i, 0, 0)),
                   tile(512), tile(512), tile(512), tile(nk)],
        out_shape=[jax.ShapeDtypeStruct((512 // LANES, t // S5_SUB, b, SUBLANES, LANES), F32)]
        + [jax.ShapeDtypeStruct((b, t, n), F32) for n in (512, 512, 512, nk)],
        compiler_params=_params(2),
        name="ev_in",
    )(x, mod, g2, w, wg2, bg)


def _gla_kernel(qk_ref, v_ref, gate_ref, r_ref, s0_ref, ng_ref, o_ref, sfin_ref, st_scr, *, n_chunks):
    t = pl.program_id(1)
    dk = qk_ref.shape[2] // 2 // GLA_HEADS
    dv = v_ref.shape[2] // GLA_HEADS
    nk = GLA_HEADS * dk
    nv = GLA_HEADS * dv
    blk = (lax.broadcasted_iota(jnp.int32, (nv, nk), 0) // dv
           == lax.broadcasted_iota(jnp.int32, (nv, nk), 1) // dk)

    @pl.when(t == 0)
    def _():
        s0 = jnp.concatenate([s0_ref[0, h] for h in range(GLA_HEADS)], axis=0)
        st_scr[...] = jnp.where(blk, jnp.concatenate([s0] * GLA_HEADS, axis=1), 0.0)

    n_hs = GLA_HEADS * CHUNK
    row_head = lax.broadcasted_iota(jnp.int32, (n_hs, 1), 0) // CHUNK
    own_k = row_head == lax.broadcasted_iota(jnp.int32, (1, nk), 1) // dk
    own_v = row_head == lax.broadcasted_iota(jnp.int32, (1, nv), 1) // dv
    causal = (lax.broadcasted_iota(jnp.int32, (CHUNK, 1), 0)
              >= lax.broadcasted_iota(jnp.int32, (1, n_hs), 1) % CHUNK)
    ng = ng_ref[...]

    def chunk(c, carry):
        rows = pl.ds(pl.multiple_of(c * CHUNK, CHUNK), CHUNK)
        k = qk_ref[0, rows, nk:2 * nk]
        v = v_ref[0, rows, :]
        cum = _cumsum_rows(gate_ref[0, rows, :])
        cum_last = cum[CHUNK - 1:CHUNK, :]
        qe = (qk_ref[0, rows, 0:nk] * dk ** -0.5 * jnp.exp(cum)).astype(BF16)
        ke = (k * jnp.exp(-cum)).astype(BF16)
        kd = (k * jnp.exp(cum_last - cum)).astype(BF16)
        v_b = v.astype(BF16)
        st = st_scr[...]
        ke_bd = jnp.where(own_k, jnp.concatenate([ke] * GLA_HEADS, axis=0), jnp.zeros((), BF16))
        v_bd = jnp.where(own_v, jnp.concatenate([v_b] * GLA_HEADS, axis=0), jnp.zeros((), BF16))
        att = lax.dot_general(qe, ke_bd, _NT, preferred_element_type=F32)
        att = jnp.where(causal, att, 0.0).astype(BF16)
        o = (jnp.dot(att, v_bd, preferred_element_type=F32)
             + lax.dot_general(qe, st.astype(BF16), _NT, preferred_element_type=F32))
        for h in range(GLA_HEADS):
            cols = slice(h * dv, (h + 1) * dv)
            oh = o[:, cols]
            ms = jnp.mean(oh * oh, axis=-1, keepdims=True)
            oh = oh * lax.rsqrt(ms + RMS_EPS) * ng
            o_ref[0, rows, cols] = oh * _silu(r_ref[0, rows, cols])
        upd = jnp.dot(v.T.astype(BF16), kd, preferred_element_type=F32)
        st_scr[...] = jnp.where(blk, st * jnp.exp(cum_last) + upd, 0.0)
        return carry

    lax.fori_loop(0, n_chunks, chunk, 0, unroll=math.gcd(n_chunks, CHUNK_UNROLL))

    @pl.when(t == pl.num_programs(1) - 1)
    def _():
        for h in range(GLA_HEADS):
            sfin_ref[0, h] = st_scr[h * dv:(h + 1) * dv, h * dk:(h + 1) * dk]


def _gla(qk, v, gate, r, s0, norm_g, tb):
    b, t, nv = v.shape
    nk = gate.shape[2]
    dk, dv = nk // GLA_HEADS, nv // GLA_HEADS
    s0t = jnp.swapaxes(s0, 2, 3)
    tile = lambda n: pl.BlockSpec((1, tb, n), lambda i, j: (i, j, 0))
    sspec = pl.BlockSpec((1, GLA_HEADS, dv, dk), lambda i, j: (i, 0, 0, 0))
    ng = norm_g.reshape(1, dv)
    o, sfin = pl.pallas_call(
        functools.partial(_gla_kernel, n_chunks=tb // CHUNK),
        grid=(b, t // tb),
        in_specs=[tile(2 * nk), tile(nv), tile(nk), tile(nv), sspec,
                  pl.BlockSpec((1, dv), lambda i, j: (0, 0))],
        out_specs=[tile(nv), sspec],
        out_shape=[jax.ShapeDtypeStruct((b, t, nv), F32),
                   jax.ShapeDtypeStruct((b, GLA_HEADS, dv, dk), F32)],
        scratch_shapes=[pltpu.VMEM((nv, nk), F32)],
        compiler_params=_params(2),
        name="gla",
    )(qk, v, gate, r, s0t, ng)
    return o, jnp.swapaxes(sfin, 2, 3)


def _chunk_transpose(x):
    s = lax.broadcasted_iota(jnp.int32, (1, SUBLANES, LANES), 1)
    c = lax.broadcasted_iota(jnp.int32, (1, SUBLANES, LANES), 2) // S5_GROUP
    for d in (4, 2, 1):
        sb = (s & d) != 0
        cb = (c & d) != 0
        if 2 * d == SUBLANES:
            t = pltpu.roll(pltpu.roll(x, d, axis=1), S5_GROUP * d, axis=2)
        else:
            xs = jnp.where(sb, pltpu.roll(x, d, axis=1), pltpu.roll(x, SUBLANES - d, axis=1))
            t = jnp.where(cb, pltpu.roll(xs, S5_GROUP * d, axis=2), pltpu.roll(xs, LANES - S5_GROUP * d, axis=2))
        x = jnp.where(sb != cb, t, x)
    return x


def _s5_kernel(u_ref, w1_ref, n_ref, a_ref, d_ref, x0_ref, y_ref, xf_ref, mm_scr, xs_scr, st_scr, *, tk, bsz):
    j = pl.program_id(1)
    n_g = st_scr.shape[0]
    rows = tk * bsz
    of_group = lambda g: pl.ds(g, rows, stride=n_g)

    @pl.when(j == 0)
    def _():
        st_scr[...] = x0_ref[0]

    for g in range(n_g):
        ub = u_ref[of_group(g), :].astype(BF16)
        mm_scr[g] = jnp.dot(ub, w1_ref[0, g], preferred_element_type=F32)

    half = n_g // 2
    for g0 in (0, half):
        coef = [a_ref[0, g] for g in range(g0, g0 + half)]

        def step(k, carry, g0=g0, coef=coef):
            r = pl.ds(pl.multiple_of(k * bsz, bsz), bsz)
            out = []
            for i in range(half):
                x, xsw = carry[2 * i], carry[2 * i + 1]
                a1, a2, a2s = coef[i][0:1], coef[i][1:2], coef[i][2:3]
                xs_scr[g0 + i, r, :] = x
                out.append(a1 * x + a2 * xsw + mm_scr[g0 + i, r, LANES:2 * LANES])
                out.append(a1 * xsw + a2s * x + mm_scr[g0 + i, r, 2 * LANES:3 * LANES])
            return tuple(out)

        fin = lax.fori_loop(0, tk, step, tuple(st_scr[g0 + i, v] for i in range(half) for v in range(2)))
        for i in range(half):
            st_scr[g0 + i, 0] = fin[2 * i]
            st_scr[g0 + i, 1] = fin[2 * i + 1]

    for g in range(n_g):
        y = (mm_scr[g, :, 0:LANES] + jnp.dot(xs_scr[g].astype(BF16), n_ref[0, g], preferred_element_type=F32)
             + d_ref[0, g] * u_ref[of_group(g), :])
        y_ref[of_group(g), :] = jax.nn.gelu(y)

    @pl.when(j == pl.num_programs(1) - 1)
    def _():
        xf_ref[0] = st_scr[:, 0]


def _cmul(ar, ai, br, bi):
    return ar * br - ai * bi, ar * bi + ai * br


def _s5_matrices(a_re, a_im, log_dt, b_re, b_im, c_re, c_im):
    g, p = a_re.shape
    hdim = b_re.shape[-1]
    dt = jnp.exp(log_dt)[:, None]
    mag = jnp.exp(a_re * dt)
    ab_re, ab_im = mag * jnp.cos(a_im * dt), mag * jnp.sin(a_im * dt)
    den = a_re * a_re + a_im * a_im
    num_re, num_im = ab_re - 1.0, ab_im
    g_re = (num_re * a_re + num_im * a_im) / den
    g_im = (num_im * a_re - num_re * a_im) / den
    bb_re = g_re[..., None] * b_re - g_im[..., None] * b_im
    bb_im = g_re[..., None] * b_im + g_im[..., None] * b_re
    pw_re, pw_im = jnp.ones((1, g, p), F32), jnp.zeros((1, g, p), F32)
    sq_re, sq_im = ab_re, ab_im
    while pw_re.shape[0] < S5_SUB + 1:
        nr, ni = _cmul(pw_re, pw_im, sq_re, sq_im)
        pw_re, pw_im = jnp.concatenate([pw_re, nr]), jnp.concatenate([pw_im, ni])
        sq_re, sq_im = _cmul(sq_re, sq_im, sq_re, sq_im)
    pw_re, pw_im = pw_re[:S5_SUB + 1], pw_im[:S5_SUB + 1]
    ca_re, ca_im = _cmul(c_re[None], c_im[None], pw_re[:, :, None, :], pw_im[:, :, None, :])
    kern = (jnp.einsum('tghp,gpk->tghk', ca_re[:S5_SUB], bb_re, precision=_HI)
            - jnp.einsum('tghp,gpk->tghk', ca_im[:S5_SUB], bb_im, precision=_HI))
    s_in = np.arange(S5_SUB)[:, None, None]
    s_out = np.arange(S5_SUB)[None, :, None]
    lag = jnp.asarray(s_out - s_in == np.arange(S5_SUB), F32)
    m = jnp.einsum('iot,tghk->ioghk', lag, kern, precision=_HI)
    m = jnp.transpose(m, (2, 0, 4, 1, 3)).reshape(g, S5_SUB * hdim, S5_SUB * hdim)
    pr, pi = _cmul(pw_re[S5_SUB - 1::-1][..., None], pw_im[S5_SUB - 1::-1][..., None], bb_re[None], bb_im[None])
    flat = lambda a: jnp.transpose(a, (1, 0, 3, 2)).reshape(g, S5_SUB * hdim, p)
    nm = jnp.concatenate([ca_re[1:], -ca_im[1:]], axis=3)
    nm = jnp.transpose(nm, (1, 3, 0, 2)).reshape(g, 2 * p, S5_SUB * hdim)
    w1 = jnp.concatenate([m, flat(pr), flat(pi), flat(pi), flat(pr)], axis=2)
    a_r, a_i = pw_re[S5_SUB], pw_im[S5_SUB]
    coef = jnp.stack([jnp.concatenate([a_r, a_r], axis=1), jnp.concatenate([-a_i, a_i], axis=1),
                      jnp.concatenate([a_i, -a_i], axis=1)], axis=1)
    return w1.astype(BF16), nm.astype(BF16), coef


def _s5(u2, mats, d_skip, x0_re, x0_im):
    w1, nm, coef = mats
    n_ct, n_sub, b, n_g, lanes = u2.shape
    g, p2 = nm.shape[0], nm.shape[1]
    hdim = lanes // S5_SUB
    tk = min(n_sub, S5_MAX_SUBS)
    x0 = jnp.transpose(jnp.concatenate([x0_re, x0_im], axis=2), (1, 0, 2))
    x0 = jnp.stack([x0, jnp.roll(x0, p2 // 2, axis=2)], axis=1)
    dt = jnp.tile(d_skip.reshape(g, 1, hdim), (1, 1, S5_SUB))
    by_tile = lambda a: a.reshape((n_ct, n_g) + a.shape[1:])
    pspec = lambda a: pl.BlockSpec((1,) + a.shape[1:], lambda c, j: (c,) + (0,) * (a.ndim - 1))
    rows = tk * b
    uspec = pl.BlockSpec((None, rows * n_g, lanes), lambda c, j: (c, j, 0))
    params = [by_tile(a) for a in (w1, nm, coef, dt, x0)]
    y, xf = pl.pallas_call(
        functools.partial(_s5_kernel, tk=tk, bsz=b),
        grid=(n_ct, n_sub // tk),
        in_specs=[uspec] + [pspec(a) for a in params],
        out_specs=[uspec, pl.BlockSpec((1, n_g, b, p2), lambda c, j: (c, 0, 0, 0))],
        out_shape=[jax.ShapeDtypeStruct((n_ct, n_sub * b * n_g, lanes), F32),
                   jax.ShapeDtypeStruct((n_ct, n_g, b, p2), F32)],
        scratch_shapes=[pltpu.VMEM((n_g, rows, 3 * lanes), F32), pltpu.VMEM((n_g, rows, lanes), F32),
                        pltpu.VMEM((n_g, 2, b, p2), F32)],
        compiler_params=_params(2),
        name="s5",
    )(u2.reshape(n_ct, n_sub * b * n_g, lanes), *params)
    xf = jnp.transpose(xf.reshape(g, b, p2), (1, 0, 2))
    return y.reshape(u2.shape), xf[:, :, :p2 // 2], xf[:, :, p2 // 2:]


def _out_ffn_kernel(*refs, with_s5, n_cb):
    if with_s5:
        (x_ref, mod_ref, m1_ref, m2_ref, wglu_ref, bglu_ref, wout_ref, n2g_ref,
         wa_ref, wg_ref, cwa_ref, cwg_ref, cba_ref, cbg_ref, wd_ref, pre_ref,
         xo_ref, ulast_ref, carry_scr, h_scr, row_scr) = refs
    else:
        (x_ref, mod_ref, m1_ref, m2_ref, wout_ref, n2g_ref,
         wa_ref, wg_ref, cwa_ref, cwg_ref, cba_ref, cbg_ref, wd_ref, pre_ref,
         xo_ref, ulast_ref, carry_scr, h_scr, row_scr) = refs
    t = pl.program_id(1)

    @pl.when(t == 0)
    def _():
        carry_scr[...] = pre_ref[0]

    mod = mod_ref[0]
    if with_s5:
        ya = jnp.concatenate([_chunk_transpose(m1_ref[c]).reshape(-1, LANES) for c in range(m1_ref.shape[0])],
                             axis=1)
        m1 = ya * jax.nn.sigmoid(_bdot(ya, wglu_ref[...]) + bglu_ref[...])
    else:
        m1 = m1_ref[0]
    half = m1.shape[1]
    out = (jnp.dot(m1.astype(BF16), wout_ref[0:half, :], preferred_element_type=F32)
           + jnp.dot(m2_ref[0].astype(BF16), wout_ref[half:, :], preferred_element_type=F32))
    x1 = x_ref[0] + mod[2:3] * out
    tb = x1.shape[0]
    _to_segments(row_scr, _norm_mod(x1, n2g_ref[...], mod[4:5], mod[3:4]))
    hn = _load_interleaved(row_scr, tb).astype(BF16)
    cb = wa_ref.shape[2]

    def up(j):
        return (jnp.dot(hn, wa_ref[j], preferred_element_type=F32),
                jnp.dot(hn, wg_ref[j], preferred_element_type=F32))

    nxt = up(0)
    for j in range(n_cb):
        ua, ug = nxt
        if j + 1 < n_cb:
            nxt = up(j + 1)
        a, carry_scr[0, j] = _causal_conv_interleaved(ua, carry_scr[0, j], cwa_ref[j], cba_ref[j])
        gt, carry_scr[1, j] = _causal_conv_interleaved(ug, carry_scr[1, j], cwg_ref[j], cbg_ref[j])
        h_scr[:, j * cb:(j + 1) * cb] = (_silu(a) * gt).astype(BF16)
    _store_interleaved(row_scr, jnp.dot(h_scr[...], wd_ref[...], preferred_element_type=F32))
    xo_ref[0] = x1 + mod[5:6] * _from_segments(row_scr, tb)

    @pl.when(t == pl.num_programs(1) - 1)
    def _():
        ulast_ref[0] = carry_scr[...]


def _out_ffn(x, mod, m1, m2, w_out, norm2_g, w_up, conv_w, conv_b, w_down, prefix, tb, s5_extra=None):
    b, t, d = x.shape
    dff = w_down.shape[0]
    cb = FFN_COL_BLOCK
    n_cb = dff // cb
    width = conv_w.shape[0]
    half = m2.shape[2]
    blocks = lambda a: a.reshape(a.shape[0], 2, n_cb, cb)
    w_up_b = jnp.transpose(blocks(w_up.astype(BF16)), (1, 2, 0, 3))
    cw = jnp.transpose(blocks(conv_w), (1, 2, 0, 3))
    cbias = jnp.transpose(blocks(conv_b.reshape(1, 2 * dff)), (1, 2, 0, 3))
    wd = w_down.astype(BF16)
    pre8 = jnp.concatenate([jnp.zeros((b, SUBLANES - (width - 1), 2 * dff), F32), prefix], axis=1)
    pre8 = jnp.transpose(pre8.reshape(b, SUBLANES, 2, n_cb, cb), (0, 2, 3, 1, 4))
    tile = lambda n: pl.BlockSpec((1, tb, n), lambda i, j: (i, j, 0))
    const = lambda a: pl.BlockSpec(a.shape, lambda i, j: (0,) * a.ndim, pipeline_mode=pl.Buffered(1))
    bspec = lambda a: pl.BlockSpec((1,) + a.shape[1:], lambda i, j: (i,) + (0,) * (a.ndim - 1))
    wo = w_out.astype(BF16)
    n2g = norm2_g.reshape(1, d)
    args = [x, mod]
    specs = [tile(d), bspec(mod)]
    if s5_extra is not None:
        w_glu, b_glu = s5_extra
        w_glu, b_glu = w_glu.astype(BF16), b_glu.reshape(1, half)
        args += [m1, m2, w_glu, b_glu]
        specs += [pl.BlockSpec((m1.shape[0], tb // S5_SUB, None) + m1.shape[3:], lambda i, j: (0, j, i, 0, 0)),
                  tile(m2.shape[2]), const(w_glu), const(b_glu)]
    else:
        args += [m1, m2]
        specs += [tile(half), tile(m2.shape[2])]
    weights = [wo, n2g, w_up_b[0], w_up_b[1], cw[0], cw[1], cbias[0], cbias[1], wd]
    args += weights + [pre8]
    specs += [const(a) for a in weights] + [bspec(pre8)]
    xo, ulast = pl.pallas_call(
        functools.partial(_out_ffn_kernel, with_s5=s5_extra is not None, n_cb=n_cb),
        grid=(b, t // tb),
        in_specs=specs,
        out_specs=[tile(d), bspec(pre8)],
        out_shape=[jax.ShapeDtypeStruct((b, t, d), F32), jax.ShapeDtypeStruct(pre8.shape, F32)],
        scratch_shapes=[pltpu.VMEM(pre8.shape[1:], F32), pltpu.VMEM((tb, dff), BF16),
                        pltpu.VMEM((d // LANES, SUBLANES * _segment_pitch(tb), LANES), F32)],
        compiler_params=_params(2),
        name="out_ffn_s5" if s5_extra is not None else "out_ffn",
    )(*args)
    ulast = jnp.transpose(ulast, (0, 3, 1, 2, 4)).reshape(b, SUBLANES, 2 * dff)
    return xo, ulast[:, SUBLANES - (width - 1):]


def _head_rms(x, ones_blk, g):
    ss = _split_dot(x * x, ones_blk, parts=2) * (1.0 / HEAD_DIM)
    return x * lax.rsqrt(ss + RMS_EPS) * g


def _od_in_kernel(x_ref, mod_ref, g_ref, w_ref, qg_ref, kg_ref, dtb_ref,
                  q_ref, k_ref, v_ref, zg_ref, xbc_ref, dt_ref):
    mod = mod_ref[0]
    hn = _norm_mod(x_ref[0], g_ref[...], mod[1:2], mod[0:1]).astype(BF16)
    dot = lambda lo, hi: jnp.dot(hn, w_ref[:, lo:hi], preferred_element_type=F32)
    nq = q_ref.shape[2]
    nkv = k_ref.shape[2]
    hr = lax.broadcasted_iota(jnp.int32, (nq, nq), 0) // HEAD_DIM
    hc = lax.broadcasted_iota(jnp.int32, (nq, nq), 1) // HEAD_DIM
    ones_blk = jnp.where(hr == hc, 1.0, 0.0).astype(BF16)
    q_ref[0] = _head_rms(dot(0, nq), ones_blk, qg_ref[...])
    k_ref[0] = _head_rms(dot(nq, nq + nkv), ones_blk[0:nkv, 0:nkv], kg_ref[...])
    o = nq + nkv
    v_ref[0] = dot(o, o + nkv)
    o += nkv
    zg_ref[0] = dot(o, o + zg_ref.shape[2])
    o += zg_ref.shape[2]
    xbc_ref[0] = dot(o, o + xbc_ref.shape[2])
    o += xbc_ref.shape[2]
    dt_ref[0] = _softplus(dot(o, o + LANES) + dtb_ref[...])


def _od_in(x, mod, norm_g, w_in, q_norm, k_norm, dt_bias, dims, tb):
    b, t, d = x.shape
    nq, nkv, nz, nxbc, nh = dims
    w = jnp.concatenate([w_in, jnp.zeros((d, LANES - nh), F32)], axis=1).astype(BF16)
    qg = jnp.tile(q_norm, nq // HEAD_DIM).reshape(1, nq)
    kg = jnp.tile(k_norm, nkv // HEAD_DIM).reshape(1, nkv)
    dtb = jnp.concatenate([dt_bias, jnp.zeros((LANES - nh,), F32)]).reshape(1, LANES)
    g2 = norm_g.reshape(1, d)
    tile = lambda n: pl.BlockSpec((1, tb, n), lambda i, j: (i, j, 0))
    const = lambda a: pl.BlockSpec(a.shape, lambda i, j: (0,) * a.ndim)
    widths = (nq, nkv, nkv, nz, nxbc, LANES)
    return pl.pallas_call(
        _od_in_kernel,
        grid=(b, t // tb),
        in_specs=[tile(d), pl.BlockSpec((1, 6, d), lambda i, j: (i, 0, 0)), const(g2), const(w),
                  const(qg), const(kg), const(dtb)],
        out_specs=[tile(n) for n in widths],
        out_shape=[jax.ShapeDtypeStruct((b, t, n), F32) for n in widths],
        compiler_params=_params(2),
        name="od_in",
    )(x, mod, g2, w, qg, kg, dtb)


def _swa_kernel(q_ref, k_ref, v_ref, k0_ref, v0_ref, bias_ref, sink_ref, o_ref, kx, vx,
                *, n_blocks, nq, mask_start):
    t = pl.program_id(1)
    tb = q_ref.shape[1]
    n_kv = k_ref.shape[2] // HEAD_DIM
    rows_q = nq * CHUNK
    tail = kx.shape[1] - WINDOW - tb
    left = lax.broadcasted_iota(jnp.int32, (1, LANES), 1) < HEAD_DIM
    ones = jnp.ones((SWA_KEYS, LANES), BF16)
    col = lax.broadcasted_iota(jnp.int32, (1, SWA_KEYS), 1)

    def place(dst, rows, x):
        rolled = pltpu.roll(x, HEAD_DIM, axis=1)
        dst[0, rows, :] = jnp.where(left, x, 0.0).astype(BF16)
        dst[1, rows, :] = jnp.where(left, 0.0, rolled).astype(BF16)
        dst[2, rows, :] = jnp.where(left, rolled, 0.0).astype(BF16)
        dst[3, rows, :] = jnp.where(left, 0.0, x).astype(BF16)

    @pl.when(t == 0)
    def _():
        place(kx, slice(0, WINDOW), k0_ref[0])
        place(vx, slice(0, WINDOW), v0_ref[0])
        for i in range(2 * n_kv if tail else 0):
            kx[i, WINDOW + tb:, :] = jnp.zeros((tail, LANES), BF16)
            vx[i, WINDOW + tb:, :] = jnp.zeros((tail, LANES), BF16)

    place(kx, slice(WINDOW, WINDOW + tb), k_ref[0])
    place(vx, slice(WINDOW, WINDOW + tb), v_ref[0])

    def block(blk, carry):
        r0 = pl.multiple_of(blk * rows_q, rows_q)
        keys = pl.ds(r0, SWA_KEYS)
        if mask_start:
            valid = t * tb + r0 - WINDOW + col >= 0
        probs = []
        sinks = []
        for j in range(n_kv):
            lo = 2 * j * LANES
            qg = jnp.concatenate([q_ref[0, pl.ds(r0 + cq * CHUNK, CHUNK), lo + r * LANES:lo + (r + 1) * LANES]
                                  for cq in range(nq) for r in range(2)], axis=0).astype(BF16)
            kcat = jnp.concatenate([kx[2 * j, keys, :], kx[2 * j + 1, keys, :]], axis=0)
            s_both = lax.dot_general(qg, kcat, _NT, preferred_element_type=F32) * HEAD_DIM ** -0.5
            for side in range(2):
                s = s_both[:, side * SWA_KEYS:(side + 1) * SWA_KEYS] + bias_ref[j, side]
                if mask_start:
                    s = jnp.where(valid, s, NEG_INF)
                m = jnp.max(s, axis=-1, keepdims=True)
                probs.append(jnp.exp(s - m).astype(BF16))
                sinks.append(jnp.exp(sink_ref[j, side] - m))
        den = jnp.dot(jnp.concatenate(probs, axis=0), ones, preferred_element_type=F32)
        nr = 2 * rows_q
        for j in range(n_kv):
            lo = 2 * j * LANES
            vcat = jnp.concatenate([vx[2 * j, keys, :], vx[2 * j + 1, keys, :]], axis=0)
            pv = jnp.dot(jnp.concatenate(probs[2 * j:2 * j + 2], axis=1), vcat, preferred_element_type=F32)
            d_l = den[(2 * j) * nr:(2 * j + 1) * nr] + sinks[2 * j]
            d_r = den[(2 * j + 1) * nr:(2 * j + 2) * nr] + sinks[2 * j + 1]
            out = pv * jnp.where(left, 1.0 / d_l, 1.0 / d_r)
            for cq in range(nq):
                for r in range(2):
                    o_ref[0, pl.ds(r0 + cq * CHUNK, CHUNK), lo + r * LANES:lo + (r + 1) * LANES] = (
                        out[(2 * cq + r) * CHUNK:(2 * cq + r + 1) * CHUNK])
        return carry

    lax.fori_loop(0, n_blocks, block, 0, unroll=math.gcd(n_blocks, 2))
    for i in range(2 * n_kv):
        kx[i, 0:WINDOW, :] = kx[i, tb:tb + WINDOW, :]
        vx[i, 0:WINDOW, :] = vx[i, tb:tb + WINDOW, :]


def _t5_bucket(rel):
    nb = T5_BUCKETS // 2
    max_exact = nb // 2
    ret = (rel > 0).astype(jnp.int32) * nb
    n = jnp.abs(rel)
    nf = jnp.maximum(n, 1).astype(F32)
    large = max_exact + (jnp.log(nf / max_exact) / math.log(T5_MAX_DIST / max_exact)
                         * (nb - max_exact)).astype(jnp.int32)
    large = jnp.minimum(large, nb - 1)
    return ret + jnp.where(n < max_exact, n, large)


def _rel_bias(table):
    rel = (jnp.arange(WINDOW + CHUNK)[None, :] - WINDOW) - jnp.arange(CHUNK)[:, None]
    onehot = (_t5_bucket(rel)[..., None] == jnp.arange(T5_BUCKETS)).astype(F32)
    return jnp.einsum('qkb,bh->hqk', onehot, table, precision=_HI)


def _swa(q, k, v, k0, v0, bias, sink, tb, mask_start):
    b, t, nq = q.shape
    nkv = k.shape[2]
    n_kv = nkv // HEAD_DIM
    assert nkv == LANES and nq == 2 * n_kv * LANES
    by_side = lambda a: jnp.transpose(a.reshape((n_kv, 2, 2) + a.shape[1:]), (0, 2, 1) + tuple(range(3, a.ndim + 2)))
    span = WINDOW + CHUNK
    n_chunks = tb // CHUNK
    cpb = 2 if n_chunks % 2 == 0 else 1
    assert WINDOW + cpb * CHUNK <= SWA_KEYS
    bias2 = by_side(bias).reshape(n_kv, 2, 2 * CHUNK, span)
    bias3 = jnp.concatenate(
        [jnp.pad(bias2, ((0, 0), (0, 0), (0, 0), (cq * CHUNK, SWA_KEYS - span - cq * CHUNK)),
                 constant_values=NEG_INF) for cq in range(cpb)], axis=2)
    sink3 = jnp.tile(jnp.repeat(by_side(sink), CHUNK, axis=2), (1, 1, cpb))
    sink3 = jnp.broadcast_to(sink3[..., None], sink3.shape + (LANES,))
    tile = lambda n: pl.BlockSpec((1, tb, n), lambda i, j: (i, j, 0))
    wspec = pl.BlockSpec((1, WINDOW, nkv), lambda i, j: (i, 0, 0))
    const = lambda a: pl.BlockSpec(a.shape, lambda i, j: (0,) * a.ndim)
    rows = WINDOW + max(tb, WINDOW) + SWA_KEYS - WINDOW - cpb * CHUNK
    return pl.pallas_call(
        functools.partial(_swa_kernel, n_blocks=n_chunks // cpb, nq=cpb, mask_start=mask_start),
        grid=(b, t // tb),
        in_specs=[tile(nq), tile(nkv), tile(nkv), wspec, wspec, const(bias3), const(sink3)],
        out_specs=tile(nq),
        out_shape=jax.ShapeDtypeStruct((b, t, nq), F32),
        scratch_shapes=[pltpu.VMEM((2 * n_kv, rows, LANES), BF16), pltpu.VMEM((2 * n_kv, rows, LANES), BF16)],
        compiler_params=_params(2),
        name="swa",
    )(q, k, v, k0, v0, bias3, sink3)


def _ssd_kernel(xbc_ref, zg_ref, dt_ref, pre_ref, cw_ref, cb_ref, aexp_ref, dexp_ref,
                e_ref, ng_ref, s0_ref, y_ref, sfin_ref, clast_ref, st_scr, xin_scr, dte_scr,
                *, n_chunks):
    t = pl.program_id(1)
    tb = xbc_ref.shape[1]
    inner = zg_ref.shape[2]
    n_heads = inner // SSD_HEAD_DIM
    hpg = n_heads // SSD_GROUPS
    gn = SSD_GROUPS * SSD_STATE
    n_hs = n_heads * CHUNK

    @pl.when(t == 0)
    def _():
        st_scr[...] = s0_ref[0]
        xin_scr[0:SUBLANES, :] = pre_ref[0]

    xin_scr[SUBLANES:, :] = xbc_ref[0]
    dte_scr[...] = _split_dot(dt_ref[0], e_ref[...])

    row_hs = lax.broadcasted_iota(jnp.int32, (n_hs, 1), 0)
    own_group = row_hs // (hpg * CHUNK) == lax.broadcasted_iota(jnp.int32, (1, gn), 1) // SSD_STATE
    own_head = row_hs // CHUNK == lax.broadcasted_iota(jnp.int32, (1, inner), 1) // SSD_HEAD_DIM
    state_group = (lax.broadcasted_iota(jnp.int32, (gn, 1), 0) // SSD_STATE
                   == lax.broadcasted_iota(jnp.int32, (1, inner), 1) // (hpg * SSD_HEAD_DIM))
    first_group = lax.broadcasted_iota(jnp.int32, (1, inner), 1) < hpg * SSD_HEAD_DIM
    step_row = lax.broadcasted_iota(jnp.int32, (CHUNK, 1), 0)
    step_lane = lax.broadcasted_iota(jnp.int32, (1, n_hs), 1) % CHUNK
    zero = jnp.zeros((), BF16)
    aexp = aexp_ref[...]

    def chunk(c, carry):
        r0 = pl.multiple_of(c * CHUNK, CHUNK)
        rows = pl.ds(r0, CHUNK)
        xc = _silu(_causal_conv(xin_scr[pl.ds(r0 + SUBLANES, CHUNK), :], xin_scr[pl.ds(r0, SUBLANES), :],
                                cw_ref[...], cb_ref[...]))
        xs = xc[:, 0:inner]
        bm = xc[:, inner:inner + gn]
        cm = xc[:, inner + gn:inner + 2 * gn].astype(BF16)
        dte = dte_scr[rows, :]
        cum = _cumsum_rows(dte * aexp)
        cum_last = cum[CHUNK - 1:CHUNK, :]
        cum_at_step = jnp.sum(jnp.where(step_row == step_lane, cum, 0.0), axis=0, keepdims=True)
        decay = jnp.exp(jnp.where(step_row >= step_lane, cum - cum_at_step, NEG_INF))
        bm_bd = jnp.where(own_group, jnp.concatenate([bm.astype(BF16)] * n_heads, axis=0), zero)
        cb = lax.dot_general(cm, bm_bd, _NT, preferred_element_type=F32)
        xdt = (xs * dte).astype(BF16)
        xdt_bd = jnp.where(own_head, jnp.concatenate([xdt] * n_heads, axis=0), zero)
        st = st_scr[...]
        st_bd = jnp.where(state_group, jnp.concatenate([st.astype(BF16)] * SSD_GROUPS, axis=0), zero)
        y = (jnp.dot((cb * decay).astype(BF16), xdt_bd, preferred_element_type=F32)
             + jnp.exp(cum) * jnp.dot(cm, st_bd, preferred_element_type=F32))
        xw = (xs * (jnp.exp(cum_last - cum) * dte)).astype(BF16)
        upd = jnp.dot(bm.T.astype(BF16), xw, preferred_element_type=F32)
        st_scr[...] = st * jnp.exp(cum_last) + jnp.where(first_group, upd[0:SSD_STATE], upd[SSD_STATE:])
        yd = (y + dexp_ref[...] * xs) * _silu(zg_ref[0, rows, :])
        ms = jnp.mean(yd * yd, axis=-1, keepdims=True)
        y_ref[0, rows, :] = yd * lax.rsqrt(ms + RMS_EPS) * ng_ref[...]
        return carry

    lax.fori_loop(0, n_chunks, chunk, 0, unroll=math.gcd(n_chunks, CHUNK_UNROLL))
    xin_scr[0:SUBLANES, :] = xin_scr[tb:tb + SUBLANES, :]

    @pl.when(t == pl.num_programs(1) - 1)
    def _():
        sfin_ref[0] = st_scr[...]
        clast_ref[0] = xin_scr[0:SUBLANES, :]


def _ssd(xbc, zg, dt, conv_prefix, conv_w, conv_b, a_log, d_skip, norm_g, s0, tb):
    b, t, nxbc = xbc.shape
    inner = zg.shape[2]
    nh = a_log.shape[0]
    width = conv_w.shape[0]
    assert SSD_GROUPS == 2 and SSD_HEAD_DIM == CHUNK
    pre8 = jnp.concatenate([jnp.zeros((b, SUBLANES - (width - 1), nxbc), F32), conv_prefix], axis=1)
    a = -jnp.exp(a_log)
    aexp = jnp.repeat(a, SSD_HEAD_DIM).reshape(1, inner)
    dexp = jnp.repeat(d_skip, SSD_HEAD_DIM).reshape(1, inner)
    expand = (np.arange(LANES)[:, None] == np.arange(inner)[None, :] // SSD_HEAD_DIM)
    expand = jnp.asarray(expand, BF16)
    s0t = jnp.transpose(s0, (0, 3, 1, 2)).reshape(b, SSD_STATE, inner)
    ng = norm_g.reshape(1, inner)
    cb2 = conv_b.reshape(1, nxbc)
    tile = lambda n: pl.BlockSpec((1, tb, n), lambda i, j: (i, j, 0))
    const = lambda arr: pl.BlockSpec(arr.shape, lambda i, j: (0,) * arr.ndim)
    bspec = lambda arr: pl.BlockSpec((1,) + arr.shape[1:], lambda i, j: (i,) + (0,) * (arr.ndim - 1))
    y, sfin, clast = pl.pallas_call(
        functools.partial(_ssd_kernel, n_chunks=tb // CHUNK),
        grid=(b, t // tb),
        in_specs=[tile(nxbc), tile(inner), tile(LANES),
                  bspec(pre8), const(conv_w), const(cb2), const(aexp), const(dexp),
                  const(expand), const(ng), bspec(s0t)],
        out_specs=[tile(inner), bspec(s0t), bspec(pre8)],
        out_shape=[jax.ShapeDtypeStruct((b, t, inner), F32), jax.ShapeDtypeStruct(s0t.shape, F32),
                   jax.ShapeDtypeStruct(pre8.shape, F32)],
        scratch_shapes=[pltpu.VMEM((SSD_STATE, inner), F32), pltpu.VMEM((SUBLANES + tb, nxbc), F32),
                        pltpu.VMEM((tb, inner), F32)],
        compiler_params=_params(2),
        name="ssd",
    )(xbc, zg, dt, pre8, conv_w, cb2, aexp, dexp, expand, ng, s0t)
    sfin = jnp.transpose(sfin.reshape(b, SSD_STATE, nh, SSD_HEAD_DIM), (0, 2, 3, 1))
    return y, sfin, clast[:, SUBLANES - (width - 1):]


def _trunk(x, mods, P, st, sample):
    b, t, d = x.shape
    tb = min(MAX_TILE, t)
    depth = P['w_mod'].shape[0]
    new = {name: [] for name in ('s5_re', 's5_im', 'gla', 'swa_k', 'swa_v', 'ssd', 'ssd_conv', 'ffn_conv')}
    for layer in range(depth):
        i = layer // 2
        mod = mods[layer].reshape(b, 6, d)
        ffn = (P['norm2_g'][layer], P['ffn_w_up'][layer], P['ffn_conv_w'][layer], P['ffn_conv_b'][layer],
               P['ffn_w_down'][layer], st['ffn_conv'][layer], tb)
        if layer % 2 == 0:
            u, qk, v, r, gate = _ev_in(x, mod, P['norm1_g'][layer], P['ev_w_in'][i], P['gla_w_gate2'][i],
                                       P['gla_b_gate'][i], tb)
            mats = _s5_matrices(P['s5_a_re'][i], P['s5_a_im'][i], P['s5_log_dt'][i], P['s5_b_re'][i],
                                P['s5_b_im'][i], P['s5_c_re'][i], P['s5_c_im'][i])
            ya, sr, si = _s5(u, mats, P['s5_d'][i], st['s5_re'][i], st['s5_im'][i])
            ob, sg = _gla(qk, v, gate, r, st['gla'][i], P['gla_norm_g'][i], tb)
            new['s5_re'].append(sr)
            new['s5_im'].append(si)
            new['gla'].append(sg)
            x, fp = _out_ffn(x, mod, ya, ob, P['ev_w_out'][i], *ffn,
                             s5_extra=(P['s5_w_glu'][i], P['s5_b_glu'][i]))
        else:
            nq = P['swa_sink'].shape[1] * HEAD_DIM
            nkv = SWA_KV_HEADS * HEAD_DIM
            inner = P['ssd_norm_g'].shape[1]
            nxbc = P['ssd_conv_w'].shape[2]
            nh = P['ssd_a_log'].shape[1]
            q, k, v, zg, xbc, dt = _od_in(x, mod, P['norm1_g'][layer], P['od_w_in'][i], P['swa_q_norm'][i],
                                               P['swa_k_norm'][i], P['ssd_dt_bias'][i],
                                               (nq, nkv, inner, nxbc, nh), tb)
            bias = _rel_bias(P['t5_bias'])
            if sample:
                k0 = st['swa_k'][i].reshape(b, WINDOW, nkv)
                v0 = st['swa_v'][i].reshape(b, WINDOW, nkv)
            else:
                k0 = v0 = jnp.zeros((b, WINDOW, nkv), F32)
            oc = _swa(q, k, v, k0, v0, bias, P['swa_sink'][i], tb, mask_start=not sample)
            yd, ss, sc = _ssd(xbc, zg, dt, st['ssd_conv'][i], P['ssd_conv_w'][i], P['ssd_conv_b'][i],
                              P['ssd_a_log'][i], P['ssd_d'][i], P['ssd_norm_g'][i], st['ssd'][i], tb)
            keep = slice(None) if sample else slice(t - WINDOW, t)
            new['swa_k'].append(k[:, keep].reshape(b, -1, SWA_KV_HEADS, HEAD_DIM))
            new['swa_v'].append(v[:, keep].reshape(b, -1, SWA_KV_HEADS, HEAD_DIM))
            new['ssd'].append(ss)
            new['ssd_conv'].append(sc)
            x, fp = _out_ffn(x, mod, oc, yd, P['od_w_out'][i], *ffn)
        new['ffn_conv'].append(fp)
    return x, {name: jnp.stack(vals) for name, vals in new.items()}


def kernel(x_prompt, x_sample, state_s5_re, state_s5_im, state_gla, cache_swa_k, cache_swa_v, state_ssd, state_ssd_conv, state_ffn_conv, c_prompt, c_sample, t5_bias, norm1_g, norm2_g, w_mod, b_mod, ffn_w_up, ffn_conv_w, ffn_conv_b, ffn_w_down, ev_w_in, ev_w_out, s5_a_re, s5_a_im, s5_log_dt, s5_b_re, s5_b_im, s5_c_re, s5_c_im, s5_d, s5_w_glu, s5_b_glu, gla_w_gate2, gla_b_gate, gla_norm_g, od_w_in, od_w_out, swa_q_norm, swa_k_norm, swa_sink, ssd_conv_w, ssd_conv_b, ssd_dt_bias, ssd_a_log, ssd_d, ssd_norm_g):
    P = dict(t5_bias=t5_bias, norm1_g=norm1_g, norm2_g=norm2_g, w_mod=w_mod, b_mod=b_mod,
             ffn_w_up=ffn_w_up, ffn_conv_w=ffn_conv_w, ffn_conv_b=ffn_conv_b, ffn_w_down=ffn_w_down,
             ev_w_in=ev_w_in, ev_w_out=ev_w_out, s5_a_re=s5_a_re, s5_a_im=s5_a_im, s5_log_dt=s5_log_dt,
             s5_b_re=s5_b_re, s5_b_im=s5_b_im, s5_c_re=s5_c_re, s5_c_im=s5_c_im, s5_d=s5_d,
             s5_w_glu=s5_w_glu, s5_b_glu=s5_b_glu, gla_w_gate2=gla_w_gate2, gla_b_gate=gla_b_gate,
             gla_norm_g=gla_norm_g, od_w_in=od_w_in, od_w_out=od_w_out, swa_q_norm=swa_q_norm,
             swa_k_norm=swa_k_norm, swa_sink=swa_sink, ssd_conv_w=ssd_conv_w, ssd_conv_b=ssd_conv_b,
             ssd_dt_bias=ssd_dt_bias, ssd_a_log=ssd_a_log, ssd_d=ssd_d, ssd_norm_g=ssd_norm_g)
    bp = x_prompt.shape[0]
    n_even, n_odd = state_s5_re.shape[0], state_ssd.shape[0]
    depth = w_mod.shape[0]
    zeros_like_b = lambda a: jnp.zeros((a.shape[0], bp) + a.shape[2:], F32)
    zero_st = dict(s5_re=zeros_like_b(state_s5_re), s5_im=zeros_like_b(state_s5_im), gla=zeros_like_b(state_gla),
                   ssd=zeros_like_b(state_ssd), ssd_conv=zeros_like_b(state_ssd_conv),
                   ffn_conv=zeros_like_b(state_ffn_conv))
    sample_st = dict(s5_re=state_s5_re, s5_im=state_s5_im, gla=state_gla, swa_k=cache_swa_k,
                     swa_v=cache_swa_v, ssd=state_ssd, ssd_conv=state_ssd_conv, ffn_conv=state_ffn_conv)
    mods = _modulation(jnp.concatenate([c_prompt, c_sample], axis=0), w_mod, b_mod)
    y_prompt, stp = _trunk(x_prompt, mods[:, :bp], P, zero_st, False)
    y_sample, sts = _trunk(x_sample, mods[:, bp:], P, sample_st, True)
    names = ('s5_re', 's5_im', 'gla', 'swa_k', 'swa_v', 'ssd', 'ssd_conv', 'ffn_conv')
    return (y_prompt, y_sample) + tuple(stp[n] for n in names) + tuple(sts[n] for n in names)
```

```python
import functools
import math

import jax
import jax.numpy as jnp
import numpy as np
from jax import lax
from jax.experimental import pallas as pl
from jax.experimental.pallas import tpu as pltpu

F32 = jnp.float32
BF16 = jnp.bfloat16

CHUNK = 64
WINDOW = 128
S5_GROUP = 16
S5_STATE = 64
S5_SUB = 8
S5_MAX_SUBS = 64
GLA_HEADS = 4
GLA_GATE_NORM = 16.0
HEAD_DIM = 64
SWA_KV_HEADS = 2
SWA_KEYS = 256
SSD_HEAD_DIM = 64
SSD_STATE = 128
SSD_GROUPS = 2
T5_BUCKETS = 32
T5_MAX_DIST = 128
RMS_EPS = 1e-6
NEG_INF = -1e30
LANES = 128
SUBLANES = 8
assert S5_SUB == SUBLANES and S5_GROUP * SUBLANES == LANES
MAX_TILE = 512
CHUNK_UNROLL = 8
FFN_COL_BLOCK = 256
VMEM_LIMIT = 56 * 1024 * 1024

_NT = (((1,), (1,)), ((), ()))
_HI = lax.Precision.HIGHEST


def _params(n_axes=2):
    sem = ("parallel",) + ("arbitrary",) * (n_axes - 1)
    return pltpu.CompilerParams(dimension_semantics=sem, vmem_limit_bytes=VMEM_LIMIT)


def _bdot(a, b):
    return jnp.dot(a.astype(BF16), b.astype(BF16), preferred_element_type=F32)


def _split_dot(x, c, parts=3):
    pieces = []
    rest = x
    for _ in range(parts):
        piece = rest.astype(BF16)
        pieces.append(piece)
        rest = rest - piece.astype(F32)
    n = x.shape[0]
    d = jnp.dot(jnp.concatenate(pieces, axis=0), c, preferred_element_type=F32)
    out = d[0:n]
    for i in range(1, parts):
        out = out + d[i * n:(i + 1) * n]
    return out


def _silu(x):
    return x * jax.nn.sigmoid(x)


def _softplus(x):
    return jnp.maximum(x, 0.0) + jnp.log1p(jnp.exp(-jnp.abs(x)))


def _log_sigmoid(x):
    return jnp.minimum(x, 0.0) - jnp.log1p(jnp.exp(-jnp.abs(x)))


def _norm_mod(x, g, scale, shift):
    ms = jnp.mean(x * x, axis=-1, keepdims=True)
    return (x * lax.rsqrt(ms + RMS_EPS) * g) * (1.0 + scale) + shift


def _cumsum_rows(x):
    n, m = x.shape
    tiles = x.reshape(n // SUBLANES, SUBLANES, m)
    sub = lax.broadcasted_iota(jnp.int32, (1, SUBLANES, 1), 1)
    d = 1
    while d < SUBLANES:
        tiles = tiles + jnp.where(sub >= d, pltpu.roll(tiles, d, axis=1), 0.0)
        d *= 2
    out = [tiles[0]]
    for i in range(1, n // SUBLANES):
        out.append(tiles[i] + out[-1][SUBLANES - 1:SUBLANES, :])
    return jnp.concatenate(out, axis=0)


def _causal_conv(u, prev8, w, b):
    width = w.shape[0]
    n = u.shape[0]
    ext = jnp.concatenate([prev8, u[0:SUBLANES]], axis=0)
    full = b
    head = b
    for j in range(width):
        sh = width - 1 - j
        if sh == 0:
            full = full + u * w[j:j + 1]
            head = head + u[0:SUBLANES] * w[j:j + 1]
        else:
            full = full + pltpu.roll(u, sh, axis=0) * w[j:j + 1]
            head = head + pltpu.roll(ext, sh, axis=0)[SUBLANES:2 * SUBLANES] * w[j:j + 1]
    if n == SUBLANES:
        return head
    return jnp.concatenate([head, full[SUBLANES:]], axis=0)


def _segment_pitch(n):
    pitch = n // SUBLANES + SUBLANES
    return pitch if (pitch // SUBLANES) % 2 else pitch + SUBLANES


def _to_segments(ref, val):
    nv = val.shape[0] // SUBLANES
    pitch = ref.shape[1] // SUBLANES
    for c in range(ref.shape[0]):
        for s in range(SUBLANES):
            ref[c, s * pitch:s * pitch + nv, :] = val[s * nv:(s + 1) * nv, c * LANES:(c + 1) * LANES]


def _from_segments(ref, n):
    nv = n // SUBLANES
    pitch = ref.shape[1] // SUBLANES
    return jnp.concatenate(
        [jnp.concatenate([ref[c, s * pitch:s * pitch + nv, :] for s in range(SUBLANES)], axis=0)
         for c in range(ref.shape[0])], axis=1)


def _rows_from_segments(ref, r0, n_rows, n):
    nv = n // SUBLANES
    pitch = ref.shape[1] // SUBLANES
    spans = []
    r = r0
    while r < r0 + n_rows:
        s, off = divmod(r, nv)
        take = min(nv - off, r0 + n_rows - r)
        spans.append(slice(s * pitch + off, s * pitch + off + take))
        r += take
    return jnp.concatenate(
        [jnp.concatenate([ref[c, sp, :] for sp in spans], axis=0) for c in range(ref.shape[0])], axis=1)


def _load_interleaved(ref, n):
    pitch = ref.shape[1] // SUBLANES
    return jnp.concatenate(
        [jnp.concatenate([ref[c, pl.ds(i, SUBLANES, stride=pitch), :] for c in range(ref.shape[0])], axis=1)
         for i in range(n // SUBLANES)], axis=0)


def _store_interleaved(ref, val):
    pitch = ref.shape[1] // SUBLANES
    for i in range(val.shape[0] // SUBLANES):
        for c in range(ref.shape[0]):
            ref[c, pl.ds(i, SUBLANES, stride=pitch), :] = val[i * SUBLANES:(i + 1) * SUBLANES, c * LANES:(c + 1) * LANES]


def _causal_conv_interleaved(u, prev8, w, b):
    width = w.shape[0]
    n, m = u.shape
    nv = n // SUBLANES
    sub = lax.broadcasted_iota(jnp.int32, (SUBLANES, 1), 0)
    tiles = u.reshape(nv, SUBLANES, m)

    def back_one(a, before):
        first = jnp.where(sub == 0, before, pltpu.roll(a[nv - 1], 1, axis=0))
        return jnp.concatenate([first[None], a[:nv - 1]], axis=0)

    delayed = [tiles]
    for k in range(1, width):
        delayed.append(back_one(delayed[-1], prev8[SUBLANES - k:SUBLANES - k + 1]))
    out = b
    for j in range(width):
        out = out + delayed[width - 1 - j] * w[j:j + 1]
    hist = tiles[nv - 1]
    for k in range(2, width):
        hist = jnp.where(sub == SUBLANES - k, pltpu.roll(tiles[nv - k], SUBLANES - k + 1, axis=0), hist)
    return out.reshape(n, m), hist


def _mod_kernel(c_ref, w_ref, b_ref, o_ref):
    o_ref[0] = _bdot(_silu(c_ref[...]), w_ref[0]) + b_ref[0]


def _modulation(c, w_mod, b_mod):
    depth, d, n = w_mod.shape
    bc = c.shape[0]
    tn = n // 4
    return pl.pallas_call(
        _mod_kernel,
        grid=(depth, n // tn),
        in_specs=[pl.BlockSpec((bc, d), lambda l, j: (0, 0)),
                  pl.BlockSpec((1, d, tn), lambda l, j: (l, 0, j)),
                  pl.BlockSpec((1, 1, tn), lambda l, j: (l, 0, j))],
        out_specs=pl.BlockSpec((1, bc, tn), lambda l, j: (l, 0, j)),
        out_shape=jax.ShapeDtypeStruct((depth, bc, n), F32),
        compiler_params=_params(2),
        name="modulation",
    )(c, w_mod, b_mod.reshape(depth, 1, n))


def _ev_in_kernel(x_ref, mod_ref, g_ref, w_ref, wg2_ref, bg_ref,
                  u_ref, qk_ref, v_ref, r_ref, gate_ref):
    mod = mod_ref[0]
    hn = _norm_mod(x_ref[0], g_ref[...], mod[1:2], mod[0:1]).astype(BF16)
    dot = lambda lo, hi: jnp.dot(hn, w_ref[:, lo:hi], preferred_element_type=F32)
    u = dot(0, 512)
    for c in range(u_ref.shape[0]):
        u_ref[c] = _chunk_transpose(u[:, c * LANES:(c + 1) * LANES].reshape(-1, SUBLANES, LANES))
    qk_ref[0] = dot(512, 1024)
    v_ref[0] = dot(1024, 1536)
    r_ref[0] = dot(1536, 2048)
    gl = dot(2048, 2048 + LANES)
    gate_ref[0] = _log_sigmoid(_bdot(gl, wg2_ref[...]) + bg_ref[...]) * (1.0 / GLA_GATE_NORM)


def _ev_in(x, mod, norm_g, w_in, w_gate2, b_gate, tb):
    b, t, d = x.shape
    rank = w_gate2.shape[0]
    nk = w_gate2.shape[1]
    wu, wq, wk, wv, wgl, wr = jnp.split(w_in, [512, 768, 1024, 1536, 1536 + rank], axis=1)
    w = jnp.concatenate([wu, wq, wk, wv, wr, wgl, jnp.zeros((d, LANES - rank), F32)], axis=1).astype(BF16)
    wg2 = jnp.concatenate([w_gate2, jnp.zeros((LANES - rank, nk), F32)], axis=0).astype(BF16)
    tile = lambda n: pl.BlockSpec((1, tb, n), lambda i, j: (i, j, 0))
    const = lambda a: pl.BlockSpec(a.shape, lambda i, j: (0,) * a.ndim)
    g2 = norm_g.reshape(1, d)
    bg = b_gate.reshape(1, nk)
    return pl.pallas_call(
        _ev_in_kernel,
        grid=(b, t // tb),
        in_specs=[tile(d), pl.BlockSpec((1, 6, d), lambda i, j: (i, 0, 0)), const(g2), const(w),
                  const(wg2), const(bg)],
        out_specs=[pl.BlockSpec((512 // LANES, tb // S5_SUB, None, SUBLANES, LANES), lambda i, j: (0, j, i, 0, 0)),
                   tile(512), tile(512), tile(512), tile(nk)],
        out_shape=[jax.ShapeDtypeStruct((512 // LANES, t // S5_SUB, b, SUBLANES, LANES), F32)]
        + [jax.ShapeDtypeStruct((b, t, n), F32) for n in (512, 512, 512, nk)],
        compiler_params=_params(2),
        name="ev_in",
    )(x, mod, g2, w, wg2, bg)


def _gla_kernel(qk_ref, v_ref, gate_ref, r_ref, s0_ref, ng_ref, o_ref, sfin_ref, st_scr, *, n_chunks):
    t = pl.program_id(1)
    dk = qk_ref.shape[2] // 2 // GLA_HEADS
    dv = v_ref.shape[2] // GLA_HEADS
    nk = GLA_HEADS * dk
    nv = GLA_HEADS * dv
    blk = (lax.broadcasted_iota(jnp.int32, (nv, nk), 0) // dv
           == lax.broadcasted_iota(jnp.int32, (nv, nk), 1) // dk)

    @pl.when(t == 0)
    def _():
        s0 = jnp.concatenate([s0_ref[0, h] for h in range(GLA_HEADS)], axis=0)
        st_scr[...] = jnp.where(blk, jnp.concatenate([s0] * GLA_HEADS, axis=1), 0.0)

    n_hs = GLA_HEADS * CHUNK
    row_head = lax.broadcasted_iota(jnp.int32, (n_hs, 1), 0) // CHUNK
    own_k = row_head == lax.broadcasted_iota(jnp.int32, (1, nk), 1) // dk
    own_v = row_head == lax.broadcasted_iota(jnp.int32, (1, nv), 1) // dv
    causal = (lax.broadcasted_iota(jnp.int32, (CHUNK, 1), 0)
              >= lax.broadcasted_iota(jnp.int32, (1, n_hs), 1) % CHUNK)
    ng = ng_ref[...]

    def chunk(c, carry):
        rows = pl.ds(pl.multiple_of(c * CHUNK, CHUNK), CHUNK)
        k = qk_ref[0, rows, nk:2 * nk]
        v = v_ref[0, rows, :]
        cum = _cumsum_rows(gate_ref[0, rows, :])
        cum_last = cum[CHUNK - 1:CHUNK, :]
        qe = (qk_ref[0, rows, 0:nk] * dk ** -0.5 * jnp.exp(cum)).astype(BF16)
        ke = (k * jnp.exp(-cum)).astype(BF16)
        kd = (k * jnp.exp(cum_last - cum)).astype(BF16)
        v_b = v.astype(BF16)
        st = st_scr[...]
        ke_bd = jnp.where(own_k, jnp.concatenate([ke] * GLA_HEADS, axis=0), jnp.zeros((), BF16))
        v_bd = jnp.where(own_v, jnp.concatenate([v_b] * GLA_HEADS, axis=0), jnp.zeros((), BF16))
        att = lax.dot_general(qe, ke_bd, _NT, preferred_element_type=F32)
        att = jnp.where(causal, att, 0.0).astype(BF16)
        o = (jnp.dot(att, v_bd, preferred_element_type=F32)
             + lax.dot_general(qe, st.astype(BF16), _NT, preferred_element_type=F32))
        for h in range(GLA_HEADS):
            cols = slice(h * dv, (h + 1) * dv)
            oh = o[:, cols]
            ms = jnp.mean(oh * oh, axis=-1, keepdims=True)
            oh = oh * lax.rsqrt(ms + RMS_EPS) * ng
            o_ref[0, rows, cols] = oh * _silu(r_ref[0, rows, cols])
        upd = jnp.dot(v.T.astype(BF16), kd, preferred_element_type=F32)
        st_scr[...] = jnp.where(blk, st * jnp.exp(cum_last) + upd, 0.0)
        return carry

    lax.fori_loop(0, n_chunks, chunk, 0, unroll=math.gcd(n_chunks, CHUNK_UNROLL))

    @pl.when(t == pl.num_programs(1) - 1)
    def _():
        for h in range(GLA_HEADS):
            sfin_ref[0, h] = st_scr[h * dv:(h + 1) * dv, h * dk:(h + 1) * dk]


def _gla(qk, v, gate, r, s0, norm_g, tb):
    b, t, nv = v.shape
    nk = gate.shape[2]
    dk, dv = nk // GLA_HEADS, nv // GLA_HEADS
    s0t = jnp.swapaxes(s0, 2, 3)
    tile = lambda n: pl.BlockSpec((1, tb, n), lambda i, j: (i, j, 0))
    sspec = pl.BlockSpec((1, GLA_HEADS, dv, dk), lambda i, j: (i, 0, 0, 0))
    ng = norm_g.reshape(1, dv)
    o, sfin = pl.pallas_call(
        functools.partial(_gla_kernel, n_chunks=tb // CHUNK),
        grid=(b, t // tb),
        in_specs=[tile(2 * nk), tile(nv), tile(nk), tile(nv), sspec,
                  pl.BlockSpec((1, dv), lambda i, j: (0, 0))],
        out_specs=[tile(nv), sspec],
        out_shape=[jax.ShapeDtypeStruct((b, t, nv), F32),
                   jax.ShapeDtypeStruct((b, GLA_HEADS, dv, dk), F32)],
        scratch_shapes=[pltpu.VMEM((nv, nk), F32)],
        compiler_params=_params(2),
        name="gla",
    )(qk, v, gate, r, s0t, ng)
    return o, jnp.swapaxes(sfin, 2, 3)


def _chunk_transpose(x):
    s = lax.broadcasted_iota(jnp.int32, (1, SUBLANES, LANES), 1)
    c = lax.broadcasted_iota(jnp.int32, (1, SUBLANES, LANES), 2) // S5_GROUP
    for d in (4, 2, 1):
        sb = (s & d) != 0
        cb = (c & d) != 0
        if 2 * d == SUBLANES:
            t = pltpu.roll(pltpu.roll(x, d, axis=1), S5_GROUP * d, axis=2)
        else:
            xs = jnp.where(sb, pltpu.roll(x, d, axis=1), pltpu.roll(x, SUBLANES - d, axis=1))
            t = jnp.where(cb, pltpu.roll(xs, S5_GROUP * d, axis=2), pltpu.roll(xs, LANES - S5_GROUP * d, axis=2))
        x = jnp.where(sb != cb, t, x)
    return x


def _s5_kernel(u_ref, w1_ref, n_ref, a_ref, d_ref, x0_ref, y_ref, xf_ref, mm_scr, xs_scr, st_scr, *, tk, bsz):
    j = pl.program_id(1)
    n_g = st_scr.shape[0]
    rows = tk * bsz
    of_group = lambda g: pl.ds(g, rows, stride=n_g)

    @pl.when(j == 0)
    def _():
        st_scr[...] = x0_ref[0]

    for g in range(n_g):
        ub = u_ref[of_group(g), :].astype(BF16)
        mm_scr[g] = jnp.dot(ub, w1_ref[0, g], preferred_element_type=F32)

    half = n_g // 2
    for g0 in (0, half):
        coef = [a_ref[0, g] for g in range(g0, g0 + half)]

        def step(k, carry, g0=g0, coef=coef):
            r = pl.ds(pl.multiple_of(k * bsz, bsz), bsz)
            out = []
            for i in range(half):
                x, xsw = carry[2 * i], carry[2 * i + 1]
                a1, a2, a2s = coef[i][0:1], coef[i][1:2], coef[i][2:3]
                xs_scr[g0 + i, r, :] = x
                out.append(a1 * x + a2 * xsw + mm_scr[g0 + i, r, LANES:2 * LANES])
                out.append(a1 * xsw + a2s * x + mm_scr[g0 + i, r, 2 * LANES:3 * LANES])
            return tuple(out)

        fin = lax.fori_loop(0, tk, step, tuple(st_scr[g0 + i, v] for i in range(half) for v in range(2)))
        for i in range(half):
            st_scr[g0 + i, 0] = fin[2 * i]
            st_scr[g0 + i, 1] = fin[2 * i + 1]

    for g in range(n_g):
        y = (mm_scr[g, :, 0:LANES] + jnp.dot(xs_scr[g].astype(BF16), n_ref[0, g], preferred_element_type=F32)
             + d_ref[0, g] * u_ref[of_group(g), :])
        y_ref[of_group(g), :] = jax.nn.gelu(y)

    @pl.when(j == pl.num_programs(1) - 1)
    def _():
        xf_ref[0] = st_scr[:, 0]


def _cmul(ar, ai, br, bi):
    return ar * br - ai * bi, ar * bi + ai * br


def _s5_matrices(a_re, a_im, log_dt, b_re, b_im, c_re, c_im):
    g, p = a_re.shape
    hdim = b_re.shape[-1]
    dt = jnp.exp(log_dt)[:, None]
    mag = jnp.exp(a_re * dt)
    ab_re, ab_im = mag * jnp.cos(a_im * dt), mag * jnp.sin(a_im * dt)
    den = a_re * a_re + a_im * a_im
    num_re, num_im = ab_re - 1.0, ab_im
    g_re = (num_re * a_re + num_im * a_im) / den
    g_im = (num_im * a_re - num_re * a_im) / den
    bb_re = g_re[..., None] * b_re - g_im[..., None] * b_im
    bb_im = g_re[..., None] * b_im + g_im[..., None] * b_re
    pw_re, pw_im = jnp.ones((1, g, p), F32), jnp.zeros((1, g, p), F32)
    sq_re, sq_im = ab_re, ab_im
    while pw_re.shape[0] < S5_SUB + 1:
        nr, ni = _cmul(pw_re, pw_im, sq_re, sq_im)
        pw_re, pw_im = jnp.concatenate([pw_re, nr]), jnp.concatenate([pw_im, ni])
        sq_re, sq_im = _cmul(sq_re, sq_im, sq_re, sq_im)
    pw_re, pw_im = pw_re[:S5_SUB + 1], pw_im[:S5_SUB + 1]
    ca_re, ca_im = _cmul(c_re[None], c_im[None], pw_re[:, :, None, :], pw_im[:, :, None, :])
    kern = (jnp.einsum('tghp,gpk->tghk', ca_re[:S5_SUB], bb_re, precision=_HI)
            - jnp.einsum('tghp,gpk->tghk', ca_im[:S5_SUB], bb_im, precision=_HI))
    s_in = np.arange(S5_SUB)[:, None, None]
    s_out = np.arange(S5_SUB)[None, :, None]
    lag = jnp.asarray(s_out - s_in == np.arange(S5_SUB), F32)
    m = jnp.einsum('iot,tghk->ioghk', lag, kern, precision=_HI)
    m = jnp.transpose(m, (2, 0, 4, 1, 3)).reshape(g, S5_SUB * hdim, S5_SUB * hdim)
    pr, pi = _cmul(pw_re[S5_SUB - 1::-1][..., None], pw_im[S5_SUB - 1::-1][..., None], bb_re[None], bb_im[None])
    flat = lambda a: jnp.transpose(a, (1, 0, 3, 2)).reshape(g, S5_SUB * hdim, p)
    nm = jnp.concatenate([ca_re[1:], -ca_im[1:]], axis=3)
    nm = jnp.transpose(nm, (1, 3, 0, 2)).reshape(g, 2 * p, S5_SUB * hdim)
    w1 = jnp.concatenate([m, flat(pr), flat(pi), flat(pi), flat(pr)], axis=2)
    a_r, a_i = pw_re[S5_SUB], pw_im[S5_SUB]
    coef = jnp.stack([jnp.concatenate([a_r, a_r], axis=1), jnp.concatenate([-a_i, a_i], axis=1),
                      jnp.concatenate([a_i, -a_i], axis=1)], axis=1)
    return w1.astype(BF16), nm.astype(BF16), coef


def _s5(u2, mats, d_skip, x0_re, x0_im):
    w1, nm, coef = mats
    n_ct, n_sub, b, n_g, lanes = u2.shape
    g, p2 = nm.shape[0], nm.shape[1]
    hdim = lanes // S5_SUB
    tk = min(n_sub, S5_MAX_SUBS)
    x0 = jnp.transpose(jnp.concatenate([x0_re, x0_im], axis=2), (1, 0, 2))
    x0 = jnp.stack([x0, jnp.roll(x0, p2 // 2, axis=2)], axis=1)
    dt = jnp.tile(d_skip.reshape(g, 1, hdim), (1, 1, S5_SUB))
    by_tile = lambda a: a.reshape((n_ct, n_g) + a.shape[1:])
    pspec = lambda a: pl.BlockSpec((1,) + a.shape[1:], lambda c, j: (c,) + (0,) * (a.ndim - 1))
    rows = tk * b
    uspec = pl.BlockSpec((None, rows * n_g, lanes), lambda c, j: (c, j, 0))
    params = [by_tile(a) for a in (w1, nm, coef, dt, x0)]
    y, xf = pl.pallas_call(
        functools.partial(_s5_kernel, tk=tk, bsz=b),
        grid=(n_ct, n_sub // tk),
        in_specs=[uspec] + [pspec(a) for a in params],
        out_specs=[uspec, pl.BlockSpec((1, n_g, b, p2), lambda c, j: (c, 0, 0, 0))],
        out_shape=[jax.ShapeDtypeStruct((n_ct, n_sub * b * n_g, lanes), F32),
                   jax.ShapeDtypeStruct((n_ct, n_g, b, p2), F32)],
        scratch_shapes=[pltpu.VMEM((n_g, rows, 3 * lanes), F32), pltpu.VMEM((n_g, rows, lanes), F32),
                        pltpu.VMEM((n_g, 2, b, p2), F32)],
        compiler_params=_params(2),
        name="s5",
    )(u2.reshape(n_ct, n_sub * b * n_g, lanes), *params)
    xf = jnp.transpose(xf.reshape(g, b, p2), (1, 0, 2))
    return y.reshape(u2.shape), xf[:, :, :p2 // 2], xf[:, :, p2 // 2:]


def _out_ffn_kernel(*refs, with_s5, cb):
    if with_s5:
        (x_ref, mod_ref, m1_ref, m2_ref, wglu_ref, bglu_ref, wout_ref, n2g_ref,
         wup_ref, cw_ref, cbias_ref, wd_ref, pre_ref, xo_ref, ulast_ref, carry_scr, h_scr, row_scr) = refs
    else:
        (x_ref, mod_ref, m1_ref, m2_ref, wout_ref, n2g_ref,
         wup_ref, cw_ref, cbias_ref, wd_ref, pre_ref, xo_ref, ulast_ref, carry_scr, h_scr, row_scr) = refs
    t = pl.program_id(1)

    @pl.when(t == 0)
    def _():
        carry_scr[...] = pre_ref[0]

    mod = mod_ref[0]
    if with_s5:
        ya = jnp.concatenate([_chunk_transpose(m1_ref[c]).reshape(-1, LANES) for c in range(m1_ref.shape[0])],
                             axis=1)
        m1 = ya * jax.nn.sigmoid(_bdot(ya, wglu_ref[...]) + bglu_ref[...])
    else:
        m1 = m1_ref[0]
    half = m1.shape[1]
    out = (jnp.dot(m1.astype(BF16), wout_ref[0:half, :], preferred_element_type=F32)
           + jnp.dot(m2_ref[0].astype(BF16), wout_ref[half:, :], preferred_element_type=F32))
    x1 = x_ref[0] + mod[2:3] * out
    tb = x1.shape[0]
    _to_segments(row_scr, _norm_mod(x1, n2g_ref[...], mod[4:5], mod[3:4]))
    hn = _load_interleaved(row_scr, tb).astype(BF16)
    dff = wd_ref.shape[0]
    n_cb = dff // cb
    a_cols = lambda j: slice(j * cb, (j + 1) * cb)
    g_cols = lambda j: slice(dff + j * cb, dff + (j + 1) * cb)

    def up(j):
        return (jnp.dot(hn, wup_ref[:, a_cols(j)], preferred_element_type=F32),
                jnp.dot(hn, wup_ref[:, g_cols(j)], preferred_element_type=F32))

    def conv(u, cols):
        out, carry_scr[:, cols] = _causal_conv_interleaved(u, carry_scr[:, cols], cw_ref[:, cols], cbias_ref[:, cols])
        return out

    nxt = up(0)
    for j in range(n_cb):
        ua, ug = nxt
        if j + 1 < n_cb:
            nxt = up(j + 1)
        h_scr[:, a_cols(j)] = (_silu(conv(ua, a_cols(j))) * conv(ug, g_cols(j))).astype(BF16)
    _store_interleaved(row_scr, jnp.dot(h_scr[...], wd_ref[...], preferred_element_type=F32))
    xo_ref[0] = x1 + mod[5:6] * _from_segments(row_scr, tb)

    @pl.when(t == pl.num_programs(1) - 1)
    def _():
        ulast_ref[0] = carry_scr[...]


def _out_ffn(x, mod, m1, m2, w_out, norm2_g, w_up, conv_w, conv_b, w_down, prefix, tb, s5_extra=None):
    b, t, d = x.shape
    dff = w_down.shape[0]
    assert dff % FFN_COL_BLOCK == 0
    width = conv_w.shape[0]
    half = m2.shape[2]
    w_up_b = w_up.astype(BF16)
    cbias = conv_b.reshape(1, 2 * dff)
    wd = w_down.astype(BF16)
    pre8 = jnp.concatenate([jnp.zeros((b, SUBLANES - (width - 1), 2 * dff), F32), prefix], axis=1)
    tile = lambda n: pl.BlockSpec((1, tb, n), lambda i, j: (i, j, 0))
    const = lambda a: pl.BlockSpec(a.shape, lambda i, j: (0,) * a.ndim, pipeline_mode=pl.Buffered(1))
    bspec = lambda a: pl.BlockSpec((1,) + a.shape[1:], lambda i, j: (i,) + (0,) * (a.ndim - 1))
    wo = w_out.astype(BF16)
    n2g = norm2_g.reshape(1, d)
    args = [x, mod]
    specs = [tile(d), bspec(mod)]
    if s5_extra is not None:
        w_glu, b_glu = s5_extra
        w_glu, b_glu = w_glu.astype(BF16), b_glu.reshape(1, half)
        args += [m1, m2, w_glu, b_glu]
        specs += [pl.BlockSpec((m1.shape[0], tb // S5_SUB, None) + m1.shape[3:], lambda i, j: (0, j, i, 0, 0)),
                  tile(m2.shape[2]), const(w_glu), const(b_glu)]
    else:
        args += [m1, m2]
        specs += [tile(half), tile(m2.shape[2])]
    weights = [wo, n2g, w_up_b, conv_w, cbias, wd]
    args += weights + [pre8]
    specs += [const(a) for a in weights] + [bspec(pre8)]
    xo, ulast = pl.pallas_call(
        functools.partial(_out_ffn_kernel, with_s5=s5_extra is not None, cb=FFN_COL_BLOCK),
        grid=(b, t // tb),
        in_specs=specs,
        out_specs=[tile(d), bspec(pre8)],
        out_shape=[jax.ShapeDtypeStruct((b, t, d), F32), jax.ShapeDtypeStruct(pre8.shape, F32)],
        scratch_shapes=[pltpu.VMEM(pre8.shape[1:], F32), pltpu.VMEM((tb, dff), BF16),
                        pltpu.VMEM((d // LANES, SUBLANES * _segment_pitch(tb), LANES), F32)],
        compiler_params=_params(2),
        name="out_ffn_s5" if s5_extra is not None else "out_ffn",
    )(*args)
    return xo, ulast[:, SUBLANES - (width - 1):]


def _head_rms(x, ones_blk, g):
    ss = _split_dot(x * x, ones_blk, parts=2) * (1.0 / HEAD_DIM)
    return x * lax.rsqrt(ss + RMS_EPS) * g


def _od_in_kernel(x_ref, mod_ref, g_ref, w_ref, qg_ref, kg_ref, dtb_ref,
                  q_ref, k_ref, v_ref, zg_ref, xbc_ref, dt_ref):
    mod = mod_ref[0]
    nq = q_ref.shape[2]
    nkv = k_ref.shape[2]
    hr = lax.broadcasted_iota(jnp.int32, (nq, nq), 0) // HEAD_DIM
    hc = lax.broadcasted_iota(jnp.int32, (nq, nq), 1) // HEAD_DIM
    ones_blk = jnp.where(hr == hc, 1.0, 0.0).astype(BF16)
    hn = _norm_mod(x_ref[0], g_ref[...], mod[1:2], mod[0:1]).astype(BF16)
    dot = lambda lo, hi: jnp.dot(hn, w_ref[:, lo:hi], preferred_element_type=F32)
    q_ref[0] = _head_rms(dot(0, nq), ones_blk, qg_ref[...])
    k_ref[0] = _head_rms(dot(nq, nq + nkv), ones_blk[0:nkv, 0:nkv], kg_ref[...])
    o = nq + nkv
    v_ref[0] = dot(o, o + nkv)
    o += nkv
    zg_ref[0] = dot(o, o + zg_ref.shape[2])
    o += zg_ref.shape[2]
    xbc_ref[0] = dot(o, o + xbc_ref.shape[2])
    o += xbc_ref.shape[2]
    dt_ref[0] = _softplus(dot(o, o + LANES) + dtb_ref[...])


def _od_in(x, mod, norm_g, w_in, q_norm, k_norm, dt_bias, dims, tb):
    b, t, d = x.shape
    nq, nkv, nz, nxbc, nh = dims
    w = jnp.concatenate([w_in, jnp.zeros((d, LANES - nh), F32)], axis=1).astype(BF16)
    qg = jnp.tile(q_norm, nq // HEAD_DIM).reshape(1, nq)
    kg = jnp.tile(k_norm, nkv // HEAD_DIM).reshape(1, nkv)
    dtb = jnp.concatenate([dt_bias, jnp.zeros((LANES - nh,), F32)]).reshape(1, LANES)
    g2 = norm_g.reshape(1, d)
    tile = lambda n: pl.BlockSpec((1, tb, n), lambda i, j: (i, j, 0))
    const = lambda a: pl.BlockSpec(a.shape, lambda i, j: (0,) * a.ndim)
    widths = (nq, nkv, nkv, nz, nxbc, LANES)
    return pl.pallas_call(
        _od_in_kernel,
        grid=(b, t // tb),
        in_specs=[tile(d), pl.BlockSpec((1, 6, d), lambda i, j: (i, 0, 0)), const(g2), const(w),
                  const(qg), const(kg), const(dtb)],
        out_specs=[tile(n) for n in widths],
        out_shape=[jax.ShapeDtypeStruct((b, t, n), F32) for n in widths],
        compiler_params=_params(2),
        name="od_in",
    )(x, mod, g2, w, qg, kg, dtb)


def _swa_kernel(q_ref, k_ref, v_ref, k0_ref, v0_ref, bias_ref, sink_ref, o_ref, kx, vx,
                *, n_blocks, nq, mask_start):
    t = pl.program_id(1)
    tb = q_ref.shape[1]
    n_kv = k_ref.shape[2] // HEAD_DIM
    rows_q = nq * CHUNK
    tail = kx.shape[1] - WINDOW - tb
    left = lax.broadcasted_iota(jnp.int32, (1, LANES), 1) < HEAD_DIM
    ones = jnp.ones((SWA_KEYS, LANES), BF16)
    col = lax.broadcasted_iota(jnp.int32, (1, SWA_KEYS), 1)

    def place(dst, rows, x):
        rolled = pltpu.roll(x, HEAD_DIM, axis=1)
        dst[0, rows, :] = jnp.where(left, x, 0.0).astype(BF16)
        dst[1, rows, :] = jnp.where(left, 0.0, rolled).astype(BF16)
        dst[2, rows, :] = jnp.where(left, rolled, 0.0).astype(BF16)
        dst[3, rows, :] = jnp.where(left, 0.0, x).astype(BF16)

    @pl.when(t == 0)
    def _():
        place(kx, slice(0, WINDOW), k0_ref[0])
        place(vx, slice(0, WINDOW), v0_ref[0])
        for i in range(2 * n_kv if tail else 0):
            kx[i, WINDOW + tb:, :] = jnp.zeros((tail, LANES), BF16)
            vx[i, WINDOW + tb:, :] = jnp.zeros((tail, LANES), BF16)

    place(kx, slice(WINDOW, WINDOW + tb), k_ref[0])
    place(vx, slice(WINDOW, WINDOW + tb), v_ref[0])

    def block(blk, carry):
        r0 = pl.multiple_of(blk * rows_q, rows_q)
        keys = pl.ds(r0, SWA_KEYS)
        if mask_start:
            valid = t * tb + r0 - WINDOW + col >= 0
        probs = []
        sinks = []
        for j in range(n_kv):
            lo = 2 * j * LANES
            qg = jnp.concatenate([q_ref[0, pl.ds(r0 + cq * CHUNK, CHUNK), lo + r * LANES:lo + (r + 1) * LANES]
                                  for cq in range(nq) for r in range(2)], axis=0).astype(BF16)
            kcat = jnp.concatenate([kx[2 * j, keys, :], kx[2 * j + 1, keys, :]], axis=0)
            s_both = lax.dot_general(qg, kcat, _NT, preferred_element_type=F32) * HEAD_DIM ** -0.5
            for side in range(2):
                s = s_both[:, side * SWA_KEYS:(side + 1) * SWA_KEYS] + bias_ref[j, side]
                if mask_start:
                    s = jnp.where(valid, s, NEG_INF)
                m = jnp.max(s, axis=-1, keepdims=True)
                probs.append(jnp.exp(s - m).astype(BF16))
                sinks.append(jnp.exp(sink_ref[j, side] - m))
        den = jnp.dot(jnp.concatenate(probs, axis=0), ones, preferred_element_type=F32)
        nr = 2 * rows_q
        for j in range(n_kv):
            lo = 2 * j * LANES
            vcat = jnp.concatenate([vx[2 * j, keys, :], vx[2 * j + 1, keys, :]], axis=0)
            pv = jnp.dot(jnp.concatenate(probs[2 * j:2 * j + 2], axis=1), vcat, preferred_element_type=F32)
            d_l = den[(2 * j) * nr:(2 * j + 1) * nr] + sinks[2 * j]
            d_r = den[(2 * j + 1) * nr:(2 * j + 2) * nr] + sinks[2 * j + 1]
            out = pv * jnp.where(left, 1.0 / d_l, 1.0 / d_r)
            for cq in range(nq):
                for r in range(2):
                    o_ref[0, pl.ds(r0 + cq * CHUNK, CHUNK), lo + r * LANES:lo + (r + 1) * LANES] = (
                        out[(2 * cq + r) * CHUNK:(2 * cq + r + 1) * CHUNK])
        return carry

    lax.fori_loop(0, n_blocks, block, 0, unroll=math.gcd(n_blocks, 2))
    for i in range(2 * n_kv):
        kx[i, 0:WINDOW, :] = kx[i, tb:tb + WINDOW, :]
        vx[i, 0:WINDOW, :] = vx[i, tb:tb + WINDOW, :]


def _t5_bucket(rel):
    nb = T5_BUCKETS // 2
    max_exact = nb // 2
    ret = (rel > 0).astype(jnp.int32) * nb
    n = jnp.abs(rel)
    nf = jnp.maximum(n, 1).astype(F32)
    large = max_exact + (jnp.log(nf / max_exact) / math.log(T5_MAX_DIST / max_exact)
                         * (nb - max_exact)).astype(jnp.int32)
    large = jnp.minimum(large, nb - 1)
    return ret + jnp.where(n < max_exact, n, large)


def _rel_bias(table):
    rel = (jnp.arange(WINDOW + CHUNK)[None, :] - WINDOW) - jnp.arange(CHUNK)[:, None]
    onehot = (_t5_bucket(rel)[..., None] == jnp.arange(T5_BUCKETS)).astype(F32)
    return jnp.einsum('qkb,bh->hqk', onehot, table, precision=_HI)


def _swa(q, k, v, k0, v0, bias, sink, tb, mask_start):
    b, t, nq = q.shape
    nkv = k.shape[2]
    n_kv = nkv // HEAD_DIM
    assert nkv == LANES and nq == 2 * n_kv * LANES
    by_side = lambda a: jnp.transpose(a.reshape((n_kv, 2, 2) + a.shape[1:]), (0, 2, 1) + tuple(range(3, a.ndim + 2)))
    span = WINDOW + CHUNK
    n_chunks = tb // CHUNK
    cpb = 2 if n_chunks % 2 == 0 else 1
    assert WINDOW + cpb * CHUNK <= SWA_KEYS
    bias2 = by_side(bias).reshape(n_kv, 2, 2 * CHUNK, span)
    bias3 = jnp.concatenate(
        [jnp.pad(bias2, ((0, 0), (0, 0), (0, 0), (cq * CHUNK, SWA_KEYS - span - cq * CHUNK)),
                 constant_values=NEG_INF) for cq in range(cpb)], axis=2)
    sink3 = jnp.tile(jnp.repeat(by_side(sink), CHUNK, axis=2), (1, 1, cpb))
    sink3 = jnp.broadcast_to(sink3[..., None], sink3.shape + (LANES,))
    tile = lambda n: pl.BlockSpec((1, tb, n), lambda i, j: (i, j, 0))
    wspec = pl.BlockSpec((1, WINDOW, nkv), lambda i, j: (i, 0, 0))
    const = lambda a: pl.BlockSpec(a.shape, lambda i, j: (0,) * a.ndim)
    rows = WINDOW + max(tb, WINDOW) + SWA_KEYS - WINDOW - cpb * CHUNK
    return pl.pallas_call(
        functools.partial(_swa_kernel, n_blocks=n_chunks // cpb, nq=cpb, mask_start=mask_start),
        grid=(b, t // tb),
        in_specs=[tile(nq), tile(nkv), tile(nkv), wspec, wspec, const(bias3), const(sink3)],
        out_specs=tile(nq),
        out_shape=jax.ShapeDtypeStruct((b, t, nq), F32),
        scratch_shapes=[pltpu.VMEM((2 * n_kv, rows, LANES), BF16), pltpu.VMEM((2 * n_kv, rows, LANES), BF16)],
        compiler_params=_params(2),
        name="swa",
    )(q, k, v, k0, v0, bias3, sink3)


def _ssd_kernel(xbc_ref, zg_ref, dt_ref, pre_ref, cw_ref, cb_ref, aexp_ref, dexp_ref,
                e_ref, ng_ref, s0_ref, y_ref, sfin_ref, clast_ref, st_scr, hist_scr, seg_scr, dte_scr,
                *, n_chunks):
    t = pl.program_id(1)
    tb = xbc_ref.shape[1]
    inner = zg_ref.shape[2]
    n_heads = inner // SSD_HEAD_DIM
    hpg = n_heads // SSD_GROUPS
    gn = SSD_GROUPS * SSD_STATE
    n_hs = n_heads * CHUNK

    @pl.when(t == 0)
    def _():
        st_scr[...] = s0_ref[0]
        hist_scr[...] = pre_ref[0]

    _to_segments(seg_scr, xbc_ref[0])
    conv, hist_scr[...] = _causal_conv_interleaved(_load_interleaved(seg_scr, tb), hist_scr[...],
                                                   cw_ref[...], cb_ref[...])
    _store_interleaved(seg_scr, _silu(conv))
    dte_scr[...] = _split_dot(dt_ref[0], e_ref[...])

    row_hs = lax.broadcasted_iota(jnp.int32, (n_hs, 1), 0)
    own_group = row_hs // (hpg * CHUNK) == lax.broadcasted_iota(jnp.int32, (1, gn), 1) // SSD_STATE
    own_head = row_hs // CHUNK == lax.broadcasted_iota(jnp.int32, (1, inner), 1) // SSD_HEAD_DIM
    state_group = (lax.broadcasted_iota(jnp.int32, (gn, 1), 0) // SSD_STATE
                   == lax.broadcasted_iota(jnp.int32, (1, inner), 1) // (hpg * SSD_HEAD_DIM))
    first_group = lax.broadcasted_iota(jnp.int32, (1, inner), 1) < hpg * SSD_HEAD_DIM
    step_row = lax.broadcasted_iota(jnp.int32, (CHUNK, 1), 0)
    step_lane = lax.broadcasted_iota(jnp.int32, (1, n_hs), 1) % CHUNK
    zero = jnp.zeros((), BF16)
    aexp = aexp_ref[...]

    for c in range(n_chunks):
        rows = slice(c * CHUNK, (c + 1) * CHUNK)
        xc = _rows_from_segments(seg_scr, c * CHUNK, CHUNK, tb)
        xs = xc[:, 0:inner]
        bm = xc[:, inner:inner + gn]
        cm = xc[:, inner + gn:inner + 2 * gn].astype(BF16)
        dte = dte_scr[rows, :]
        cum = _cumsum_rows(dte * aexp)
        cum_last = cum[CHUNK - 1:CHUNK, :]
        cum_at_step = jnp.sum(jnp.where(step_row == step_lane, cum, 0.0), axis=0, keepdims=True)
        decay = jnp.exp(jnp.where(step_row >= step_lane, cum - cum_at_step, NEG_INF))
        bm_bd = jnp.where(own_group, jnp.concatenate([bm.astype(BF16)] * n_heads, axis=0), zero)
        cb = lax.dot_general(cm, bm_bd, _NT, preferred_element_type=F32)
        xdt = (xs * dte).astype(BF16)
        xdt_bd = jnp.where(own_head, jnp.concatenate([xdt] * n_heads, axis=0), zero)
        st = st_scr[...]
        st_bd = jnp.where(state_group, jnp.concatenate([st.astype(BF16)] * SSD_GROUPS, axis=0), zero)
        y = (jnp.dot((cb * decay).astype(BF16), xdt_bd, preferred_element_type=F32)
             + jnp.exp(cum) * jnp.dot(cm, st_bd, preferred_element_type=F32))
        xw = (xs * (jnp.exp(cum_last - cum) * dte)).astype(BF16)
        upd = jnp.dot(bm.T.astype(BF16), xw, preferred_element_type=F32)
        st_scr[...] = st * jnp.exp(cum_last) + jnp.where(first_group, upd[0:SSD_STATE], upd[SSD_STATE:])
        yd = (y + dexp_ref[...] * xs) * _silu(zg_ref[0, rows, :])
        ms = jnp.mean(yd * yd, axis=-1, keepdims=True)
        y_ref[0, rows, :] = yd * lax.rsqrt(ms + RMS_EPS) * ng_ref[...]

    @pl.when(t == pl.num_programs(1) - 1)
    def _():
        sfin_ref[0] = st_scr[...]
        clast_ref[0] = hist_scr[...]


def _ssd(xbc, zg, dt, conv_prefix, conv_w, conv_b, a_log, d_skip, norm_g, s0, tb):
    b, t, nxbc = xbc.shape
    inner = zg.shape[2]
    nh = a_log.shape[0]
    width = conv_w.shape[0]
    assert SSD_GROUPS == 2 and SSD_HEAD_DIM == CHUNK
    pre8 = jnp.concatenate([jnp.zeros((b, SUBLANES - (width - 1), nxbc), F32), conv_prefix], axis=1)
    a = -jnp.exp(a_log)
    aexp = jnp.repeat(a, SSD_HEAD_DIM).reshape(1, inner)
    dexp = jnp.repeat(d_skip, SSD_HEAD_DIM).reshape(1, inner)
    expand = (np.arange(LANES)[:, None] == np.arange(inner)[None, :] // SSD_HEAD_DIM)
    expand = jnp.asarray(expand, BF16)
    s0t = jnp.transpose(s0, (0, 3, 1, 2)).reshape(b, SSD_STATE, inner)
    ng = norm_g.reshape(1, inner)
    cb2 = conv_b.reshape(1, nxbc)
    tile = lambda n: pl.BlockSpec((1, tb, n), lambda i, j: (i, j, 0))
    const = lambda arr: pl.BlockSpec(arr.shape, lambda i, j: (0,) * arr.ndim)
    bspec = lambda arr: pl.BlockSpec((1,) + arr.shape[1:], lambda i, j: (i,) + (0,) * (arr.ndim - 1))
    y, sfin, clast = pl.pallas_call(
        functools.partial(_ssd_kernel, n_chunks=tb // CHUNK),
        grid=(b, t // tb),
        in_specs=[tile(nxbc), tile(inner), tile(LANES),
                  bspec(pre8), const(conv_w), const(cb2), const(aexp), const(dexp),
                  const(expand), const(ng), bspec(s0t)],
        out_specs=[tile(inner), bspec(s0t), bspec(pre8)],
        out_shape=[jax.ShapeDtypeStruct((b, t, inner), F32), jax.ShapeDtypeStruct(s0t.shape, F32),
                   jax.ShapeDtypeStruct(pre8.shape, F32)],
        scratch_shapes=[pltpu.VMEM((SSD_STATE, inner), F32), pltpu.VMEM((SUBLANES, nxbc), F32),
                        pltpu.VMEM((nxbc // LANES, SUBLANES * _segment_pitch(tb), LANES), F32),
                        pltpu.VMEM((tb, inner), F32)],
        compiler_params=_params(2),
        name="ssd",
    )(xbc, zg, dt, pre8, conv_w, cb2, aexp, dexp, expand, ng, s0t)
    sfin = jnp.transpose(sfin.reshape(b, SSD_STATE, nh, SSD_HEAD_DIM), (0, 2, 3, 1))
    return y, sfin, clast[:, SUBLANES - (width - 1):]


def _trunk(x, mods, P, st, sample):
    b, t, d = x.shape
    tb = min(MAX_TILE, t)
    depth = P['w_mod'].shape[0]
    new = {name: [] for name in ('s5_re', 's5_im', 'gla', 'swa_k', 'swa_v', 'ssd', 'ssd_conv', 'ffn_conv')}
    for layer in range(depth):
        i = layer // 2
        mod = mods[layer].reshape(b, 6, d)
        ffn = (P['norm2_g'][layer], P['ffn_w_up'][layer], P['ffn_conv_w'][layer], P['ffn_conv_b'][layer],
               P['ffn_w_down'][layer], st['ffn_conv'][layer], tb)
        if layer % 2 == 0:
            u, qk, v, r, gate = _ev_in(x, mod, P['norm1_g'][layer], P['ev_w_in'][i], P['gla_w_gate2'][i],
                                       P['gla_b_gate'][i], tb)
            mats = _s5_matrices(P['s5_a_re'][i], P['s5_a_im'][i], P['s5_log_dt'][i], P['s5_b_re'][i],
                                P['s5_b_im'][i], P['s5_c_re'][i], P['s5_c_im'][i])
            ya, sr, si = _s5(u, mats, P['s5_d'][i], st['s5_re'][i], st['s5_im'][i])
            ob, sg = _gla(qk, v, gate, r, st['gla'][i], P['gla_norm_g'][i], tb)
            new['s5_re'].append(sr)
            new['s5_im'].append(si)
            new['gla'].append(sg)
            x, fp = _out_ffn(x, mod, ya, ob, P['ev_w_out'][i], *ffn,
                             s5_extra=(P['s5_w_glu'][i], P['s5_b_glu'][i]))
        else:
            nq = P['swa_sink'].shape[1] * HEAD_DIM
            nkv = SWA_KV_HEADS * HEAD_DIM
            inner = P['ssd_norm_g'].shape[1]
            nxbc = P['ssd_conv_w'].shape[2]
            nh = P['ssd_a_log'].shape[1]
            q, k, v, zg, xbc, dt = _od_in(x, mod, P['norm1_g'][layer], P['od_w_in'][i], P['swa_q_norm'][i],
                                               P['swa_k_norm'][i], P['ssd_dt_bias'][i],
                                               (nq, nkv, inner, nxbc, nh), tb)
            bias = _rel_bias(P['t5_bias'])
            if sample:
                k0 = st['swa_k'][i].reshape(b, WINDOW, nkv)
                v0 = st['swa_v'][i].reshape(b, WINDOW, nkv)
            else:
                k0 = v0 = jnp.zeros((b, WINDOW, nkv), F32)
            oc = _swa(q, k, v, k0, v0, bias, P['swa_sink'][i], tb, mask_start=not sample)
            yd, ss, sc = _ssd(xbc, zg, dt, st['ssd_conv'][i], P['ssd_conv_w'][i], P['ssd_conv_b'][i],
                              P['ssd_a_log'][i], P['ssd_d'][i], P['ssd_norm_g'][i], st['ssd'][i], tb)
            keep = slice(None) if sample else slice(t - WINDOW, t)
            new['swa_k'].append(k[:, keep].reshape(b, -1, SWA_KV_HEADS, HEAD_DIM))
            new['swa_v'].append(v[:, keep].reshape(b, -1, SWA_KV_HEADS, HEAD_DIM))
            new['ssd'].append(ss)
            new['ssd_conv'].append(sc)
            x, fp = _out_ffn(x, mod, oc, yd, P['od_w_out'][i], *ffn)
        new['ffn_conv'].append(fp)
    return x, {name: jnp.stack(vals) for name, vals in new.items()}


def kernel(x_prompt, x_sample, state_s5_re, state_s5_im, state_gla, cache_swa_k, cache_swa_v, state_ssd, state_ssd_conv, state_ffn_conv, c_prompt, c_sample, t5_bias, norm1_g, norm2_g, w_mod, b_mod, ffn_w_up, ffn_conv_w, ffn_conv_b, ffn_w_down, ev_w_in, ev_w_out, s5_a_re, s5_a_im, s5_log_dt, s5_b_re, s5_b_im, s5_c_re, s5_c_im, s5_d, s5_w_glu, s5_b_glu, gla_w_gate2, gla_b_gate, gla_norm_g, od_w_in, od_w_out, swa_q_norm, swa_k_norm, swa_sink, ssd_conv_w, ssd_conv_b, ssd_dt_bias, ssd_a_log, ssd_d, ssd_norm_g):
    P = dict(t5_bias=t5_bias, norm1_g=norm1_g, norm2_g=norm2_g, w_mod=w_mod, b_mod=b_mod,
             ffn_w_up=ffn_w_up, ffn_conv_w=ffn_conv_w, ffn_conv_b=ffn_conv_b, ffn_w_down=ffn_w_down,
             ev_w_in=ev_w_in, ev_w_out=ev_w_out, s5_a_re=s5_a_re, s5_a_im=s5_a_im, s5_log_dt=s5_log_dt,
             s5_b_re=s5_b_re, s5_b_im=s5_b_im, s5_c_re=s5_c_re, s5_c_im=s5_c_im, s5_d=s5_d,
             s5_w_glu=s5_w_glu, s5_b_glu=s5_b_glu, gla_w_gate2=gla_w_gate2, gla_b_gate=gla_b_gate,
             gla_norm_g=gla_norm_g, od_w_in=od_w_in, od_w_out=od_w_out, swa_q_norm=swa_q_norm,
             swa_k_norm=swa_k_norm, swa_sink=swa_sink, ssd_conv_w=ssd_conv_w, ssd_conv_b=ssd_conv_b,
             ssd_dt_bias=ssd_dt_bias, ssd_a_log=ssd_a_log, ssd_d=ssd_d, ssd_norm_g=ssd_norm_g)
    bp = x_prompt.shape[0]
    n_even, n_odd = state_s5_re.shape[0], state_ssd.shape[0]
    depth = w_mod.shape[0]
    zeros_like_b = lambda a: jnp.zeros((a.shape[0], bp) + a.shape[2:], F32)
    zero_st = dict(s5_re=zeros_like_b(state_s5_re), s5_im=zeros_like_b(state_s5_im), gla=zeros_like_b(state_gla),
                   ssd=zeros_like_b(state_ssd), ssd_conv=zeros_like_b(state_ssd_conv),
                   ffn_conv=zeros_like_b(state_ffn_conv))
    sample_st = dict(s5_re=state_s5_re, s5_im=state_s5_im, gla=state_gla, swa_k=cache_swa_k,
                     swa_v=cache_swa_v, ssd=state_ssd, ssd_conv=state_ssd_conv, ffn_conv=state_ffn_conv)
    mods = _modulation(jnp.concatenate([c_prompt, c_sample], axis=0), w_mod, b_mod)
    y_prompt, stp = _trunk(x_prompt, mods[:, :bp], P, zero_st, False)
    y_sample, sts = _trunk(x_sample, mods[:, bp:], P, sample_st, True)
    names = ('s5_re', 's5_im', 'gla', 'swa_k', 'swa_v', 'ssd', 'ssd_conv', 'ffn_conv')
    return (y_prompt, y_sample) + tuple(stp[n] for n in names) + tuple(sts[n] for n in names)
```

```python
import functools
import math

import jax
import jax.numpy as jnp
import numpy as np
from jax import lax
from jax.experimental import pallas as pl
from jax.experimental.pallas import tpu as pltpu

F32 = jnp.float32
BF16 = jnp.bfloat16

CHUNK = 64
WINDOW = 128
S5_GROUP = 16
S5_STATE = 64
S5_SUB = 8
S5_MAX_SUBS = 64
GLA_HEADS = 4
GLA_GATE_NORM = 16.0
HEAD_DIM = 64
SWA_KV_HEADS = 2
SWA_KEYS = 256
SSD_HEAD_DIM = 64
SSD_STATE = 128
SSD_GROUPS = 2
T5_BUCKETS = 32
T5_MAX_DIST = 128
RMS_EPS = 1e-6
NEG_INF = -1e30
LANES = 128
SUBLANES = 8
assert S5_SUB == SUBLANES and S5_GROUP * SUBLANES == LANES
MAX_TILE = 512
CHUNK_UNROLL = 8
FFN_COL_BLOCK = 256
VMEM_LIMIT = 56 * 1024 * 1024

_NT = (((1,), (1,)), ((), ()))
_HI = lax.Precision.HIGHEST


def _params(n_axes=2):
    sem = ("parallel",) + ("arbitrary",) * (n_axes - 1)
    return pltpu.CompilerParams(dimension_semantics=sem, vmem_limit_bytes=VMEM_LIMIT)


def _bdot(a, b):
    return jnp.dot(a.astype(BF16), b.astype(BF16), preferred_element_type=F32)


def _split_dot(x, c, parts=3):
    pieces = []
    rest = x
    for _ in range(parts):
        piece = rest.astype(BF16)
        pieces.append(piece)
        rest = rest - piece.astype(F32)
    n = x.shape[0]
    d = jnp.dot(jnp.concatenate(pieces, axis=0), c, preferred_element_type=F32)
    out = d[0:n]
    for i in range(1, parts):
        out = out + d[i * n:(i + 1) * n]
    return out


def _silu(x):
    return x * jax.nn.sigmoid(x)


def _softplus(x):
    return jnp.maximum(x, 0.0) + jnp.log1p(jnp.exp(-jnp.abs(x)))


def _log_sigmoid(x):
    return jnp.minimum(x, 0.0) - jnp.log1p(jnp.exp(-jnp.abs(x)))


def _norm_mod(x, g, scale, shift):
    ms = jnp.mean(x * x, axis=-1, keepdims=True)
    return (x * lax.rsqrt(ms + RMS_EPS) * g) * (1.0 + scale) + shift


def _cumsum_rows(x):
    n, m = x.shape
    tiles = x.reshape(n // SUBLANES, SUBLANES, m)
    sub = lax.broadcasted_iota(jnp.int32, (1, SUBLANES, 1), 1)
    d = 1
    while d < SUBLANES:
        tiles = tiles + jnp.where(sub >= d, pltpu.roll(tiles, d, axis=1), 0.0)
        d *= 2
    out = [tiles[0]]
    for i in range(1, n // SUBLANES):
        out.append(tiles[i] + out[-1][SUBLANES - 1:SUBLANES, :])
    return jnp.concatenate(out, axis=0)


def _causal_conv(u, prev8, w, b):
    width = w.shape[0]
    n = u.shape[0]
    ext = jnp.concatenate([prev8, u[0:SUBLANES]], axis=0)
    full = b
    head = b
    for j in range(width):
        sh = width - 1 - j
        if sh == 0:
            full = full + u * w[j:j + 1]
            head = head + u[0:SUBLANES] * w[j:j + 1]
        else:
            full = full + pltpu.roll(u, sh, axis=0) * w[j:j + 1]
            head = head + pltpu.roll(ext, sh, axis=0)[SUBLANES:2 * SUBLANES] * w[j:j + 1]
    if n == SUBLANES:
        return head
    return jnp.concatenate([head, full[SUBLANES:]], axis=0)


def _segment_pitch(n):
    pitch = n // SUBLANES + SUBLANES
    return pitch if (pitch // SUBLANES) % 2 else pitch + SUBLANES


def _to_segments(ref, val):
    nv = val.shape[0] // SUBLANES
    pitch = ref.shape[1] // SUBLANES
    for c in range(ref.shape[0]):
        for s in range(SUBLANES):
            ref[c, s * pitch:s * pitch + nv, :] = val[s * nv:(s + 1) * nv, c * LANES:(c + 1) * LANES]


def _from_segments(ref, n):
    nv = n // SUBLANES
    pitch = ref.shape[1] // SUBLANES
    return jnp.concatenate(
        [jnp.concatenate([ref[c, s * pitch:s * pitch + nv, :] for s in range(SUBLANES)], axis=0)
         for c in range(ref.shape[0])], axis=1)


def _rows_from_segments(ref, r0, n_rows, n):
    nv = n // SUBLANES
    pitch = ref.shape[1] // SUBLANES
    spans = []
    r = r0
    while r < r0 + n_rows:
        s, off = divmod(r, nv)
        take = min(nv - off, r0 + n_rows - r)
        spans.append(slice(s * pitch + off, s * pitch + off + take))
        r += take
    return jnp.concatenate(
        [jnp.concatenate([ref[c, sp, :] for sp in spans], axis=0) for c in range(ref.shape[0])], axis=1)


def _load_interleaved(ref, n):
    pitch = ref.shape[1] // SUBLANES
    return jnp.concatenate(
        [jnp.concatenate([ref[c, pl.ds(i, SUBLANES, stride=pitch), :] for c in range(ref.shape[0])], axis=1)
         for i in range(n // SUBLANES)], axis=0)


def _store_interleaved(ref, val):
    pitch = ref.shape[1] // SUBLANES
    for i in range(val.shape[0] // SUBLANES):
        for c in range(ref.shape[0]):
            ref[c, pl.ds(i, SUBLANES, stride=pitch), :] = val[i * SUBLANES:(i + 1) * SUBLANES, c * LANES:(c + 1) * LANES]


def _causal_conv_interleaved(u, prev8, w, b):
    width = w.shape[0]
    n, m = u.shape
    nv = n // SUBLANES
    sub = lax.broadcasted_iota(jnp.int32, (SUBLANES, 1), 0)
    tiles = u.reshape(nv, SUBLANES, m)

    def back_one(a, before):
        first = jnp.where(sub == 0, before, pltpu.roll(a[nv - 1], 1, axis=0))
        return jnp.concatenate([first[None], a[:nv - 1]], axis=0)

    delayed = [tiles]
    for k in range(1, width):
        delayed.append(back_one(delayed[-1], prev8[SUBLANES - k:SUBLANES - k + 1]))
    out = b
    for j in range(width):
        out = out + delayed[width - 1 - j] * w[j:j + 1]
    hist = tiles[nv - 1]
    for k in range(2, width):
        hist = jnp.where(sub == SUBLANES - k, pltpu.roll(tiles[nv - k], SUBLANES - k + 1, axis=0), hist)
    return out.reshape(n, m), hist


def _mod_kernel(c_ref, w_ref, b_ref, o_ref):
    o_ref[0] = _bdot(_silu(c_ref[...]), w_ref[0]) + b_ref[0]


def _modulation(c, w_mod, b_mod):
    depth, d, n = w_mod.shape
    bc = c.shape[0]
    tn = n // 4
    return pl.pallas_call(
        _mod_kernel,
        grid=(depth, n // tn),
        in_specs=[pl.BlockSpec((bc, d), lambda l, j: (0, 0)),
                  pl.BlockSpec((1, d, tn), lambda l, j: (l, 0, j)),
                  pl.BlockSpec((1, 1, tn), lambda l, j: (l, 0, j))],
        out_specs=pl.BlockSpec((1, bc, tn), lambda l, j: (l, 0, j)),
        out_shape=jax.ShapeDtypeStruct((depth, bc, n), F32),
        compiler_params=_params(2),
        name="modulation",
    )(c, w_mod, b_mod.reshape(depth, 1, n))


def _ev_in_kernel(x_ref, mod_ref, g_ref, w_ref, wg2_ref, bg_ref,
                  u_ref, qk_ref, v_ref, r_ref, gate_ref):
    mod = mod_ref[0]
    hn = _norm_mod(x_ref[0], g_ref[...], mod[1:2], mod[0:1]).astype(BF16)
    dot = lambda lo, hi: jnp.dot(hn, w_ref[:, lo:hi], preferred_element_type=F32)
    u = dot(0, 512)
    for c in range(u_ref.shape[0]):
        u_ref[c] = _chunk_transpose(u[:, c * LANES:(c + 1) * LANES].reshape(-1, SUBLANES, LANES))
    qk_ref[0] = dot(512, 1024)
    v_ref[0] = dot(1024, 1536)
    r_ref[0] = dot(1536, 2048)
    gl = dot(2048, 2048 + LANES)
    gate_ref[0] = _log_sigmoid(_bdot(gl, wg2_ref[...]) + bg_ref[...]) * (1.0 / GLA_GATE_NORM)


def _ev_in(x, mod, norm_g, w_in, w_gate2, b_gate, tb):
    b, t, d = x.shape
    rank = w_gate2.shape[0]
    nk = w_gate2.shape[1]
    wu, wq, wk, wv, wgl, wr = jnp.split(w_in, [512, 768, 1024, 1536, 1536 + rank], axis=1)
    w = jnp.concatenate([wu, wq, wk, wv, wr, wgl, jnp.zeros((d, LANES - rank), F32)], axis=1).astype(BF16)
    wg2 = jnp.concatenate([w_gate2, jnp.zeros((LANES - rank, nk), F32)], axis=0).astype(BF16)
    tile = lambda n: pl.BlockSpec((1, tb, n), lambda i, j: (i, j, 0))
    const = lambda a: pl.BlockSpec(a.shape, lambda i, j: (0,) * a.ndim)
    g2 = norm_g.reshape(1, d)
    bg = b_gate.reshape(1, nk)
    return pl.pallas_call(
        _ev_in_kernel,
        grid=(b, t // tb),
        in_specs=[tile(d), pl.BlockSpec((1, 6, d), lambda i, j: (i, 0, 0)), const(g2), const(w),
                  const(wg2), const(bg)],
        out_specs=[pl.BlockSpec((512 // LANES, tb // S5_SUB, None, SUBLANES, LANES), lambda i, j: (0, j, i, 0, 0)),
                   tile(512), tile(512), tile(512), tile(nk)],
        out_shape=[jax.ShapeDtypeStruct((512 // LANES, t // S5_SUB, b, SUBLANES, LANES), F32)]
        + [jax.ShapeDtypeStruct((b, t, n), F32) for n in (512, 512, 512, nk)],
        compiler_params=_params(2),
        name="ev_in",
    )(x, mod, g2, w, wg2, bg)


def _gla_kernel(qk_ref, v_ref, gate_ref, r_ref, s0_ref, ng_ref, o_ref, sfin_ref, st_scr, *, n_chunks):
    t = pl.program_id(1)
    dk = qk_ref.shape[2] // 2 // GLA_HEADS
    dv = v_ref.shape[2] // GLA_HEADS
    nk = GLA_HEADS * dk
    nv = GLA_HEADS * dv
    blk = (lax.broadcasted_iota(jnp.int32, (nv, nk), 0) // dv
           == lax.broadcasted_iota(jnp.int32, (nv, nk), 1) // dk)

    @pl.when(t == 0)
    def _():
        s0 = jnp.concatenate([s0_ref[0, h] for h in range(GLA_HEADS)], axis=0)
        st_scr[...] = jnp.where(blk, jnp.concatenate([s0] * GLA_HEADS, axis=1), 0.0)

    n_hs = GLA_HEADS * CHUNK
    row_head = lax.broadcasted_iota(jnp.int32, (n_hs, 1), 0) // CHUNK
    own_k = row_head == lax.broadcasted_iota(jnp.int32, (1, nk), 1) // dk
    own_v = row_head == lax.broadcasted_iota(jnp.int32, (1, nv), 1) // dv
    causal = (lax.broadcasted_iota(jnp.int32, (CHUNK, 1), 0)
              >= lax.broadcasted_iota(jnp.int32, (1, n_hs), 1) % CHUNK)
    ng = ng_ref[...]

    def chunk(c, carry):
        rows = pl.ds(pl.multiple_of(c * CHUNK, CHUNK), CHUNK)
        k = qk_ref[0, rows, nk:2 * nk]
        v = v_ref[0, rows, :]
        cum = _cumsum_rows(gate_ref[0, rows, :])
        cum_last = cum[CHUNK - 1:CHUNK, :]
        qe = (qk_ref[0, rows, 0:nk] * dk ** -0.5 * jnp.exp(cum)).astype(BF16)
        ke = (k * jnp.exp(-cum)).astype(BF16)
        kd = (k * jnp.exp(cum_last - cum)).astype(BF16)
        v_b = v.astype(BF16)
        st = st_scr[...]
        ke_bd = jnp.where(own_k, jnp.concatenate([ke] * GLA_HEADS, axis=0), jnp.zeros((), BF16))
        v_bd = jnp.where(own_v, jnp.concatenate([v_b] * GLA_HEADS, axis=0), jnp.zeros((), BF16))
        att = lax.dot_general(qe, ke_bd, _NT, preferred_element_type=F32)
        att = jnp.where(causal, att, 0.0).astype(BF16)
        o = (jnp.dot(att, v_bd, preferred_element_type=F32)
             + lax.dot_general(qe, st.astype(BF16), _NT, preferred_element_type=F32))
        for h in range(GLA_HEADS):
            cols = slice(h * dv, (h + 1) * dv)
            oh = o[:, cols]
            ms = jnp.mean(oh * oh, axis=-1, keepdims=True)
            oh = oh * lax.rsqrt(ms + RMS_EPS) * ng
            o_ref[0, rows, cols] = oh * _silu(r_ref[0, rows, cols])
        upd = jnp.dot(v.T.astype(BF16), kd, preferred_element_type=F32)
        st_scr[...] = jnp.where(blk, st * jnp.exp(cum_last) + upd, 0.0)
        return carry

    lax.fori_loop(0, n_chunks, chunk, 0, unroll=math.gcd(n_chunks, CHUNK_UNROLL))

    @pl.when(t == pl.num_programs(1) - 1)
    def _():
        for h in range(GLA_HEADS):
            sfin_ref[0, h] = st_scr[h * dv:(h + 1) * dv, h * dk:(h + 1) * dk]


def _gla(qk, v, gate, r, s0, norm_g, tb):
    b, t, nv = v.shape
    nk = gate.shape[2]
    dk, dv = nk // GLA_HEADS, nv // GLA_HEADS
    s0t = jnp.swapaxes(s0, 2, 3)
    tile = lambda n: pl.BlockSpec((1, tb, n), lambda i, j: (i, j, 0))
    sspec = pl.BlockSpec((1, GLA_HEADS, dv, dk), lambda i, j: (i, 0, 0, 0))
    ng = norm_g.reshape(1, dv)
    o, sfin = pl.pallas_call(
        functools.partial(_gla_kernel, n_chunks=tb // CHUNK),
        grid=(b, t // tb),
        in_specs=[tile(2 * nk), tile(nv), tile(nk), tile(nv), sspec,
                  pl.BlockSpec((1, dv), lambda i, j: (0, 0))],
        out_specs=[tile(nv), sspec],
        out_shape=[jax.ShapeDtypeStruct((b, t, nv), F32),
                   jax.ShapeDtypeStruct((b, GLA_HEADS, dv, dk), F32)],
        scratch_shapes=[pltpu.VMEM((nv, nk), F32)],
        compiler_params=_params(2),
        name="gla",
    )(qk, v, gate, r, s0t, ng)
    return o, jnp.swapaxes(sfin, 2, 3)


def _chunk_transpose(x):
    s = lax.broadcasted_iota(jnp.int32, (1, SUBLANES, LANES), 1)
    c = lax.broadcasted_iota(jnp.int32, (1, SUBLANES, LANES), 2) // S5_GROUP
    for d in (4, 2, 1):
        sb = (s & d) != 0
        cb = (c & d) != 0
        if 2 * d == SUBLANES:
            t = pltpu.roll(pltpu.roll(x, d, axis=1), S5_GROUP * d, axis=2)
        else:
            xs = jnp.where(sb, pltpu.roll(x, d, axis=1), pltpu.roll(x, SUBLANES - d, axis=1))
            t = jnp.where(cb, pltpu.roll(xs, S5_GROUP * d, axis=2), pltpu.roll(xs, LANES - S5_GROUP * d, axis=2))
        x = jnp.where(sb != cb, t, x)
    return x


def _s5_kernel(u_ref, w1_ref, n_ref, a_ref, d_ref, x0_ref, y_ref, xf_ref, mm_scr, xs_scr, st_scr, *, tk, bsz):
    j = pl.program_id(1)
    n_g = st_scr.shape[0]
    rows = tk * bsz
    of_group = lambda g: pl.ds(g, rows, stride=n_g)

    @pl.when(j == 0)
    def _():
        st_scr[...] = x0_ref[0]

    for g in range(n_g):
        ub = u_ref[of_group(g), :].astype(BF16)
        mm_scr[g] = jnp.dot(ub, w1_ref[0, g], preferred_element_type=F32)

    half = n_g // 2
    for g0 in (0, half):
        coef = [a_ref[0, g] for g in range(g0, g0 + half)]

        def step(k, carry, g0=g0, coef=coef):
            r = pl.ds(pl.multiple_of(k * bsz, bsz), bsz)
            out = []
            for i in range(half):
                x, xsw = carry[2 * i], carry[2 * i + 1]
                a1, a2, a2s = coef[i][0:1], coef[i][1:2], coef[i][2:3]
                xs_scr[g0 + i, r, :] = x
                out.append(a1 * x + a2 * xsw + mm_scr[g0 + i, r, LANES:2 * LANES])
                out.append(a1 * xsw + a2s * x + mm_scr[g0 + i, r, 2 * LANES:3 * LANES])
            return tuple(out)

        fin = lax.fori_loop(0, tk, step, tuple(st_scr[g0 + i, v] for i in range(half) for v in range(2)))
        for i in range(half):
            st_scr[g0 + i, 0] = fin[2 * i]
            st_scr[g0 + i, 1] = fin[2 * i + 1]

    for g in range(n_g):
        y = (mm_scr[g, :, 0:LANES] + jnp.dot(xs_scr[g].astype(BF16), n_ref[0, g], preferred_element_type=F32)
             + d_ref[0, g] * u_ref[of_group(g), :])
        y_ref[of_group(g), :] = jax.nn.gelu(y)

    @pl.when(j == pl.num_programs(1) - 1)
    def _():
        xf_ref[0] = st_scr[:, 0]


def _cmul(ar, ai, br, bi):
    return ar * br - ai * bi, ar * bi + ai * br


def _s5_matrices(a_re, a_im, log_dt, b_re, b_im, c_re, c_im):
    g, p = a_re.shape
    hdim = b_re.shape[-1]
    dt = jnp.exp(log_dt)[:, None]
    mag = jnp.exp(a_re * dt)
    ab_re, ab_im = mag * jnp.cos(a_im * dt), mag * jnp.sin(a_im * dt)
    den = a_re * a_re + a_im * a_im
    num_re, num_im = ab_re - 1.0, ab_im
    g_re = (num_re * a_re + num_im * a_im) / den
    g_im = (num_im * a_re - num_re * a_im) / den
    bb_re = g_re[..., None] * b_re - g_im[..., None] * b_im
    bb_im = g_re[..., None] * b_im + g_im[..., None] * b_re
    pw_re, pw_im = jnp.ones((1, g, p), F32), jnp.zeros((1, g, p), F32)
    sq_re, sq_im = ab_re, ab_im
    while pw_re.shape[0] < S5_SUB + 1:
        nr, ni = _cmul(pw_re, pw_im, sq_re, sq_im)
        pw_re, pw_im = jnp.concatenate([pw_re, nr]), jnp.concatenate([pw_im, ni])
        sq_re, sq_im = _cmul(sq_re, sq_im, sq_re, sq_im)
    pw_re, pw_im = pw_re[:S5_SUB + 1], pw_im[:S5_SUB + 1]
    ca_re, ca_im = _cmul(c_re[None], c_im[None], pw_re[:, :, None, :], pw_im[:, :, None, :])
    kern = (jnp.einsum('tghp,gpk->tghk', ca_re[:S5_SUB], bb_re, precision=_HI)
            - jnp.einsum('tghp,gpk->tghk', ca_im[:S5_SUB], bb_im, precision=_HI))
    s_in = np.arange(S5_SUB)[:, None, None]
    s_out = np.arange(S5_SUB)[None, :, None]
    lag = jnp.asarray(s_out - s_in == np.arange(S5_SUB), F32)
    m = jnp.einsum('iot,tghk->ioghk', lag, kern, precision=_HI)
    m = jnp.transpose(m, (2, 0, 4, 1, 3)).reshape(g, S5_SUB * hdim, S5_SUB * hdim)
    pr, pi = _cmul(pw_re[S5_SUB - 1::-1][..., None], pw_im[S5_SUB - 1::-1][..., None], bb_re[None], bb_im[None])
    flat = lambda a: jnp.transpose(a, (1, 0, 3, 2)).reshape(g, S5_SUB * hdim, p)
    nm = jnp.concatenate([ca_re[1:], -ca_im[1:]], axis=3)
    nm = jnp.transpose(nm, (1, 3, 0, 2)).reshape(g, 2 * p, S5_SUB * hdim)
    w1 = jnp.concatenate([m, flat(pr), flat(pi), flat(pi), flat(pr)], axis=2)
    a_r, a_i = pw_re[S5_SUB], pw_im[S5_SUB]
    coef = jnp.stack([jnp.concatenate([a_r, a_r], axis=1), jnp.concatenate([-a_i, a_i], axis=1),
                      jnp.concatenate([a_i, -a_i], axis=1)], axis=1)
    return w1.astype(BF16), nm.astype(BF16), coef


def _s5(u2, mats, d_skip, x0_re, x0_im):
    w1, nm, coef = mats
    n_ct, n_sub, b, n_g, lanes = u2.shape
    g, p2 = nm.shape[0], nm.shape[1]
    hdim = lanes // S5_SUB
    tk = min(n_sub, S5_MAX_SUBS)
    x0 = jnp.transpose(jnp.concatenate([x0_re, x0_im], axis=2), (1, 0, 2))
    x0 = jnp.stack([x0, jnp.roll(x0, p2 // 2, axis=2)], axis=1)
    dt = jnp.tile(d_skip.reshape(g, 1, hdim), (1, 1, S5_SUB))
    by_tile = lambda a: a.reshape((n_ct, n_g) + a.shape[1:])
    pspec = lambda a: pl.BlockSpec((1,) + a.shape[1:], lambda c, j: (c,) + (0,) * (a.ndim - 1))
    rows = tk * b
    uspec = pl.BlockSpec((None, rows * n_g, lanes), lambda c, j: (c, j, 0))
    params = [by_tile(a) for a in (w1, nm, coef, dt, x0)]
    y, xf = pl.pallas_call(
        functools.partial(_s5_kernel, tk=tk, bsz=b),
        grid=(n_ct, n_sub // tk),
        in_specs=[uspec] + [pspec(a) for a in params],
        out_specs=[uspec, pl.BlockSpec((1, n_g, b, p2), lambda c, j: (c, 0, 0, 0))],
        out_shape=[jax.ShapeDtypeStruct((n_ct, n_sub * b * n_g, lanes), F32),
                   jax.ShapeDtypeStruct((n_ct, n_g, b, p2), F32)],
        scratch_shapes=[pltpu.VMEM((n_g, rows, 3 * lanes), F32), pltpu.VMEM((n_g, rows, lanes), F32),
                        pltpu.VMEM((n_g, 2, b, p2), F32)],
        compiler_params=_params(2),
        name="s5",
    )(u2.reshape(n_ct, n_sub * b * n_g, lanes), *params)
    xf = jnp.transpose(xf.reshape(g, b, p2), (1, 0, 2))
    return y.reshape(u2.shape), xf[:, :, :p2 // 2], xf[:, :, p2 // 2:]


def _out_ffn_kernel(*refs, with_s5, cb):
    if with_s5:
        (x_ref, mod_ref, m1_ref, m2_ref, wglu_ref, bglu_ref, wout_ref, n2g_ref,
         wup_ref, cw_ref, cbias_ref, wd_ref, pre_ref, xo_ref, ulast_ref, carry_scr, h_scr, row_scr) = refs
    else:
        (x_ref, mod_ref, m1_ref, m2_ref, wout_ref, n2g_ref,
         wup_ref, cw_ref, cbias_ref, wd_ref, pre_ref, xo_ref, ulast_ref, carry_scr, h_scr, row_scr) = refs
    t = pl.program_id(1)

    @pl.when(t == 0)
    def _():
        carry_scr[...] = pre_ref[0]

    mod = mod_ref[0]
    if with_s5:
        ya = jnp.concatenate([_chunk_transpose(m1_ref[c]).reshape(-1, LANES) for c in range(m1_ref.shape[0])],
                             axis=1)
        m1 = ya * jax.nn.sigmoid(_bdot(ya, wglu_ref[...]) + bglu_ref[...])
    else:
        m1 = m1_ref[0]
    half = m1.shape[1]
    out = (jnp.dot(m1.astype(BF16), wout_ref[0:half, :], preferred_element_type=F32)
           + jnp.dot(m2_ref[0].astype(BF16), wout_ref[half:, :], preferred_element_type=F32))
    x1 = x_ref[0] + mod[2:3] * out
    tb = x1.shape[0]
    _to_segments(row_scr, _norm_mod(x1, n2g_ref[...], mod[4:5], mod[3:4]))
    hn = _load_interleaved(row_scr, tb).astype(BF16)
    dff = wd_ref.shape[0]
    n_cb = dff // cb
    a_cols = lambda j: slice(j * cb, (j + 1) * cb)
    g_cols = lambda j: slice(dff + j * cb, dff + (j + 1) * cb)

    def up(j):
        return (jnp.dot(hn, wup_ref[:, a_cols(j)], preferred_element_type=F32),
                jnp.dot(hn, wup_ref[:, g_cols(j)], preferred_element_type=F32))

    def conv(u, cols):
        out, carry_scr[:, cols] = _causal_conv_interleaved(u, carry_scr[:, cols], cw_ref[:, cols], cbias_ref[:, cols])
        return out

    nxt = up(0)
    for j in range(n_cb):
        ua, ug = nxt
        if j + 1 < n_cb:
            nxt = up(j + 1)
        h_scr[:, a_cols(j)] = (_silu(conv(ua, a_cols(j))) * conv(ug, g_cols(j))).astype(BF16)
    _store_interleaved(row_scr, jnp.dot(h_scr[...], wd_ref[...], preferred_element_type=F32))
    xo_ref[0] = x1 + mod[5:6] * _from_segments(row_scr, tb)

    @pl.when(t == pl.num_programs(1) - 1)
    def _():
        ulast_ref[0] = carry_scr[...]


def _out_ffn(x, mod, m1, m2, w_out, norm2_g, w_up, conv_w, conv_b, w_down, prefix, tb, s5_extra=None):
    b, t, d = x.shape
    dff = w_down.shape[0]
    assert dff % FFN_COL_BLOCK == 0
    width = conv_w.shape[0]
    half = m2.shape[2]
    w_up_b = w_up.astype(BF16)
    cbias = conv_b.reshape(1, 2 * dff)
    wd = w_down.astype(BF16)
    pre8 = jnp.concatenate([jnp.zeros((b, SUBLANES - (width - 1), 2 * dff), F32), prefix], axis=1)
    tile = lambda n: pl.BlockSpec((1, tb, n), lambda i, j: (i, j, 0))
    const = lambda a: pl.BlockSpec(a.shape, lambda i, j: (0,) * a.ndim, pipeline_mode=pl.Buffered(1))
    bspec = lambda a: pl.BlockSpec((1,) + a.shape[1:], lambda i, j: (i,) + (0,) * (a.ndim - 1))
    wo = w_out.astype(BF16)
    n2g = norm2_g.reshape(1, d)
    args = [x, mod]
    specs = [tile(d), bspec(mod)]
    if s5_extra is not None:
        w_glu, b_glu = s5_extra
        w_glu, b_glu = w_glu.astype(BF16), b_glu.reshape(1, half)
        args += [m1, m2, w_glu, b_glu]
        specs += [pl.BlockSpec((m1.shape[0], tb // S5_SUB, None) + m1.shape[3:], lambda i, j: (0, j, i, 0, 0)),
                  tile(m2.shape[2]), const(w_glu), const(b_glu)]
    else:
        args += [m1, m2]
        specs += [tile(half), tile(m2.shape[2])]
    weights = [wo, n2g, w_up_b, conv_w, cbias, wd]
    args += weights + [pre8]
    specs += [const(a) for a in weights] + [bspec(pre8)]
    xo, ulast = pl.pallas_call(
        functools.partial(_out_ffn_kernel, with_s5=s5_extra is not None, cb=FFN_COL_BLOCK),
        grid=(b, t // tb),
        in_specs=specs,
        out_specs=[tile(d), bspec(pre8)],
        out_shape=[jax.ShapeDtypeStruct((b, t, d), F32), jax.ShapeDtypeStruct(pre8.shape, F32)],
        scratch_shapes=[pltpu.VMEM(pre8.shape[1:], F32), pltpu.VMEM((tb, dff), BF16),
                        pltpu.VMEM((d // LANES, SUBLANES * _segment_pitch(tb), LANES), F32)],
        compiler_params=_params(2),
        name="out_ffn_s5" if s5_extra is not None else "out_ffn",
    )(*args)
    return xo, ulast[:, SUBLANES - (width - 1):]


def _head_rms(x, g):
    left = lax.broadcasted_iota(jnp.int32, (1, LANES), 1) < HEAD_DIM
    out = []
    for c in range(x.shape[1] // LANES):
        xt = x[:, c * LANES:(c + 1) * LANES]
        sq = xt * xt
        s_left = jnp.sum(jnp.where(left, sq, 0.0), axis=-1, keepdims=True)
        s_right = jnp.sum(jnp.where(left, 0.0, sq), axis=-1, keepdims=True)
        ms = jnp.where(left, s_left, s_right) * (1.0 / HEAD_DIM)
        out.append(xt * lax.rsqrt(ms + RMS_EPS))
    return jnp.concatenate(out, axis=1) * g


def _od_in_kernel(x_ref, mod_ref, g_ref, w_ref, qg_ref, kg_ref, dtb_ref,
                  q_ref, k_ref, v_ref, zg_ref, xbc_ref, dt_ref):
    mod = mod_ref[0]
    nq = q_ref.shape[2]
    nkv = k_ref.shape[2]
    hn = _norm_mod(x_ref[0], g_ref[...], mod[1:2], mod[0:1]).astype(BF16)
    dot = lambda lo, hi: jnp.dot(hn, w_ref[:, lo:hi], preferred_element_type=F32)
    q_ref[0] = _head_rms(dot(0, nq), qg_ref[...])
    kv = dot(nq, nq + 2 * nkv)
    k_ref[0] = _head_rms(kv[:, 0:nkv], kg_ref[...])
    v_ref[0] = kv[:, nkv:2 * nkv]
    o = nq + 2 * nkv
    zg_ref[0] = dot(o, o + zg_ref.shape[2])
    o += zg_ref.shape[2]
    xbc_ref[0] = dot(o, o + xbc_ref.shape[2])
    o += xbc_ref.shape[2]
    dt_ref[0] = _softplus(dot(o, o + LANES) + dtb_ref[...])


def _od_in(x, mod, norm_g, w_in, q_norm, k_norm, dt_bias, dims, tb):
    b, t, d = x.shape
    nq, nkv, nz, nxbc, nh = dims
    w = jnp.concatenate([w_in, jnp.zeros((d, LANES - nh), F32)], axis=1).astype(BF16)
    qg = jnp.tile(q_norm, nq // HEAD_DIM).reshape(1, nq)
    kg = jnp.tile(k_norm, nkv // HEAD_DIM).reshape(1, nkv)
    dtb = jnp.concatenate([dt_bias, jnp.zeros((LANES - nh,), F32)]).reshape(1, LANES)
    g2 = norm_g.reshape(1, d)
    tile = lambda n: pl.BlockSpec((1, tb, n), lambda i, j: (i, j, 0))
    const = lambda a: pl.BlockSpec(a.shape, lambda i, j: (0,) * a.ndim)
    widths = (nq, nkv, nkv, nz, nxbc, LANES)
    return pl.pallas_call(
        _od_in_kernel,
        grid=(b, t // tb),
        in_specs=[tile(d), pl.BlockSpec((1, 6, d), lambda i, j: (i, 0, 0)), const(g2), const(w),
                  const(qg), const(kg), const(dtb)],
        out_specs=[tile(n) for n in widths],
        out_shape=[jax.ShapeDtypeStruct((b, t, n), F32) for n in widths],
        compiler_params=_params(2),
        name="od_in",
    )(x, mod, g2, w, qg, kg, dtb)


def _swa_kernel(q_ref, k_ref, v_ref, k0_ref, v0_ref, bias_ref, sink_ref, o_ref, kx, vx,
                *, n_blocks, nq, mask_start):
    t = pl.program_id(1)
    tb = q_ref.shape[1]
    n_kv = k_ref.shape[2] // HEAD_DIM
    rows_q = nq * CHUNK
    tail = kx.shape[1] - WINDOW - tb
    left = lax.broadcasted_iota(jnp.int32, (1, LANES), 1) < HEAD_DIM
    first_side = lax.broadcasted_iota(jnp.int32, (2 * SWA_KEYS, 1), 0) < SWA_KEYS
    side_ones = jnp.where(first_side == left, 1.0, 0.0).astype(BF16)
    col = lax.broadcasted_iota(jnp.int32, (1, SWA_KEYS), 1)

    def place(dst, rows, x):
        rolled = pltpu.roll(x, HEAD_DIM, axis=1)
        dst[0, rows, :] = jnp.where(left, x, 0.0).astype(BF16)
        dst[1, rows, :] = jnp.where(left, 0.0, rolled).astype(BF16)
        dst[2, rows, :] = jnp.where(left, rolled, 0.0).astype(BF16)
        dst[3, rows, :] = jnp.where(left, 0.0, x).astype(BF16)

    @pl.when(t == 0)
    def _():
        place(kx, slice(0, WINDOW), k0_ref[0])
        place(vx, slice(0, WINDOW), v0_ref[0])
        for i in range(2 * n_kv if tail else 0):
            kx[i, WINDOW + tb:, :] = jnp.zeros((tail, LANES), BF16)
            vx[i, WINDOW + tb:, :] = jnp.zeros((tail, LANES), BF16)

    place(kx, slice(WINDOW, WINDOW + tb), k_ref[0])
    place(vx, slice(WINDOW, WINDOW + tb), v_ref[0])

    def block(blk, carry):
        r0 = pl.multiple_of(blk * rows_q, rows_q)
        keys = pl.ds(r0, SWA_KEYS)
        if mask_start:
            valid = t * tb + r0 - WINDOW + col >= 0
        probs = []
        sinks = []
        for j in range(n_kv):
            lo = 2 * j * LANES
            qg = jnp.concatenate([q_ref[0, pl.ds(r0 + cq * CHUNK, CHUNK), lo + r * LANES:lo + (r + 1) * LANES]
                                  for cq in range(nq) for r in range(2)], axis=0).astype(BF16)
            kcat = jnp.concatenate([kx[2 * j, keys, :], kx[2 * j + 1, keys, :]], axis=0)
            s_both = lax.dot_general(qg, kcat, _NT, preferred_element_type=F32) * HEAD_DIM ** -0.5
            for side in range(2):
                s = s_both[:, side * SWA_KEYS:(side + 1) * SWA_KEYS] + bias_ref[j, side]
                if mask_start:
                    s = jnp.where(valid, s, NEG_INF)
                m = jnp.max(s, axis=-1, keepdims=True)
                probs.append(jnp.exp(s - m).astype(BF16))
                sinks.append(jnp.exp(sink_ref[j, side] - m))
        for j in range(n_kv):
            lo = 2 * j * LANES
            vcat = jnp.concatenate(
                [jnp.concatenate([vx[2 * j, keys, :], vx[2 * j + 1, keys, :]], axis=0), side_ones], axis=1)
            pv = jnp.dot(jnp.concatenate(probs[2 * j:2 * j + 2], axis=1), vcat, preferred_element_type=F32)
            den = pv[:, LANES:] + jnp.where(left, sinks[2 * j], sinks[2 * j + 1])
            out = pv[:, 0:LANES] / den
            for cq in range(nq):
                for r in range(2):
                    o_ref[0, pl.ds(r0 + cq * CHUNK, CHUNK), lo + r * LANES:lo + (r + 1) * LANES] = (
                        out[(2 * cq + r) * CHUNK:(2 * cq + r + 1) * CHUNK])
        return carry

    lax.fori_loop(0, n_blocks, block, 0, unroll=math.gcd(n_blocks, 2))
    for i in range(2 * n_kv):
        kx[i, 0:WINDOW, :] = kx[i, tb:tb + WINDOW, :]
        vx[i, 0:WINDOW, :] = vx[i, tb:tb + WINDOW, :]


def _t5_bucket(rel):
    nb = T5_BUCKETS // 2
    max_exact = nb // 2
    ret = (rel > 0).astype(jnp.int32) * nb
    n = jnp.abs(rel)
    nf = jnp.maximum(n, 1).astype(F32)
    large = max_exact + (jnp.log(nf / max_exact) / math.log(T5_MAX_DIST / max_exact)
                         * (nb - max_exact)).astype(jnp.int32)
    large = jnp.minimum(large, nb - 1)
    return ret + jnp.where(n < max_exact, n, large)


def _rel_bias(table):
    rel = (jnp.arange(WINDOW + CHUNK)[None, :] - WINDOW) - jnp.arange(CHUNK)[:, None]
    onehot = (_t5_bucket(rel)[..., None] == jnp.arange(T5_BUCKETS)).astype(F32)
    return jnp.einsum('qkb,bh->hqk', onehot, table, precision=_HI)


def _swa(q, k, v, k0, v0, bias, sink, tb, mask_start):
    b, t, nq = q.shape
    nkv = k.shape[2]
    n_kv = nkv // HEAD_DIM
    assert nkv == LANES and nq == 2 * n_kv * LANES
    by_side = lambda a: jnp.transpose(a.reshape((n_kv, 2, 2) + a.shape[1:]), (0, 2, 1) + tuple(range(3, a.ndim + 2)))
    span = WINDOW + CHUNK
    n_chunks = tb // CHUNK
    cpb = 2 if n_chunks % 2 == 0 else 1
    assert WINDOW + cpb * CHUNK <= SWA_KEYS
    bias2 = by_side(bias).reshape(n_kv, 2, 2 * CHUNK, span)
    bias3 = jnp.concatenate(
        [jnp.pad(bias2, ((0, 0), (0, 0), (0, 0), (cq * CHUNK, SWA_KEYS - span - cq * CHUNK)),
                 constant_values=NEG_INF) for cq in range(cpb)], axis=2)
    sink3 = jnp.tile(jnp.repeat(by_side(sink), CHUNK, axis=2), (1, 1, cpb))
    sink3 = jnp.broadcast_to(sink3[..., None], sink3.shape + (LANES,))
    tile = lambda n: pl.BlockSpec((1, tb, n), lambda i, j: (i, j, 0))
    wspec = pl.BlockSpec((1, WINDOW, nkv), lambda i, j: (i, 0, 0))
    const = lambda a: pl.BlockSpec(a.shape, lambda i, j: (0,) * a.ndim)
    rows = WINDOW + max(tb, WINDOW) + SWA_KEYS - WINDOW - cpb * CHUNK
    return pl.pallas_call(
        functools.partial(_swa_kernel, n_blocks=n_chunks // cpb, nq=cpb, mask_start=mask_start),
        grid=(b, t // tb),
        in_specs=[tile(nq), tile(nkv), tile(nkv), wspec, wspec, const(bias3), const(sink3)],
        out_specs=tile(nq),
        out_shape=jax.ShapeDtypeStruct((b, t, nq), F32),
        scratch_shapes=[pltpu.VMEM((2 * n_kv, rows, LANES), BF16), pltpu.VMEM((2 * n_kv, rows, LANES), BF16)],
        compiler_params=_params(2),
        name="swa",
    )(q, k, v, k0, v0, bias3, sink3)


def _ssd_kernel(xbc_ref, zg_ref, dt_ref, pre_ref, cw_ref, cb_ref, aexp_ref, dexp_ref,
                e_ref, ng_ref, s0_ref, y_ref, sfin_ref, clast_ref, st_scr, hist_scr, seg_scr, dte_scr,
                *, n_chunks):
    t = pl.program_id(1)
    tb = xbc_ref.shape[1]
    inner = zg_ref.shape[2]
    n_heads = inner // SSD_HEAD_DIM
    hpg = n_heads // SSD_GROUPS
    gn = SSD_GROUPS * SSD_STATE
    n_hs = n_heads * CHUNK

    @pl.when(t == 0)
    def _():
        st_scr[...] = s0_ref[0]
        hist_scr[...] = pre_ref[0]

    _to_segments(seg_scr, xbc_ref[0])
    conv, hist_scr[...] = _causal_conv_interleaved(_load_interleaved(seg_scr, tb), hist_scr[...],
                                                   cw_ref[...], cb_ref[...])
    _store_interleaved(seg_scr, _silu(conv))
    dte_scr[...] = _split_dot(dt_ref[0], e_ref[...])

    row_hs = lax.broadcasted_iota(jnp.int32, (n_hs, 1), 0)
    own_group = row_hs // (hpg * CHUNK) == lax.broadcasted_iota(jnp.int32, (1, gn), 1) // SSD_STATE
    own_head = row_hs // CHUNK == lax.broadcasted_iota(jnp.int32, (1, inner), 1) // SSD_HEAD_DIM
    state_group = (lax.broadcasted_iota(jnp.int32, (gn, 1), 0) // SSD_STATE
                   == lax.broadcasted_iota(jnp.int32, (1, inner), 1) // (hpg * SSD_HEAD_DIM))
    first_group = lax.broadcasted_iota(jnp.int32, (1, inner), 1) < hpg * SSD_HEAD_DIM
    step_row = lax.broadcasted_iota(jnp.int32, (CHUNK, 1), 0)
    step_lane = lax.broadcasted_iota(jnp.int32, (1, n_hs), 1) % CHUNK
    zero = jnp.zeros((), BF16)
    aexp = aexp_ref[...]

    for c in range(n_chunks):
        rows = slice(c * CHUNK, (c + 1) * CHUNK)
        xc = _rows_from_segments(seg_scr, c * CHUNK, CHUNK, tb)
        xs = xc[:, 0:inner]
        bm = xc[:, inner:inner + gn]
        cm = xc[:, inner + gn:inner + 2 * gn].astype(BF16)
        dte = dte_scr[rows, :]
        cum = _cumsum_rows(dte * aexp)
        cum_last = cum[CHUNK - 1:CHUNK, :]
        cum_at_step = jnp.sum(jnp.where(step_row == step_lane, cum, 0.0), axis=0, keepdims=True)
        decay = jnp.exp(jnp.where(step_row >= step_lane, cum - cum_at_step, NEG_INF))
        bm_bd = jnp.where(own_group, jnp.concatenate([bm.astype(BF16)] * n_heads, axis=0), zero)
        cb = lax.dot_general(cm, bm_bd, _NT, preferred_element_type=F32)
        xdt = (xs * dte).astype(BF16)
        xdt_bd = jnp.where(own_head, jnp.concatenate([xdt] * n_heads, axis=0), zero)
        st = st_scr[...]
        st_bd = jnp.where(state_group, jnp.concatenate([st.astype(BF16)] * SSD_GROUPS, axis=0), zero)
        y = (jnp.dot((cb * decay).astype(BF16), xdt_bd, preferred_element_type=F32)
             + jnp.exp(cum) * jnp.dot(cm, st_bd, preferred_element_type=F32))
        xw = (xs * (jnp.exp(cum_last - cum) * dte)).astype(BF16)
        upd = jnp.dot(bm.T.astype(BF16), xw, preferred_element_type=F32)
        st_scr[...] = st * jnp.exp(cum_last) + jnp.where(first_group, upd[0:SSD_STATE], upd[SSD_STATE:])
        yd = (y + dexp_ref[...] * xs) * _silu(zg_ref[0, rows, :])
        ms = jnp.mean(yd * yd, axis=-1, keepdims=True)
        y_ref[0, rows, :] = yd * lax.rsqrt(ms + RMS_EPS) * ng_ref[...]

    @pl.when(t == pl.num_programs(1) - 1)
    def _():
        sfin_ref[0] = st_scr[...]
        clast_ref[0] = hist_scr[...]


def _ssd(xbc, zg, dt, conv_prefix, conv_w, conv_b, a_log, d_skip, norm_g, s0, tb):
    b, t, nxbc = xbc.shape
    inner = zg.shape[2]
    nh = a_log.shape[0]
    width = conv_w.shape[0]
    assert SSD_GROUPS == 2 and SSD_HEAD_DIM == CHUNK
    pre8 = jnp.concatenate([jnp.zeros((b, SUBLANES - (width - 1), nxbc), F32), conv_prefix], axis=1)
    a = -jnp.exp(a_log)
    aexp = jnp.repeat(a, SSD_HEAD_DIM).reshape(1, inner)
    dexp = jnp.repeat(d_skip, SSD_HEAD_DIM).reshape(1, inner)
    expand = (np.arange(LANES)[:, None] == np.arange(inner)[None, :] // SSD_HEAD_DIM)
    expand = jnp.asarray(expand, BF16)
    s0t = jnp.transpose(s0, (0, 3, 1, 2)).reshape(b, SSD_STATE, inner)
    ng = norm_g.reshape(1, inner)
    cb2 = conv_b.reshape(1, nxbc)
    tile = lambda n: pl.BlockSpec((1, tb, n), lambda i, j: (i, j, 0))
    const = lambda arr: pl.BlockSpec(arr.shape, lambda i, j: (0,) * arr.ndim)
    bspec = lambda arr: pl.BlockSpec((1,) + arr.shape[1:], lambda i, j: (i,) + (0,) * (arr.ndim - 1))
    y, sfin, clast = pl.pallas_call(
        functools.partial(_ssd_kernel, n_chunks=tb // CHUNK),
        grid=(b, t // tb),
        in_specs=[tile(nxbc), tile(inner), tile(LANES),
                  bspec(pre8), const(conv_w), const(cb2), const(aexp), const(dexp),
                  const(expand), const(ng), bspec(s0t)],
        out_specs=[tile(inner), bspec(s0t), bspec(pre8)],
        out_shape=[jax.ShapeDtypeStruct((b, t, inner), F32), jax.ShapeDtypeStruct(s0t.shape, F32),
                   jax.ShapeDtypeStruct(pre8.shape, F32)],
        scratch_shapes=[pltpu.VMEM((SSD_STATE, inner), F32), pltpu.VMEM((SUBLANES, nxbc), F32),
                        pltpu.VMEM((nxbc // LANES, SUBLANES * _segment_pitch(tb), LANES), F32),
                        pltpu.VMEM((tb, inner), F32)],
        compiler_params=_params(2),
        name="ssd",
    )(xbc, zg, dt, pre8, conv_w, cb2, aexp, dexp, expand, ng, s0t)
    sfin = jnp.transpose(sfin.reshape(b, SSD_STATE, nh, SSD_HEAD_DIM), (0, 2, 3, 1))
    return y, sfin, clast[:, SUBLANES - (width - 1):]


def _trunk(x, mods, P, st, sample):
    b, t, d = x.shape
    tb = min(MAX_TILE, t)
    depth = P['w_mod'].shape[0]
    new = {name: [] for name in ('s5_re', 's5_im', 'gla', 'swa_k', 'swa_v', 'ssd', 'ssd_conv', 'ffn_conv')}
    for layer in range(depth):
        i = layer // 2
        mod = mods[layer].reshape(b, 6, d)
        ffn = (P['norm2_g'][layer], P['ffn_w_up'][layer], P['ffn_conv_w'][layer], P['ffn_conv_b'][layer],
               P['ffn_w_down'][layer], st['ffn_conv'][layer], tb)
        if layer % 2 == 0:
            u, qk, v, r, gate = _ev_in(x, mod, P['norm1_g'][layer], P['ev_w_in'][i], P['gla_w_gate2'][i],
                                       P['gla_b_gate'][i], tb)
            mats = _s5_matrices(P['s5_a_re'][i], P['s5_a_im'][i], P['s5_log_dt'][i], P['s5_b_re'][i],
                                P['s5_b_im'][i], P['s5_c_re'][i], P['s5_c_im'][i])
            ya, sr, si = _s5(u, mats, P['s5_d'][i], st['s5_re'][i], st['s5_im'][i])
            ob, sg = _gla(qk, v, gate, r, st['gla'][i], P['gla_norm_g'][i], tb)
            new['s5_re'].append(sr)
            new['s5_im'].append(si)
            new['gla'].append(sg)
            x, fp = _out_ffn(x, mod, ya, ob, P['ev_w_out'][i], *ffn,
                             s5_extra=(P['s5_w_glu'][i], P['s5_b_glu'][i]))
        else:
            nq = P['swa_sink'].shape[1] * HEAD_DIM
            nkv = SWA_KV_HEADS * HEAD_DIM
            inner = P['ssd_norm_g'].shape[1]
            nxbc = P['ssd_conv_w'].shape[2]
            nh = P['ssd_a_log'].shape[1]
            q, k, v, zg, xbc, dt = _od_in(x, mod, P['norm1_g'][layer], P['od_w_in'][i], P['swa_q_norm'][i],
                                               P['swa_k_norm'][i], P['ssd_dt_bias'][i],
                                               (nq, nkv, inner, nxbc, nh), tb)
            bias = _rel_bias(P['t5_bias'])
            if sample:
                k0 = st['swa_k'][i].reshape(b, WINDOW, nkv)
                v0 = st['swa_v'][i].reshape(b, WINDOW, nkv)
            else:
                k0 = v0 = jnp.zeros((b, WINDOW, nkv), F32)
            oc = _swa(q, k, v, k0, v0, bias, P['swa_sink'][i], tb, mask_start=not sample)
            yd, ss, sc = _ssd(xbc, zg, dt, st['ssd_conv'][i], P['ssd_conv_w'][i], P['ssd_conv_b'][i],
                              P['ssd_a_log'][i], P['ssd_d'][i], P['ssd_norm_g'][i], st['ssd'][i], tb)
            keep = slice(None) if sample else slice(t - WINDOW, t)
            new['swa_k'].append(k[:, keep].reshape(b, -1, SWA_KV_HEADS, HEAD_DIM))
            new['swa_v'].append(v[:, keep].reshape(b, -1, SWA_KV_HEADS, HEAD_DIM))
            new['ssd'].append(ss)
            new['ssd_conv'].append(sc)
            x, fp = _out_ffn(x, mod, oc, yd, P['od_w_out'][i], *ffn)
        new['ffn_conv'].append(fp)
    return x, {name: jnp.stack(vals) for name, vals in new.items()}


def kernel(x_prompt, x_sample, state_s5_re, state_s5_im, state_gla, cache_swa_k, cache_swa_v, state_ssd, state_ssd_conv, state_ffn_conv, c_prompt, c_sample, t5_bias, norm1_g, norm2_g, w_mod, b_mod, ffn_w_up, ffn_conv_w, ffn_conv_b, ffn_w_down, ev_w_in, ev_w_out, s5_a_re, s5_a_im, s5_log_dt, s5_b_re, s5_b_im, s5_c_re, s5_c_im, s5_d, s5_w_glu, s5_b_glu, gla_w_gate2, gla_b_gate, gla_norm_g, od_w_in, od_w_out, swa_q_norm, swa_k_norm, swa_sink, ssd_conv_w, ssd_conv_b, ssd_dt_bias, ssd_a_log, ssd_d, ssd_norm_g):
    P = dict(t5_bias=t5_bias, norm1_g=norm1_g, norm2_g=norm2_g, w_mod=w_mod, b_mod=b_mod,
             ffn_w_up=ffn_w_up, ffn_conv_w=ffn_conv_w, ffn_conv_b=ffn_conv_b, ffn_w_down=ffn_w_down,
             ev_w_in=ev_w_in, ev_w_out=ev_w_out, s5_a_re=s5_a_re, s5_a_im=s5_a_im, s5_log_dt=s5_log_dt,
             s5_b_re=s5_b_re, s5_b_im=s5_b_im, s5_c_re=s5_c_re, s5_c_im=s5_c_im, s5_d=s5_d,
             s5_w_glu=s5_w_glu, s5_b_glu=s5_b_glu, gla_w_gate2=gla_w_gate2, gla_b_gate=gla_b_gate,
             gla_norm_g=gla_norm_g, od_w_in=od_w_in, od_w_out=od_w_out, swa_q_norm=swa_q_norm,
             swa_k_norm=swa_k_norm, swa_sink=swa_sink, ssd_conv_w=ssd_conv_w, ssd_conv_b=ssd_conv_b,
             ssd_dt_bias=ssd_dt_bias, ssd_a_log=ssd_a_log, ssd_d=ssd_d, ssd_norm_g=ssd_norm_g)
    bp = x_prompt.shape[0]
    n_even, n_odd = state_s5_re.shape[0], state_ssd.shape[0]
    depth = w_mod.shape[0]
    zeros_like_b = lambda a: jnp.zeros((a.shape[0], bp) + a.shape[2:], F32)
    zero_st = dict(s5_re=zeros_like_b(state_s5_re), s5_im=zeros_like_b(state_s5_im), gla=zeros_like_b(state_gla),
                   ssd=zeros_like_b(state_ssd), ssd_conv=zeros_like_b(state_ssd_conv),
                   ffn_conv=zeros_like_b(state_ffn_conv))
    sample_st = dict(s5_re=state_s5_re, s5_im=state_s5_im, gla=state_gla, swa_k=cache_swa_k,
                     swa_v=cache_swa_v, ssd=state_ssd, ssd_conv=state_ssd_conv, ffn_conv=state_ffn_conv)
    mods = _modulation(jnp.concatenate([c_prompt, c_sample], axis=0), w_mod, b_mod)
    y_prompt, stp = _trunk(x_prompt, mods[:, :bp], P, zero_st, False)
    y_sample, sts = _trunk(x_sample, mods[:, bp:], P, sample_st, True)
    names = ('s5_re', 's5_im', 'gla', 'swa_k', 'swa_v', 'ssd', 'ssd_conv', 'ffn_conv')
    return (y_prompt, y_sample) + tuple(stp[n] for n in names) + tuple(sts[n] for n in names)
```

```python
import functools
import math

import jax
import jax.numpy as jnp
import numpy as np
from jax import lax
from jax.experimental import pallas as pl
from jax.experimental.pallas import tpu as pltpu

F32 = jnp.float32
BF16 = jnp.bfloat16

CHUNK = 64
WINDOW = 128
S5_GROUP = 16
S5_STATE = 64
S5_SUB = 8
S5_MAX_SUBS = 64
GLA_HEADS = 4
GLA_GATE_NORM = 16.0
HEAD_DIM = 64
SWA_KV_HEADS = 2
SWA_KEYS = 256
SSD_HEAD_DIM = 64
SSD_STATE = 128
SSD_GROUPS = 2
T5_BUCKETS = 32
T5_MAX_DIST = 128
RMS_EPS = 1e-6
NEG_INF = -1e30
LANES = 128
SUBLANES = 8
assert S5_SUB == SUBLANES and S5_GROUP * SUBLANES == LANES
MAX_TILE = 512
CHUNK_UNROLL = 8
FFN_COL_BLOCK = 256
VMEM_LIMIT = 56 * 1024 * 1024

_NT = (((1,), (1,)), ((), ()))
_HI = lax.Precision.HIGHEST


def _params(n_axes=2):
    sem = ("parallel",) + ("arbitrary",) * (n_axes - 1)
    return pltpu.CompilerParams(dimension_semantics=sem, vmem_limit_bytes=VMEM_LIMIT)


def _bdot(a, b):
    return jnp.dot(a.astype(BF16), b.astype(BF16), preferred_element_type=F32)


def _split_dot(x, c, parts=3):
    pieces = []
    rest = x
    for _ in range(parts):
        piece = rest.astype(BF16)
        pieces.append(piece)
        rest = rest - piece.astype(F32)
    n = x.shape[0]
    d = jnp.dot(jnp.concatenate(pieces, axis=0), c, preferred_element_type=F32)
    out = d[0:n]
    for i in range(1, parts):
        out = out + d[i * n:(i + 1) * n]
    return out


def _silu(x):
    return x * jax.nn.sigmoid(x)


def _softplus(x):
    return jnp.maximum(x, 0.0) + jnp.log1p(jnp.exp(-jnp.abs(x)))


def _log_sigmoid(x):
    return jnp.minimum(x, 0.0) - jnp.log1p(jnp.exp(-jnp.abs(x)))


def _norm_mod(x, g, scale, shift):
    ms = jnp.mean(x * x, axis=-1, keepdims=True)
    return (x * lax.rsqrt(ms + RMS_EPS) * g) * (1.0 + scale) + shift


def _cumsum_rows(x):
    n, m = x.shape
    tiles = x.reshape(n // SUBLANES, SUBLANES, m)
    sub = lax.broadcasted_iota(jnp.int32, (1, SUBLANES, 1), 1)
    d = 1
    while d < SUBLANES:
        tiles = tiles + jnp.where(sub >= d, pltpu.roll(tiles, d, axis=1), 0.0)
        d *= 2
    out = [tiles[0]]
    for i in range(1, n // SUBLANES):
        out.append(tiles[i] + out[-1][SUBLANES - 1:SUBLANES, :])
    return jnp.concatenate(out, axis=0)


def _causal_conv(u, prev8, w, b):
    width = w.shape[0]
    n = u.shape[0]
    ext = jnp.concatenate([prev8, u[0:SUBLANES]], axis=0)
    full = b
    head = b
    for j in range(width):
        sh = width - 1 - j
        if sh == 0:
            full = full + u * w[j:j + 1]
            head = head + u[0:SUBLANES] * w[j:j + 1]
        else:
            full = full + pltpu.roll(u, sh, axis=0) * w[j:j + 1]
            head = head + pltpu.roll(ext, sh, axis=0)[SUBLANES:2 * SUBLANES] * w[j:j + 1]
    if n == SUBLANES:
        return head
    return jnp.concatenate([head, full[SUBLANES:]], axis=0)


def _segment_pitch(n):
    pitch = n // SUBLANES + SUBLANES
    return pitch if (pitch // SUBLANES) % 2 else pitch + SUBLANES


def _to_segments(ref, val):
    nv = val.shape[0] // SUBLANES
    pitch = ref.shape[1] // SUBLANES
    for c in range(ref.shape[0]):
        for s in range(SUBLANES):
            ref[c, s * pitch:s * pitch + nv, :] = val[s * nv:(s + 1) * nv, c * LANES:(c + 1) * LANES]


def _from_segments(ref, n):
    nv = n // SUBLANES
    pitch = ref.shape[1] // SUBLANES
    return jnp.concatenate(
        [jnp.concatenate([ref[c, s * pitch:s * pitch + nv, :] for s in range(SUBLANES)], axis=0)
         for c in range(ref.shape[0])], axis=1)


def _rows_from_segments(ref, r0, n_rows, n):
    nv = n // SUBLANES
    pitch = ref.shape[1] // SUBLANES
    spans = []
    r = r0
    while r < r0 + n_rows:
        s, off = divmod(r, nv)
        take = min(nv - off, r0 + n_rows - r)
        spans.append(slice(s * pitch + off, s * pitch + off + take))
        r += take
    return jnp.concatenate(
        [jnp.concatenate([ref[c, sp, :] for sp in spans], axis=0) for c in range(ref.shape[0])], axis=1)


def _load_interleaved(ref, n):
    pitch = ref.shape[1] // SUBLANES
    return jnp.concatenate(
        [jnp.concatenate([ref[c, pl.ds(i, SUBLANES, stride=pitch), :] for c in range(ref.shape[0])], axis=1)
         for i in range(n // SUBLANES)], axis=0)


def _store_interleaved(ref, val):
    pitch = ref.shape[1] // SUBLANES
    for i in range(val.shape[0] // SUBLANES):
        for c in range(ref.shape[0]):
            ref[c, pl.ds(i, SUBLANES, stride=pitch), :] = val[i * SUBLANES:(i + 1) * SUBLANES, c * LANES:(c + 1) * LANES]


def _causal_conv_interleaved(u, prev8, w, b):
    width = w.shape[0]
    n, m = u.shape
    nv = n // SUBLANES
    sub = lax.broadcasted_iota(jnp.int32, (SUBLANES, 1), 0)
    tiles = u.reshape(nv, SUBLANES, m)

    def back_one(a, before):
        first = jnp.where(sub == 0, before, pltpu.roll(a[nv - 1], 1, axis=0))
        return jnp.concatenate([first[None], a[:nv - 1]], axis=0)

    delayed = [tiles]
    for k in range(1, width):
        delayed.append(back_one(delayed[-1], prev8[SUBLANES - k:SUBLANES - k + 1]))
    out = b
    for j in range(width):
        out = out + delayed[width - 1 - j] * w[j:j + 1]
    hist = tiles[nv - 1]
    for k in range(2, width):
        hist = jnp.where(sub == SUBLANES - k, pltpu.roll(tiles[nv - k], SUBLANES - k + 1, axis=0), hist)
    return out.reshape(n, m), hist


def _mod_kernel(c_ref, w_ref, b_ref, o_ref):
    o_ref[0] = _bdot(_silu(c_ref[...]), w_ref[0]) + b_ref[0]


def _modulation(c, w_mod, b_mod):
    depth, d, n = w_mod.shape
    bc = c.shape[0]
    tn = n // 4
    return pl.pallas_call(
        _mod_kernel,
        grid=(depth, n // tn),
        in_specs=[pl.BlockSpec((bc, d), lambda l, j: (0, 0)),
                  pl.BlockSpec((1, d, tn), lambda l, j: (l, 0, j)),
                  pl.BlockSpec((1, 1, tn), lambda l, j: (l, 0, j))],
        out_specs=pl.BlockSpec((1, bc, tn), lambda l, j: (l, 0, j)),
        out_shape=jax.ShapeDtypeStruct((depth, bc, n), F32),
        compiler_params=_params(2),
        name="modulation",
    )(c, w_mod, b_mod.reshape(depth, 1, n))


def _ev_in_kernel(x_ref, mod_ref, g_ref, w_ref, wg2_ref, bg_ref,
                  u_ref, qk_ref, v_ref, r_ref, gate_ref):
    mod = mod_ref[0]
    hn = _norm_mod(x_ref[0], g_ref[...], mod[1:2], mod[0:1]).astype(BF16)
    dot = lambda lo, hi: jnp.dot(hn, w_ref[:, lo:hi], preferred_element_type=F32)
    gl = dot(2048, 2048 + LANES)
    gate_ref[0] = _log_sigmoid(_bdot(gl, wg2_ref[...]) + bg_ref[...]) * (1.0 / GLA_GATE_NORM)
    u = dot(0, 512)
    for c in range(u_ref.shape[0]):
        u_ref[c] = _chunk_transpose(u[:, c * LANES:(c + 1) * LANES].reshape(-1, SUBLANES, LANES))
    qk_ref[0] = dot(512, 1024)
    v_ref[0] = dot(1024, 1536)
    r_ref[0] = dot(1536, 2048)


def _ev_in(x, mod, norm_g, w_in, w_gate2, b_gate, tb):
    b, t, d = x.shape
    rank = w_gate2.shape[0]
    nk = w_gate2.shape[1]
    wu, wq, wk, wv, wgl, wr = jnp.split(w_in, [512, 768, 1024, 1536, 1536 + rank], axis=1)
    w = jnp.concatenate([wu, wq, wk, wv, wr, wgl, jnp.zeros((d, LANES - rank), F32)], axis=1).astype(BF16)
    wg2 = jnp.concatenate([w_gate2, jnp.zeros((LANES - rank, nk), F32)], axis=0).astype(BF16)
    tile = lambda n: pl.BlockSpec((1, tb, n), lambda i, j: (i, j, 0))
    const = lambda a: pl.BlockSpec(a.shape, lambda i, j: (0,) * a.ndim)
    g2 = norm_g.reshape(1, d)
    bg = b_gate.reshape(1, nk)
    return pl.pallas_call(
        _ev_in_kernel,
        grid=(b, t // tb),
        in_specs=[tile(d), pl.BlockSpec((1, 6, d), lambda i, j: (i, 0, 0)), const(g2), const(w),
                  const(wg2), const(bg)],
        out_specs=[pl.BlockSpec((512 // LANES, tb // S5_SUB, None, SUBLANES, LANES), lambda i, j: (0, j, i, 0, 0)),
                   tile(512), tile(512), tile(512), tile(nk)],
        out_shape=[jax.ShapeDtypeStruct((512 // LANES, t // S5_SUB, b, SUBLANES, LANES), F32)]
        + [jax.ShapeDtypeStruct((b, t, n), F32) for n in (512, 512, 512, nk)],
        compiler_params=_params(2),
        name="ev_in",
    )(x, mod, g2, w, wg2, bg)


def _gla_kernel(qk_ref, v_ref, gate_ref, r_ref, s0_ref, ng_ref, o_ref, sfin_ref, st_scr, *, n_chunks):
    t = pl.program_id(1)
    dk = qk_ref.shape[2] // 2 // GLA_HEADS
    dv = v_ref.shape[2] // GLA_HEADS
    nk = GLA_HEADS * dk
    nv = GLA_HEADS * dv
    blk = (lax.broadcasted_iota(jnp.int32, (nv, nk), 0) // dv
           == lax.broadcasted_iota(jnp.int32, (nv, nk), 1) // dk)

    @pl.when(t == 0)
    def _():
        s0 = jnp.concatenate([s0_ref[0, h] for h in range(GLA_HEADS)], axis=0)
        st_scr[...] = jnp.where(blk, jnp.concatenate([s0] * GLA_HEADS, axis=1), 0.0)

    n_hs = GLA_HEADS * CHUNK
    row_head = lax.broadcasted_iota(jnp.int32, (n_hs, 1), 0) // CHUNK
    own_k = row_head == lax.broadcasted_iota(jnp.int32, (1, nk), 1) // dk
    own_v = row_head == lax.broadcasted_iota(jnp.int32, (1, nv), 1) // dv
    causal = (lax.broadcasted_iota(jnp.int32, (CHUNK, 1), 0)
              >= lax.broadcasted_iota(jnp.int32, (1, n_hs), 1) % CHUNK)
    ng = ng_ref[...]

    def chunk(c, carry):
        rows = pl.ds(pl.multiple_of(c * CHUNK, CHUNK), CHUNK)
        k = qk_ref[0, rows, nk:2 * nk]
        v = v_ref[0, rows, :]
        cum = _cumsum_rows(gate_ref[0, rows, :])
        cum_last = cum[CHUNK - 1:CHUNK, :]
        qe = (qk_ref[0, rows, 0:nk] * dk ** -0.5 * jnp.exp(cum)).astype(BF16)
        ke = (k * jnp.exp(-cum)).astype(BF16)
        kd = (k * jnp.exp(cum_last - cum)).astype(BF16)
        v_b = v.astype(BF16)
        st = st_scr[...]
        ke_bd = jnp.where(own_k, jnp.concatenate([ke] * GLA_HEADS, axis=0), jnp.zeros((), BF16))
        v_bd = jnp.where(own_v, jnp.concatenate([v_b] * GLA_HEADS, axis=0), jnp.zeros((), BF16))
        att = lax.dot_general(qe, ke_bd, _NT, preferred_element_type=F32)
        att = jnp.where(causal, att, 0.0).astype(BF16)
        o = (jnp.dot(att, v_bd, preferred_element_type=F32)
             + lax.dot_general(qe, st.astype(BF16), _NT, preferred_element_type=F32))
        for h in range(GLA_HEADS):
            cols = slice(h * dv, (h + 1) * dv)
            oh = o[:, cols]
            ms = jnp.mean(oh * oh, axis=-1, keepdims=True)
            oh = oh * lax.rsqrt(ms + RMS_EPS) * ng
            o_ref[0, rows, cols] = oh * _silu(r_ref[0, rows, cols])
        upd = jnp.dot(v.T.astype(BF16), kd, preferred_element_type=F32)
        st_scr[...] = jnp.where(blk, st * jnp.exp(cum_last) + upd, 0.0)
        return carry

    lax.fori_loop(0, n_chunks, chunk, 0, unroll=math.gcd(n_chunks, CHUNK_UNROLL))

    @pl.when(t == pl.num_programs(1) - 1)
    def _():
        for h in range(GLA_HEADS):
            sfin_ref[0, h] = st_scr[h * dv:(h + 1) * dv, h * dk:(h + 1) * dk]


def _gla(qk, v, gate, r, s0, norm_g, tb):
    b, t, nv = v.shape
    nk = gate.shape[2]
    dk, dv = nk // GLA_HEADS, nv // GLA_HEADS
    s0t = jnp.swapaxes(s0, 2, 3)
    tile = lambda n: pl.BlockSpec((1, tb, n), lambda i, j: (i, j, 0))
    sspec = pl.BlockSpec((1, GLA_HEADS, dv, dk), lambda i, j: (i, 0, 0, 0))
    ng = norm_g.reshape(1, dv)
    o, sfin = pl.pallas_call(
        functools.partial(_gla_kernel, n_chunks=tb // CHUNK),
        grid=(b, t // tb),
        in_specs=[tile(2 * nk), tile(nv), tile(nk), tile(nv), sspec,
                  pl.BlockSpec((1, dv), lambda i, j: (0, 0))],
        out_specs=[tile(nv), sspec],
        out_shape=[jax.ShapeDtypeStruct((b, t, nv), F32),
                   jax.ShapeDtypeStruct((b, GLA_HEADS, dv, dk), F32)],
        scratch_shapes=[pltpu.VMEM((nv, nk), F32)],
        compiler_params=_params(2),
        name="gla",
    )(qk, v, gate, r, s0t, ng)
    return o, jnp.swapaxes(sfin, 2, 3)


def _chunk_transpose(x):
    s = lax.broadcasted_iota(jnp.int32, (1, SUBLANES, LANES), 1)
    c = lax.broadcasted_iota(jnp.int32, (1, SUBLANES, LANES), 2) // S5_GROUP
    for d in (4, 2, 1):
        sb = (s & d) != 0
        cb = (c & d) != 0
        if 2 * d == SUBLANES:
            t = pltpu.roll(pltpu.roll(x, d, axis=1), S5_GROUP * d, axis=2)
        else:
            xs = jnp.where(sb, pltpu.roll(x, d, axis=1), pltpu.roll(x, SUBLANES - d, axis=1))
            t = jnp.where(cb, pltpu.roll(xs, S5_GROUP * d, axis=2), pltpu.roll(xs, LANES - S5_GROUP * d, axis=2))
        x = jnp.where(sb != cb, t, x)
    return x


def _s5_kernel(u_ref, w1_ref, n_ref, a_ref, d_ref, x0_ref, y_ref, xf_ref, mm_scr, xs_scr, st_scr, *, tk, bsz):
    j = pl.program_id(1)
    n_g = st_scr.shape[0]
    rows = tk * bsz
    of_group = lambda g: pl.ds(g, rows, stride=n_g)

    @pl.when(j == 0)
    def _():
        st_scr[...] = x0_ref[0]

    for g in range(n_g):
        ub = u_ref[of_group(g), :].astype(BF16)
        mm_scr[g] = jnp.dot(ub, w1_ref[0, g], preferred_element_type=F32)

    half = n_g // 2
    for g0 in (0, half):
        coef = [a_ref[0, g] for g in range(g0, g0 + half)]

        def step(k, carry, g0=g0, coef=coef):
            r = pl.ds(pl.multiple_of(k * bsz, bsz), bsz)
            out = []
            for i in range(half):
                x, xsw = carry[2 * i], carry[2 * i + 1]
                a1, a2, a2s = coef[i][0:1], coef[i][1:2], coef[i][2:3]
                xs_scr[g0 + i, r, :] = x
                out.append(a1 * x + a2 * xsw + mm_scr[g0 + i, r, LANES:2 * LANES])
                out.append(a1 * xsw + a2s * x + mm_scr[g0 + i, r, 2 * LANES:3 * LANES])
            return tuple(out)

        fin = lax.fori_loop(0, tk, step, tuple(st_scr[g0 + i, v] for i in range(half) for v in range(2)))
        for i in range(half):
            st_scr[g0 + i, 0] = fin[2 * i]
            st_scr[g0 + i, 1] = fin[2 * i + 1]

    for g in range(n_g):
        y = (mm_scr[g, :, 0:LANES] + jnp.dot(xs_scr[g].astype(BF16), n_ref[0, g], preferred_element_type=F32)
             + d_ref[0, g] * u_ref[of_group(g), :])
        y_ref[of_group(g), :] = jax.nn.gelu(y)

    @pl.when(j == pl.num_programs(1) - 1)
    def _():
        xf_ref[0] = st_scr[:, 0]


def _cmul(ar, ai, br, bi):
    return ar * br - ai * bi, ar * bi + ai * br


def _s5_matrices(a_re, a_im, log_dt, b_re, b_im, c_re, c_im):
    g, p = a_re.shape
    hdim = b_re.shape[-1]
    dt = jnp.exp(log_dt)[:, None]
    mag = jnp.exp(a_re * dt)
    ab_re, ab_im = mag * jnp.cos(a_im * dt), mag * jnp.sin(a_im * dt)
    den = a_re * a_re + a_im * a_im
    num_re, num_im = ab_re - 1.0, ab_im
    g_re = (num_re * a_re + num_im * a_im) / den
    g_im = (num_im * a_re - num_re * a_im) / den
    bb_re = g_re[..., None] * b_re - g_im[..., None] * b_im
    bb_im = g_re[..., None] * b_im + g_im[..., None] * b_re
    pw_re, pw_im = jnp.ones((1, g, p), F32), jnp.zeros((1, g, p), F32)
    sq_re, sq_im = ab_re, ab_im
    while pw_re.shape[0] < S5_SUB + 1:
        nr, ni = _cmul(pw_re, pw_im, sq_re, sq_im)
        pw_re, pw_im = jnp.concatenate([pw_re, nr]), jnp.concatenate([pw_im, ni])
        sq_re, sq_im = _cmul(sq_re, sq_im, sq_re, sq_im)
    pw_re, pw_im = pw_re[:S5_SUB + 1], pw_im[:S5_SUB + 1]
    ca_re, ca_im = _cmul(c_re[None], c_im[None], pw_re[:, :, None, :], pw_im[:, :, None, :])
    kern = (jnp.einsum('tghp,gpk->tghk', ca_re[:S5_SUB], bb_re, precision=_HI)
            - jnp.einsum('tghp,gpk->tghk', ca_im[:S5_SUB], bb_im, precision=_HI))
    s_in = np.arange(S5_SUB)[:, None, None]
    s_out = np.arange(S5_SUB)[None, :, None]
    lag = jnp.asarray(s_out - s_in == np.arange(S5_SUB), F32)
    m = jnp.einsum('iot,tghk->ioghk', lag, kern, precision=_HI)
    m = jnp.transpose(m, (2, 0, 4, 1, 3)).reshape(g, S5_SUB * hdim, S5_SUB * hdim)
    pr, pi = _cmul(pw_re[S5_SUB - 1::-1][..., None], pw_im[S5_SUB - 1::-1][..., None], bb_re[None], bb_im[None])
    flat = lambda a: jnp.transpose(a, (1, 0, 3, 2)).reshape(g, S5_SUB * hdim, p)
    nm = jnp.concatenate([ca_re[1:], -ca_im[1:]], axis=3)
    nm = jnp.transpose(nm, (1, 3, 0, 2)).reshape(g, 2 * p, S5_SUB * hdim)
    w1 = jnp.concatenate([m, flat(pr), flat(pi), flat(pi), flat(pr)], axis=2)
    a_r, a_i = pw_re[S5_SUB], pw_im[S5_SUB]
    coef = jnp.stack([jnp.concatenate([a_r, a_r], axis=1), jnp.concatenate([-a_i, a_i], axis=1),
                      jnp.concatenate([a_i, -a_i], axis=1)], axis=1)
    return w1.astype(BF16), nm.astype(BF16), coef


def _s5(u2, mats, d_skip, x0_re, x0_im):
    w1, nm, coef = mats
    n_ct, n_sub, b, n_g, lanes = u2.shape
    g, p2 = nm.shape[0], nm.shape[1]
    hdim = lanes // S5_SUB
    tk = min(n_sub, S5_MAX_SUBS)
    x0 = jnp.transpose(jnp.concatenate([x0_re, x0_im], axis=2), (1, 0, 2))
    x0 = jnp.stack([x0, jnp.roll(x0, p2 // 2, axis=2)], axis=1)
    dt = jnp.tile(d_skip.reshape(g, 1, hdim), (1, 1, S5_SUB))
    by_tile = lambda a: a.reshape((n_ct, n_g) + a.shape[1:])
    pspec = lambda a: pl.BlockSpec((1,) + a.shape[1:], lambda c, j: (c,) + (0,) * (a.ndim - 1))
    rows = tk * b
    uspec = pl.BlockSpec((None, rows * n_g, lanes), lambda c, j: (c, j, 0))
    params = [by_tile(a) for a in (w1, nm, coef, dt, x0)]
    y, xf = pl.pallas_call(
        functools.partial(_s5_kernel, tk=tk, bsz=b),
        grid=(n_ct, n_sub // tk),
        in_specs=[uspec] + [pspec(a) for a in params],
        out_specs=[uspec, pl.BlockSpec((1, n_g, b, p2), lambda c, j: (c, 0, 0, 0))],
        out_shape=[jax.ShapeDtypeStruct((n_ct, n_sub * b * n_g, lanes), F32),
                   jax.ShapeDtypeStruct((n_ct, n_g, b, p2), F32)],
        scratch_shapes=[pltpu.VMEM((n_g, rows, 3 * lanes), F32), pltpu.VMEM((n_g, rows, lanes), F32),
                        pltpu.VMEM((n_g, 2, b, p2), F32)],
        compiler_params=_params(2),
        name="s5",
    )(u2.reshape(n_ct, n_sub * b * n_g, lanes), *params)
    xf = jnp.transpose(xf.reshape(g, b, p2), (1, 0, 2))
    return y.reshape(u2.shape), xf[:, :, :p2 // 2], xf[:, :, p2 // 2:]


def _out_ffn_kernel(*refs, with_s5, cb):
    if with_s5:
        (x_ref, mod_ref, m1_ref, m2_ref, wglu_ref, bglu_ref, wout_ref, n2g_ref,
         wup_ref, cw_ref, cbias_ref, wd_ref, pre_ref, xo_ref, ulast_ref, carry_scr, h_scr, row_scr) = refs
    else:
        (x_ref, mod_ref, m1_ref, m2_ref, wout_ref, n2g_ref,
         wup_ref, cw_ref, cbias_ref, wd_ref, pre_ref, xo_ref, ulast_ref, carry_scr, h_scr, row_scr) = refs
    t = pl.program_id(1)

    @pl.when(t == 0)
    def _():
        carry_scr[...] = pre_ref[0]

    mod = mod_ref[0]
    half = m2_ref.shape[2]
    out2 = jnp.dot(m2_ref[0].astype(BF16), wout_ref[half:, :], preferred_element_type=F32)
    if with_s5:
        tiles = []
        glu = bglu_ref[...]
        for c in range(m1_ref.shape[0]):
            tiles.append(_chunk_transpose(m1_ref[c]).reshape(-1, LANES))
            glu = glu + jnp.dot(tiles[-1].astype(BF16), wglu_ref[c * LANES:(c + 1) * LANES, :],
                                preferred_element_type=F32)
        m1 = jnp.concatenate(tiles, axis=1) * jax.nn.sigmoid(glu)
    else:
        m1 = m1_ref[0]
    out = jnp.dot(m1.astype(BF16), wout_ref[0:half, :], preferred_element_type=F32) + out2
    x1 = x_ref[0] + mod[2:3] * out
    tb = x1.shape[0]
    _to_segments(row_scr, _norm_mod(x1, n2g_ref[...], mod[4:5], mod[3:4]))
    hn = _load_interleaved(row_scr, tb).astype(BF16)
    dff = wd_ref.shape[0]
    n_cb = dff // cb
    a_cols = lambda j: slice(j * cb, (j + 1) * cb)
    g_cols = lambda j: slice(dff + j * cb, dff + (j + 1) * cb)

    def up(j):
        return (jnp.dot(hn, wup_ref[:, a_cols(j)], preferred_element_type=F32),
                jnp.dot(hn, wup_ref[:, g_cols(j)], preferred_element_type=F32))

    def conv(u, cols):
        out, carry_scr[:, cols] = _causal_conv_interleaved(u, carry_scr[:, cols], cw_ref[:, cols], cbias_ref[:, cols])
        return out

    nxt = up(0)
    for j in range(n_cb):
        ua, ug = nxt
        if j + 1 < n_cb:
            nxt = up(j + 1)
        h_scr[:, a_cols(j)] = (_silu(conv(ua, a_cols(j))) * conv(ug, g_cols(j))).astype(BF16)
    _store_interleaved(row_scr, jnp.dot(h_scr[...], wd_ref[...], preferred_element_type=F32))
    xo_ref[0] = x1 + mod[5:6] * _from_segments(row_scr, tb)

    @pl.when(t == pl.num_programs(1) - 1)
    def _():
        ulast_ref[0] = carry_scr[...]


def _out_ffn(x, mod, m1, m2, w_out, norm2_g, w_up, conv_w, conv_b, w_down, prefix, tb, s5_extra=None):
    b, t, d = x.shape
    dff = w_down.shape[0]
    assert dff % FFN_COL_BLOCK == 0
    width = conv_w.shape[0]
    half = m2.shape[2]
    w_up_b = w_up.astype(BF16)
    cbias = conv_b.reshape(1, 2 * dff)
    wd = w_down.astype(BF16)
    pre8 = jnp.concatenate([jnp.zeros((b, SUBLANES - (width - 1), 2 * dff), F32), prefix], axis=1)
    tile = lambda n: pl.BlockSpec((1, tb, n), lambda i, j: (i, j, 0))
    const = lambda a: pl.BlockSpec(a.shape, lambda i, j: (0,) * a.ndim, pipeline_mode=pl.Buffered(1))
    bspec = lambda a: pl.BlockSpec((1,) + a.shape[1:], lambda i, j: (i,) + (0,) * (a.ndim - 1))
    wo = w_out.astype(BF16)
    n2g = norm2_g.reshape(1, d)
    args = [x, mod]
    specs = [tile(d), bspec(mod)]
    if s5_extra is not None:
        w_glu, b_glu = s5_extra
        w_glu, b_glu = w_glu.astype(BF16), b_glu.reshape(1, half)
        args += [m1, m2, w_glu, b_glu]
        specs += [pl.BlockSpec((m1.shape[0], tb // S5_SUB, None) + m1.shape[3:], lambda i, j: (0, j, i, 0, 0)),
                  tile(m2.shape[2]), const(w_glu), const(b_glu)]
    else:
        args += [m1, m2]
        specs += [tile(half), tile(m2.shape[2])]
    weights = [wo, n2g, w_up_b, conv_w, cbias, wd]
    args += weights + [pre8]
    specs += [const(a) for a in weights] + [bspec(pre8)]
    xo, ulast = pl.pallas_call(
        functools.partial(_out_ffn_kernel, with_s5=s5_extra is not None, cb=FFN_COL_BLOCK),
        grid=(b, t // tb),
        in_specs=specs,
        out_specs=[tile(d), bspec(pre8)],
        out_shape=[jax.ShapeDtypeStruct((b, t, d), F32), jax.ShapeDtypeStruct(pre8.shape, F32)],
        scratch_shapes=[pltpu.VMEM(pre8.shape[1:], F32), pltpu.VMEM((tb, dff), BF16),
                        pltpu.VMEM((d // LANES, SUBLANES * _segment_pitch(tb), LANES), F32)],
        compiler_params=_params(2),
        name="out_ffn_s5" if s5_extra is not None else "out_ffn",
    )(*args)
    return xo, ulast[:, SUBLANES - (width - 1):]


def _head_rms(x, g):
    left = lax.broadcasted_iota(jnp.int32, (1, LANES), 1) < HEAD_DIM
    out = []
    for c in range(x.shape[1] // LANES):
        xt = x[:, c * LANES:(c + 1) * LANES]
        sq = xt * xt
        s_left = jnp.sum(jnp.where(left, sq, 0.0), axis=-1, keepdims=True)
        s_right = jnp.sum(jnp.where(left, 0.0, sq), axis=-1, keepdims=True)
        ms = jnp.where(left, s_left, s_right) * (1.0 / HEAD_DIM)
        out.append(xt * lax.rsqrt(ms + RMS_EPS))
    return jnp.concatenate(out, axis=1) * g


def _od_in_kernel(x_ref, mod_ref, g_ref, w_ref, qg_ref, kg_ref, dtb_ref,
                  q_ref, k_ref, v_ref, zg_ref, xbc_ref, dt_ref):
    mod = mod_ref[0]
    nq = q_ref.shape[2]
    nkv = k_ref.shape[2]
    hn = _norm_mod(x_ref[0], g_ref[...], mod[1:2], mod[0:1]).astype(BF16)
    dot = lambda lo, hi: jnp.dot(hn, w_ref[:, lo:hi], preferred_element_type=F32)
    o_zg = nq + 2 * nkv
    o_xbc = o_zg + zg_ref.shape[2]
    o_dt = o_xbc + xbc_ref.shape[2]
    dt_ref[0] = _softplus(dot(o_dt, o_dt + LANES) + dtb_ref[...])
    q_ref[0] = _head_rms(dot(0, nq), qg_ref[...])
    kv = dot(nq, nq + 2 * nkv)
    k_ref[0] = _head_rms(kv[:, 0:nkv], kg_ref[...])
    v_ref[0] = kv[:, nkv:2 * nkv]
    zg_ref[0] = dot(o_zg, o_xbc)
    xbc_ref[0] = dot(o_xbc, o_dt)


def _od_in(x, mod, norm_g, w_in, q_norm, k_norm, dt_bias, dims, tb):
    b, t, d = x.shape
    nq, nkv, nz, nxbc, nh = dims
    w = jnp.concatenate([w_in, jnp.zeros((d, LANES - nh), F32)], axis=1).astype(BF16)
    qg = jnp.tile(q_norm, nq // HEAD_DIM).reshape(1, nq)
    kg = jnp.tile(k_norm, nkv // HEAD_DIM).reshape(1, nkv)
    dtb = jnp.concatenate([dt_bias, jnp.zeros((LANES - nh,), F32)]).reshape(1, LANES)
    g2 = norm_g.reshape(1, d)
    tile = lambda n: pl.BlockSpec((1, tb, n), lambda i, j: (i, j, 0))
    const = lambda a: pl.BlockSpec(a.shape, lambda i, j: (0,) * a.ndim)
    widths = (nq, nkv, nkv, nz, nxbc, LANES)
    return pl.pallas_call(
        _od_in_kernel,
        grid=(b, t // tb),
        in_specs=[tile(d), pl.BlockSpec((1, 6, d), lambda i, j: (i, 0, 0)), const(g2), const(w),
                  const(qg), const(kg), const(dtb)],
        out_specs=[tile(n) for n in widths],
        out_shape=[jax.ShapeDtypeStruct((b, t, n), F32) for n in widths],
        compiler_params=_params(2),
        name="od_in",
    )(x, mod, g2, w, qg, kg, dtb)


def _swa_kernel(q_ref, k_ref, v_ref, k0_ref, v0_ref, bias_ref, sink_ref, o_ref, kx, vx,
                *, n_blocks, nq, mask_start):
    t = pl.program_id(1)
    tb = q_ref.shape[1]
    n_kv = k_ref.shape[2] // HEAD_DIM
    rows_q = nq * CHUNK
    tail = kx.shape[1] - WINDOW - tb
    left = lax.broadcasted_iota(jnp.int32, (1, LANES), 1) < HEAD_DIM
    first_side = lax.broadcasted_iota(jnp.int32, (2 * SWA_KEYS, 1), 0) < SWA_KEYS
    side_ones = jnp.where(first_side == left, 1.0, 0.0).astype(BF16)
    col = lax.broadcasted_iota(jnp.int32, (1, SWA_KEYS), 1)

    def place(dst, rows, x):
        rolled = pltpu.roll(x, HEAD_DIM, axis=1)
        dst[0, rows, :] = jnp.where(left, x, 0.0).astype(BF16)
        dst[1, rows, :] = jnp.where(left, 0.0, rolled).astype(BF16)
        dst[2, rows, :] = jnp.where(left, rolled, 0.0).astype(BF16)
        dst[3, rows, :] = jnp.where(left, 0.0, x).astype(BF16)

    @pl.when(t == 0)
    def _():
        place(kx, slice(0, WINDOW), k0_ref[0])
        place(vx, slice(0, WINDOW), v0_ref[0])
        for i in range(2 * n_kv if tail else 0):
            kx[i, WINDOW + tb:, :] = jnp.zeros((tail, LANES), BF16)
            vx[i, WINDOW + tb:, :] = jnp.zeros((tail, LANES), BF16)

    place(kx, slice(WINDOW, WINDOW + tb), k_ref[0])
    place(vx, slice(WINDOW, WINDOW + tb), v_ref[0])

    def block(blk, carry):
        r0 = pl.multiple_of(blk * rows_q, rows_q)
        keys = pl.ds(r0, SWA_KEYS)
        if mask_start:
            valid = t * tb + r0 - WINDOW + col >= 0
        probs = []
        sinks = []
        for j in range(n_kv):
            lo = 2 * j * LANES
            qg = jnp.concatenate([q_ref[0, pl.ds(r0 + cq * CHUNK, CHUNK), lo + r * LANES:lo + (r + 1) * LANES]
                                  for cq in range(nq) for r in range(2)], axis=0).astype(BF16)
            kcat = jnp.concatenate([kx[2 * j, keys, :], kx[2 * j + 1, keys, :]], axis=0)
            s_both = lax.dot_general(qg, kcat, _NT, preferred_element_type=F32) * HEAD_DIM ** -0.5
            for side in range(2):
                s = s_both[:, side * SWA_KEYS:(side + 1) * SWA_KEYS] + bias_ref[j, side]
                if mask_start:
                    s = jnp.where(valid, s, NEG_INF)
                m = jnp.max(s, axis=-1, keepdims=True)
                probs.append(jnp.exp(s - m).astype(BF16))
                sinks.append(jnp.exp(sink_ref[j, side] - m))
        for j in range(n_kv):
            lo = 2 * j * LANES
            vcat = jnp.concatenate(
                [jnp.concatenate([vx[2 * j, keys, :], vx[2 * j + 1, keys, :]], axis=0), side_ones], axis=1)
            pv = jnp.dot(jnp.concatenate(probs[2 * j:2 * j + 2], axis=1), vcat, preferred_element_type=F32)
            den = pv[:, LANES:] + jnp.where(left, sinks[2 * j], sinks[2 * j + 1])
            out = pv[:, 0:LANES] / den
            for cq in range(nq):
                for r in range(2):
                    o_ref[0, pl.ds(r0 + cq * CHUNK, CHUNK), lo + r * LANES:lo + (r + 1) * LANES] = (
                        out[(2 * cq + r) * CHUNK:(2 * cq + r + 1) * CHUNK])
        return carry

    lax.fori_loop(0, n_blocks, block, 0, unroll=math.gcd(n_blocks, 2))
    for i in range(2 * n_kv):
        kx[i, 0:WINDOW, :] = kx[i, tb:tb + WINDOW, :]
        vx[i, 0:WINDOW, :] = vx[i, tb:tb + WINDOW, :]


def _t5_bucket(rel):
    nb = T5_BUCKETS // 2
    max_exact = nb // 2
    ret = (rel > 0).astype(jnp.int32) * nb
    n = jnp.abs(rel)
    nf = jnp.maximum(n, 1).astype(F32)
    large = max_exact + (jnp.log(nf / max_exact) / math.log(T5_MAX_DIST / max_exact)
                         * (nb - max_exact)).astype(jnp.int32)
    large = jnp.minimum(large, nb - 1)
    return ret + jnp.where(n < max_exact, n, large)


def _rel_bias(table):
    rel = (jnp.arange(WINDOW + CHUNK)[None, :] - WINDOW) - jnp.arange(CHUNK)[:, None]
    onehot = (_t5_bucket(rel)[..., None] == jnp.arange(T5_BUCKETS)).astype(F32)
    return jnp.einsum('qkb,bh->hqk', onehot, table, precision=_HI)


def _swa(q, k, v, k0, v0, bias, sink, tb, mask_start):
    b, t, nq = q.shape
    nkv = k.shape[2]
    n_kv = nkv // HEAD_DIM
    assert nkv == LANES and nq == 2 * n_kv * LANES
    by_side = lambda a: jnp.transpose(a.reshape((n_kv, 2, 2) + a.shape[1:]), (0, 2, 1) + tuple(range(3, a.ndim + 2)))
    span = WINDOW + CHUNK
    n_chunks = tb // CHUNK
    cpb = 2 if n_chunks % 2 == 0 else 1
    assert WINDOW + cpb * CHUNK <= SWA_KEYS
    bias2 = by_side(bias).reshape(n_kv, 2, 2 * CHUNK, span)
    bias3 = jnp.concatenate(
        [jnp.pad(bias2, ((0, 0), (0, 0), (0, 0), (cq * CHUNK, SWA_KEYS - span - cq * CHUNK)),
                 constant_values=NEG_INF) for cq in range(cpb)], axis=2)
    sink3 = jnp.tile(jnp.repeat(by_side(sink), CHUNK, axis=2), (1, 1, cpb))
    sink3 = jnp.broadcast_to(sink3[..., None], sink3.shape + (LANES,))
    tile = lambda n: pl.BlockSpec((1, tb, n), lambda i, j: (i, j, 0))
    wspec = pl.BlockSpec((1, WINDOW, nkv), lambda i, j: (i, 0, 0))
    const = lambda a: pl.BlockSpec(a.shape, lambda i, j: (0,) * a.ndim)
    rows = WINDOW + max(tb, WINDOW) + SWA_KEYS - WINDOW - cpb * CHUNK
    return pl.pallas_call(
        functools.partial(_swa_kernel, n_blocks=n_chunks // cpb, nq=cpb, mask_start=mask_start),
        grid=(b, t // tb),
        in_specs=[tile(nq), tile(nkv), tile(nkv), wspec, wspec, const(bias3), const(sink3)],
        out_specs=tile(nq),
        out_shape=jax.ShapeDtypeStruct((b, t, nq), F32),
        scratch_shapes=[pltpu.VMEM((2 * n_kv, rows, LANES), BF16), pltpu.VMEM((2 * n_kv, rows, LANES), BF16)],
        compiler_params=_params(2),
        name="swa",
    )(q, k, v, k0, v0, bias3, sink3)


def _ssd_kernel(xbc_ref, zg_ref, dt_ref, pre_ref, cw_ref, cb_ref, aexp_ref, dexp_ref,
                e_ref, ng_ref, s0_ref, y_ref, sfin_ref, clast_ref, st_scr, hist_scr, seg_scr, dte_scr,
                *, n_chunks):
    t = pl.program_id(1)
    tb = xbc_ref.shape[1]
    inner = zg_ref.shape[2]
    n_heads = inner // SSD_HEAD_DIM
    hpg = n_heads // SSD_GROUPS
    gn = SSD_GROUPS * SSD_STATE
    n_hs = n_heads * CHUNK

    @pl.when(t == 0)
    def _():
        st_scr[...] = s0_ref[0]
        hist_scr[...] = pre_ref[0]

    _to_segments(seg_scr, xbc_ref[0])
    conv, hist_scr[...] = _causal_conv_interleaved(_load_interleaved(seg_scr, tb), hist_scr[...],
                                                   cw_ref[...], cb_ref[...])
    _store_interleaved(seg_scr, _silu(conv))
    dte_scr[...] = _split_dot(dt_ref[0], e_ref[...])

    row_hs = lax.broadcasted_iota(jnp.int32, (n_hs, 1), 0)
    own_group = row_hs // (hpg * CHUNK) == lax.broadcasted_iota(jnp.int32, (1, gn), 1) // SSD_STATE
    own_head = row_hs // CHUNK == lax.broadcasted_iota(jnp.int32, (1, inner), 1) // SSD_HEAD_DIM
    state_group = (lax.broadcasted_iota(jnp.int32, (gn, 1), 0) // SSD_STATE
                   == lax.broadcasted_iota(jnp.int32, (1, inner), 1) // (hpg * SSD_HEAD_DIM))
    first_group = lax.broadcasted_iota(jnp.int32, (1, inner), 1) < hpg * SSD_HEAD_DIM
    step_row = lax.broadcasted_iota(jnp.int32, (CHUNK, 1), 0)
    step_lane = lax.broadcasted_iota(jnp.int32, (1, n_hs), 1) % CHUNK
    zero = jnp.zeros((), BF16)
    aexp = aexp_ref[...]

    for c in range(n_chunks):
        rows = slice(c * CHUNK, (c + 1) * CHUNK)
        xc = _rows_from_segments(seg_scr, c * CHUNK, CHUNK, tb)
        xs = xc[:, 0:inner]
        bm = xc[:, inner:inner + gn]
        cm = xc[:, inner + gn:inner + 2 * gn].astype(BF16)
        dte = dte_scr[rows, :]
        cum = _cumsum_rows(dte * aexp)
        cum_last = cum[CHUNK - 1:CHUNK, :]
        cum_at_step = jnp.sum(jnp.where(step_row == step_lane, cum, 0.0), axis=0, keepdims=True)
        decay = jnp.exp(jnp.where(step_row >= step_lane, cum - cum_at_step, NEG_INF))
        bm_bd = jnp.where(own_group, jnp.concatenate([bm.astype(BF16)] * n_heads, axis=0), zero)
        cb = lax.dot_general(cm, bm_bd, _NT, preferred_element_type=F32)
        xdt = (xs * dte).astype(BF16)
        xdt_bd = jnp.where(own_head, jnp.concatenate([xdt] * n_heads, axis=0), zero)
        st = st_scr[...]
        st_bd = jnp.where(state_group, jnp.concatenate([st.astype(BF16)] * SSD_GROUPS, axis=0), zero)
        y = (jnp.dot((cb * decay).astype(BF16), xdt_bd, preferred_element_type=F32)
             + jnp.exp(cum) * jnp.dot(cm, st_bd, preferred_element_type=F32))
        xw = (xs * (jnp.exp(cum_last - cum) * dte)).astype(BF16)
        upd = jnp.dot(bm.T.astype(BF16), xw, preferred_element_type=F32)
        st_scr[...] = st * jnp.exp(cum_last) + jnp.where(first_group, upd[0:SSD_STATE], upd[SSD_STATE:])
        yd = (y + dexp_ref[...] * xs) * _silu(zg_ref[0, rows, :])
        ms = jnp.mean(yd * yd, axis=-1, keepdims=True)
        y_ref[0, rows, :] = yd * lax.rsqrt(ms + RMS_EPS) * ng_ref[...]

    @pl.when(t == pl.num_programs(1) - 1)
    def _():
        sfin_ref[0] = st_scr[...]
        clast_ref[0] = hist_scr[...]


def _ssd(xbc, zg, dt, conv_prefix, conv_w, conv_b, a_log, d_skip, norm_g, s0, tb):
    b, t, nxbc = xbc.shape
    inner = zg.shape[2]
    nh = a_log.shape[0]
    width = conv_w.shape[0]
    assert SSD_GROUPS == 2 and SSD_HEAD_DIM == CHUNK
    pre8 = jnp.concatenate([jnp.zeros((b, SUBLANES - (width - 1), nxbc), F32), conv_prefix], axis=1)
    a = -jnp.exp(a_log)
    aexp = jnp.repeat(a, SSD_HEAD_DIM).reshape(1, inner)
    dexp = jnp.repeat(d_skip, SSD_HEAD_DIM).reshape(1, inner)
    expand = (np.arange(LANES)[:, None] == np.arange(inner)[None, :] // SSD_HEAD_DIM)
    expand = jnp.asarray(expand, BF16)
    s0t = jnp.transpose(s0, (0, 3, 1, 2)).reshape(b, SSD_STATE, inner)
    ng = norm_g.reshape(1, inner)
    cb2 = conv_b.reshape(1, nxbc)
    tile = lambda n: pl.BlockSpec((1, tb, n), lambda i, j: (i, j, 0))
    const = lambda arr: pl.BlockSpec(arr.shape, lambda i, j: (0,) * arr.ndim)
    bspec = lambda arr: pl.BlockSpec((1,) + arr.shape[1:], lambda i, j: (i,) + (0,) * (arr.ndim - 1))
    y, sfin, clast = pl.pallas_call(
        functools.partial(_ssd_kernel, n_chunks=tb // CHUNK),
        grid=(b, t // tb),
        in_specs=[tile(nxbc), tile(inner), tile(LANES),
                  bspec(pre8), const(conv_w), const(cb2), const(aexp), const(dexp),
                  const(expand), const(ng), bspec(s0t)],
        out_specs=[tile(inner), bspec(s0t), bspec(pre8)],
        out_shape=[jax.ShapeDtypeStruct((b, t, inner), F32), jax.ShapeDtypeStruct(s0t.shape, F32),
                   jax.ShapeDtypeStruct(pre8.shape, F32)],
        scratch_shapes=[pltpu.VMEM((SSD_STATE, inner), F32), pltpu.VMEM((SUBLANES, nxbc), F32),
                        pltpu.VMEM((nxbc // LANES, SUBLANES * _segment_pitch(tb), LANES), F32),
                        pltpu.VMEM((tb, inner), F32)],
        compiler_params=_params(2),
        name="ssd",
    )(xbc, zg, dt, pre8, conv_w, cb2, aexp, dexp, expand, ng, s0t)
    sfin = jnp.transpose(sfin.reshape(b, SSD_STATE, nh, SSD_HEAD_DIM), (0, 2, 3, 1))
    return y, sfin, clast[:, SUBLANES - (width - 1):]


def _trunk(x, mods, P, st, sample):
    b, t, d = x.shape
    tb = min(MAX_TILE, t)
    depth = P['w_mod'].shape[0]
    new = {name: [] for name in ('s5_re', 's5_im', 'gla', 'swa_k', 'swa_v', 'ssd', 'ssd_conv', 'ffn_conv')}
    for layer in range(depth):
        i = layer // 2
        mod = mods[layer].reshape(b, 6, d)
        ffn = (P['norm2_g'][layer], P['ffn_w_up'][layer], P['ffn_conv_w'][layer], P['ffn_conv_b'][layer],
               P['ffn_w_down'][layer], st['ffn_conv'][layer], tb)
        if layer % 2 == 0:
            u, qk, v, r, gate = _ev_in(x, mod, P['norm1_g'][layer], P['ev_w_in'][i], P['gla_w_gate2'][i],
                                       P['gla_b_gate'][i], tb)
            mats = _s5_matrices(P['s5_a_re'][i], P['s5_a_im'][i], P['s5_log_dt'][i], P['s5_b_re'][i],
                                P['s5_b_im'][i], P['s5_c_re'][i], P['s5_c_im'][i])
            ya, sr, si = _s5(u, mats, P['s5_d'][i], st['s5_re'][i], st['s5_im'][i])
            ob, sg = _gla(qk, v, gate, r, st['gla'][i], P['gla_norm_g'][i], tb)
            new['s5_re'].append(sr)
            new['s5_im'].append(si)
            new['gla'].append(sg)
            x, fp = _out_ffn(x, mod, ya, ob, P['ev_w_out'][i], *ffn,
                             s5_extra=(P['s5_w_glu'][i], P['s5_b_glu'][i]))
        else:
            nq = P['swa_sink'].shape[1] * HEAD_DIM
            nkv = SWA_KV_HEADS * HEAD_DIM
            inner = P['ssd_norm_g'].shape[1]
            nxbc = P['ssd_conv_w'].shape[2]
            nh = P['ssd_a_log'].shape[1]
            q, k, v, zg, xbc, dt = _od_in(x, mod, P['norm1_g'][layer], P['od_w_in'][i], P['swa_q_norm'][i],
                                               P['swa_k_norm'][i], P['ssd_dt_bias'][i],
                                               (nq, nkv, inner, nxbc, nh), tb)
            bias = _rel_bias(P['t5_bias'])
            if sample:
                k0 = st['swa_k'][i].reshape(b, WINDOW, nkv)
                v0 = st['swa_v'][i].reshape(b, WINDOW, nkv)
            else:
                k0 = v0 = jnp.zeros((b, WINDOW, nkv), F32)
            oc = _swa(q, k, v, k0, v0, bias, P['swa_sink'][i], tb, mask_start=not sample)
            yd, ss, sc = _ssd(xbc, zg, dt, st['ssd_conv'][i], P['ssd_conv_w'][i], P['ssd_conv_b'][i],
                              P['ssd_a_log'][i], P['ssd_d'][i], P['ssd_norm_g'][i], st['ssd'][i], tb)
            keep = slice(None) if sample else slice(t - WINDOW, t)
            new['swa_k'].append(k[:, keep].reshape(b, -1, SWA_KV_HEADS, HEAD_DIM))
            new['swa_v'].append(v[:, keep].reshape(b, -1, SWA_KV_HEADS, HEAD_DIM))
            new['ssd'].append(ss)
            new['ssd_conv'].append(sc)
            x, fp = _out_ffn(x, mod, oc, yd, P['od_w_out'][i], *ffn)
        new['ffn_conv'].append(fp)
    return x, {name: jnp.stack(vals) for name, vals in new.items()}


def kernel(x_prompt, x_sample, state_s5_re, state_s5_im, state_gla, cache_swa_k, cache_swa_v, state_ssd, state_ssd_conv, state_ffn_conv, c_prompt, c_sample, t5_bias, norm1_g, norm2_g, w_mod, b_mod, ffn_w_up, ffn_conv_w, ffn_conv_b, ffn_w_down, ev_w_in, ev_w_out, s5_a_re, s5_a_im, s5_log_dt, s5_b_re, s5_b_im, s5_c_re, s5_c_im, s5_d, s5_w_glu, s5_b_glu, gla_w_gate2, gla_b_gate, gla_norm_g, od_w_in, od_w_out, swa_q_norm, swa_k_norm, swa_sink, ssd_conv_w, ssd_conv_b, ssd_dt_bias, ssd_a_log, ssd_d, ssd_norm_g):
    P = dict(t5_bias=t5_bias, norm1_g=norm1_g, norm2_g=norm2_g, w_mod=w_mod, b_mod=b_mod,
             ffn_w_up=ffn_w_up, ffn_conv_w=ffn_conv_w, ffn_conv_b=ffn_conv_b, ffn_w_down=ffn_w_down,
             ev_w_in=ev_w_in, ev_w_out=ev_w_out, s5_a_re=s5_a_re, s5_a_im=s5_a_im, s5_log_dt=s5_log_dt,
             s5_b_re=s5_b_re, s5_b_im=s5_b_im, s5_c_re=s5_c_re, s5_c_im=s5_c_im, s5_d=s5_d,
             s5_w_glu=s5_w_glu, s5_b_glu=s5_b_glu, gla_w_gate2=gla_w_gate2, gla_b_gate=gla_b_gate,
             gla_norm_g=gla_norm_g, od_w_in=od_w_in, od_w_out=od_w_out, swa_q_norm=swa_q_norm,
             swa_k_norm=swa_k_norm, swa_sink=swa_sink, ssd_conv_w=ssd_conv_w, ssd_conv_b=ssd_conv_b,
             ssd_dt_bias=ssd_dt_bias, ssd_a_log=ssd_a_log, ssd_d=ssd_d, ssd_norm_g=ssd_norm_g)
    bp = x_prompt.shape[0]
    n_even, n_odd = state_s5_re.shape[0], state_ssd.shape[0]
    depth = w_mod.shape[0]
    zeros_like_b = lambda a: jnp.zeros((a.shape[0], bp) + a.shape[2:], F32)
    zero_st = dict(s5_re=zeros_like_b(state_s5_re), s5_im=zeros_like_b(state_s5_im), gla=zeros_like_b(state_gla),
                   ssd=zeros_like_b(state_ssd), ssd_conv=zeros_like_b(state_ssd_conv),
                   ffn_conv=zeros_like_b(state_ffn_conv))
    sample_st = dict(s5_re=state_s5_re, s5_im=state_s5_im, gla=state_gla, swa_k=cache_swa_k,
                     swa_v=cache_swa_v, ssd=state_ssd, ssd_conv=state_ssd_conv, ffn_conv=state_ffn_conv)
    mods = _modulation(jnp.concatenate([c_prompt, c_sample], axis=0), w_mod, b_mod)
    y_prompt, stp = _trunk(x_prompt, mods[:, :bp], P, zero_st, False)
    y_sample, sts = _trunk(x_sample, mods[:, bp:], P, sample_st, True)
    names = ('s5_re', 's5_im', 'gla', 'swa_k', 'swa_v', 'ssd', 'ssd_conv', 'ffn_conv')
    return (y_prompt, y_sample) + tuple(stp[n] for n in names) + tuple(sts[n] for n in names)
```

```python
import functools
import math

import jax
import jax.numpy as jnp
import numpy as np
from jax import lax
from jax.experimental import pallas as pl
from jax.experimental.pallas import tpu as pltpu

F32 = jnp.float32
BF16 = jnp.bfloat16

CHUNK = 64
WINDOW = 128
S5_GROUP = 16
S5_STATE = 64
S5_SUB = 8
S5_MAX_SUBS = 64
GLA_HEADS = 4
GLA_GATE_NORM = 16.0
HEAD_DIM = 64
SWA_KV_HEADS = 2
SWA_KEYS = 256
SSD_HEAD_DIM = 64
SSD_STATE = 128
SSD_GROUPS = 2
T5_BUCKETS = 32
T5_MAX_DIST = 128
RMS_EPS = 1e-6
NEG_INF = -1e30
LANES = 128
SUBLANES = 8
assert S5_SUB == SUBLANES and S5_GROUP * SUBLANES == LANES
MAX_TILE = 512
CHUNK_UNROLL = 8
FFN_COL_BLOCK = 256
VMEM_LIMIT = 56 * 1024 * 1024

_NT = (((1,), (1,)), ((), ()))
_HI = lax.Precision.HIGHEST


def _params(n_axes=2):
    sem = ("parallel",) + ("arbitrary",) * (n_axes - 1)
    return pltpu.CompilerParams(dimension_semantics=sem, vmem_limit_bytes=VMEM_LIMIT)


def _bdot(a, b):
    return jnp.dot(a.astype(BF16), b.astype(BF16), preferred_element_type=F32)


def _split_dot(x, c, parts=3):
    pieces = []
    rest = x
    for _ in range(parts):
        piece = rest.astype(BF16)
        pieces.append(piece)
        rest = rest - piece.astype(F32)
    n = x.shape[0]
    d = jnp.dot(jnp.concatenate(pieces, axis=0), c, preferred_element_type=F32)
    out = d[0:n]
    for i in range(1, parts):
        out = out + d[i * n:(i + 1) * n]
    return out


def _silu(x):
    return x * jax.nn.sigmoid(x)


def _softplus(x):
    return jnp.maximum(x, 0.0) + jnp.log1p(jnp.exp(-jnp.abs(x)))


def _log_sigmoid(x):
    return jnp.minimum(x, 0.0) - jnp.log1p(jnp.exp(-jnp.abs(x)))


def _norm_mod(x, g, scale, shift):
    ms = jnp.mean(x * x, axis=-1, keepdims=True)
    return (x * lax.rsqrt(ms + RMS_EPS) * g) * (1.0 + scale) + shift


def _cumsum_rows(x):
    n, m = x.shape
    tiles = x.reshape(n // SUBLANES, SUBLANES, m)
    sub = lax.broadcasted_iota(jnp.int32, (1, SUBLANES, 1), 1)
    d = 1
    while d < SUBLANES:
        tiles = tiles + jnp.where(sub >= d, pltpu.roll(tiles, d, axis=1), 0.0)
        d *= 2
    out = [tiles[0]]
    for i in range(1, n // SUBLANES):
        out.append(tiles[i] + out[-1][SUBLANES - 1:SUBLANES, :])
    return jnp.concatenate(out, axis=0)


def _causal_conv(u, prev8, w, b):
    width = w.shape[0]
    n = u.shape[0]
    ext = jnp.concatenate([prev8, u[0:SUBLANES]], axis=0)
    full = b
    head = b
    for j in range(width):
        sh = width - 1 - j
        if sh == 0:
            full = full + u * w[j:j + 1]
            head = head + u[0:SUBLANES] * w[j:j + 1]
        else:
            full = full + pltpu.roll(u, sh, axis=0) * w[j:j + 1]
            head = head + pltpu.roll(ext, sh, axis=0)[SUBLANES:2 * SUBLANES] * w[j:j + 1]
    if n == SUBLANES:
        return head
    return jnp.concatenate([head, full[SUBLANES:]], axis=0)


def _segment_pitch(n):
    pitch = n // SUBLANES + SUBLANES
    return pitch if (pitch // SUBLANES) % 2 else pitch + SUBLANES


def _to_segments(ref, val):
    nv = val.shape[0] // SUBLANES
    pitch = ref.shape[1] // SUBLANES
    for c in range(ref.shape[0]):
        for s in range(SUBLANES):
            ref[c, s * pitch:s * pitch + nv, :] = val[s * nv:(s + 1) * nv, c * LANES:(c + 1) * LANES]


def _from_segments(ref, n):
    nv = n // SUBLANES
    pitch = ref.shape[1] // SUBLANES
    return jnp.concatenate(
        [jnp.concatenate([ref[c, s * pitch:s * pitch + nv, :] for s in range(SUBLANES)], axis=0)
         for c in range(ref.shape[0])], axis=1)


def _rows_from_segments(ref, r0, n_rows, n):
    nv = n // SUBLANES
    pitch = ref.shape[1] // SUBLANES
    spans = []
    r = r0
    while r < r0 + n_rows:
        s, off = divmod(r, nv)
        take = min(nv - off, r0 + n_rows - r)
        spans.append(slice(s * pitch + off, s * pitch + off + take))
        r += take
    return jnp.concatenate(
        [jnp.concatenate([ref[c, sp, :] for sp in spans], axis=0) for c in range(ref.shape[0])], axis=1)


def _load_interleaved(ref, n):
    pitch = ref.shape[1] // SUBLANES
    return jnp.concatenate(
        [jnp.concatenate([ref[c, pl.ds(i, SUBLANES, stride=pitch), :] for c in range(ref.shape[0])], axis=1)
         for i in range(n // SUBLANES)], axis=0)


def _store_interleaved(ref, val):
    pitch = ref.shape[1] // SUBLANES
    for i in range(val.shape[0] // SUBLANES):
        for c in range(ref.shape[0]):
            ref[c, pl.ds(i, SUBLANES, stride=pitch), :] = val[i * SUBLANES:(i + 1) * SUBLANES, c * LANES:(c + 1) * LANES]


def _causal_conv_interleaved(u, prev8, w, b):
    width = w.shape[0]
    n, m = u.shape
    nv = n // SUBLANES
    sub = lax.broadcasted_iota(jnp.int32, (SUBLANES, 1), 0)
    tiles = u.reshape(nv, SUBLANES, m)

    def back_one(a, before):
        first = jnp.where(sub == 0, before, pltpu.roll(a[nv - 1], 1, axis=0))
        return jnp.concatenate([first[None], a[:nv - 1]], axis=0)

    delayed = [tiles]
    for k in range(1, width):
        delayed.append(back_one(delayed[-1], prev8[SUBLANES - k:SUBLANES - k + 1]))
    out = b
    for j in range(width):
        out = out + delayed[width - 1 - j] * w[j:j + 1]
    hist = tiles[nv - 1]
    for k in range(2, width):
        hist = jnp.where(sub == SUBLANES - k, pltpu.roll(tiles[nv - k], SUBLANES - k + 1, axis=0), hist)
    return out.reshape(n, m), hist


def _mod_kernel(c_ref, w_ref, b_ref, o_ref):
    o_ref[0] = _bdot(_silu(c_ref[...]), w_ref[0]) + b_ref[0]


def _modulation(c, w_mod, b_mod):
    depth, d, n = w_mod.shape
    bc = c.shape[0]
    tn = n // 4
    return pl.pallas_call(
        _mod_kernel,
        grid=(depth, n // tn),
        in_specs=[pl.BlockSpec((bc, d), lambda l, j: (0, 0)),
                  pl.BlockSpec((1, d, tn), lambda l, j: (l, 0, j)),
                  pl.BlockSpec((1, 1, tn), lambda l, j: (l, 0, j))],
        out_specs=pl.BlockSpec((1, bc, tn), lambda l, j: (l, 0, j)),
        out_shape=jax.ShapeDtypeStruct((depth, bc, n), F32),
        compiler_params=_params(2),
        name="modulation",
    )(c, w_mod, b_mod.reshape(depth, 1, n))


def _ev_in_kernel(x_ref, mod_ref, g_ref, w_ref, wg2_ref, bg_ref,
                  u_ref, qk_ref, v_ref, r_ref, gate_ref):
    mod = mod_ref[0]
    hn = _norm_mod(x_ref[0], g_ref[...], mod[1:2], mod[0:1]).astype(BF16)
    dot = lambda lo, hi: jnp.dot(hn, w_ref[:, lo:hi], preferred_element_type=F32)
    gl = dot(2048, 2048 + LANES)
    gate_ref[0] = _log_sigmoid(_bdot(gl, wg2_ref[...]) + bg_ref[...]) * (1.0 / GLA_GATE_NORM)
    u = dot(0, 512)
    for c in range(u_ref.shape[0]):
        u_ref[c] = _chunk_transpose(u[:, c * LANES:(c + 1) * LANES].reshape(-1, SUBLANES, LANES))
    qk_ref[0] = dot(512, 1024)
    v_ref[0] = dot(1024, 1536).astype(BF16)
    r_ref[0] = dot(1536, 2048)


def _ev_in(x, mod, norm_g, w_in, w_gate2, b_gate, tb):
    b, t, d = x.shape
    rank = w_gate2.shape[0]
    nk = w_gate2.shape[1]
    wu, wq, wk, wv, wgl, wr = jnp.split(w_in, [512, 768, 1024, 1536, 1536 + rank], axis=1)
    w = jnp.concatenate([wu, wq, wk, wv, wr, wgl, jnp.zeros((d, LANES - rank), F32)], axis=1).astype(BF16)
    wg2 = jnp.concatenate([w_gate2, jnp.zeros((LANES - rank, nk), F32)], axis=0).astype(BF16)
    tile = lambda n: pl.BlockSpec((1, tb, n), lambda i, j: (i, j, 0))
    const = lambda a: pl.BlockSpec(a.shape, lambda i, j: (0,) * a.ndim)
    g2 = norm_g.reshape(1, d)
    bg = b_gate.reshape(1, nk)
    return pl.pallas_call(
        _ev_in_kernel,
        grid=(b, t // tb),
        in_specs=[tile(d), pl.BlockSpec((1, 6, d), lambda i, j: (i, 0, 0)), const(g2), const(w),
                  const(wg2), const(bg)],
        out_specs=[pl.BlockSpec((512 // LANES, tb // S5_SUB, None, SUBLANES, LANES), lambda i, j: (0, j, i, 0, 0)),
                   tile(512), tile(512), tile(512), tile(nk)],
        out_shape=[jax.ShapeDtypeStruct((512 // LANES, t // S5_SUB, b, SUBLANES, LANES), F32)]
        + [jax.ShapeDtypeStruct((b, t, n), dt) for n, dt in ((512, F32), (512, BF16), (512, F32), (nk, F32))],
        compiler_params=_params(2),
        name="ev_in",
    )(x, mod, g2, w, wg2, bg)


def _gla_kernel(qk_ref, v_ref, gate_ref, r_ref, s0_ref, ng_ref, o_ref, sfin_ref, st_scr, *, n_chunks):
    t = pl.program_id(1)
    dk = qk_ref.shape[2] // 2 // GLA_HEADS
    dv = v_ref.shape[2] // GLA_HEADS
    nk = GLA_HEADS * dk
    nv = GLA_HEADS * dv
    blk = (lax.broadcasted_iota(jnp.int32, (nv, nk), 0) // dv
           == lax.broadcasted_iota(jnp.int32, (nv, nk), 1) // dk)

    @pl.when(t == 0)
    def _():
        s0 = jnp.concatenate([s0_ref[0, h] for h in range(GLA_HEADS)], axis=0)
        st_scr[...] = jnp.where(blk, jnp.concatenate([s0] * GLA_HEADS, axis=1), 0.0)

    n_hs = GLA_HEADS * CHUNK
    row_head = lax.broadcasted_iota(jnp.int32, (n_hs, 1), 0) // CHUNK
    own_k = row_head == lax.broadcasted_iota(jnp.int32, (1, nk), 1) // dk
    own_v = row_head == lax.broadcasted_iota(jnp.int32, (1, nv), 1) // dv
    causal = (lax.broadcasted_iota(jnp.int32, (CHUNK, 1), 0)
              >= lax.broadcasted_iota(jnp.int32, (1, n_hs), 1) % CHUNK)
    ng = ng_ref[...]

    def chunk(c, carry):
        rows = pl.ds(pl.multiple_of(c * CHUNK, CHUNK), CHUNK)
        k = qk_ref[0, rows, nk:2 * nk]
        v_b = v_ref[0, rows, :]
        cum = _cumsum_rows(gate_ref[0, rows, :])
        cum_last = cum[CHUNK - 1:CHUNK, :]
        qe = (qk_ref[0, rows, 0:nk] * dk ** -0.5 * jnp.exp(cum)).astype(BF16)
        ke = (k * jnp.exp(-cum)).astype(BF16)
        kd = (k * jnp.exp(cum_last - cum)).astype(BF16)
        st = st_scr[...]
        ke_bd = jnp.where(own_k, jnp.concatenate([ke] * GLA_HEADS, axis=0), jnp.zeros((), BF16))
        v_bd = jnp.where(own_v, jnp.concatenate([v_b] * GLA_HEADS, axis=0), jnp.zeros((), BF16))
        att = lax.dot_general(qe, ke_bd, _NT, preferred_element_type=F32)
        att = jnp.where(causal, att, 0.0).astype(BF16)
        o = (jnp.dot(att, v_bd, preferred_element_type=F32)
             + lax.dot_general(qe, st.astype(BF16), _NT, preferred_element_type=F32))
        for h in range(GLA_HEADS):
            cols = slice(h * dv, (h + 1) * dv)
            oh = o[:, cols]
            ms = jnp.mean(oh * oh, axis=-1, keepdims=True)
            oh = oh * lax.rsqrt(ms + RMS_EPS) * ng
            o_ref[0, rows, cols] = (oh * _silu(r_ref[0, rows, cols])).astype(o_ref.dtype)
        upd = jnp.dot(v_b.astype(F32).T.astype(BF16), kd, preferred_element_type=F32)
        st_scr[...] = jnp.where(blk, st * jnp.exp(cum_last) + upd, 0.0)
        return carry

    lax.fori_loop(0, n_chunks, chunk, 0, unroll=math.gcd(n_chunks, CHUNK_UNROLL))

    @pl.when(t == pl.num_programs(1) - 1)
    def _():
        for h in range(GLA_HEADS):
            sfin_ref[0, h] = st_scr[h * dv:(h + 1) * dv, h * dk:(h + 1) * dk]


def _gla(qk, v, gate, r, s0, norm_g, tb):
    b, t, nv = v.shape
    nk = gate.shape[2]
    dk, dv = nk // GLA_HEADS, nv // GLA_HEADS
    s0t = jnp.swapaxes(s0, 2, 3)
    tile = lambda n: pl.BlockSpec((1, tb, n), lambda i, j: (i, j, 0))
    sspec = pl.BlockSpec((1, GLA_HEADS, dv, dk), lambda i, j: (i, 0, 0, 0))
    ng = norm_g.reshape(1, dv)
    o, sfin = pl.pallas_call(
        functools.partial(_gla_kernel, n_chunks=tb // CHUNK),
        grid=(b, t // tb),
        in_specs=[tile(2 * nk), tile(nv), tile(nk), tile(nv), sspec,
                  pl.BlockSpec((1, dv), lambda i, j: (0, 0))],
        out_specs=[tile(nv), sspec],
        out_shape=[jax.ShapeDtypeStruct((b, t, nv), BF16),
                   jax.ShapeDtypeStruct((b, GLA_HEADS, dv, dk), F32)],
        scratch_shapes=[pltpu.VMEM((nv, nk), F32)],
        compiler_params=_params(2),
        name="gla",
    )(qk, v, gate, r, s0t, ng)
    return o, jnp.swapaxes(sfin, 2, 3)


def _chunk_transpose(x):
    s = lax.broadcasted_iota(jnp.int32, (1, SUBLANES, LANES), 1)
    c = lax.broadcasted_iota(jnp.int32, (1, SUBLANES, LANES), 2) // S5_GROUP
    for d in (4, 2, 1):
        sb = (s & d) != 0
        cb = (c & d) != 0
        if 2 * d == SUBLANES:
            t = pltpu.roll(pltpu.roll(x, d, axis=1), S5_GROUP * d, axis=2)
        else:
            xs = jnp.where(sb, pltpu.roll(x, d, axis=1), pltpu.roll(x, SUBLANES - d, axis=1))
            t = jnp.where(cb, pltpu.roll(xs, S5_GROUP * d, axis=2), pltpu.roll(xs, LANES - S5_GROUP * d, axis=2))
        x = jnp.where(sb != cb, t, x)
    return x


def _s5_kernel(u_ref, w1_ref, n_ref, a_ref, d_ref, x0_ref, y_ref, xf_ref, mm_scr, xs_scr, st_scr, *, tk, bsz):
    j = pl.program_id(1)
    n_g = st_scr.shape[0]
    rows = tk * bsz
    of_group = lambda g: pl.ds(g, rows, stride=n_g)

    @pl.when(j == 0)
    def _():
        st_scr[...] = x0_ref[0]

    for g in range(n_g):
        ub = u_ref[of_group(g), :].astype(BF16)
        mm_scr[g] = jnp.dot(ub, w1_ref[0, g], preferred_element_type=F32)

    half = n_g // 2
    for g0 in (0, half):
        coef = [a_ref[0, g] for g in range(g0, g0 + half)]

        def step(k, carry, g0=g0, coef=coef):
            r = pl.ds(pl.multiple_of(k * bsz, bsz), bsz)
            out = []
            for i in range(half):
                x, xsw = carry[2 * i], carry[2 * i + 1]
                a1, a2, a2s = coef[i][0:1], coef[i][1:2], coef[i][2:3]
                xs_scr[g0 + i, r, :] = x
                out.append(a1 * x + a2 * xsw + mm_scr[g0 + i, r, LANES:2 * LANES])
                out.append(a1 * xsw + a2s * x + mm_scr[g0 + i, r, 2 * LANES:3 * LANES])
            return tuple(out)

        fin = lax.fori_loop(0, tk, step, tuple(st_scr[g0 + i, v] for i in range(half) for v in range(2)))
        for i in range(half):
            st_scr[g0 + i, 0] = fin[2 * i]
            st_scr[g0 + i, 1] = fin[2 * i + 1]

    for g in range(n_g):
        y = (mm_scr[g, :, 0:LANES] + jnp.dot(xs_scr[g].astype(BF16), n_ref[0, g], preferred_element_type=F32)
             + d_ref[0, g] * u_ref[of_group(g), :])
        y_ref[of_group(g), :] = jax.nn.gelu(y)

    @pl.when(j == pl.num_programs(1) - 1)
    def _():
        xf_ref[0] = st_scr[:, 0]


def _cmul(ar, ai, br, bi):
    return ar * br - ai * bi, ar * bi + ai * br


def _s5_matrices(a_re, a_im, log_dt, b_re, b_im, c_re, c_im):
    g, p = a_re.shape
    hdim = b_re.shape[-1]
    dt = jnp.exp(log_dt)[:, None]
    mag = jnp.exp(a_re * dt)
    ab_re, ab_im = mag * jnp.cos(a_im * dt), mag * jnp.sin(a_im * dt)
    den = a_re * a_re + a_im * a_im
    num_re, num_im = ab_re - 1.0, ab_im
    g_re = (num_re * a_re + num_im * a_im) / den
    g_im = (num_im * a_re - num_re * a_im) / den
    bb_re = g_re[..., None] * b_re - g_im[..., None] * b_im
    bb_im = g_re[..., None] * b_im + g_im[..., None] * b_re
    pw_re, pw_im = jnp.ones((1, g, p), F32), jnp.zeros((1, g, p), F32)
    sq_re, sq_im = ab_re, ab_im
    while pw_re.shape[0] < S5_SUB + 1:
        nr, ni = _cmul(pw_re, pw_im, sq_re, sq_im)
        pw_re, pw_im = jnp.concatenate([pw_re, nr]), jnp.concatenate([pw_im, ni])
        sq_re, sq_im = _cmul(sq_re, sq_im, sq_re, sq_im)
    pw_re, pw_im = pw_re[:S5_SUB + 1], pw_im[:S5_SUB + 1]
    ca_re, ca_im = _cmul(c_re[None], c_im[None], pw_re[:, :, None, :], pw_im[:, :, None, :])
    kern = (jnp.einsum('tghp,gpk->tghk', ca_re[:S5_SUB], bb_re, precision=_HI)
            - jnp.einsum('tghp,gpk->tghk', ca_im[:S5_SUB], bb_im, precision=_HI))
    s_in = np.arange(S5_SUB)[:, None, None]
    s_out = np.arange(S5_SUB)[None, :, None]
    lag = jnp.asarray(s_out - s_in == np.arange(S5_SUB), F32)
    m = jnp.einsum('iot,tghk->ioghk', lag, kern, precision=_HI)
    m = jnp.transpose(m, (2, 0, 4, 1, 3)).reshape(g, S5_SUB * hdim, S5_SUB * hdim)
    pr, pi = _cmul(pw_re[S5_SUB - 1::-1][..., None], pw_im[S5_SUB - 1::-1][..., None], bb_re[None], bb_im[None])
    flat = lambda a: jnp.transpose(a, (1, 0, 3, 2)).reshape(g, S5_SUB * hdim, p)
    nm = jnp.concatenate([ca_re[1:], -ca_im[1:]], axis=3)
    nm = jnp.transpose(nm, (1, 3, 0, 2)).reshape(g, 2 * p, S5_SUB * hdim)
    w1 = jnp.concatenate([m, flat(pr), flat(pi), flat(pi), flat(pr)], axis=2)
    a_r, a_i = pw_re[S5_SUB], pw_im[S5_SUB]
    coef = jnp.stack([jnp.concatenate([a_r, a_r], axis=1), jnp.concatenate([-a_i, a_i], axis=1),
                      jnp.concatenate([a_i, -a_i], axis=1)], axis=1)
    return w1.astype(BF16), nm.astype(BF16), coef


def _s5(u2, mats, d_skip, x0_re, x0_im):
    w1, nm, coef = mats
    n_ct, n_sub, b, n_g, lanes = u2.shape
    g, p2 = nm.shape[0], nm.shape[1]
    hdim = lanes // S5_SUB
    tk = min(n_sub, S5_MAX_SUBS)
    x0 = jnp.transpose(jnp.concatenate([x0_re, x0_im], axis=2), (1, 0, 2))
    x0 = jnp.stack([x0, jnp.roll(x0, p2 // 2, axis=2)], axis=1)
    dt = jnp.tile(d_skip.reshape(g, 1, hdim), (1, 1, S5_SUB))
    by_tile = lambda a: a.reshape((n_ct, n_g) + a.shape[1:])
    pspec = lambda a: pl.BlockSpec((1,) + a.shape[1:], lambda c, j: (c,) + (0,) * (a.ndim - 1))
    rows = tk * b
    uspec = pl.BlockSpec((None, rows * n_g, lanes), lambda c, j: (c, j, 0))
    params = [by_tile(a) for a in (w1, nm, coef, dt, x0)]
    y, xf = pl.pallas_call(
        functools.partial(_s5_kernel, tk=tk, bsz=b),
        grid=(n_ct, n_sub // tk),
        in_specs=[uspec] + [pspec(a) for a in params],
        out_specs=[uspec, pl.BlockSpec((1, n_g, b, p2), lambda c, j: (c, 0, 0, 0))],
        out_shape=[jax.ShapeDtypeStruct((n_ct, n_sub * b * n_g, lanes), F32),
                   jax.ShapeDtypeStruct((n_ct, n_g, b, p2), F32)],
        scratch_shapes=[pltpu.VMEM((n_g, rows, 3 * lanes), F32), pltpu.VMEM((n_g, rows, lanes), F32),
                        pltpu.VMEM((n_g, 2, b, p2), F32)],
        compiler_params=_params(2),
        name="s5",
    )(u2.reshape(n_ct, n_sub * b * n_g, lanes), *params)
    xf = jnp.transpose(xf.reshape(g, b, p2), (1, 0, 2))
    return y.reshape(u2.shape), xf[:, :, :p2 // 2], xf[:, :, p2 // 2:]


def _out_ffn_kernel(*refs, with_s5, cb):
    if with_s5:
        (x_ref, mod_ref, m1_ref, m2_ref, wglu_ref, bglu_ref, wout_ref, n2g_ref,
         wup_ref, cw_ref, cbias_ref, wd_ref, pre_ref, xo_ref, ulast_ref, carry_scr, h_scr, row_scr) = refs
    else:
        (x_ref, mod_ref, m1_ref, m2_ref, wout_ref, n2g_ref,
         wup_ref, cw_ref, cbias_ref, wd_ref, pre_ref, xo_ref, ulast_ref, carry_scr, h_scr, row_scr) = refs
    t = pl.program_id(1)

    @pl.when(t == 0)
    def _():
        carry_scr[...] = pre_ref[0]

    mod = mod_ref[0]
    half = m2_ref.shape[2]
    out2 = jnp.dot(m2_ref[0], wout_ref[half:, :], preferred_element_type=F32)
    if with_s5:
        tiles = []
        glu = bglu_ref[...]
        for c in range(m1_ref.shape[0]):
            tiles.append(_chunk_transpose(m1_ref[c]).reshape(-1, LANES))
            glu = glu + jnp.dot(tiles[-1].astype(BF16), wglu_ref[c * LANES:(c + 1) * LANES, :],
                                preferred_element_type=F32)
        m1 = (jnp.concatenate(tiles, axis=1) * jax.nn.sigmoid(glu)).astype(BF16)
    else:
        m1 = m1_ref[0]
    out = jnp.dot(m1, wout_ref[0:half, :], preferred_element_type=F32) + out2
    x1 = x_ref[0] + mod[2:3] * out
    tb = x1.shape[0]
    _to_segments(row_scr, _norm_mod(x1, n2g_ref[...], mod[4:5], mod[3:4]))
    hn = _load_interleaved(row_scr, tb).astype(BF16)
    dff = wd_ref.shape[0]
    n_cb = dff // cb
    a_cols = lambda j: slice(j * cb, (j + 1) * cb)
    g_cols = lambda j: slice(dff + j * cb, dff + (j + 1) * cb)

    def up(j):
        return (jnp.dot(hn, wup_ref[:, a_cols(j)], preferred_element_type=F32),
                jnp.dot(hn, wup_ref[:, g_cols(j)], preferred_element_type=F32))

    def conv(u, cols):
        out, carry_scr[:, cols] = _causal_conv_interleaved(u, carry_scr[:, cols], cw_ref[:, cols], cbias_ref[:, cols])
        return out

    nxt = up(0)
    for j in range(n_cb):
        ua, ug = nxt
        if j + 1 < n_cb:
            nxt = up(j + 1)
        h_scr[:, a_cols(j)] = (_silu(conv(ua, a_cols(j))) * conv(ug, g_cols(j))).astype(BF16)
    _store_interleaved(row_scr, jnp.dot(h_scr[...], wd_ref[...], preferred_element_type=F32))
    xo_ref[0] = x1 + mod[5:6] * _from_segments(row_scr, tb)

    @pl.when(t == pl.num_programs(1) - 1)
    def _():
        ulast_ref[0] = carry_scr[...]


def _out_ffn(x, mod, m1, m2, w_out, norm2_g, w_up, conv_w, conv_b, w_down, prefix, tb, s5_extra=None):
    b, t, d = x.shape
    dff = w_down.shape[0]
    assert dff % FFN_COL_BLOCK == 0
    width = conv_w.shape[0]
    half = m2.shape[2]
    w_up_b = w_up.astype(BF16)
    cbias = conv_b.reshape(1, 2 * dff)
    wd = w_down.astype(BF16)
    pre8 = jnp.concatenate([jnp.zeros((b, SUBLANES - (width - 1), 2 * dff), F32), prefix], axis=1)
    tile = lambda n: pl.BlockSpec((1, tb, n), lambda i, j: (i, j, 0))
    const = lambda a: pl.BlockSpec(a.shape, lambda i, j: (0,) * a.ndim, pipeline_mode=pl.Buffered(1))
    bspec = lambda a: pl.BlockSpec((1,) + a.shape[1:], lambda i, j: (i,) + (0,) * (a.ndim - 1))
    wo = w_out.astype(BF16)
    n2g = norm2_g.reshape(1, d)
    args = [x, mod]
    specs = [tile(d), bspec(mod)]
    if s5_extra is not None:
        w_glu, b_glu = s5_extra
        w_glu, b_glu = w_glu.astype(BF16), b_glu.reshape(1, half)
        args += [m1, m2, w_glu, b_glu]
        specs += [pl.BlockSpec((m1.shape[0], tb // S5_SUB, None) + m1.shape[3:], lambda i, j: (0, j, i, 0, 0)),
                  tile(m2.shape[2]), const(w_glu), const(b_glu)]
    else:
        args += [m1, m2]
        specs += [tile(half), tile(m2.shape[2])]
    weights = [wo, n2g, w_up_b, conv_w, cbias, wd]
    args += weights + [pre8]
    specs += [const(a) for a in weights] + [bspec(pre8)]
    xo, ulast = pl.pallas_call(
        functools.partial(_out_ffn_kernel, with_s5=s5_extra is not None, cb=FFN_COL_BLOCK),
        grid=(b, t // tb),
        in_specs=specs,
        out_specs=[tile(d), bspec(pre8)],
        out_shape=[jax.ShapeDtypeStruct((b, t, d), F32), jax.ShapeDtypeStruct(pre8.shape, F32)],
        scratch_shapes=[pltpu.VMEM(pre8.shape[1:], F32), pltpu.VMEM((tb, dff), BF16),
                        pltpu.VMEM((d // LANES, SUBLANES * _segment_pitch(tb), LANES), F32)],
        compiler_params=_params(2),
        name="out_ffn_s5" if s5_extra is not None else "out_ffn",
    )(*args)
    return xo, ulast[:, SUBLANES - (width - 1):]


def _head_rms(x, g):
    left = lax.broadcasted_iota(jnp.int32, (1, LANES), 1) < HEAD_DIM
    out = []
    for c in range(x.shape[1] // LANES):
        xt = x[:, c * LANES:(c + 1) * LANES]
        sq = xt * xt
        s_left = jnp.sum(jnp.where(left, sq, 0.0), axis=-1, keepdims=True)
        s_right = jnp.sum(jnp.where(left, 0.0, sq), axis=-1, keepdims=True)
        ms = jnp.where(left, s_left, s_right) * (1.0 / HEAD_DIM)
        out.append(xt * lax.rsqrt(ms + RMS_EPS))
    return jnp.concatenate(out, axis=1) * g


def _od_in_kernel(x_ref, mod_ref, g_ref, w_ref, qg_ref, kg_ref, dtb_ref,
                  q_ref, k_ref, v_ref, zg_ref, xbc_ref, dt_ref):
    mod = mod_ref[0]
    nq = q_ref.shape[2]
    nkv = k_ref.shape[2]
    hn = _norm_mod(x_ref[0], g_ref[...], mod[1:2], mod[0:1]).astype(BF16)
    dot = lambda lo, hi: jnp.dot(hn, w_ref[:, lo:hi], preferred_element_type=F32)
    o_zg = nq + 2 * nkv
    o_xbc = o_zg + zg_ref.shape[2]
    o_dt = o_xbc + xbc_ref.shape[2]
    dt_ref[0] = _softplus(dot(o_dt, o_dt + LANES) + dtb_ref[...])
    q_ref[0] = _head_rms(dot(0, nq), qg_ref[...]).astype(BF16)
    kv = dot(nq, nq + 2 * nkv)
    k_ref[0] = _head_rms(kv[:, 0:nkv], kg_ref[...])
    v_ref[0] = kv[:, nkv:2 * nkv]
    zg_ref[0] = dot(o_zg, o_xbc)
    xbc_ref[0] = dot(o_xbc, o_dt)


def _od_in(x, mod, norm_g, w_in, q_norm, k_norm, dt_bias, dims, tb):
    b, t, d = x.shape
    nq, nkv, nz, nxbc, nh = dims
    w = jnp.concatenate([w_in, jnp.zeros((d, LANES - nh), F32)], axis=1).astype(BF16)
    qg = jnp.tile(q_norm, nq // HEAD_DIM).reshape(1, nq)
    kg = jnp.tile(k_norm, nkv // HEAD_DIM).reshape(1, nkv)
    dtb = jnp.concatenate([dt_bias, jnp.zeros((LANES - nh,), F32)]).reshape(1, LANES)
    g2 = norm_g.reshape(1, d)
    tile = lambda n: pl.BlockSpec((1, tb, n), lambda i, j: (i, j, 0))
    const = lambda a: pl.BlockSpec(a.shape, lambda i, j: (0,) * a.ndim)
    widths = (nq, nkv, nkv, nz, nxbc, LANES)
    return pl.pallas_call(
        _od_in_kernel,
        grid=(b, t // tb),
        in_specs=[tile(d), pl.BlockSpec((1, 6, d), lambda i, j: (i, 0, 0)), const(g2), const(w),
                  const(qg), const(kg), const(dtb)],
        out_specs=[tile(n) for n in widths],
        out_shape=[jax.ShapeDtypeStruct((b, t, n), BF16 if i == 0 else F32) for i, n in enumerate(widths)],
        compiler_params=_params(2),
        name="od_in",
    )(x, mod, g2, w, qg, kg, dtb)


def _swa_kernel(q_ref, k_ref, v_ref, k0_ref, v0_ref, bias_ref, sink_ref, o_ref, kx, vx,
                *, n_blocks, nq, mask_start):
    t = pl.program_id(1)
    tb = q_ref.shape[1]
    n_kv = k_ref.shape[2] // HEAD_DIM
    rows_q = nq * CHUNK
    tail = kx.shape[1] - WINDOW - tb
    left = lax.broadcasted_iota(jnp.int32, (1, LANES), 1) < HEAD_DIM
    first_side = lax.broadcasted_iota(jnp.int32, (2 * SWA_KEYS, 1), 0) < SWA_KEYS
    side_ones = jnp.where(first_side == left, 1.0, 0.0).astype(BF16)
    col = lax.broadcasted_iota(jnp.int32, (1, SWA_KEYS), 1)

    def place(dst, rows, x):
        rolled = pltpu.roll(x, HEAD_DIM, axis=1)
        dst[0, rows, :] = jnp.where(left, x, 0.0).astype(BF16)
        dst[1, rows, :] = jnp.where(left, 0.0, rolled).astype(BF16)
        dst[2, rows, :] = jnp.where(left, rolled, 0.0).astype(BF16)
        dst[3, rows, :] = jnp.where(left, 0.0, x).astype(BF16)

    @pl.when(t == 0)
    def _():
        place(kx, slice(0, WINDOW), k0_ref[0])
        place(vx, slice(0, WINDOW), v0_ref[0])
        for i in range(2 * n_kv if tail else 0):
            kx[i, WINDOW + tb:, :] = jnp.zeros((tail, LANES), BF16)
            vx[i, WINDOW + tb:, :] = jnp.zeros((tail, LANES), BF16)

    place(kx, slice(WINDOW, WINDOW + tb), k_ref[0])
    place(vx, slice(WINDOW, WINDOW + tb), v_ref[0])

    def block(blk, carry):
        r0 = pl.multiple_of(blk * rows_q, rows_q)
        keys = pl.ds(r0, SWA_KEYS)
        if mask_start:
            valid = t * tb + r0 - WINDOW + col >= 0
        probs = []
        sinks = []
        for j in range(n_kv):
            lo = 2 * j * LANES
            qg = jnp.concatenate([q_ref[0, pl.ds(r0 + cq * CHUNK, CHUNK), lo + r * LANES:lo + (r + 1) * LANES]
                                  for cq in range(nq) for r in range(2)], axis=0).astype(BF16)
            kcat = jnp.concatenate([kx[2 * j, keys, :], kx[2 * j + 1, keys, :]], axis=0)
            s_both = lax.dot_general(qg, kcat, _NT, preferred_element_type=F32) * HEAD_DIM ** -0.5
            for side in range(2):
                s = s_both[:, side * SWA_KEYS:(side + 1) * SWA_KEYS] + bias_ref[j, side]
                if mask_start:
                    s = jnp.where(valid, s, NEG_INF)
                m = jnp.max(s, axis=-1, keepdims=True)
                probs.append(jnp.exp(s - m).astype(BF16))
                sinks.append(jnp.exp(sink_ref[j, side] - m))
        for j in range(n_kv):
            lo = 2 * j * LANES
            vcat = jnp.concatenate(
                [jnp.concatenate([vx[2 * j, keys, :], vx[2 * j + 1, keys, :]], axis=0), side_ones], axis=1)
            pv = jnp.dot(jnp.concatenate(probs[2 * j:2 * j + 2], axis=1), vcat, preferred_element_type=F32)
            den = pv[:, LANES:] + jnp.where(left, sinks[2 * j], sinks[2 * j + 1])
            out = pv[:, 0:LANES] / den
            for cq in range(nq):
                for r in range(2):
                    o_ref[0, pl.ds(r0 + cq * CHUNK, CHUNK), lo + r * LANES:lo + (r + 1) * LANES] = (
                        out[(2 * cq + r) * CHUNK:(2 * cq + r + 1) * CHUNK].astype(o_ref.dtype))
        return carry

    lax.fori_loop(0, n_blocks, block, 0, unroll=math.gcd(n_blocks, 2))
    for i in range(2 * n_kv):
        kx[i, 0:WINDOW, :] = kx[i, tb:tb + WINDOW, :]
        vx[i, 0:WINDOW, :] = vx[i, tb:tb + WINDOW, :]


def _t5_bucket(rel):
    nb = T5_BUCKETS // 2
    max_exact = nb // 2
    ret = (rel > 0).astype(jnp.int32) * nb
    n = jnp.abs(rel)
    nf = jnp.maximum(n, 1).astype(F32)
    large = max_exact + (jnp.log(nf / max_exact) / math.log(T5_MAX_DIST / max_exact)
                         * (nb - max_exact)).astype(jnp.int32)
    large = jnp.minimum(large, nb - 1)
    return ret + jnp.where(n < max_exact, n, large)


def _rel_bias(table):
    rel = (jnp.arange(WINDOW + CHUNK)[None, :] - WINDOW) - jnp.arange(CHUNK)[:, None]
    onehot = (_t5_bucket(rel)[..., None] == jnp.arange(T5_BUCKETS)).astype(F32)
    return jnp.einsum('qkb,bh->hqk', onehot, table, precision=_HI)


def _swa(q, k, v, k0, v0, bias, sink, tb, mask_start):
    b, t, nq = q.shape
    nkv = k.shape[2]
    n_kv = nkv // HEAD_DIM
    assert nkv == LANES and nq == 2 * n_kv * LANES
    by_side = lambda a: jnp.transpose(a.reshape((n_kv, 2, 2) + a.shape[1:]), (0, 2, 1) + tuple(range(3, a.ndim + 2)))
    span = WINDOW + CHUNK
    n_chunks = tb // CHUNK
    cpb = 2 if n_chunks % 2 == 0 else 1
    assert WINDOW + cpb * CHUNK <= SWA_KEYS
    bias2 = by_side(bias).reshape(n_kv, 2, 2 * CHUNK, span)
    bias3 = jnp.concatenate(
        [jnp.pad(bias2, ((0, 0), (0, 0), (0, 0), (cq * CHUNK, SWA_KEYS - span - cq * CHUNK)),
                 constant_values=NEG_INF) for cq in range(cpb)], axis=2)
    sink3 = jnp.tile(jnp.repeat(by_side(sink), CHUNK, axis=2), (1, 1, cpb))
    sink3 = jnp.broadcast_to(sink3[..., None], sink3.shape + (LANES,))
    tile = lambda n: pl.BlockSpec((1, tb, n), lambda i, j: (i, j, 0))
    wspec = pl.BlockSpec((1, WINDOW, nkv), lambda i, j: (i, 0, 0))
    const = lambda a: pl.BlockSpec(a.shape, lambda i, j: (0,) * a.ndim)
    rows = WINDOW + max(tb, WINDOW) + SWA_KEYS - WINDOW - cpb * CHUNK
    return pl.pallas_call(
        functools.partial(_swa_kernel, n_blocks=n_chunks // cpb, nq=cpb, mask_start=mask_start),
        grid=(b, t // tb),
        in_specs=[tile(nq), tile(nkv), tile(nkv), wspec, wspec, const(bias3), const(sink3)],
        out_specs=tile(nq),
        out_shape=jax.ShapeDtypeStruct((b, t, nq), BF16),
        scratch_shapes=[pltpu.VMEM((2 * n_kv, rows, LANES), BF16), pltpu.VMEM((2 * n_kv, rows, LANES), BF16)],
        compiler_params=_params(2),
        name="swa",
    )(q, k, v, k0, v0, bias3, sink3)


def _ssd_kernel(xbc_ref, zg_ref, dt_ref, pre_ref, cw_ref, cb_ref, aexp_ref, dexp_ref,
                e_ref, ng_ref, s0_ref, y_ref, sfin_ref, clast_ref, st_scr, hist_scr, seg_scr, dte_scr,
                *, n_chunks):
    t = pl.program_id(1)
    tb = xbc_ref.shape[1]
    inner = zg_ref.shape[2]
    n_heads = inner // SSD_HEAD_DIM
    hpg = n_heads // SSD_GROUPS
    gn = SSD_GROUPS * SSD_STATE
    n_hs = n_heads * CHUNK

    @pl.when(t == 0)
    def _():
        st_scr[...] = s0_ref[0]
        hist_scr[...] = pre_ref[0]

    _to_segments(seg_scr, xbc_ref[0])
    conv, hist_scr[...] = _causal_conv_interleaved(_load_interleaved(seg_scr, tb), hist_scr[...],
                                                   cw_ref[...], cb_ref[...])
    _store_interleaved(seg_scr, _silu(conv))
    dte_scr[...] = _split_dot(dt_ref[0], e_ref[...])

    row_hs = lax.broadcasted_iota(jnp.int32, (n_hs, 1), 0)
    own_group = row_hs // (hpg * CHUNK) == lax.broadcasted_iota(jnp.int32, (1, gn), 1) // SSD_STATE
    own_head = row_hs // CHUNK == lax.broadcasted_iota(jnp.int32, (1, inner), 1) // SSD_HEAD_DIM
    state_group = (lax.broadcasted_iota(jnp.int32, (gn, 1), 0) // SSD_STATE
                   == lax.broadcasted_iota(jnp.int32, (1, inner), 1) // (hpg * SSD_HEAD_DIM))
    first_group = lax.broadcasted_iota(jnp.int32, (1, inner), 1) < hpg * SSD_HEAD_DIM
    step_row = lax.broadcasted_iota(jnp.int32, (CHUNK, 1), 0)
    step_lane = lax.broadcasted_iota(jnp.int32, (1, n_hs), 1) % CHUNK
    zero = jnp.zeros((), BF16)
    aexp = aexp_ref[...]

    for c in range(n_chunks):
        rows = slice(c * CHUNK, (c + 1) * CHUNK)
        xc = _rows_from_segments(seg_scr, c * CHUNK, CHUNK, tb)
        xs = xc[:, 0:inner]
        bm = xc[:, inner:inner + gn]
        cm = xc[:, inner + gn:inner + 2 * gn].astype(BF16)
        dte = dte_scr[rows, :]
        cum = _cumsum_rows(dte * aexp)
        cum_last = cum[CHUNK - 1:CHUNK, :]
        cum_at_step = jnp.sum(jnp.where(step_row == step_lane, cum, 0.0), axis=0, keepdims=True)
        decay = jnp.exp(jnp.where(step_row >= step_lane, cum - cum_at_step, NEG_INF))
        bm_bd = jnp.where(own_group, jnp.concatenate([bm.astype(BF16)] * n_heads, axis=0), zero)
        cb = lax.dot_general(cm, bm_bd, _NT, preferred_element_type=F32)
        xdt = (xs * dte).astype(BF16)
        xdt_bd = jnp.where(own_head, jnp.concatenate([xdt] * n_heads, axis=0), zero)
        st = st_scr[...]
        st_bd = jnp.where(state_group, jnp.concatenate([st.astype(BF16)] * SSD_GROUPS, axis=0), zero)
        y = (jnp.dot((cb * decay).astype(BF16), xdt_bd, preferred_element_type=F32)
             + jnp.exp(cum) * jnp.dot(cm, st_bd, preferred_element_type=F32))
        xw = (xs * (jnp.exp(cum_last - cum) * dte)).astype(BF16)
        upd = jnp.dot(bm.T.astype(BF16), xw, preferred_element_type=F32)
        st_scr[...] = st * jnp.exp(cum_last) + jnp.where(first_group, upd[0:SSD_STATE], upd[SSD_STATE:])
        yd = (y + dexp_ref[...] * xs) * _silu(zg_ref[0, rows, :])
        ms = jnp.mean(yd * yd, axis=-1, keepdims=True)
        y_ref[0, rows, :] = (yd * lax.rsqrt(ms + RMS_EPS) * ng_ref[...]).astype(y_ref.dtype)

    @pl.when(t == pl.num_programs(1) - 1)
    def _():
        sfin_ref[0] = st_scr[...]
        clast_ref[0] = hist_scr[...]


def _ssd(xbc, zg, dt, conv_prefix, conv_w, conv_b, a_log, d_skip, norm_g, s0, tb):
    b, t, nxbc = xbc.shape
    inner = zg.shape[2]
    nh = a_log.shape[0]
    width = conv_w.shape[0]
    assert SSD_GROUPS == 2 and SSD_HEAD_DIM == CHUNK
    pre8 = jnp.concatenate([jnp.zeros((b, SUBLANES - (width - 1), nxbc), F32), conv_prefix], axis=1)
    a = -jnp.exp(a_log)
    aexp = jnp.repeat(a, SSD_HEAD_DIM).reshape(1, inner)
    dexp = jnp.repeat(d_skip, SSD_HEAD_DIM).reshape(1, inner)
    expand = (np.arange(LANES)[:, None] == np.arange(inner)[None, :] // SSD_HEAD_DIM)
    expand = jnp.asarray(expand, BF16)
    s0t = jnp.transpose(s0, (0, 3, 1, 2)).reshape(b, SSD_STATE, inner)
    ng = norm_g.reshape(1, inner)
    cb2 = conv_b.reshape(1, nxbc)
    tile = lambda n: pl.BlockSpec((1, tb, n), lambda i, j: (i, j, 0))
    const = lambda arr: pl.BlockSpec(arr.shape, lambda i, j: (0,) * arr.ndim)
    bspec = lambda arr: pl.BlockSpec((1,) + arr.shape[1:], lambda i, j: (i,) + (0,) * (arr.ndim - 1))
    y, sfin, clast = pl.pallas_call(
        functools.partial(_ssd_kernel, n_chunks=tb // CHUNK),
        grid=(b, t // tb),
        in_specs=[tile(nxbc), tile(inner), tile(LANES),
                  bspec(pre8), const(conv_w), const(cb2), const(aexp), const(dexp),
                  const(expand), const(ng), bspec(s0t)],
        out_specs=[tile(inner), bspec(s0t), bspec(pre8)],
        out_shape=[jax.ShapeDtypeStruct((b, t, inner), BF16),
                   jax.ShapeDtypeStruct(s0t.shape, F32),
                   jax.ShapeDtypeStruct(pre8.shape, F32)],
        scratch_shapes=[pltpu.VMEM((SSD_STATE, inner), F32), pltpu.VMEM((SUBLANES, nxbc), F32),
                        pltpu.VMEM((nxbc // LANES, SUBLANES * _segment_pitch(tb), LANES), F32),
                        pltpu.VMEM((tb, inner), F32)],
        compiler_params=_params(2),
        name="ssd",
    )(xbc, zg, dt, pre8, conv_w, cb2, aexp, dexp, expand, ng, s0t)
    sfin = jnp.transpose(sfin.reshape(b, SSD_STATE, nh, SSD_HEAD_DIM), (0, 2, 3, 1))
    return y, sfin, clast[:, SUBLANES - (width - 1):]


def _trunk(x, mods, P, st, sample):
    b, t, d = x.shape
    tb = min(MAX_TILE, t)
    depth = P['w_mod'].shape[0]
    new = {name: [] for name in ('s5_re', 's5_im', 'gla', 'swa_k', 'swa_v', 'ssd', 'ssd_conv', 'ffn_conv')}
    for layer in range(depth):
        i = layer // 2
        mod = mods[layer].reshape(b, 6, d)
        ffn = (P['norm2_g'][layer], P['ffn_w_up'][layer], P['ffn_conv_w'][layer], P['ffn_conv_b'][layer],
               P['ffn_w_down'][layer], st['ffn_conv'][layer], tb)
        if layer % 2 == 0:
            u, qk, v, r, gate = _ev_in(x, mod, P['norm1_g'][layer], P['ev_w_in'][i], P['gla_w_gate2'][i],
                                       P['gla_b_gate'][i], tb)
            mats = _s5_matrices(P['s5_a_re'][i], P['s5_a_im'][i], P['s5_log_dt'][i], P['s5_b_re'][i],
                                P['s5_b_im'][i], P['s5_c_re'][i], P['s5_c_im'][i])
            ya, sr, si = _s5(u, mats, P['s5_d'][i], st['s5_re'][i], st['s5_im'][i])
            ob, sg = _gla(qk, v, gate, r, st['gla'][i], P['gla_norm_g'][i], tb)
            new['s5_re'].append(sr)
            new['s5_im'].append(si)
            new['gla'].append(sg)
            x, fp = _out_ffn(x, mod, ya, ob, P['ev_w_out'][i], *ffn,
                             s5_extra=(P['s5_w_glu'][i], P['s5_b_glu'][i]))
        else:
            nq = P['swa_sink'].shape[1] * HEAD_DIM
            nkv = SWA_KV_HEADS * HEAD_DIM
            inner = P['ssd_norm_g'].shape[1]
            nxbc = P['ssd_conv_w'].shape[2]
            nh = P['ssd_a_log'].shape[1]
            q, k, v, zg, xbc, dt = _od_in(x, mod, P['norm1_g'][layer], P['od_w_in'][i], P['swa_q_norm'][i],
                                               P['swa_k_norm'][i], P['ssd_dt_bias'][i],
                                               (nq, nkv, inner, nxbc, nh), tb)
            bias = _rel_bias(P['t5_bias'])
            if sample:
                k0 = st['swa_k'][i].reshape(b, WINDOW, nkv)
                v0 = st['swa_v'][i].reshape(b, WINDOW, nkv)
            else:
                k0 = v0 = jnp.zeros((b, WINDOW, nkv), F32)
            oc = _swa(q, k, v, k0, v0, bias, P['swa_sink'][i], tb, mask_start=not sample)
            yd, ss, sc = _ssd(xbc, zg, dt, st['ssd_conv'][i], P['ssd_conv_w'][i], P['ssd_conv_b'][i],
                              P['ssd_a_log'][i], P['ssd_d'][i], P['ssd_norm_g'][i], st['ssd'][i], tb)
            keep = slice(None) if sample else slice(t - WINDOW, t)
            new['swa_k'].append(k[:, keep].reshape(b, -1, SWA_KV_HEADS, HEAD_DIM))
            new['swa_v'].append(v[:, keep].reshape(b, -1, SWA_KV_HEADS, HEAD_DIM))
            new['ssd'].append(ss)
            new['ssd_conv'].append(sc)
            x, fp = _out_ffn(x, mod, oc, yd, P['od_w_out'][i], *ffn)
        new['ffn_conv'].append(fp)
    return x, {name: jnp.stack(vals) for name, vals in new.items()}


def kernel(x_prompt, x_sample, state_s5_re, state_s5_im, state_gla, cache_swa_k, cache_swa_v, state_ssd, state_ssd_conv, state_ffn_conv, c_prompt, c_sample, t5_bias, norm1_g, norm2_g, w_mod, b_mod, ffn_w_up, ffn_conv_w, ffn_conv_b, ffn_w_down, ev_w_in, ev_w_out, s5_a_re, s5_a_im, s5_log_dt, s5_b_re, s5_b_im, s5_c_re, s5_c_im, s5_d, s5_w_glu, s5_b_glu, gla_w_gate2, gla_b_gate, gla_norm_g, od_w_in, od_w_out, swa_q_norm, swa_k_norm, swa_sink, ssd_conv_w, ssd_conv_b, ssd_dt_bias, ssd_a_log, ssd_d, ssd_norm_g):
    P = dict(t5_bias=t5_bias, norm1_g=norm1_g, norm2_g=norm2_g, w_mod=w_mod, b_mod=b_mod,
             ffn_w_up=ffn_w_up, ffn_conv_w=ffn_conv_w, ffn_conv_b=ffn_conv_b, ffn_w_down=ffn_w_down,
             ev_w_in=ev_w_in, ev_w_out=ev_w_out, s5_a_re=s5_a_re, s5_a_im=s5_a_im, s5_log_dt=s5_log_dt,
             s5_b_re=s5_b_re, s5_b_im=s5_b_im, s5_c_re=s5_c_re, s5_c_im=s5_c_im, s5_d=s5_d,
             s5_w_glu=s5_w_glu, s5_b_glu=s5_b_glu, gla_w_gate2=gla_w_gate2, gla_b_gate=gla_b_gate,
             gla_norm_g=gla_norm_g, od_w_in=od_w_in, od_w_out=od_w_out, swa_q_norm=swa_q_norm,
             swa_k_norm=swa_k_norm, swa_sink=swa_sink, ssd_conv_w=ssd_conv_w, ssd_conv_b=ssd_conv_b,
             ssd_dt_bias=ssd_dt_bias, ssd_a_log=ssd_a_log, ssd_d=ssd_d, ssd_norm_g=ssd_norm_g)
    bp = x_prompt.shape[0]
    n_even, n_odd = state_s5_re.shape[0], state_ssd.shape[0]
    depth = w_mod.shape[0]
    zeros_like_b = lambda a: jnp.zeros((a.shape[0], bp) + a.shape[2:], F32)
    zero_st = dict(s5_re=zeros_like_b(state_s5_re), s5_im=zeros_like_b(state_s5_im), gla=zeros_like_b(state_gla),
                   ssd=zeros_like_b(state_ssd), ssd_conv=zeros_like_b(state_ssd_conv),
                   ffn_conv=zeros_like_b(state_ffn_conv))
    sample_st = dict(s5_re=state_s5_re, s5_im=state_s5_im, gla=state_gla, swa_k=cache_swa_k,
                     swa_v=cache_swa_v, ssd=state_ssd, ssd_conv=state_ssd_conv, ffn_conv=state_ffn_conv)
    mods = _modulation(jnp.concatenate([c_prompt, c_sample], axis=0), w_mod, b_mod)
    y_prompt, stp = _trunk(x_prompt, mods[:, :bp], P, zero_st, False)
    y_sample, sts = _trunk(x_sample, mods[:, bp:], P, sample_st, True)
    names = ('s5_re', 's5_im', 'gla', 'swa_k', 'swa_v', 'ssd', 'ssd_conv', 'ffn_conv')
    return (y_prompt, y_sample) + tuple(stp[n] for n in names) + tuple(sts[n] for n in names)
```

```python
import functools
import math

import jax
import jax.numpy as jnp
import numpy as np
from jax import lax
from jax.experimental import pallas as pl
from jax.experimental.pallas import tpu as pltpu

F32 = jnp.float32
BF16 = jnp.bfloat16

CHUNK = 64
WINDOW = 128
S5_GROUP = 16
S5_STATE = 64
S5_SUB = 8
S5_MAX_SUBS = 64
GLA_HEADS = 4
GLA_GATE_NORM = 16.0
HEAD_DIM = 64
SWA_KV_HEADS = 2
SWA_KEYS = 256
SSD_HEAD_DIM = 64
SSD_STATE = 128
SSD_GROUPS = 2
T5_BUCKETS = 32
T5_MAX_DIST = 128
RMS_EPS = 1e-6
NEG_INF = -1e30
LANES = 128
SUBLANES = 8
assert S5_SUB == SUBLANES and S5_GROUP * SUBLANES == LANES
MAX_TILE = 512
MIXER_TILE = 1024
CHUNK_UNROLL = 8
FFN_COL_BLOCK = 256
VMEM_LIMIT = 56 * 1024 * 1024

_NT = (((1,), (1,)), ((), ()))
_HI = lax.Precision.HIGHEST


def _params(n_axes=2):
    sem = ("parallel",) + ("arbitrary",) * (n_axes - 1)
    return pltpu.CompilerParams(dimension_semantics=sem, vmem_limit_bytes=VMEM_LIMIT)


def _bdot(a, b):
    return jnp.dot(a.astype(BF16), b.astype(BF16), preferred_element_type=F32)


def _split_dot(x, c, parts=3):
    pieces = []
    rest = x
    for _ in range(parts):
        piece = rest.astype(BF16)
        pieces.append(piece)
        rest = rest - piece.astype(F32)
    n = x.shape[0]
    d = jnp.dot(jnp.concatenate(pieces, axis=0), c, preferred_element_type=F32)
    out = d[0:n]
    for i in range(1, parts):
        out = out + d[i * n:(i + 1) * n]
    return out


def _silu(x):
    return x * jax.nn.sigmoid(x)


def _softplus(x):
    return jnp.maximum(x, 0.0) + jnp.log1p(jnp.exp(-jnp.abs(x)))


def _log_sigmoid(x):
    return jnp.minimum(x, 0.0) - jnp.log1p(jnp.exp(-jnp.abs(x)))


def _norm_mod(x, g, scale, shift):
    ms = jnp.mean(x * x, axis=-1, keepdims=True)
    return (x * lax.rsqrt(ms + RMS_EPS) * g) * (1.0 + scale) + shift


def _cumsum_rows(x):
    n, m = x.shape
    tiles = x.reshape(n // SUBLANES, SUBLANES, m)
    sub = lax.broadcasted_iota(jnp.int32, (1, SUBLANES, 1), 1)
    d = 1
    while d < SUBLANES:
        tiles = tiles + jnp.where(sub >= d, pltpu.roll(tiles, d, axis=1), 0.0)
        d *= 2
    out = [tiles[0]]
    for i in range(1, n // SUBLANES):
        out.append(tiles[i] + out[-1][SUBLANES - 1:SUBLANES, :])
    return jnp.concatenate(out, axis=0)


def _causal_conv(u, prev8, w, b):
    width = w.shape[0]
    n = u.shape[0]
    ext = jnp.concatenate([prev8, u[0:SUBLANES]], axis=0)
    full = b
    head = b
    for j in range(width):
        sh = width - 1 - j
        if sh == 0:
            full = full + u * w[j:j + 1]
            head = head + u[0:SUBLANES] * w[j:j + 1]
        else:
            full = full + pltpu.roll(u, sh, axis=0) * w[j:j + 1]
            head = head + pltpu.roll(ext, sh, axis=0)[SUBLANES:2 * SUBLANES] * w[j:j + 1]
    if n == SUBLANES:
        return head
    return jnp.concatenate([head, full[SUBLANES:]], axis=0)


def _segment_pitch(n):
    pitch = n // SUBLANES + SUBLANES
    return pitch if (pitch // SUBLANES) % 2 else pitch + SUBLANES


def _to_segments(ref, val):
    nv = val.shape[0] // SUBLANES
    pitch = ref.shape[1] // SUBLANES
    for c in range(ref.shape[0]):
        for s in range(SUBLANES):
            ref[c, s * pitch:s * pitch + nv, :] = val[s * nv:(s + 1) * nv, c * LANES:(c + 1) * LANES]


def _from_segments(ref, n):
    nv = n // SUBLANES
    pitch = ref.shape[1] // SUBLANES
    return jnp.concatenate(
        [jnp.concatenate([ref[c, s * pitch:s * pitch + nv, :] for s in range(SUBLANES)], axis=0)
         for c in range(ref.shape[0])], axis=1)


def _rows_from_segments(ref, r0, n_rows, n):
    nv = n // SUBLANES
    pitch = ref.shape[1] // SUBLANES
    spans = []
    r = r0
    while r < r0 + n_rows:
        s, off = divmod(r, nv)
        take = min(nv - off, r0 + n_rows - r)
        spans.append(slice(s * pitch + off, s * pitch + off + take))
        r += take
    return jnp.concatenate(
        [jnp.concatenate([ref[c, sp, :] for sp in spans], axis=0) for c in range(ref.shape[0])], axis=1)


def _load_interleaved(ref, n):
    pitch = ref.shape[1] // SUBLANES
    return jnp.concatenate(
        [jnp.concatenate([ref[c, pl.ds(i, SUBLANES, stride=pitch), :] for c in range(ref.shape[0])], axis=1)
         for i in range(n // SUBLANES)], axis=0)


def _store_interleaved(ref, val):
    pitch = ref.shape[1] // SUBLANES
    for i in range(val.shape[0] // SUBLANES):
        for c in range(ref.shape[0]):
            ref[c, pl.ds(i, SUBLANES, stride=pitch), :] = val[i * SUBLANES:(i + 1) * SUBLANES, c * LANES:(c + 1) * LANES]


def _causal_conv_interleaved(u, prev8, w, b):
    width = w.shape[0]
    n, m = u.shape
    nv = n // SUBLANES
    sub = lax.broadcasted_iota(jnp.int32, (SUBLANES, 1), 0)
    tiles = u.reshape(nv, SUBLANES, m)

    def back_one(a, before):
        first = jnp.where(sub == 0, before, pltpu.roll(a[nv - 1], 1, axis=0))
        return jnp.concatenate([first[None], a[:nv - 1]], axis=0)

    delayed = [tiles]
    for k in range(1, width):
        delayed.append(back_one(delayed[-1], prev8[SUBLANES - k:SUBLANES - k + 1]))
    out = b
    for j in range(width):
        out = out + delayed[width - 1 - j] * w[j:j + 1]
    hist = tiles[nv - 1]
    for k in range(2, width):
        hist = jnp.where(sub == SUBLANES - k, pltpu.roll(tiles[nv - k], SUBLANES - k + 1, axis=0), hist)
    return out.reshape(n, m), hist


def _mod_kernel(c_ref, w_ref, b_ref, o_ref):
    o_ref[0] = _bdot(_silu(c_ref[...]), w_ref[0]) + b_ref[0]


def _modulation(c, w_mod, b_mod):
    depth, d, n = w_mod.shape
    bc = c.shape[0]
    tn = n // 4
    return pl.pallas_call(
        _mod_kernel,
        grid=(depth, n // tn),
        in_specs=[pl.BlockSpec((bc, d), lambda l, j: (0, 0)),
                  pl.BlockSpec((1, d, tn), lambda l, j: (l, 0, j)),
                  pl.BlockSpec((1, 1, tn), lambda l, j: (l, 0, j))],
        out_specs=pl.BlockSpec((1, bc, tn), lambda l, j: (l, 0, j)),
        out_shape=jax.ShapeDtypeStruct((depth, bc, n), F32),
        compiler_params=_params(2),
        name="modulation",
    )(c, w_mod, b_mod.reshape(depth, 1, n))


def _ev_in_kernel(x_ref, mod_ref, g_ref, w_ref, wg2_ref, bg_ref,
                  u_ref, qk_ref, v_ref, r_ref, gate_ref):
    mod = mod_ref[0]
    hn = _norm_mod(x_ref[0], g_ref[...], mod[1:2], mod[0:1]).astype(BF16)
    dot = lambda lo, hi: jnp.dot(hn, w_ref[:, lo:hi], preferred_element_type=F32)
    gl = dot(2048, 2048 + LANES)
    gate_ref[0] = _log_sigmoid(_bdot(gl, wg2_ref[...]) + bg_ref[...]) * (1.0 / GLA_GATE_NORM)
    u = dot(0, 512)
    for c in range(u_ref.shape[0]):
        u_ref[c] = _chunk_transpose(u[:, c * LANES:(c + 1) * LANES].reshape(-1, SUBLANES, LANES))
    qk_ref[0] = dot(512, 1024)
    v_ref[0] = dot(1024, 1536).astype(BF16)
    r_ref[0] = dot(1536, 2048)


def _ev_in(x, mod, norm_g, w_in, w_gate2, b_gate, tb):
    b, t, d = x.shape
    rank = w_gate2.shape[0]
    nk = w_gate2.shape[1]
    wu, wq, wk, wv, wgl, wr = jnp.split(w_in, [512, 768, 1024, 1536, 1536 + rank], axis=1)
    w = jnp.concatenate([wu, wq, wk, wv, wr, wgl, jnp.zeros((d, LANES - rank), F32)], axis=1).astype(BF16)
    wg2 = jnp.concatenate([w_gate2, jnp.zeros((LANES - rank, nk), F32)], axis=0).astype(BF16)
    tile = lambda n: pl.BlockSpec((1, tb, n), lambda i, j: (i, j, 0))
    const = lambda a: pl.BlockSpec(a.shape, lambda i, j: (0,) * a.ndim)
    g2 = norm_g.reshape(1, d)
    bg = b_gate.reshape(1, nk)
    return pl.pallas_call(
        _ev_in_kernel,
        grid=(b, t // tb),
        in_specs=[tile(d), pl.BlockSpec((1, 6, d), lambda i, j: (i, 0, 0)), const(g2), const(w),
                  const(wg2), const(bg)],
        out_specs=[pl.BlockSpec((512 // LANES, tb // S5_SUB, None, SUBLANES, LANES), lambda i, j: (0, j, i, 0, 0)),
                   tile(512), tile(512), tile(512), tile(nk)],
        out_shape=[jax.ShapeDtypeStruct((512 // LANES, t // S5_SUB, b, SUBLANES, LANES), F32)]
        + [jax.ShapeDtypeStruct((b, t, n), dt) for n, dt in ((512, F32), (512, BF16), (512, F32), (nk, F32))],
        compiler_params=_params(2),
        name="ev_in",
    )(x, mod, g2, w, wg2, bg)


def _gla_kernel(qk_ref, v_ref, gate_ref, r_ref, s0_ref, ng_ref, o_ref, sfin_ref, st_scr, *, n_chunks):
    t = pl.program_id(1)
    dk = qk_ref.shape[2] // 2 // GLA_HEADS
    dv = v_ref.shape[2] // GLA_HEADS
    nk = GLA_HEADS * dk
    nv = GLA_HEADS * dv
    blk = (lax.broadcasted_iota(jnp.int32, (nv, nk), 0) // dv
           == lax.broadcasted_iota(jnp.int32, (nv, nk), 1) // dk)

    @pl.when(t == 0)
    def _():
        s0 = jnp.concatenate([s0_ref[0, h] for h in range(GLA_HEADS)], axis=0)
        st_scr[...] = jnp.where(blk, jnp.concatenate([s0] * GLA_HEADS, axis=1), 0.0)

    n_hs = GLA_HEADS * CHUNK
    row_head = lax.broadcasted_iota(jnp.int32, (n_hs, 1), 0) // CHUNK
    own_k = row_head == lax.broadcasted_iota(jnp.int32, (1, nk), 1) // dk
    own_v = row_head == lax.broadcasted_iota(jnp.int32, (1, nv), 1) // dv
    causal = (lax.broadcasted_iota(jnp.int32, (CHUNK, 1), 0)
              >= lax.broadcasted_iota(jnp.int32, (1, n_hs), 1) % CHUNK)
    ng = ng_ref[...]

    def chunk(c, carry):
        rows = pl.ds(pl.multiple_of(c * CHUNK, CHUNK), CHUNK)
        k = qk_ref[0, rows, nk:2 * nk]
        v_b = v_ref[0, rows, :]
        cum = _cumsum_rows(gate_ref[0, rows, :])
        cum_last = cum[CHUNK - 1:CHUNK, :]
        qe = (qk_ref[0, rows, 0:nk] * dk ** -0.5 * jnp.exp(cum)).astype(BF16)
        ke = (k * jnp.exp(-cum)).astype(BF16)
        kd = (k * jnp.exp(cum_last - cum)).astype(BF16)
        st = st_scr[...]
        ke_bd = jnp.where(own_k, jnp.concatenate([ke] * GLA_HEADS, axis=0), jnp.zeros((), BF16))
        v_bd = jnp.where(own_v, jnp.concatenate([v_b] * GLA_HEADS, axis=0), jnp.zeros((), BF16))
        att = lax.dot_general(qe, ke_bd, _NT, preferred_element_type=F32)
        att = jnp.where(causal, att, 0.0).astype(BF16)
        o = (jnp.dot(att, v_bd, preferred_element_type=F32)
             + lax.dot_general(qe, st.astype(BF16), _NT, preferred_element_type=F32))
        for h in range(GLA_HEADS):
            cols = slice(h * dv, (h + 1) * dv)
            oh = o[:, cols]
            ms = jnp.mean(oh * oh, axis=-1, keepdims=True)
            oh = oh * lax.rsqrt(ms + RMS_EPS) * ng
            o_ref[0, rows, cols] = (oh * _silu(r_ref[0, rows, cols])).astype(o_ref.dtype)
        upd = jnp.dot(v_b.astype(F32).T.astype(BF16), kd, preferred_element_type=F32)
        st_scr[...] = jnp.where(blk, st * jnp.exp(cum_last) + upd, 0.0)
        return carry

    lax.fori_loop(0, n_chunks, chunk, 0, unroll=math.gcd(n_chunks, CHUNK_UNROLL))

    @pl.when(t == pl.num_programs(1) - 1)
    def _():
        for h in range(GLA_HEADS):
            sfin_ref[0, h] = st_scr[h * dv:(h + 1) * dv, h * dk:(h + 1) * dk]


def _gla(qk, v, gate, r, s0, norm_g, tb):
    b, t, nv = v.shape
    nk = gate.shape[2]
    dk, dv = nk // GLA_HEADS, nv // GLA_HEADS
    s0t = jnp.swapaxes(s0, 2, 3)
    tile = lambda n: pl.BlockSpec((1, tb, n), lambda i, j: (i, j, 0))
    sspec = pl.BlockSpec((1, GLA_HEADS, dv, dk), lambda i, j: (i, 0, 0, 0))
    ng = norm_g.reshape(1, dv)
    o, sfin = pl.pallas_call(
        functools.partial(_gla_kernel, n_chunks=tb // CHUNK),
        grid=(b, t // tb),
        in_specs=[tile(2 * nk), tile(nv), tile(nk), tile(nv), sspec,
                  pl.BlockSpec((1, dv), lambda i, j: (0, 0))],
        out_specs=[tile(nv), sspec],
        out_shape=[jax.ShapeDtypeStruct((b, t, nv), BF16),
                   jax.ShapeDtypeStruct((b, GLA_HEADS, dv, dk), F32)],
        scratch_shapes=[pltpu.VMEM((nv, nk), F32)],
        compiler_params=_params(2),
        name="gla",
    )(qk, v, gate, r, s0t, ng)
    return o, jnp.swapaxes(sfin, 2, 3)


def _chunk_transpose(x):
    s = lax.broadcasted_iota(jnp.int32, (1, SUBLANES, LANES), 1)
    c = lax.broadcasted_iota(jnp.int32, (1, SUBLANES, LANES), 2) // S5_GROUP
    for d in (4, 2, 1):
        sb = (s & d) != 0
        cb = (c & d) != 0
        if 2 * d == SUBLANES:
            t = pltpu.roll(pltpu.roll(x, d, axis=1), S5_GROUP * d, axis=2)
        else:
            xs = jnp.where(sb, pltpu.roll(x, d, axis=1), pltpu.roll(x, SUBLANES - d, axis=1))
            t = jnp.where(cb, pltpu.roll(xs, S5_GROUP * d, axis=2), pltpu.roll(xs, LANES - S5_GROUP * d, axis=2))
        x = jnp.where(sb != cb, t, x)
    return x


def _s5_kernel(u_ref, w1_ref, n_ref, a_ref, d_ref, x0_ref, y_ref, xf_ref, mm_scr, xs_scr, st_scr, *, tk, bsz):
    j = pl.program_id(1)
    n_g = st_scr.shape[0]
    rows = tk * bsz
    of_group = lambda g: pl.ds(g, rows, stride=n_g)

    @pl.when(j == 0)
    def _():
        st_scr[...] = x0_ref[0]

    for g in range(n_g):
        ub = u_ref[of_group(g), :].astype(BF16)
        mm_scr[g] = jnp.dot(ub, w1_ref[0, g], preferred_element_type=F32)

    half = n_g // 2
    for g0 in (0, half):
        coef = [a_ref[0, g] for g in range(g0, g0 + half)]

        def step(k, carry, g0=g0, coef=coef):
            r = pl.ds(pl.multiple_of(k * bsz, bsz), bsz)
            out = []
            for i in range(half):
                x, xsw = carry[2 * i], carry[2 * i + 1]
                a1, a2, a2s = coef[i][0:1], coef[i][1:2], coef[i][2:3]
                xs_scr[g0 + i, r, :] = x
                out.append(a1 * x + a2 * xsw + mm_scr[g0 + i, r, LANES:2 * LANES])
                out.append(a1 * xsw + a2s * x + mm_scr[g0 + i, r, 2 * LANES:3 * LANES])
            return tuple(out)

        fin = lax.fori_loop(0, tk, step, tuple(st_scr[g0 + i, v] for i in range(half) for v in range(2)))
        for i in range(half):
            st_scr[g0 + i, 0] = fin[2 * i]
            st_scr[g0 + i, 1] = fin[2 * i + 1]

    for g in range(n_g):
        y = (mm_scr[g, :, 0:LANES] + jnp.dot(xs_scr[g].astype(BF16), n_ref[0, g], preferred_element_type=F32)
             + d_ref[0, g] * u_ref[of_group(g), :])
        y_ref[of_group(g), :] = jax.nn.gelu(y)

    @pl.when(j == pl.num_programs(1) - 1)
    def _():
        xf_ref[0] = st_scr[:, 0]


def _cmul(ar, ai, br, bi):
    return ar * br - ai * bi, ar * bi + ai * br


def _s5_matrices(a_re, a_im, log_dt, b_re, b_im, c_re, c_im):
    g, p = a_re.shape
    hdim = b_re.shape[-1]
    dt = jnp.exp(log_dt)[:, None]
    mag = jnp.exp(a_re * dt)
    ab_re, ab_im = mag * jnp.cos(a_im * dt), mag * jnp.sin(a_im * dt)
    den = a_re * a_re + a_im * a_im
    num_re, num_im = ab_re - 1.0, ab_im
    g_re = (num_re * a_re + num_im * a_im) / den
    g_im = (num_im * a_re - num_re * a_im) / den
    bb_re = g_re[..., None] * b_re - g_im[..., None] * b_im
    bb_im = g_re[..., None] * b_im + g_im[..., None] * b_re
    pw_re, pw_im = jnp.ones((1, g, p), F32), jnp.zeros((1, g, p), F32)
    sq_re, sq_im = ab_re, ab_im
    while pw_re.shape[0] < S5_SUB + 1:
        nr, ni = _cmul(pw_re, pw_im, sq_re, sq_im)
        pw_re, pw_im = jnp.concatenate([pw_re, nr]), jnp.concatenate([pw_im, ni])
        sq_re, sq_im = _cmul(sq_re, sq_im, sq_re, sq_im)
    pw_re, pw_im = pw_re[:S5_SUB + 1], pw_im[:S5_SUB + 1]
    ca_re, ca_im = _cmul(c_re[None], c_im[None], pw_re[:, :, None, :], pw_im[:, :, None, :])
    kern = (jnp.einsum('tghp,gpk->tghk', ca_re[:S5_SUB], bb_re, precision=_HI)
            - jnp.einsum('tghp,gpk->tghk', ca_im[:S5_SUB], bb_im, precision=_HI))
    s_in = np.arange(S5_SUB)[:, None, None]
    s_out = np.arange(S5_SUB)[None, :, None]
    lag = jnp.asarray(s_out - s_in == np.arange(S5_SUB), F32)
    m = jnp.einsum('iot,tghk->ioghk', lag, kern, precision=_HI)
    m = jnp.transpose(m, (2, 0, 4, 1, 3)).reshape(g, S5_SUB * hdim, S5_SUB * hdim)
    pr, pi = _cmul(pw_re[S5_SUB - 1::-1][..., None], pw_im[S5_SUB - 1::-1][..., None], bb_re[None], bb_im[None])
    flat = lambda a: jnp.transpose(a, (1, 0, 3, 2)).reshape(g, S5_SUB * hdim, p)
    nm = jnp.concatenate([ca_re[1:], -ca_im[1:]], axis=3)
    nm = jnp.transpose(nm, (1, 3, 0, 2)).reshape(g, 2 * p, S5_SUB * hdim)
    w1 = jnp.concatenate([m, flat(pr), flat(pi), flat(pi), flat(pr)], axis=2)
    a_r, a_i = pw_re[S5_SUB], pw_im[S5_SUB]
    coef = jnp.stack([jnp.concatenate([a_r, a_r], axis=1), jnp.concatenate([-a_i, a_i], axis=1),
                      jnp.concatenate([a_i, -a_i], axis=1)], axis=1)
    return w1.astype(BF16), nm.astype(BF16), coef


def _s5(u2, mats, d_skip, x0_re, x0_im):
    w1, nm, coef = mats
    n_ct, n_sub, b, n_g, lanes = u2.shape
    g, p2 = nm.shape[0], nm.shape[1]
    hdim = lanes // S5_SUB
    tk = min(n_sub, S5_MAX_SUBS)
    x0 = jnp.transpose(jnp.concatenate([x0_re, x0_im], axis=2), (1, 0, 2))
    x0 = jnp.stack([x0, jnp.roll(x0, p2 // 2, axis=2)], axis=1)
    dt = jnp.tile(d_skip.reshape(g, 1, hdim), (1, 1, S5_SUB))
    by_tile = lambda a: a.reshape((n_ct, n_g) + a.shape[1:])
    pspec = lambda a: pl.BlockSpec((1,) + a.shape[1:], lambda c, j: (c,) + (0,) * (a.ndim - 1))
    rows = tk * b
    uspec = pl.BlockSpec((None, rows * n_g, lanes), lambda c, j: (c, j, 0))
    params = [by_tile(a) for a in (w1, nm, coef, dt, x0)]
    y, xf = pl.pallas_call(
        functools.partial(_s5_kernel, tk=tk, bsz=b),
        grid=(n_ct, n_sub // tk),
        in_specs=[uspec] + [pspec(a) for a in params],
        out_specs=[uspec, pl.BlockSpec((1, n_g, b, p2), lambda c, j: (c, 0, 0, 0))],
        out_shape=[jax.ShapeDtypeStruct((n_ct, n_sub * b * n_g, lanes), F32),
                   jax.ShapeDtypeStruct((n_ct, n_g, b, p2), F32)],
        scratch_shapes=[pltpu.VMEM((n_g, rows, 3 * lanes), F32), pltpu.VMEM((n_g, rows, lanes), F32),
                        pltpu.VMEM((n_g, 2, b, p2), F32)],
        compiler_params=_params(2),
        name="s5",
    )(u2.reshape(n_ct, n_sub * b * n_g, lanes), *params)
    xf = jnp.transpose(xf.reshape(g, b, p2), (1, 0, 2))
    return y.reshape(u2.shape), xf[:, :, :p2 // 2], xf[:, :, p2 // 2:]


def _out_ffn_kernel(*refs, with_s5, cb):
    if with_s5:
        (x_ref, mod_ref, m1_ref, m2_ref, wglu_ref, bglu_ref, wout_ref, n2g_ref,
         wup_ref, cw_ref, cbias_ref, wd_ref, pre_ref, xo_ref, ulast_ref, carry_scr, h_scr, row_scr) = refs
    else:
        (x_ref, mod_ref, m1_ref, m2_ref, wout_ref, n2g_ref,
         wup_ref, cw_ref, cbias_ref, wd_ref, pre_ref, xo_ref, ulast_ref, carry_scr, h_scr, row_scr) = refs
    t = pl.program_id(1)

    @pl.when(t == 0)
    def _():
        carry_scr[...] = pre_ref[0]

    mod = mod_ref[0]
    half = m2_ref.shape[2]
    out2 = jnp.dot(m2_ref[0], wout_ref[half:, :], preferred_element_type=F32)
    if with_s5:
        tiles = []
        glu = bglu_ref[...]
        for c in range(m1_ref.shape[0]):
            tiles.append(_chunk_transpose(m1_ref[c]).reshape(-1, LANES))
            glu = glu + jnp.dot(tiles[-1].astype(BF16), wglu_ref[c * LANES:(c + 1) * LANES, :],
                                preferred_element_type=F32)
        m1 = (jnp.concatenate(tiles, axis=1) * jax.nn.sigmoid(glu)).astype(BF16)
    else:
        m1 = m1_ref[0]
    out = jnp.dot(m1, wout_ref[0:half, :], preferred_element_type=F32) + out2
    x1 = x_ref[0] + mod[2:3] * out
    tb = x1.shape[0]
    _to_segments(row_scr, _norm_mod(x1, n2g_ref[...], mod[4:5], mod[3:4]))
    hn = _load_interleaved(row_scr, tb).astype(BF16)
    dff = wd_ref.shape[0]
    n_cb = dff // cb
    a_cols = lambda j: slice(j * cb, (j + 1) * cb)
    g_cols = lambda j: slice(dff + j * cb, dff + (j + 1) * cb)

    def up(j):
        return (jnp.dot(hn, wup_ref[:, a_cols(j)], preferred_element_type=F32),
                jnp.dot(hn, wup_ref[:, g_cols(j)], preferred_element_type=F32))

    def conv(u, cols):
        out, carry_scr[:, cols] = _causal_conv_interleaved(u, carry_scr[:, cols], cw_ref[:, cols], cbias_ref[:, cols])
        return out

    nxt = up(0)
    for j in range(n_cb):
        ua, ug = nxt
        if j + 1 < n_cb:
            nxt = up(j + 1)
        h_scr[:, a_cols(j)] = (_silu(conv(ua, a_cols(j))) * conv(ug, g_cols(j))).astype(BF16)
    _store_interleaved(row_scr, jnp.dot(h_scr[...], wd_ref[...], preferred_element_type=F32))
    xo_ref[0] = x1 + mod[5:6] * _from_segments(row_scr, tb)

    @pl.when(t == pl.num_programs(1) - 1)
    def _():
        ulast_ref[0] = carry_scr[...]


def _out_ffn(x, mod, m1, m2, w_out, norm2_g, w_up, conv_w, conv_b, w_down, prefix, tb, s5_extra=None):
    b, t, d = x.shape
    dff = w_down.shape[0]
    assert dff % FFN_COL_BLOCK == 0
    width = conv_w.shape[0]
    half = m2.shape[2]
    w_up_b = w_up.astype(BF16)
    cbias = conv_b.reshape(1, 2 * dff)
    wd = w_down.astype(BF16)
    pre8 = jnp.concatenate([jnp.zeros((b, SUBLANES - (width - 1), 2 * dff), F32), prefix], axis=1)
    tile = lambda n: pl.BlockSpec((1, tb, n), lambda i, j: (i, j, 0))
    const = lambda a: pl.BlockSpec(a.shape, lambda i, j: (0,) * a.ndim, pipeline_mode=pl.Buffered(1))
    bspec = lambda a: pl.BlockSpec((1,) + a.shape[1:], lambda i, j: (i,) + (0,) * (a.ndim - 1))
    wo = w_out.astype(BF16)
    n2g = norm2_g.reshape(1, d)
    args = [x, mod]
    specs = [tile(d), bspec(mod)]
    if s5_extra is not None:
        w_glu, b_glu = s5_extra
        w_glu, b_glu = w_glu.astype(BF16), b_glu.reshape(1, half)
        args += [m1, m2, w_glu, b_glu]
        specs += [pl.BlockSpec((m1.shape[0], tb // S5_SUB, None) + m1.shape[3:], lambda i, j: (0, j, i, 0, 0)),
                  tile(m2.shape[2]), const(w_glu), const(b_glu)]
    else:
        args += [m1, m2]
        specs += [tile(half), tile(m2.shape[2])]
    weights = [wo, n2g, w_up_b, conv_w, cbias, wd]
    args += weights + [pre8]
    specs += [const(a) for a in weights] + [bspec(pre8)]
    xo, ulast = pl.pallas_call(
        functools.partial(_out_ffn_kernel, with_s5=s5_extra is not None, cb=FFN_COL_BLOCK),
        grid=(b, t // tb),
        in_specs=specs,
        out_specs=[tile(d), bspec(pre8)],
        out_shape=[jax.ShapeDtypeStruct((b, t, d), F32), jax.ShapeDtypeStruct(pre8.shape, F32)],
        scratch_shapes=[pltpu.VMEM(pre8.shape[1:], F32), pltpu.VMEM((tb, dff), BF16),
                        pltpu.VMEM((d // LANES, SUBLANES * _segment_pitch(tb), LANES), F32)],
        compiler_params=_params(2),
        name="out_ffn_s5" if s5_extra is not None else "out_ffn",
    )(*args)
    return xo, ulast[:, SUBLANES - (width - 1):]


def _head_rms(x, g):
    left = lax.broadcasted_iota(jnp.int32, (1, LANES), 1) < HEAD_DIM
    out = []
    for c in range(x.shape[1] // LANES):
        xt = x[:, c * LANES:(c + 1) * LANES]
        sq = xt * xt
        s_left = jnp.sum(jnp.where(left, sq, 0.0), axis=-1, keepdims=True)
        s_right = jnp.sum(jnp.where(left, 0.0, sq), axis=-1, keepdims=True)
        ms = jnp.where(left, s_left, s_right) * (1.0 / HEAD_DIM)
        out.append(xt * lax.rsqrt(ms + RMS_EPS))
    return jnp.concatenate(out, axis=1) * g


def _od_in_kernel(x_ref, mod_ref, g_ref, w_ref, qg_ref, kg_ref, dtb_ref,
                  q_ref, k_ref, v_ref, zg_ref, xbc_ref, dt_ref):
    mod = mod_ref[0]
    nq = q_ref.shape[2]
    nkv = k_ref.shape[2]
    hn = _norm_mod(x_ref[0], g_ref[...], mod[1:2], mod[0:1]).astype(BF16)
    dot = lambda lo, hi: jnp.dot(hn, w_ref[:, lo:hi], preferred_element_type=F32)
    o_zg = nq + 2 * nkv
    o_xbc = o_zg + zg_ref.shape[2]
    o_dt = o_xbc + xbc_ref.shape[2]
    dt_ref[0] = _softplus(dot(o_dt, o_dt + LANES) + dtb_ref[...])
    q_ref[0] = _head_rms(dot(0, nq), qg_ref[...]).astype(BF16)
    kv = dot(nq, nq + 2 * nkv)
    k_ref[0] = _head_rms(kv[:, 0:nkv], kg_ref[...])
    v_ref[0] = kv[:, nkv:2 * nkv]
    zg_ref[0] = dot(o_zg, o_xbc)
    xbc_ref[0] = dot(o_xbc, o_dt)


def _od_in(x, mod, norm_g, w_in, q_norm, k_norm, dt_bias, dims, tb):
    b, t, d = x.shape
    nq, nkv, nz, nxbc, nh = dims
    w = jnp.concatenate([w_in, jnp.zeros((d, LANES - nh), F32)], axis=1).astype(BF16)
    qg = jnp.tile(q_norm, nq // HEAD_DIM).reshape(1, nq)
    kg = jnp.tile(k_norm, nkv // HEAD_DIM).reshape(1, nkv)
    dtb = jnp.concatenate([dt_bias, jnp.zeros((LANES - nh,), F32)]).reshape(1, LANES)
    g2 = norm_g.reshape(1, d)
    tile = lambda n: pl.BlockSpec((1, tb, n), lambda i, j: (i, j, 0))
    const = lambda a: pl.BlockSpec(a.shape, lambda i, j: (0,) * a.ndim)
    widths = (nq, nkv, nkv, nz, nxbc, LANES)
    return pl.pallas_call(
        _od_in_kernel,
        grid=(b, t // tb),
        in_specs=[tile(d), pl.BlockSpec((1, 6, d), lambda i, j: (i, 0, 0)), const(g2), const(w),
                  const(qg), const(kg), const(dtb)],
        out_specs=[tile(n) for n in widths],
        out_shape=[jax.ShapeDtypeStruct((b, t, n), BF16 if i == 0 else F32) for i, n in enumerate(widths)],
        compiler_params=_params(2),
        name="od_in",
    )(x, mod, g2, w, qg, kg, dtb)


def _swa_kernel(q_ref, k_ref, v_ref, k0_ref, v0_ref, bias_ref, sink_ref, o_ref, kx, vx,
                *, n_blocks, nq, mask_start):
    t = pl.program_id(1)
    tb = q_ref.shape[1]
    n_kv = k_ref.shape[2] // HEAD_DIM
    rows_q = nq * CHUNK
    tail = kx.shape[1] - WINDOW - tb
    left = lax.broadcasted_iota(jnp.int32, (1, LANES), 1) < HEAD_DIM
    first_side = lax.broadcasted_iota(jnp.int32, (2 * SWA_KEYS, 1), 0) < SWA_KEYS
    side_ones = jnp.where(first_side == left, 1.0, 0.0).astype(BF16)
    col = lax.broadcasted_iota(jnp.int32, (1, SWA_KEYS), 1)

    def place(dst, rows, x):
        rolled = pltpu.roll(x, HEAD_DIM, axis=1)
        dst[0, rows, :] = jnp.where(left, x, 0.0).astype(BF16)
        dst[1, rows, :] = jnp.where(left, 0.0, rolled).astype(BF16)
        dst[2, rows, :] = jnp.where(left, rolled, 0.0).astype(BF16)
        dst[3, rows, :] = jnp.where(left, 0.0, x).astype(BF16)

    @pl.when(t == 0)
    def _():
        place(kx, slice(0, WINDOW), k0_ref[0])
        place(vx, slice(0, WINDOW), v0_ref[0])
        for i in range(2 * n_kv if tail else 0):
            kx[i, WINDOW + tb:, :] = jnp.zeros((tail, LANES), BF16)
            vx[i, WINDOW + tb:, :] = jnp.zeros((tail, LANES), BF16)

    place(kx, slice(WINDOW, WINDOW + tb), k_ref[0])
    place(vx, slice(WINDOW, WINDOW + tb), v_ref[0])

    def block(blk, carry):
        r0 = pl.multiple_of(blk * rows_q, rows_q)
        keys = pl.ds(r0, SWA_KEYS)
        if mask_start:
            valid = t * tb + r0 - WINDOW + col >= 0
        probs = []
        sinks = []
        for j in range(n_kv):
            lo = 2 * j * LANES
            qg = jnp.concatenate([q_ref[0, pl.ds(r0 + cq * CHUNK, CHUNK), lo + r * LANES:lo + (r + 1) * LANES]
                                  for cq in range(nq) for r in range(2)], axis=0).astype(BF16)
            kcat = jnp.concatenate([kx[2 * j, keys, :], kx[2 * j + 1, keys, :]], axis=0)
            s_both = lax.dot_general(qg, kcat, _NT, preferred_element_type=F32) * HEAD_DIM ** -0.5
            for side in range(2):
                s = s_both[:, side * SWA_KEYS:(side + 1) * SWA_KEYS] + bias_ref[j, side]
                if mask_start:
                    s = jnp.where(valid, s, NEG_INF)
                m = jnp.max(s, axis=-1, keepdims=True)
                probs.append(jnp.exp(s - m).astype(BF16))
                sinks.append(jnp.exp(sink_ref[j, side] - m))
        for j in range(n_kv):
            lo = 2 * j * LANES
            vcat = jnp.concatenate(
                [jnp.concatenate([vx[2 * j, keys, :], vx[2 * j + 1, keys, :]], axis=0), side_ones], axis=1)
            pv = jnp.dot(jnp.concatenate(probs[2 * j:2 * j + 2], axis=1), vcat, preferred_element_type=F32)
            den = pv[:, LANES:] + jnp.where(left, sinks[2 * j], sinks[2 * j + 1])
            out = pv[:, 0:LANES] / den
            for cq in range(nq):
                for r in range(2):
                    o_ref[0, pl.ds(r0 + cq * CHUNK, CHUNK), lo + r * LANES:lo + (r + 1) * LANES] = (
                        out[(2 * cq + r) * CHUNK:(2 * cq + r + 1) * CHUNK].astype(o_ref.dtype))
        return carry

    lax.fori_loop(0, n_blocks, block, 0, unroll=math.gcd(n_blocks, 2))
    for i in range(2 * n_kv):
        kx[i, 0:WINDOW, :] = kx[i, tb:tb + WINDOW, :]
        vx[i, 0:WINDOW, :] = vx[i, tb:tb + WINDOW, :]


def _t5_bucket(rel):
    nb = T5_BUCKETS // 2
    max_exact = nb // 2
    ret = (rel > 0).astype(jnp.int32) * nb
    n = jnp.abs(rel)
    nf = jnp.maximum(n, 1).astype(F32)
    large = max_exact + (jnp.log(nf / max_exact) / math.log(T5_MAX_DIST / max_exact)
                         * (nb - max_exact)).astype(jnp.int32)
    large = jnp.minimum(large, nb - 1)
    return ret + jnp.where(n < max_exact, n, large)


def _rel_bias(table):
    rel = (jnp.arange(WINDOW + CHUNK)[None, :] - WINDOW) - jnp.arange(CHUNK)[:, None]
    onehot = (_t5_bucket(rel)[..., None] == jnp.arange(T5_BUCKETS)).astype(F32)
    return jnp.einsum('qkb,bh->hqk', onehot, table, precision=_HI)


def _swa(q, k, v, k0, v0, bias, sink, tb, mask_start):
    b, t, nq = q.shape
    nkv = k.shape[2]
    n_kv = nkv // HEAD_DIM
    assert nkv == LANES and nq == 2 * n_kv * LANES
    by_side = lambda a: jnp.transpose(a.reshape((n_kv, 2, 2) + a.shape[1:]), (0, 2, 1) + tuple(range(3, a.ndim + 2)))
    span = WINDOW + CHUNK
    n_chunks = tb // CHUNK
    cpb = 2 if n_chunks % 2 == 0 else 1
    assert WINDOW + cpb * CHUNK <= SWA_KEYS
    bias2 = by_side(bias).reshape(n_kv, 2, 2 * CHUNK, span)
    bias3 = jnp.concatenate(
        [jnp.pad(bias2, ((0, 0), (0, 0), (0, 0), (cq * CHUNK, SWA_KEYS - span - cq * CHUNK)),
                 constant_values=NEG_INF) for cq in range(cpb)], axis=2)
    sink3 = jnp.tile(jnp.repeat(by_side(sink), CHUNK, axis=2), (1, 1, cpb))
    sink3 = jnp.broadcast_to(sink3[..., None], sink3.shape + (LANES,))
    tile = lambda n: pl.BlockSpec((1, tb, n), lambda i, j: (i, j, 0))
    wspec = pl.BlockSpec((1, WINDOW, nkv), lambda i, j: (i, 0, 0))
    const = lambda a: pl.BlockSpec(a.shape, lambda i, j: (0,) * a.ndim)
    rows = WINDOW + max(tb, WINDOW) + SWA_KEYS - WINDOW - cpb * CHUNK
    return pl.pallas_call(
        functools.partial(_swa_kernel, n_blocks=n_chunks // cpb, nq=cpb, mask_start=mask_start),
        grid=(b, t // tb),
        in_specs=[tile(nq), tile(nkv), tile(nkv), wspec, wspec, const(bias3), const(sink3)],
        out_specs=tile(nq),
        out_shape=jax.ShapeDtypeStruct((b, t, nq), BF16),
        scratch_shapes=[pltpu.VMEM((2 * n_kv, rows, LANES), BF16), pltpu.VMEM((2 * n_kv, rows, LANES), BF16)],
        compiler_params=_params(2),
        name="swa",
    )(q, k, v, k0, v0, bias3, sink3)


def _ssd_kernel(xbc_ref, zg_ref, dt_ref, pre_ref, cw_ref, cb_ref, aexp_ref, dexp_ref,
                e_ref, ng_ref, s0_ref, y_ref, sfin_ref, clast_ref, st_scr, hist_scr, seg_scr, dte_scr,
                *, n_chunks):
    t = pl.program_id(1)
    tb = xbc_ref.shape[1]
    inner = zg_ref.shape[2]
    n_heads = inner // SSD_HEAD_DIM
    hpg = n_heads // SSD_GROUPS
    gn = SSD_GROUPS * SSD_STATE
    n_hs = n_heads * CHUNK

    @pl.when(t == 0)
    def _():
        st_scr[...] = s0_ref[0]
        hist_scr[...] = pre_ref[0]

    _to_segments(seg_scr, xbc_ref[0])
    conv, hist_scr[...] = _causal_conv_interleaved(_load_interleaved(seg_scr, tb), hist_scr[...],
                                                   cw_ref[...], cb_ref[...])
    _store_interleaved(seg_scr, _silu(conv))
    dte_scr[...] = _split_dot(dt_ref[0], e_ref[...])

    row_hs = lax.broadcasted_iota(jnp.int32, (n_hs, 1), 0)
    own_group = row_hs // (hpg * CHUNK) == lax.broadcasted_iota(jnp.int32, (1, gn), 1) // SSD_STATE
    own_head = row_hs // CHUNK == lax.broadcasted_iota(jnp.int32, (1, inner), 1) // SSD_HEAD_DIM
    state_group = (lax.broadcasted_iota(jnp.int32, (gn, 1), 0) // SSD_STATE
                   == lax.broadcasted_iota(jnp.int32, (1, inner), 1) // (hpg * SSD_HEAD_DIM))
    first_group = lax.broadcasted_iota(jnp.int32, (1, inner), 1) < hpg * SSD_HEAD_DIM
    step_row = lax.broadcasted_iota(jnp.int32, (CHUNK, 1), 0)
    step_lane = lax.broadcasted_iota(jnp.int32, (1, n_hs), 1) % CHUNK
    zero = jnp.zeros((), BF16)
    aexp = aexp_ref[...]

    for c in range(n_chunks):
        rows = slice(c * CHUNK, (c + 1) * CHUNK)
        xc = _rows_from_segments(seg_scr, c * CHUNK, CHUNK, tb)
        xs = xc[:, 0:inner]
        bm = xc[:, inner:inner + gn]
        cm = xc[:, inner + gn:inner + 2 * gn].astype(BF16)
        dte = dte_scr[rows, :]
        cum = _cumsum_rows(dte * aexp)
        cum_last = cum[CHUNK - 1:CHUNK, :]
        cum_at_step = jnp.sum(jnp.where(step_row == step_lane, cum, 0.0), axis=0, keepdims=True)
        decay = jnp.exp(jnp.where(step_row >= step_lane, cum - cum_at_step, NEG_INF))
        bm_bd = jnp.where(own_group, jnp.concatenate([bm.astype(BF16)] * n_heads, axis=0), zero)
        cb = lax.dot_general(cm, bm_bd, _NT, preferred_element_type=F32)
        xdt = (xs * dte).astype(BF16)
        xdt_bd = jnp.where(own_head, jnp.concatenate([xdt] * n_heads, axis=0), zero)
        st = st_scr[...]
        st_bd = jnp.where(state_group, jnp.concatenate([st.astype(BF16)] * SSD_GROUPS, axis=0), zero)
        y = (jnp.dot((cb * decay).astype(BF16), xdt_bd, preferred_element_type=F32)
             + jnp.exp(cum) * jnp.dot(cm, st_bd, preferred_element_type=F32))
        xw = (xs * (jnp.exp(cum_last - cum) * dte)).astype(BF16)
        upd = jnp.dot(bm.T.astype(BF16), xw, preferred_element_type=F32)
        st_scr[...] = st * jnp.exp(cum_last) + jnp.where(first_group, upd[0:SSD_STATE], upd[SSD_STATE:])
        yd = (y + dexp_ref[...] * xs) * _silu(zg_ref[0, rows, :])
        ms = jnp.mean(yd * yd, axis=-1, keepdims=True)
        y_ref[0, rows, :] = (yd * lax.rsqrt(ms + RMS_EPS) * ng_ref[...]).astype(y_ref.dtype)

    @pl.when(t == pl.num_programs(1) - 1)
    def _():
        sfin_ref[0] = st_scr[...]
        clast_ref[0] = hist_scr[...]


def _ssd(xbc, zg, dt, conv_prefix, conv_w, conv_b, a_log, d_skip, norm_g, s0, tb):
    b, t, nxbc = xbc.shape
    inner = zg.shape[2]
    nh = a_log.shape[0]
    width = conv_w.shape[0]
    assert SSD_GROUPS == 2 and SSD_HEAD_DIM == CHUNK
    pre8 = jnp.concatenate([jnp.zeros((b, SUBLANES - (width - 1), nxbc), F32), conv_prefix], axis=1)
    a = -jnp.exp(a_log)
    aexp = jnp.repeat(a, SSD_HEAD_DIM).reshape(1, inner)
    dexp = jnp.repeat(d_skip, SSD_HEAD_DIM).reshape(1, inner)
    expand = (np.arange(LANES)[:, None] == np.arange(inner)[None, :] // SSD_HEAD_DIM)
    expand = jnp.asarray(expand, BF16)
    s0t = jnp.transpose(s0, (0, 3, 1, 2)).reshape(b, SSD_STATE, inner)
    ng = norm_g.reshape(1, inner)
    cb2 = conv_b.reshape(1, nxbc)
    tile = lambda n: pl.BlockSpec((1, tb, n), lambda i, j: (i, j, 0))
    const = lambda arr: pl.BlockSpec(arr.shape, lambda i, j: (0,) * arr.ndim)
    bspec = lambda arr: pl.BlockSpec((1,) + arr.shape[1:], lambda i, j: (i,) + (0,) * (arr.ndim - 1))
    y, sfin, clast = pl.pallas_call(
        functools.partial(_ssd_kernel, n_chunks=tb // CHUNK),
        grid=(b, t // tb),
        in_specs=[tile(nxbc), tile(inner), tile(LANES),
                  bspec(pre8), const(conv_w), const(cb2), const(aexp), const(dexp),
                  const(expand), const(ng), bspec(s0t)],
        out_specs=[tile(inner), bspec(s0t), bspec(pre8)],
        out_shape=[jax.ShapeDtypeStruct((b, t, inner), BF16),
                   jax.ShapeDtypeStruct(s0t.shape, F32),
                   jax.ShapeDtypeStruct(pre8.shape, F32)],
        scratch_shapes=[pltpu.VMEM((SSD_STATE, inner), F32), pltpu.VMEM((SUBLANES, nxbc), F32),
                        pltpu.VMEM((nxbc // LANES, SUBLANES * _segment_pitch(tb), LANES), F32),
                        pltpu.VMEM((tb, inner), F32)],
        compiler_params=_params(2),
        name="ssd",
    )(xbc, zg, dt, pre8, conv_w, cb2, aexp, dexp, expand, ng, s0t)
    sfin = jnp.transpose(sfin.reshape(b, SSD_STATE, nh, SSD_HEAD_DIM), (0, 2, 3, 1))
    return y, sfin, clast[:, SUBLANES - (width - 1):]


def _trunk(x, mods, P, st, sample):
    b, t, d = x.shape
    tb = min(MAX_TILE, t)
    tbm = min(MIXER_TILE, t)
    depth = P['w_mod'].shape[0]
    new = {name: [] for name in ('s5_re', 's5_im', 'gla', 'swa_k', 'swa_v', 'ssd', 'ssd_conv', 'ffn_conv')}
    for layer in range(depth):
        i = layer // 2
        mod = mods[layer].reshape(b, 6, d)
        ffn = (P['norm2_g'][layer], P['ffn_w_up'][layer], P['ffn_conv_w'][layer], P['ffn_conv_b'][layer],
               P['ffn_w_down'][layer], st['ffn_conv'][layer], tb)
        if layer % 2 == 0:
            u, qk, v, r, gate = _ev_in(x, mod, P['norm1_g'][layer], P['ev_w_in'][i], P['gla_w_gate2'][i],
                                       P['gla_b_gate'][i], tbm)
            mats = _s5_matrices(P['s5_a_re'][i], P['s5_a_im'][i], P['s5_log_dt'][i], P['s5_b_re'][i],
                                P['s5_b_im'][i], P['s5_c_re'][i], P['s5_c_im'][i])
            ya, sr, si = _s5(u, mats, P['s5_d'][i], st['s5_re'][i], st['s5_im'][i])
            ob, sg = _gla(qk, v, gate, r, st['gla'][i], P['gla_norm_g'][i], tbm)
            new['s5_re'].append(sr)
            new['s5_im'].append(si)
            new['gla'].append(sg)
            x, fp = _out_ffn(x, mod, ya, ob, P['ev_w_out'][i], *ffn,
                             s5_extra=(P['s5_w_glu'][i], P['s5_b_glu'][i]))
        else:
            nq = P['swa_sink'].shape[1] * HEAD_DIM
            nkv = SWA_KV_HEADS * HEAD_DIM
            inner = P['ssd_norm_g'].shape[1]
            nxbc = P['ssd_conv_w'].shape[2]
            nh = P['ssd_a_log'].shape[1]
            q, k, v, zg, xbc, dt = _od_in(x, mod, P['norm1_g'][layer], P['od_w_in'][i], P['swa_q_norm'][i],
                                               P['swa_k_norm'][i], P['ssd_dt_bias'][i],
                                               (nq, nkv, inner, nxbc, nh), tbm)
            bias = _rel_bias(P['t5_bias'])
            if sample:
                k0 = st['swa_k'][i].reshape(b, WINDOW, nkv)
                v0 = st['swa_v'][i].reshape(b, WINDOW, nkv)
            else:
                k0 = v0 = jnp.zeros((b, WINDOW, nkv), F32)
            oc = _swa(q, k, v, k0, v0, bias, P['swa_sink'][i], tbm, mask_start=not sample)
            yd, ss, sc = _ssd(xbc, zg, dt, st['ssd_conv'][i], P['ssd_conv_w'][i], P['ssd_conv_b'][i],
                              P['ssd_a_log'][i], P['ssd_d'][i], P['ssd_norm_g'][i], st['ssd'][i], tbm)
            keep = slice(None) if sample else slice(t - WINDOW, t)
            new['swa_k'].append(k[:, keep].reshape(b, -1, SWA_KV_HEADS, HEAD_DIM))
            new['swa_v'].append(v[:, keep].reshape(b, -1, SWA_KV_HEADS, HEAD_DIM))
            new['ssd'].append(ss)
            new['ssd_conv'].append(sc)
            x, fp = _out_ffn(x, mod, oc, yd, P['od_w_out'][i], *ffn)
        new['ffn_conv'].append(fp)
    return x, {name: jnp.stack(vals) for name, vals in new.items()}


def kernel(x_prompt, x_sample, state_s5_re, state_s5_im, state_gla, cache_swa_k, cache_swa_v, state_ssd, state_ssd_conv, state_ffn_conv, c_prompt, c_sample, t5_bias, norm1_g, norm2_g, w_mod, b_mod, ffn_w_up, ffn_conv_w, ffn_conv_b, ffn_w_down, ev_w_in, ev_w_out, s5_a_re, s5_a_im, s5_log_dt, s5_b_re, s5_b_im, s5_c_re, s5_c_im, s5_d, s5_w_glu, s5_b_glu, gla_w_gate2, gla_b_gate, gla_norm_g, od_w_in, od_w_out, swa_q_norm, swa_k_norm, swa_sink, ssd_conv_w, ssd_conv_b, ssd_dt_bias, ssd_a_log, ssd_d, ssd_norm_g):
    P = dict(t5_bias=t5_bias, norm1_g=norm1_g, norm2_g=norm2_g, w_mod=w_mod, b_mod=b_mod,
             ffn_w_up=ffn_w_up, ffn_conv_w=ffn_conv_w, ffn_conv_b=ffn_conv_b, ffn_w_down=ffn_w_down,
             ev_w_in=ev_w_in, ev_w_out=ev_w_out, s5_a_re=s5_a_re, s5_a_im=s5_a_im, s5_log_dt=s5_log_dt,
             s5_b_re=s5_b_re, s5_b_im=s5_b_im, s5_c_re=s5_c_re, s5_c_im=s5_c_im, s5_d=s5_d,
             s5_w_glu=s5_w_glu, s5_b_glu=s5_b_glu, gla_w_gate2=gla_w_gate2, gla_b_gate=gla_b_gate,
             gla_norm_g=gla_norm_g, od_w_in=od_w_in, od_w_out=od_w_out, swa_q_norm=swa_q_norm,
             swa_k_norm=swa_k_norm, swa_sink=swa_sink, ssd_conv_w=ssd_conv_w, ssd_conv_b=ssd_conv_b,
             ssd_dt_bias=ssd_dt_bias, ssd_a_log=ssd_a_log, ssd_d=ssd_d, ssd_norm_g=ssd_norm_g)
    bp = x_prompt.shape[0]
    n_even, n_odd = state_s5_re.shape[0], state_ssd.shape[0]
    depth = w_mod.shape[0]
    zeros_like_b = lambda a: jnp.zeros((a.shape[0], bp) + a.shape[2:], F32)
    zero_st = dict(s5_re=zeros_like_b(state_s5_re), s5_im=zeros_like_b(state_s5_im), gla=zeros_like_b(state_gla),
                   ssd=zeros_like_b(state_ssd), ssd_conv=zeros_like_b(state_ssd_conv),
                   ffn_conv=zeros_like_b(state_ffn_conv))
    sample_st = dict(s5_re=state_s5_re, s5_im=state_s5_im, gla=state_gla, swa_k=cache_swa_k,
                     swa_v=cache_swa_v, ssd=state_ssd, ssd_conv=state_ssd_conv, ffn_conv=state_ffn_conv)
    mods = _modulation(jnp.concatenate([c_prompt, c_sample], axis=0), w_mod, b_mod)
    y_prompt, stp = _trunk(x_prompt, mods[:, :bp], P, zero_st, False)
    y_sample, sts = _trunk(x_sample, mods[:, bp:], P, sample_st, True)
    names = ('s5_re', 's5_im', 'gla', 'swa_k', 'swa_v', 'ssd', 'ssd_conv', 'ffn_conv')
    return (y_prompt, y_sample) + tuple(stp[n] for n in names) + tuple(sts[n] for n in names)
```

```python
import functools
import math

import jax
import jax.numpy as jnp
import numpy as np
from jax import lax
from jax.experimental import pallas as pl
from jax.experimental.pallas import tpu as pltpu

F32 = jnp.float32
BF16 = jnp.bfloat16

CHUNK = 64
WINDOW = 128
S5_GROUP = 16
S5_SUB = 8
S5_MAX_SUBS = 64
GLA_HEADS = 4
GLA_GATE_NORM = 16.0
HEAD_DIM = 64
SWA_KV_HEADS = 2
SWA_KEYS = 256
SSD_HEAD_DIM = 64
SSD_STATE = 128
SSD_GROUPS = 2
T5_BUCKETS = 32
T5_MAX_DIST = 128
RMS_EPS = 1e-6
NEG_INF = -1e30
LANES = 128
SUBLANES = 8
assert S5_SUB == SUBLANES and S5_GROUP * SUBLANES == LANES
MAX_TILE = 512
MIXER_TILE = 1024
CHUNK_UNROLL = 8
FFN_COL_BLOCK = 256
VMEM_LIMIT = 56 * 1024 * 1024

_NT = (((1,), (1,)), ((), ()))
_HI = lax.Precision.HIGHEST


def _params(n_axes=2):
    sem = ("parallel",) + ("arbitrary",) * (n_axes - 1)
    return pltpu.CompilerParams(dimension_semantics=sem, vmem_limit_bytes=VMEM_LIMIT)


def _bdot(a, b):
    return jnp.dot(a.astype(BF16), b.astype(BF16), preferred_element_type=F32)


def _split_dot(x, c, parts=3):
    pieces = []
    rest = x
    for _ in range(parts):
        piece = rest.astype(BF16)
        pieces.append(piece)
        rest = rest - piece.astype(F32)
    n = x.shape[0]
    d = jnp.dot(jnp.concatenate(pieces, axis=0), c, preferred_element_type=F32)
    out = d[0:n]
    for i in range(1, parts):
        out = out + d[i * n:(i + 1) * n]
    return out


def _silu(x):
    return x * jax.nn.sigmoid(x)


def _softplus(x):
    return jnp.maximum(x, 0.0) + jnp.log1p(jnp.exp(-jnp.abs(x)))


def _log_sigmoid(x):
    return jnp.minimum(x, 0.0) - jnp.log1p(jnp.exp(-jnp.abs(x)))


def _norm_mod(x, g, scale, shift):
    ms = jnp.mean(x * x, axis=-1, keepdims=True)
    return (x * lax.rsqrt(ms + RMS_EPS)) * (g * (1.0 + scale)) + shift


def _cumsum_rows(x):
    n, m = x.shape
    tiles = x.reshape(n // SUBLANES, SUBLANES, m)
    sub = lax.broadcasted_iota(jnp.int32, (1, SUBLANES, 1), 1)
    d = 1
    while d < SUBLANES:
        tiles = tiles + jnp.where(sub >= d, pltpu.roll(tiles, d, axis=1), 0.0)
        d *= 2
    out = [tiles[0]]
    for i in range(1, n // SUBLANES):
        out.append(tiles[i] + out[-1][SUBLANES - 1:SUBLANES, :])
    return jnp.concatenate(out, axis=0)


def _segment_pitch(n):
    pitch = n // SUBLANES + SUBLANES
    return pitch if (pitch // SUBLANES) % 2 else pitch + SUBLANES


def _to_segments(ref, val):
    nv = val.shape[0] // SUBLANES
    pitch = ref.shape[1] // SUBLANES
    for c in range(ref.shape[0]):
        for s in range(SUBLANES):
            ref[c, s * pitch:s * pitch + nv, :] = val[s * nv:(s + 1) * nv, c * LANES:(c + 1) * LANES]


def _from_segments(ref, n):
    nv = n // SUBLANES
    pitch = ref.shape[1] // SUBLANES
    return jnp.concatenate(
        [jnp.concatenate([ref[c, s * pitch:s * pitch + nv, :] for s in range(SUBLANES)], axis=0)
         for c in range(ref.shape[0])], axis=1)


def _rows_from_segments(ref, r0, n_rows, n):
    nv = n // SUBLANES
    pitch = ref.shape[1] // SUBLANES
    spans = []
    r = r0
    while r < r0 + n_rows:
        s, off = divmod(r, nv)
        take = min(nv - off, r0 + n_rows - r)
        spans.append(slice(s * pitch + off, s * pitch + off + take))
        r += take
    return jnp.concatenate(
        [jnp.concatenate([ref[c, sp, :] for sp in spans], axis=0) for c in range(ref.shape[0])], axis=1)


def _load_interleaved(ref, n):
    pitch = ref.shape[1] // SUBLANES
    return jnp.concatenate(
        [jnp.concatenate([ref[c, pl.ds(i, SUBLANES, stride=pitch), :] for c in range(ref.shape[0])], axis=1)
         for i in range(n // SUBLANES)], axis=0)


def _store_interleaved(ref, val):
    pitch = ref.shape[1] // SUBLANES
    for i in range(val.shape[0] // SUBLANES):
        for c in range(ref.shape[0]):
            ref[c, pl.ds(i, SUBLANES, stride=pitch), :] = val[i * SUBLANES:(i + 1) * SUBLANES, c * LANES:(c + 1) * LANES]


def _causal_conv_interleaved(u, prev8, w, b):
    width = w.shape[0]
    n, m = u.shape
    nv = n // SUBLANES
    sub = lax.broadcasted_iota(jnp.int32, (SUBLANES, 1), 0)
    tiles = u.reshape(nv, SUBLANES, m)

    def back_one(a, before):
        first = jnp.where(sub == 0, before, pltpu.roll(a[nv - 1], 1, axis=0))
        return jnp.concatenate([first[None], a[:nv - 1]], axis=0)

    delayed = [tiles]
    for k in range(1, width):
        delayed.append(back_one(delayed[-1], prev8[SUBLANES - k:SUBLANES - k + 1]))
    out = b
    for j in range(width):
        out = out + delayed[width - 1 - j] * w[j:j + 1]
    hist = tiles[nv - 1]
    for k in range(2, width):
        hist = jnp.where(sub == SUBLANES - k, pltpu.roll(tiles[nv - k], SUBLANES - k + 1, axis=0), hist)
    return out.reshape(n, m), hist


def _mod_kernel(c_ref, w_ref, b_ref, o_ref):
    o_ref[0] = _bdot(_silu(c_ref[...]), w_ref[0]) + b_ref[0]


def _modulation(c, w_mod, b_mod):
    depth, d, n = w_mod.shape
    bc = c.shape[0]
    tn = n // 4
    return pl.pallas_call(
        _mod_kernel,
        grid=(depth, n // tn),
        in_specs=[pl.BlockSpec((bc, d), lambda l, j: (0, 0)),
                  pl.BlockSpec((1, d, tn), lambda l, j: (l, 0, j)),
                  pl.BlockSpec((1, 1, tn), lambda l, j: (l, 0, j))],
        out_specs=pl.BlockSpec((1, bc, tn), lambda l, j: (l, 0, j)),
        out_shape=jax.ShapeDtypeStruct((depth, bc, n), F32),
        compiler_params=_params(2),
        name="modulation",
    )(c, w_mod, b_mod.reshape(depth, 1, n))


def _ev_in_kernel(x_ref, mod_ref, g_ref, w_ref, wg2_ref, bg_ref,
                  u_ref, qk_ref, v_ref, r_ref, gate_ref):
    mod = mod_ref[0]
    hn = _norm_mod(x_ref[0], g_ref[...], mod[1:2], mod[0:1]).astype(BF16)
    dot = lambda lo, hi: jnp.dot(hn, w_ref[:, lo:hi], preferred_element_type=F32)
    gl = dot(2048, 2048 + LANES)
    gate_ref[0] = _log_sigmoid(_bdot(gl, wg2_ref[...]) + bg_ref[...]) * (1.0 / GLA_GATE_NORM)
    u = dot(0, 512)
    for c in range(u_ref.shape[0]):
        u_ref[c] = _chunk_transpose(u[:, c * LANES:(c + 1) * LANES].reshape(-1, SUBLANES, LANES))
    qk_ref[0] = dot(512, 1024)
    v_ref[0] = dot(1024, 1536).astype(BF16)
    r_ref[0] = dot(1536, 2048)


def _ev_in(x, mod, norm_g, w_in, w_gate2, b_gate, tb):
    b, t, d = x.shape
    rank = w_gate2.shape[0]
    nk = w_gate2.shape[1]
    wu, wq, wk, wv, wgl, wr = jnp.split(w_in, [512, 768, 1024, 1536, 1536 + rank], axis=1)
    w = jnp.concatenate([wu, wq, wk, wv, wr, wgl, jnp.zeros((d, LANES - rank), F32)], axis=1).astype(BF16)
    wg2 = jnp.concatenate([w_gate2, jnp.zeros((LANES - rank, nk), F32)], axis=0).astype(BF16)
    tile = lambda n: pl.BlockSpec((1, tb, n), lambda i, j: (i, j, 0))
    const = lambda a: pl.BlockSpec(a.shape, lambda i, j: (0,) * a.ndim)
    g2 = norm_g.reshape(1, d)
    bg = b_gate.reshape(1, nk)
    return pl.pallas_call(
        _ev_in_kernel,
        grid=(b, t // tb),
        in_specs=[tile(d), pl.BlockSpec((1, 6, d), lambda i, j: (i, 0, 0)), const(g2), const(w),
                  const(wg2), const(bg)],
        out_specs=[pl.BlockSpec((512 // LANES, tb // S5_SUB, None, SUBLANES, LANES), lambda i, j: (0, j, i, 0, 0)),
                   tile(512), tile(512), tile(512), tile(nk)],
        out_shape=[jax.ShapeDtypeStruct((512 // LANES, t // S5_SUB, b, SUBLANES, LANES), F32)]
        + [jax.ShapeDtypeStruct((b, t, n), dt) for n, dt in ((512, F32), (512, BF16), (512, F32), (nk, F32))],
        compiler_params=_params(2),
        name="ev_in",
    )(x, mod, g2, w, wg2, bg)


def _gla_kernel(qk_ref, v_ref, gate_ref, r_ref, s0_ref, ng_ref, o_ref, sfin_ref, st_scr, *, n_chunks):
    t = pl.program_id(1)
    dk = qk_ref.shape[2] // 2 // GLA_HEADS
    dv = v_ref.shape[2] // GLA_HEADS
    nk = GLA_HEADS * dk
    nv = GLA_HEADS * dv
    blk = (lax.broadcasted_iota(jnp.int32, (nv, nk), 0) // dv
           == lax.broadcasted_iota(jnp.int32, (nv, nk), 1) // dk)

    @pl.when(t == 0)
    def _():
        s0 = jnp.concatenate([s0_ref[0, h] for h in range(GLA_HEADS)], axis=0)
        st_scr[...] = jnp.where(blk, jnp.concatenate([s0] * GLA_HEADS, axis=1), 0.0)

    n_hs = GLA_HEADS * CHUNK
    row_head = lax.broadcasted_iota(jnp.int32, (n_hs, 1), 0) // CHUNK
    own_k = row_head == lax.broadcasted_iota(jnp.int32, (1, nk), 1) // dk
    own_v = row_head == lax.broadcasted_iota(jnp.int32, (1, nv), 1) // dv
    causal = (lax.broadcasted_iota(jnp.int32, (CHUNK, 1), 0)
              >= lax.broadcasted_iota(jnp.int32, (1, n_hs), 1) % CHUNK)
    ng = ng_ref[...]

    def chunk(c, carry):
        rows = pl.ds(pl.multiple_of(c * CHUNK, CHUNK), CHUNK)
        k = qk_ref[0, rows, nk:2 * nk]
        v_b = v_ref[0, rows, :]
        cum = _cumsum_rows(gate_ref[0, rows, :])
        cum_last = cum[CHUNK - 1:CHUNK, :]
        qe = (qk_ref[0, rows, 0:nk] * dk ** -0.5 * jnp.exp(cum)).astype(BF16)
        ke = (k * jnp.exp(-cum)).astype(BF16)
        kd = (k * jnp.exp(cum_last - cum)).astype(BF16)
        st = st_scr[...]
        ke_bd = jnp.where(own_k, jnp.concatenate([ke] * GLA_HEADS, axis=0), jnp.zeros((), BF16))
        v_bd = jnp.where(own_v, jnp.concatenate([v_b] * GLA_HEADS, axis=0), jnp.zeros((), BF16))
        att = lax.dot_general(qe, ke_bd, _NT, preferred_element_type=F32)
        att = jnp.where(causal, att, 0.0).astype(BF16)
        o = (jnp.dot(att, v_bd, preferred_element_type=F32)
             + lax.dot_general(qe, st.astype(BF16), _NT, preferred_element_type=F32))
        for h in range(GLA_HEADS):
            cols = slice(h * dv, (h + 1) * dv)
            oh = o[:, cols]
            ms = jnp.mean(oh * oh, axis=-1, keepdims=True)
            oh = oh * lax.rsqrt(ms + RMS_EPS) * ng
            o_ref[0, rows, cols] = (oh * _silu(r_ref[0, rows, cols])).astype(o_ref.dtype)
        upd = jnp.dot(v_b.astype(F32).T.astype(BF16), kd, preferred_element_type=F32)
        st_scr[...] = jnp.where(blk, st * jnp.exp(cum_last) + upd, 0.0)
        return carry

    lax.fori_loop(0, n_chunks, chunk, 0, unroll=math.gcd(n_chunks, CHUNK_UNROLL))

    @pl.when(t == pl.num_programs(1) - 1)
    def _():
        for h in range(GLA_HEADS):
            sfin_ref[0, h] = st_scr[h * dv:(h + 1) * dv, h * dk:(h + 1) * dk]


def _gla(qk, v, gate, r, s0, norm_g, tb):
    b, t, nv = v.shape
    nk = gate.shape[2]
    dk, dv = nk // GLA_HEADS, nv // GLA_HEADS
    s0t = jnp.swapaxes(s0, 2, 3)
    tile = lambda n: pl.BlockSpec((1, tb, n), lambda i, j: (i, j, 0))
    sspec = pl.BlockSpec((1, GLA_HEADS, dv, dk), lambda i, j: (i, 0, 0, 0))
    ng = norm_g.reshape(1, dv)
    o, sfin = pl.pallas_call(
        functools.partial(_gla_kernel, n_chunks=tb // CHUNK),
        grid=(b, t // tb),
        in_specs=[tile(2 * nk), tile(nv), tile(nk), tile(nv), sspec,
                  pl.BlockSpec((1, dv), lambda i, j: (0, 0))],
        out_specs=[tile(nv), sspec],
        out_shape=[jax.ShapeDtypeStruct((b, t, nv), BF16),
                   jax.ShapeDtypeStruct((b, GLA_HEADS, dv, dk), F32)],
        scratch_shapes=[pltpu.VMEM((nv, nk), F32)],
        compiler_params=_params(2),
        name="gla",
    )(qk, v, gate, r, s0t, ng)
    return o, jnp.swapaxes(sfin, 2, 3)


def _chunk_transpose(x):
    s = lax.broadcasted_iota(jnp.int32, (1, SUBLANES, LANES), 1)
    c = lax.broadcasted_iota(jnp.int32, (1, SUBLANES, LANES), 2) // S5_GROUP
    for d in (4, 2, 1):
        sb = (s & d) != 0
        cb = (c & d) != 0
        if 2 * d == SUBLANES:
            t = pltpu.roll(pltpu.roll(x, d, axis=1), S5_GROUP * d, axis=2)
        else:
            xs = jnp.where(sb, pltpu.roll(x, d, axis=1), pltpu.roll(x, SUBLANES - d, axis=1))
            t = jnp.where(cb, pltpu.roll(xs, S5_GROUP * d, axis=2), pltpu.roll(xs, LANES - S5_GROUP * d, axis=2))
        x = jnp.where(sb != cb, t, x)
    return x


def _s5_kernel(u_ref, w1_ref, n_ref, a_ref, d_ref, x0_ref, y_ref, xf_ref, mm_scr, xs_scr, st_scr, *, tk, bsz):
    j = pl.program_id(1)
    n_g = st_scr.shape[0]
    rows = tk * bsz
    of_group = lambda g: pl.ds(g, rows, stride=n_g)

    @pl.when(j == 0)
    def _():
        st_scr[...] = x0_ref[0]

    for g in range(n_g):
        ub = u_ref[of_group(g), :].astype(BF16)
        mm_scr[g] = jnp.dot(ub, w1_ref[0, g], preferred_element_type=F32)

    half = n_g // 2
    for g0 in (0, half):
        coef = [a_ref[0, g] for g in range(g0, g0 + half)]

        def step(k, carry, g0=g0, coef=coef):
            r = pl.ds(pl.multiple_of(k * bsz, bsz), bsz)
            out = []
            for i in range(half):
                x, xsw = carry[2 * i], carry[2 * i + 1]
                a1, a2, a2s = coef[i][0:1], coef[i][1:2], coef[i][2:3]
                xs_scr[g0 + i, r, :] = x
                out.append(a1 * x + a2 * xsw + mm_scr[g0 + i, r, LANES:2 * LANES])
                out.append(a1 * xsw + a2s * x + mm_scr[g0 + i, r, 2 * LANES:3 * LANES])
            return tuple(out)

        fin = lax.fori_loop(0, tk, step, tuple(st_scr[g0 + i, v] for i in range(half) for v in range(2)))
        for i in range(half):
            st_scr[g0 + i, 0] = fin[2 * i]
            st_scr[g0 + i, 1] = fin[2 * i + 1]

    for g in range(n_g):
        y = (mm_scr[g, :, 0:LANES] + jnp.dot(xs_scr[g].astype(BF16), n_ref[0, g], preferred_element_type=F32)
             + d_ref[0, g] * u_ref[of_group(g), :])
        y_ref[of_group(g), :] = jax.nn.gelu(y)

    @pl.when(j == pl.num_programs(1) - 1)
    def _():
        xf_ref[0] = st_scr[:, 0]


def _cmul(ar, ai, br, bi):
    return ar * br - ai * bi, ar * bi + ai * br


def _s5_matrices(a_re, a_im, log_dt, b_re, b_im, c_re, c_im):
    g, p = a_re.shape
    hdim = b_re.shape[-1]
    dt = jnp.exp(log_dt)[:, None]
    mag = jnp.exp(a_re * dt)
    ab_re, ab_im = mag * jnp.cos(a_im * dt), mag * jnp.sin(a_im * dt)
    den = a_re * a_re + a_im * a_im
    num_re, num_im = ab_re - 1.0, ab_im
    g_re = (num_re * a_re + num_im * a_im) / den
    g_im = (num_im * a_re - num_re * a_im) / den
    bb_re = g_re[..., None] * b_re - g_im[..., None] * b_im
    bb_im = g_re[..., None] * b_im + g_im[..., None] * b_re
    pw_re, pw_im = jnp.ones((1, g, p), F32), jnp.zeros((1, g, p), F32)
    sq_re, sq_im = ab_re, ab_im
    while pw_re.shape[0] < S5_SUB + 1:
        nr, ni = _cmul(pw_re, pw_im, sq_re, sq_im)
        pw_re, pw_im = jnp.concatenate([pw_re, nr]), jnp.concatenate([pw_im, ni])
        sq_re, sq_im = _cmul(sq_re, sq_im, sq_re, sq_im)
    pw_re, pw_im = pw_re[:S5_SUB + 1], pw_im[:S5_SUB + 1]
    ca_re, ca_im = _cmul(c_re[None], c_im[None], pw_re[:, :, None, :], pw_im[:, :, None, :])
    kern = (jnp.einsum('tghp,gpk->tghk', ca_re[:S5_SUB], bb_re, precision=_HI)
            - jnp.einsum('tghp,gpk->tghk', ca_im[:S5_SUB], bb_im, precision=_HI))
    s_in = np.arange(S5_SUB)[:, None, None]
    s_out = np.arange(S5_SUB)[None, :, None]
    lag = jnp.asarray(s_out - s_in == np.arange(S5_SUB), F32)
    m = jnp.einsum('iot,tghk->ioghk', lag, kern, precision=_HI)
    m = jnp.transpose(m, (2, 0, 4, 1, 3)).reshape(g, S5_SUB * hdim, S5_SUB * hdim)
    pr, pi = _cmul(pw_re[S5_SUB - 1::-1][..., None], pw_im[S5_SUB - 1::-1][..., None], bb_re[None], bb_im[None])
    flat = lambda a: jnp.transpose(a, (1, 0, 3, 2)).reshape(g, S5_SUB * hdim, p)
    nm = jnp.concatenate([ca_re[1:], -ca_im[1:]], axis=3)
    nm = jnp.transpose(nm, (1, 3, 0, 2)).reshape(g, 2 * p, S5_SUB * hdim)
    w1 = jnp.concatenate([m, flat(pr), flat(pi), flat(pi), flat(pr)], axis=2)
    a_r, a_i = pw_re[S5_SUB], pw_im[S5_SUB]
    coef = jnp.stack([jnp.concatenate([a_r, a_r], axis=1), jnp.concatenate([-a_i, a_i], axis=1),
                      jnp.concatenate([a_i, -a_i], axis=1)], axis=1)
    return w1.astype(BF16), nm.astype(BF16), coef


def _s5(u2, mats, d_skip, x0_re, x0_im):
    w1, nm, coef = mats
    n_ct, n_sub, b, n_g, lanes = u2.shape
    g, p2 = nm.shape[0], nm.shape[1]
    hdim = lanes // S5_SUB
    tk = min(n_sub, S5_MAX_SUBS)
    x0 = jnp.transpose(jnp.concatenate([x0_re, x0_im], axis=2), (1, 0, 2))
    x0 = jnp.stack([x0, jnp.roll(x0, p2 // 2, axis=2)], axis=1)
    dt = jnp.tile(d_skip.reshape(g, 1, hdim), (1, 1, S5_SUB))
    by_tile = lambda a: a.reshape((n_ct, n_g) + a.shape[1:])
    pspec = lambda a: pl.BlockSpec((1,) + a.shape[1:], lambda c, j: (c,) + (0,) * (a.ndim - 1))
    rows = tk * b
    uspec = pl.BlockSpec((None, rows * n_g, lanes), lambda c, j: (c, j, 0))
    params = [by_tile(a) for a in (w1, nm, coef, dt, x0)]
    y, xf = pl.pallas_call(
        functools.partial(_s5_kernel, tk=tk, bsz=b),
        grid=(n_ct, n_sub // tk),
        in_specs=[uspec] + [pspec(a) for a in params],
        out_specs=[uspec, pl.BlockSpec((1, n_g, b, p2), lambda c, j: (c, 0, 0, 0))],
        out_shape=[jax.ShapeDtypeStruct((n_ct, n_sub * b * n_g, lanes), F32),
                   jax.ShapeDtypeStruct((n_ct, n_g, b, p2), F32)],
        scratch_shapes=[pltpu.VMEM((n_g, rows, 3 * lanes), F32), pltpu.VMEM((n_g, rows, lanes), F32),
                        pltpu.VMEM((n_g, 2, b, p2), F32)],
        compiler_params=_params(2),
        name="s5",
    )(u2.reshape(n_ct, n_sub * b * n_g, lanes), *params)
    xf = jnp.transpose(xf.reshape(g, b, p2), (1, 0, 2))
    return y.reshape(u2.shape), xf[:, :, :p2 // 2], xf[:, :, p2 // 2:]


def _out_ffn_kernel(*refs, with_s5, cb):
    if with_s5:
        (x_ref, mod_ref, m1_ref, m2_ref, wglu_ref, bglu_ref, wout_ref, n2g_ref,
         wup_ref, cw_ref, cbias_ref, wd_ref, pre_ref, xo_ref, ulast_ref, carry_scr, h_scr, row_scr) = refs
    else:
        (x_ref, mod_ref, m1_ref, m2_ref, wout_ref, n2g_ref,
         wup_ref, cw_ref, cbias_ref, wd_ref, pre_ref, xo_ref, ulast_ref, carry_scr, h_scr, row_scr) = refs
    t = pl.program_id(1)

    @pl.when(t == 0)
    def _():
        carry_scr[...] = pre_ref[0]

    mod = mod_ref[0]
    half = m2_ref.shape[2]
    out2 = jnp.dot(m2_ref[0], wout_ref[half:, :], preferred_element_type=F32)
    if with_s5:
        tiles = []
        glu = bglu_ref[...]
        for c in range(m1_ref.shape[0]):
            tiles.append(_chunk_transpose(m1_ref[c]).reshape(-1, LANES))
            glu = glu + jnp.dot(tiles[-1].astype(BF16), wglu_ref[c * LANES:(c + 1) * LANES, :],
                                preferred_element_type=F32)
        m1 = (jnp.concatenate(tiles, axis=1) * jax.nn.sigmoid(glu)).astype(BF16)
    else:
        m1 = m1_ref[0]
    out = jnp.dot(m1, wout_ref[0:half, :], preferred_element_type=F32) + out2
    x1 = x_ref[0] + mod[2:3] * out
    tb = x1.shape[0]
    _to_segments(row_scr, _norm_mod(x1, n2g_ref[...], mod[4:5], mod[3:4]))
    hn = _load_interleaved(row_scr, tb).astype(BF16)
    dff = wd_ref.shape[0]
    n_cb = dff // cb
    a_cols = lambda j: slice(j * cb, (j + 1) * cb)
    g_cols = lambda j: slice(dff + j * cb, dff + (j + 1) * cb)

    def up(j):
        return (jnp.dot(hn, wup_ref[:, a_cols(j)], preferred_element_type=F32),
                jnp.dot(hn, wup_ref[:, g_cols(j)], preferred_element_type=F32))

    def conv(u, cols):
        out, carry_scr[:, cols] = _causal_conv_interleaved(u, carry_scr[:, cols], cw_ref[:, cols], cbias_ref[:, cols])
        return out

    nxt = up(0)
    for j in range(n_cb):
        ua, ug = nxt
        if j + 1 < n_cb:
            nxt = up(j + 1)
        h_scr[:, a_cols(j)] = (_silu(conv(ua, a_cols(j))) * conv(ug, g_cols(j))).astype(BF16)
    _store_interleaved(row_scr, jnp.dot(h_scr[...], wd_ref[...], preferred_element_type=F32))
    xo_ref[0] = x1 + mod[5:6] * _from_segments(row_scr, tb)

    @pl.when(t == pl.num_programs(1) - 1)
    def _():
        ulast_ref[0] = carry_scr[...]


def _out_ffn(x, mod, m1, m2, w_out, norm2_g, w_up, conv_w, conv_b, w_down, prefix, tb, s5_extra=None):
    b, t, d = x.shape
    dff = w_down.shape[0]
    assert dff % FFN_COL_BLOCK == 0
    width = conv_w.shape[0]
    half = m2.shape[2]
    w_up_b = w_up.astype(BF16)
    cbias = conv_b.reshape(1, 2 * dff)
    wd = w_down.astype(BF16)
    pre8 = jnp.concatenate([jnp.zeros((b, SUBLANES - (width - 1), 2 * dff), F32), prefix], axis=1)
    tile = lambda n: pl.BlockSpec((1, tb, n), lambda i, j: (i, j, 0))
    const = lambda a: pl.BlockSpec(a.shape, lambda i, j: (0,) * a.ndim, pipeline_mode=pl.Buffered(1))
    bspec = lambda a: pl.BlockSpec((1,) + a.shape[1:], lambda i, j: (i,) + (0,) * (a.ndim - 1))
    wo = w_out.astype(BF16)
    n2g = norm2_g.reshape(1, d)
    args = [x, mod]
    specs = [tile(d), bspec(mod)]
    if s5_extra is not None:
        w_glu, b_glu = s5_extra
        w_glu, b_glu = w_glu.astype(BF16), b_glu.reshape(1, half)
        args += [m1, m2, w_glu, b_glu]
        specs += [pl.BlockSpec((m1.shape[0], tb // S5_SUB, None) + m1.shape[3:], lambda i, j: (0, j, i, 0, 0)),
                  tile(m2.shape[2]), const(w_glu), const(b_glu)]
    else:
        args += [m1, m2]
        specs += [tile(half), tile(m2.shape[2])]
    weights = [wo, n2g, w_up_b, conv_w, cbias, wd]
    args += weights + [pre8]
    specs += [const(a) for a in weights] + [bspec(pre8)]
    xo, ulast = pl.pallas_call(
        functools.partial(_out_ffn_kernel, with_s5=s5_extra is not None, cb=FFN_COL_BLOCK),
        grid=(b, t // tb),
        in_specs=specs,
        out_specs=[tile(d), bspec(pre8)],
        out_shape=[jax.ShapeDtypeStruct((b, t, d), F32), jax.ShapeDtypeStruct(pre8.shape, F32)],
        scratch_shapes=[pltpu.VMEM(pre8.shape[1:], F32), pltpu.VMEM((tb, dff), BF16),
                        pltpu.VMEM((d // LANES, SUBLANES * _segment_pitch(tb), LANES), F32)],
        compiler_params=_params(2),
        name="out_ffn_s5" if s5_extra is not None else "out_ffn",
    )(*args)
    return xo, ulast[:, SUBLANES - (width - 1):]


def _head_rms(x, g):
    left = lax.broadcasted_iota(jnp.int32, (1, LANES), 1) < HEAD_DIM
    out = []
    for c in range(x.shape[1] // LANES):
        xt = x[:, c * LANES:(c + 1) * LANES]
        sq = xt * xt
        s_left = jnp.sum(jnp.where(left, sq, 0.0), axis=-1, keepdims=True)
        s_right = jnp.sum(jnp.where(left, 0.0, sq), axis=-1, keepdims=True)
        ms = jnp.where(left, s_left, s_right) * (1.0 / HEAD_DIM)
        out.append(xt * lax.rsqrt(ms + RMS_EPS))
    return jnp.concatenate(out, axis=1) * g


def _od_in_kernel(x_ref, mod_ref, g_ref, w_ref, qg_ref, kg_ref, dtb_ref,
                  q_ref, k_ref, v_ref, zg_ref, xbc_ref, dt_ref):
    mod = mod_ref[0]
    nq = q_ref.shape[2]
    nkv = k_ref.shape[2]
    hn = _norm_mod(x_ref[0], g_ref[...], mod[1:2], mod[0:1]).astype(BF16)
    dot = lambda lo, hi: jnp.dot(hn, w_ref[:, lo:hi], preferred_element_type=F32)
    o_zg = nq + 2 * nkv
    o_xbc = o_zg + zg_ref.shape[2]
    o_dt = o_xbc + xbc_ref.shape[2]
    dt_ref[0] = _softplus(dot(o_dt, o_dt + LANES) + dtb_ref[...])
    q_ref[0] = _head_rms(dot(0, nq), qg_ref[...]).astype(BF16)
    kv = dot(nq, nq + 2 * nkv)
    k_ref[0] = _head_rms(kv[:, 0:nkv], kg_ref[...])
    v_ref[0] = kv[:, nkv:2 * nkv]
    zg_ref[0] = dot(o_zg, o_xbc)
    xbc_ref[0] = dot(o_xbc, o_dt)


def _od_in(x, mod, norm_g, w_in, q_norm, k_norm, dt_bias, dims, tb):
    b, t, d = x.shape
    nq, nkv, nz, nxbc, nh = dims
    w = jnp.concatenate([w_in, jnp.zeros((d, LANES - nh), F32)], axis=1).astype(BF16)
    qg = jnp.tile(q_norm, nq // HEAD_DIM).reshape(1, nq)
    kg = jnp.tile(k_norm, nkv // HEAD_DIM).reshape(1, nkv)
    dtb = jnp.concatenate([dt_bias, jnp.zeros((LANES - nh,), F32)]).reshape(1, LANES)
    g2 = norm_g.reshape(1, d)
    tile = lambda n: pl.BlockSpec((1, tb, n), lambda i, j: (i, j, 0))
    const = lambda a: pl.BlockSpec(a.shape, lambda i, j: (0,) * a.ndim)
    widths = (nq, nkv, nkv, nz, nxbc, LANES)
    return pl.pallas_call(
        _od_in_kernel,
        grid=(b, t // tb),
        in_specs=[tile(d), pl.BlockSpec((1, 6, d), lambda i, j: (i, 0, 0)), const(g2), const(w),
                  const(qg), const(kg), const(dtb)],
        out_specs=[tile(n) for n in widths],
        out_shape=[jax.ShapeDtypeStruct((b, t, n), BF16 if i == 0 else F32) for i, n in enumerate(widths)],
        compiler_params=_params(2),
        name="od_in",
    )(x, mod, g2, w, qg, kg, dtb)


def _swa_kernel(q_ref, k_ref, v_ref, k0_ref, v0_ref, bias_ref, sink_ref, o_ref, kx, vx,
                *, n_blocks, nq, mask_start):
    t = pl.program_id(1)
    tb = q_ref.shape[1]
    n_kv = k_ref.shape[2] // HEAD_DIM
    rows_q = nq * CHUNK
    tail = kx.shape[1] - WINDOW - tb
    left = lax.broadcasted_iota(jnp.int32, (1, LANES), 1) < HEAD_DIM
    first_side = lax.broadcasted_iota(jnp.int32, (2 * SWA_KEYS, 1), 0) < SWA_KEYS
    side_ones = jnp.where(first_side == left, 1.0, 0.0).astype(BF16)
    col = lax.broadcasted_iota(jnp.int32, (1, SWA_KEYS), 1)

    def place(dst, rows, x):
        rolled = pltpu.roll(x, HEAD_DIM, axis=1)
        dst[0, rows, :] = jnp.where(left, x, 0.0).astype(BF16)
        dst[1, rows, :] = jnp.where(left, 0.0, rolled).astype(BF16)
        dst[2, rows, :] = jnp.where(left, rolled, 0.0).astype(BF16)
        dst[3, rows, :] = jnp.where(left, 0.0, x).astype(BF16)

    @pl.when(t == 0)
    def _():
        place(kx, slice(0, WINDOW), k0_ref[0])
        place(vx, slice(0, WINDOW), v0_ref[0])
        for i in range(2 * n_kv if tail else 0):
            kx[i, WINDOW + tb:, :] = jnp.zeros((tail, LANES), BF16)
            vx[i, WINDOW + tb:, :] = jnp.zeros((tail, LANES), BF16)

    place(kx, slice(WINDOW, WINDOW + tb), k_ref[0])
    place(vx, slice(WINDOW, WINDOW + tb), v_ref[0])

    def block(blk, carry):
        r0 = pl.multiple_of(blk * rows_q, rows_q)
        keys = pl.ds(r0, SWA_KEYS)
        if mask_start:
            valid = t * tb + r0 - WINDOW + col >= 0
        probs = []
        sinks = []
        for j in range(n_kv):
            lo = 2 * j * LANES
            qg = jnp.concatenate([q_ref[0, pl.ds(r0 + cq * CHUNK, CHUNK), lo + r * LANES:lo + (r + 1) * LANES]
                                  for cq in range(nq) for r in range(2)], axis=0).astype(BF16)
            kcat = jnp.concatenate([kx[2 * j, keys, :], kx[2 * j + 1, keys, :]], axis=0)
            s_both = lax.dot_general(qg, kcat, _NT, preferred_element_type=F32) * HEAD_DIM ** -0.5
            for side in range(2):
                s = s_both[:, side * SWA_KEYS:(side + 1) * SWA_KEYS] + bias_ref[j, side]
                if mask_start:
                    s = jnp.where(valid, s, NEG_INF)
                m = jnp.max(s, axis=-1, keepdims=True)
                probs.append(jnp.exp(s - m).astype(BF16))
                sinks.append(jnp.exp(sink_ref[j, side] - m))
        for j in range(n_kv):
            lo = 2 * j * LANES
            vcat = jnp.concatenate(
                [jnp.concatenate([vx[2 * j, keys, :], vx[2 * j + 1, keys, :]], axis=0), side_ones], axis=1)
            pv = jnp.dot(jnp.concatenate(probs[2 * j:2 * j + 2], axis=1), vcat, preferred_element_type=F32)
            den = pv[:, LANES:] + jnp.where(left, sinks[2 * j], sinks[2 * j + 1])
            out = pv[:, 0:LANES] / den
            for cq in range(nq):
                for r in range(2):
                    o_ref[0, pl.ds(r0 + cq * CHUNK, CHUNK), lo + r * LANES:lo + (r + 1) * LANES] = (
                        out[(2 * cq + r) * CHUNK:(2 * cq + r + 1) * CHUNK].astype(o_ref.dtype))
        return carry

    lax.fori_loop(0, n_blocks, block, 0, unroll=math.gcd(n_blocks, 2))
    for i in range(2 * n_kv):
        kx[i, 0:WINDOW, :] = kx[i, tb:tb + WINDOW, :]
        vx[i, 0:WINDOW, :] = vx[i, tb:tb + WINDOW, :]


def _t5_bucket(rel):
    nb = T5_BUCKETS // 2
    max_exact = nb // 2
    ret = (rel > 0).astype(jnp.int32) * nb
    n = jnp.abs(rel)
    nf = jnp.maximum(n, 1).astype(F32)
    large = max_exact + (jnp.log(nf / max_exact) / math.log(T5_MAX_DIST / max_exact)
                         * (nb - max_exact)).astype(jnp.int32)
    large = jnp.minimum(large, nb - 1)
    return ret + jnp.where(n < max_exact, n, large)


def _rel_bias(table):
    rel = (jnp.arange(WINDOW + CHUNK)[None, :] - WINDOW) - jnp.arange(CHUNK)[:, None]
    onehot = (_t5_bucket(rel)[..., None] == jnp.arange(T5_BUCKETS)).astype(F32)
    return jnp.einsum('qkb,bh->hqk', onehot, table, precision=_HI)


def _swa(q, k, v, k0, v0, bias, sink, tb, mask_start):
    b, t, nq = q.shape
    nkv = k.shape[2]
    n_kv = nkv // HEAD_DIM
    assert nkv == LANES and nq == 2 * n_kv * LANES
    by_side = lambda a: jnp.transpose(a.reshape((n_kv, 2, 2) + a.shape[1:]), (0, 2, 1) + tuple(range(3, a.ndim + 2)))
    span = WINDOW + CHUNK
    n_chunks = tb // CHUNK
    cpb = 2 if n_chunks % 2 == 0 else 1
    assert WINDOW + cpb * CHUNK <= SWA_KEYS
    bias2 = by_side(bias).reshape(n_kv, 2, 2 * CHUNK, span)
    bias3 = jnp.concatenate(
        [jnp.pad(bias2, ((0, 0), (0, 0), (0, 0), (cq * CHUNK, SWA_KEYS - span - cq * CHUNK)),
                 constant_values=NEG_INF) for cq in range(cpb)], axis=2)
    sink3 = jnp.tile(jnp.repeat(by_side(sink), CHUNK, axis=2), (1, 1, cpb))
    sink3 = jnp.broadcast_to(sink3[..., None], sink3.shape + (LANES,))
    tile = lambda n: pl.BlockSpec((1, tb, n), lambda i, j: (i, j, 0))
    wspec = pl.BlockSpec((1, WINDOW, nkv), lambda i, j: (i, 0, 0))
    const = lambda a: pl.BlockSpec(a.shape, lambda i, j: (0,) * a.ndim)
    rows = WINDOW + max(tb, WINDOW) + SWA_KEYS - WINDOW - cpb * CHUNK
    return pl.pallas_call(
        functools.partial(_swa_kernel, n_blocks=n_chunks // cpb, nq=cpb, mask_start=mask_start),
        grid=(b, t // tb),
        in_specs=[tile(nq), tile(nkv), tile(nkv), wspec, wspec, const(bias3), const(sink3)],
        out_specs=tile(nq),
        out_shape=jax.ShapeDtypeStruct((b, t, nq), BF16),
        scratch_shapes=[pltpu.VMEM((2 * n_kv, rows, LANES), BF16), pltpu.VMEM((2 * n_kv, rows, LANES), BF16)],
        compiler_params=_params(2),
        name="swa",
    )(q, k, v, k0, v0, bias3, sink3)


def _ssd_kernel(xbc_ref, zg_ref, dt_ref, pre_ref, cw_ref, cb_ref, aexp_ref, dexp_ref,
                e_ref, ng_ref, s0_ref, y_ref, sfin_ref, clast_ref, st_scr, hist_scr, seg_scr, dte_scr,
                *, n_chunks):
    t = pl.program_id(1)
    tb = xbc_ref.shape[1]
    inner = zg_ref.shape[2]
    n_heads = inner // SSD_HEAD_DIM
    hpg = n_heads // SSD_GROUPS
    gn = SSD_GROUPS * SSD_STATE
    n_hs = n_heads * CHUNK

    @pl.when(t == 0)
    def _():
        st_scr[...] = s0_ref[0]
        hist_scr[...] = pre_ref[0]

    _to_segments(seg_scr, xbc_ref[0])
    conv, hist_scr[...] = _causal_conv_interleaved(_load_interleaved(seg_scr, tb), hist_scr[...],
                                                   cw_ref[...], cb_ref[...])
    _store_interleaved(seg_scr, _silu(conv))
    dte_scr[...] = _split_dot(dt_ref[0], e_ref[...])

    row_hs = lax.broadcasted_iota(jnp.int32, (n_hs, 1), 0)
    own_group = row_hs // (hpg * CHUNK) == lax.broadcasted_iota(jnp.int32, (1, gn), 1) // SSD_STATE
    own_head = row_hs // CHUNK == lax.broadcasted_iota(jnp.int32, (1, inner), 1) // SSD_HEAD_DIM
    state_group = (lax.broadcasted_iota(jnp.int32, (gn, 1), 0) // SSD_STATE
                   == lax.broadcasted_iota(jnp.int32, (1, inner), 1) // (hpg * SSD_HEAD_DIM))
    first_group = lax.broadcasted_iota(jnp.int32, (1, inner), 1) < hpg * SSD_HEAD_DIM
    step_row = lax.broadcasted_iota(jnp.int32, (CHUNK, 1), 0)
    step_lane = lax.broadcasted_iota(jnp.int32, (1, n_hs), 1) % CHUNK
    zero = jnp.zeros((), BF16)
    aexp = aexp_ref[...]

    for c in range(n_chunks):
        rows = slice(c * CHUNK, (c + 1) * CHUNK)
        xc = _rows_from_segments(seg_scr, c * CHUNK, CHUNK, tb)
        xs = xc[:, 0:inner]
        bm = xc[:, inner:inner + gn]
        cm = xc[:, inner + gn:inner + 2 * gn].astype(BF16)
        dte = dte_scr[rows, :]
        cum = _cumsum_rows(dte * aexp)
        cum_last = cum[CHUNK - 1:CHUNK, :]
        cum_at_step = jnp.sum(jnp.where(step_row == step_lane, cum, 0.0), axis=0, keepdims=True)
        decay = jnp.exp(jnp.where(step_row >= step_lane, cum - cum_at_step, NEG_INF))
        bm_bd = jnp.where(own_group, jnp.concatenate([bm.astype(BF16)] * n_heads, axis=0), zero)
        cb = lax.dot_general(cm, bm_bd, _NT, preferred_element_type=F32)
        xdt = (xs * dte).astype(BF16)
        xdt_bd = jnp.where(own_head, jnp.concatenate([xdt] * n_heads, axis=0), zero)
        st = st_scr[...]
        st_bd = jnp.where(state_group, jnp.concatenate([st.astype(BF16)] * SSD_GROUPS, axis=0), zero)
        y = (jnp.dot((cb * decay).astype(BF16), xdt_bd, preferred_element_type=F32)
             + jnp.exp(cum) * jnp.dot(cm, st_bd, preferred_element_type=F32))
        xw = (xs * (jnp.exp(cum_last - cum) * dte)).astype(BF16)
        upd = jnp.dot(bm.T.astype(BF16), xw, preferred_element_type=F32)
        st_scr[...] = st * jnp.exp(cum_last) + jnp.where(first_group, upd[0:SSD_STATE], upd[SSD_STATE:])
        yd = (y + dexp_ref[...] * xs) * _silu(zg_ref[0, rows, :])
        ms = jnp.mean(yd * yd, axis=-1, keepdims=True)
        y_ref[0, rows, :] = (yd * lax.rsqrt(ms + RMS_EPS) * ng_ref[...]).astype(y_ref.dtype)

    @pl.when(t == pl.num_programs(1) - 1)
    def _():
        sfin_ref[0] = st_scr[...]
        clast_ref[0] = hist_scr[...]


def _ssd(xbc, zg, dt, conv_prefix, conv_w, conv_b, a_log, d_skip, norm_g, s0, tb):
    b, t, nxbc = xbc.shape
    inner = zg.shape[2]
    nh = a_log.shape[0]
    width = conv_w.shape[0]
    assert SSD_GROUPS == 2 and SSD_HEAD_DIM == CHUNK
    pre8 = jnp.concatenate([jnp.zeros((b, SUBLANES - (width - 1), nxbc), F32), conv_prefix], axis=1)
    a = -jnp.exp(a_log)
    aexp = jnp.repeat(a, SSD_HEAD_DIM).reshape(1, inner)
    dexp = jnp.repeat(d_skip, SSD_HEAD_DIM).reshape(1, inner)
    expand = (np.arange(LANES)[:, None] == np.arange(inner)[None, :] // SSD_HEAD_DIM)
    expand = jnp.asarray(expand, BF16)
    s0t = jnp.transpose(s0, (0, 3, 1, 2)).reshape(b, SSD_STATE, inner)
    ng = norm_g.reshape(1, inner)
    cb2 = conv_b.reshape(1, nxbc)
    tile = lambda n: pl.BlockSpec((1, tb, n), lambda i, j: (i, j, 0))
    const = lambda arr: pl.BlockSpec(arr.shape, lambda i, j: (0,) * arr.ndim)
    bspec = lambda arr: pl.BlockSpec((1,) + arr.shape[1:], lambda i, j: (i,) + (0,) * (arr.ndim - 1))
    y, sfin, clast = pl.pallas_call(
        functools.partial(_ssd_kernel, n_chunks=tb // CHUNK),
        grid=(b, t // tb),
        in_specs=[tile(nxbc), tile(inner), tile(LANES),
                  bspec(pre8), const(conv_w), const(cb2), const(aexp), const(dexp),
                  const(expand), const(ng), bspec(s0t)],
        out_specs=[tile(inner), bspec(s0t), bspec(pre8)],
        out_shape=[jax.ShapeDtypeStruct((b, t, inner), BF16),
                   jax.ShapeDtypeStruct(s0t.shape, F32),
                   jax.ShapeDtypeStruct(pre8.shape, F32)],
        scratch_shapes=[pltpu.VMEM((SSD_STATE, inner), F32), pltpu.VMEM((SUBLANES, nxbc), F32),
                        pltpu.VMEM((nxbc // LANES, SUBLANES * _segment_pitch(tb), LANES), F32),
                        pltpu.VMEM((tb, inner), F32)],
        compiler_params=_params(2),
        name="ssd",
    )(xbc, zg, dt, pre8, conv_w, cb2, aexp, dexp, expand, ng, s0t)
    sfin = jnp.transpose(sfin.reshape(b, SSD_STATE, nh, SSD_HEAD_DIM), (0, 2, 3, 1))
    return y, sfin, clast[:, SUBLANES - (width - 1):]


def _trunk(x, mods, P, st, sample):
    b, t, d = x.shape
    tb = min(MAX_TILE, t)
    tbm = min(MIXER_TILE, t)
    depth = P['w_mod'].shape[0]
    new = {name: [] for name in ('s5_re', 's5_im', 'gla', 'swa_k', 'swa_v', 'ssd', 'ssd_conv', 'ffn_conv')}
    for layer in range(depth):
        i = layer // 2
        mod = mods[layer].reshape(b, 6, d)
        ffn = (P['norm2_g'][layer], P['ffn_w_up'][layer], P['ffn_conv_w'][layer], P['ffn_conv_b'][layer],
               P['ffn_w_down'][layer], st['ffn_conv'][layer], tb)
        if layer % 2 == 0:
            u, qk, v, r, gate = _ev_in(x, mod, P['norm1_g'][layer], P['ev_w_in'][i], P['gla_w_gate2'][i],
                                       P['gla_b_gate'][i], tbm)
            mats = _s5_matrices(P['s5_a_re'][i], P['s5_a_im'][i], P['s5_log_dt'][i], P['s5_b_re'][i],
                                P['s5_b_im'][i], P['s5_c_re'][i], P['s5_c_im'][i])
            ya, sr, si = _s5(u, mats, P['s5_d'][i], st['s5_re'][i], st['s5_im'][i])
            ob, sg = _gla(qk, v, gate, r, st['gla'][i], P['gla_norm_g'][i], tb)
            new['s5_re'].append(sr)
            new['s5_im'].append(si)
            new['gla'].append(sg)
            x, fp = _out_ffn(x, mod, ya, ob, P['ev_w_out'][i], *ffn,
                             s5_extra=(P['s5_w_glu'][i], P['s5_b_glu'][i]))
        else:
            nq = P['swa_sink'].shape[1] * HEAD_DIM
            nkv = SWA_KV_HEADS * HEAD_DIM
            inner = P['ssd_norm_g'].shape[1]
            nxbc = P['ssd_conv_w'].shape[2]
            nh = P['ssd_a_log'].shape[1]
            q, k, v, zg, xbc, dt = _od_in(x, mod, P['norm1_g'][layer], P['od_w_in'][i], P['swa_q_norm'][i],
                                               P['swa_k_norm'][i], P['ssd_dt_bias'][i],
                                               (nq, nkv, inner, nxbc, nh), tbm)
            bias = _rel_bias(P['t5_bias'])
            if sample:
                k0 = st['swa_k'][i].reshape(b, WINDOW, nkv)
                v0 = st['swa_v'][i].reshape(b, WINDOW, nkv)
            else:
                k0 = v0 = jnp.zeros((b, WINDOW, nkv), F32)
            oc = _swa(q, k, v, k0, v0, bias, P['swa_sink'][i], tbm, mask_start=not sample)
            yd, ss, sc = _ssd(xbc, zg, dt, st['ssd_conv'][i], P['ssd_conv_w'][i], P['ssd_conv_b'][i],
                              P['ssd_a_log'][i], P['ssd_d'][i], P['ssd_norm_g'][i], st['ssd'][i], tbm)
            keep = slice(None) if sample else slice(t - WINDOW, t)
            new['swa_k'].append(k[:, keep].reshape(b, -1, SWA_KV_HEADS, HEAD_DIM))
            new['swa_v'].append(v[:, keep].reshape(b, -1, SWA_KV_HEADS, HEAD_DIM))
            new['ssd'].append(ss)
            new['ssd_conv'].append(sc)
            x, fp = _out_ffn(x, mod, oc, yd, P['od_w_out'][i], *ffn)
        new['ffn_conv'].append(fp)
    return x, {name: jnp.stack(vals) for name, vals in new.items()}


def kernel(x_prompt, x_sample, state_s5_re, state_s5_im, state_gla, cache_swa_k, cache_swa_v, state_ssd, state_ssd_conv, state_ffn_conv, c_prompt, c_sample, t5_bias, norm1_g, norm2_g, w_mod, b_mod, ffn_w_up, ffn_conv_w, ffn_conv_b, ffn_w_down, ev_w_in, ev_w_out, s5_a_re, s5_a_im, s5_log_dt, s5_b_re, s5_b_im, s5_c_re, s5_c_im, s5_d, s5_w_glu, s5_b_glu, gla_w_gate2, gla_b_gate, gla_norm_g, od_w_in, od_w_out, swa_q_norm, swa_k_norm, swa_sink, ssd_conv_w, ssd_conv_b, ssd_dt_bias, ssd_a_log, ssd_d, ssd_norm_g):
    P = dict(t5_bias=t5_bias, norm1_g=norm1_g, norm2_g=norm2_g, w_mod=w_mod, b_mod=b_mod,
             ffn_w_up=ffn_w_up, ffn_conv_w=ffn_conv_w, ffn_conv_b=ffn_conv_b, ffn_w_down=ffn_w_down,
             ev_w_in=ev_w_in, ev_w_out=ev_w_out, s5_a_re=s5_a_re, s5_a_im=s5_a_im, s5_log_dt=s5_log_dt,
             s5_b_re=s5_b_re, s5_b_im=s5_b_im, s5_c_re=s5_c_re, s5_c_im=s5_c_im, s5_d=s5_d,
             s5_w_glu=s5_w_glu, s5_b_glu=s5_b_glu, gla_w_gate2=gla_w_gate2, gla_b_gate=gla_b_gate,
             gla_norm_g=gla_norm_g, od_w_in=od_w_in, od_w_out=od_w_out, swa_q_norm=swa_q_norm,
             swa_k_norm=swa_k_norm, swa_sink=swa_sink, ssd_conv_w=ssd_conv_w, ssd_conv_b=ssd_conv_b,
             ssd_dt_bias=ssd_dt_bias, ssd_a_log=ssd_a_log, ssd_d=ssd_d, ssd_norm_g=ssd_norm_g)
    bp = x_prompt.shape[0]
    n_even, n_odd = state_s5_re.shape[0], state_ssd.shape[0]
    depth = w_mod.shape[0]
    zeros_like_b = lambda a: jnp.zeros((a.shape[0], bp) + a.shape[2:], F32)
    zero_st = dict(s5_re=zeros_like_b(state_s5_re), s5_im=zeros_like_b(state_s5_im), gla=zeros_like_b(state_gla),
                   ssd=zeros_like_b(state_ssd), ssd_conv=zeros_like_b(state_ssd_conv),
                   ffn_conv=zeros_like_b(state_ffn_conv))
    sample_st = dict(s5_re=state_s5_re, s5_im=state_s5_im, gla=state_gla, swa_k=cache_swa_k,
                     swa_v=cache_swa_v, ssd=state_ssd, ssd_conv=state_ssd_conv, ffn_conv=state_ffn_conv)
    mods = _modulation(jnp.concatenate([c_prompt, c_sample], axis=0), w_mod, b_mod)
    y_prompt, stp = _trunk(x_prompt, mods[:, :bp], P, zero_st, False)
    y_sample, sts = _trunk(x_sample, mods[:, bp:], P, sample_st, True)
    names = ('s5_re', 's5_im', 'gla', 'swa_k', 'swa_v', 'ssd', 'ssd_conv', 'ffn_conv')
    return (y_prompt, y_sample) + tuple(stp[n] for n in names) + tuple(sts[n] for n in names)
```

```python
import functools
import math

import jax
import jax.numpy as jnp
import numpy as np
from jax import lax
from jax.experimental import pallas as pl
from jax.experimental.pallas import tpu as pltpu

F32 = jnp.float32
BF16 = jnp.bfloat16

CHUNK = 64
WINDOW = 128
S5_GROUP = 16
S5_SUB = 8
S5_MAX_SUBS = 64
GLA_HEADS = 4
GLA_GATE_NORM = 16.0
HEAD_DIM = 64
SWA_KV_HEADS = 2
SWA_KEYS = 256
SSD_HEAD_DIM = 64
SSD_STATE = 128
SSD_GROUPS = 2
T5_BUCKETS = 32
T5_MAX_DIST = 128
RMS_EPS = 1e-6
NEG_INF = -1e30
LANES = 128
SUBLANES = 8
assert S5_SUB == SUBLANES and S5_GROUP * SUBLANES == LANES
MAX_TILE = 512
MIXER_TILE = 1024
CHUNK_UNROLL = 8
FFN_COL_BLOCK = 256
VMEM_LIMIT = 56 * 1024 * 1024

_NT = (((1,), (1,)), ((), ()))
_HI = lax.Precision.HIGHEST


def _params(n_axes=2):
    sem = ("parallel",) + ("arbitrary",) * (n_axes - 1)
    return pltpu.CompilerParams(dimension_semantics=sem, vmem_limit_bytes=VMEM_LIMIT)


def _bdot(a, b):
    return jnp.dot(a.astype(BF16), b.astype(BF16), preferred_element_type=F32)


def _split_dot(x, c, parts=3):
    pieces = []
    rest = x
    for _ in range(parts):
        piece = rest.astype(BF16)
        pieces.append(piece)
        rest = rest - piece.astype(F32)
    n = x.shape[0]
    d = jnp.dot(jnp.concatenate(pieces, axis=0), c, preferred_element_type=F32)
    out = d[0:n]
    for i in range(1, parts):
        out = out + d[i * n:(i + 1) * n]
    return out


def _silu(x):
    return x * jax.nn.sigmoid(x)


def _softplus(x):
    return jnp.maximum(x, 0.0) + jnp.log1p(jnp.exp(-jnp.abs(x)))


def _log_sigmoid(x):
    return jnp.minimum(x, 0.0) - jnp.log1p(jnp.exp(-jnp.abs(x)))


def _norm_mod(x, g, scale, shift):
    ms = jnp.mean(x * x, axis=-1, keepdims=True)
    return (x * lax.rsqrt(ms + RMS_EPS)) * (g * (1.0 + scale)) + shift


def _cumsum_rows(x):
    n, m = x.shape
    tiles = x.reshape(n // SUBLANES, SUBLANES, m)
    sub = lax.broadcasted_iota(jnp.int32, (1, SUBLANES, 1), 1)
    d = 1
    while d < SUBLANES:
        tiles = tiles + jnp.where(sub >= d, pltpu.roll(tiles, d, axis=1), 0.0)
        d *= 2
    out = [tiles[0]]
    for i in range(1, n // SUBLANES):
        out.append(tiles[i] + out[-1][SUBLANES - 1:SUBLANES, :])
    return jnp.concatenate(out, axis=0)


def _segment_pitch(n):
    pitch = n // SUBLANES + SUBLANES
    return pitch if (pitch // SUBLANES) % 2 else pitch + SUBLANES


def _to_segments(ref, val):
    nv = val.shape[0] // SUBLANES
    pitch = ref.shape[1] // SUBLANES
    for c in range(ref.shape[0]):
        for s in range(SUBLANES):
            ref[c, s * pitch:s * pitch + nv, :] = val[s * nv:(s + 1) * nv, c * LANES:(c + 1) * LANES]


def _from_segments(ref, n):
    nv = n // SUBLANES
    pitch = ref.shape[1] // SUBLANES
    return jnp.concatenate(
        [jnp.concatenate([ref[c, s * pitch:s * pitch + nv, :] for s in range(SUBLANES)], axis=0)
         for c in range(ref.shape[0])], axis=1)


def _rows_from_segments(ref, r0, n_rows, n):
    nv = n // SUBLANES
    pitch = ref.shape[1] // SUBLANES
    spans = []
    r = r0
    while r < r0 + n_rows:
        s, off = divmod(r, nv)
        take = min(nv - off, r0 + n_rows - r)
        spans.append(slice(s * pitch + off, s * pitch + off + take))
        r += take
    return jnp.concatenate(
        [jnp.concatenate([ref[c, sp, :] for sp in spans], axis=0) for c in range(ref.shape[0])], axis=1)


def _load_interleaved(ref, n):
    pitch = ref.shape[1] // SUBLANES
    return jnp.concatenate(
        [jnp.concatenate([ref[c, pl.ds(i, SUBLANES, stride=pitch), :] for c in range(ref.shape[0])], axis=1)
         for i in range(n // SUBLANES)], axis=0)


def _store_interleaved(ref, val):
    pitch = ref.shape[1] // SUBLANES
    for i in range(val.shape[0] // SUBLANES):
        for c in range(ref.shape[0]):
            ref[c, pl.ds(i, SUBLANES, stride=pitch), :] = val[i * SUBLANES:(i + 1) * SUBLANES, c * LANES:(c + 1) * LANES]


def _causal_conv_interleaved(u, prev8, w, b):
    width = w.shape[0]
    n, m = u.shape
    nv = n // SUBLANES
    sub = lax.broadcasted_iota(jnp.int32, (SUBLANES, 1), 0)
    tiles = u.reshape(nv, SUBLANES, m)

    def back_one(a, before):
        first = jnp.where(sub == 0, before, pltpu.roll(a[nv - 1], 1, axis=0))
        return jnp.concatenate([first[None], a[:nv - 1]], axis=0)

    delayed = [tiles]
    for k in range(1, width):
        delayed.append(back_one(delayed[-1], prev8[SUBLANES - k:SUBLANES - k + 1]))
    out = b
    for j in range(width):
        out = out + delayed[width - 1 - j] * w[j:j + 1]
    hist = tiles[nv - 1]
    for k in range(2, width):
        hist = jnp.where(sub == SUBLANES - k, pltpu.roll(tiles[nv - k], SUBLANES - k + 1, axis=0), hist)
    return out.reshape(n, m), hist


def _mod_kernel(c_ref, w_ref, b_ref, o_ref):
    o_ref[0] = _bdot(_silu(c_ref[...]), w_ref[0]) + b_ref[0]


def _modulation(c, w_mod, b_mod):
    depth, d, n = w_mod.shape
    bc = c.shape[0]
    tn = n // 4
    return pl.pallas_call(
        _mod_kernel,
        grid=(depth, n // tn),
        in_specs=[pl.BlockSpec((bc, d), lambda l, j: (0, 0)),
                  pl.BlockSpec((1, d, tn), lambda l, j: (l, 0, j)),
                  pl.BlockSpec((1, 1, tn), lambda l, j: (l, 0, j))],
        out_specs=pl.BlockSpec((1, bc, tn), lambda l, j: (l, 0, j)),
        out_shape=jax.ShapeDtypeStruct((depth, bc, n), F32),
        compiler_params=_params(2),
        name="modulation",
    )(c, w_mod, b_mod.reshape(depth, 1, n))


def _ev_in_kernel(x_ref, mod_ref, g_ref, w_ref, wg2_ref, bg_ref,
                  u_ref, qk_ref, v_ref, r_ref, gate_ref):
    mod = mod_ref[0]
    hn = _norm_mod(x_ref[0], g_ref[...], mod[1:2], mod[0:1]).astype(BF16)
    dot = lambda lo, hi: jnp.dot(hn, w_ref[:, lo:hi], preferred_element_type=F32)
    gl = dot(2048, 2048 + LANES)
    gate_ref[0] = _log_sigmoid(_bdot(gl, wg2_ref[...]) + bg_ref[...]) * (1.0 / GLA_GATE_NORM)
    u = dot(0, 512)
    for c in range(u_ref.shape[0]):
        u_ref[c] = _chunk_transpose(u[:, c * LANES:(c + 1) * LANES].reshape(-1, SUBLANES, LANES))
    qk_ref[0] = dot(512, 1024)
    v_ref[0] = dot(1024, 1536).astype(BF16)
    r_ref[0] = dot(1536, 2048)


def _ev_in(x, mod, norm_g, w_in, w_gate2, b_gate, tb):
    b, t, d = x.shape
    rank = w_gate2.shape[0]
    nk = w_gate2.shape[1]
    wu, wq, wk, wv, wgl, wr = jnp.split(w_in, [512, 768, 1024, 1536, 1536 + rank], axis=1)
    w = jnp.concatenate([wu, wq, wk, wv, wr, wgl, jnp.zeros((d, LANES - rank), F32)], axis=1).astype(BF16)
    wg2 = jnp.concatenate([w_gate2, jnp.zeros((LANES - rank, nk), F32)], axis=0).astype(BF16)
    tile = lambda n: pl.BlockSpec((1, tb, n), lambda i, j: (i, j, 0))
    const = lambda a: pl.BlockSpec(a.shape, lambda i, j: (0,) * a.ndim)
    g2 = norm_g.reshape(1, d)
    bg = b_gate.reshape(1, nk)
    return pl.pallas_call(
        _ev_in_kernel,
        grid=(b, t // tb),
        in_specs=[tile(d), pl.BlockSpec((1, 6, d), lambda i, j: (i, 0, 0)), const(g2), const(w),
                  const(wg2), const(bg)],
        out_specs=[pl.BlockSpec((512 // LANES, tb // S5_SUB, None, SUBLANES, LANES), lambda i, j: (0, j, i, 0, 0)),
                   tile(512), tile(512), tile(512), tile(nk)],
        out_shape=[jax.ShapeDtypeStruct((512 // LANES, t // S5_SUB, b, SUBLANES, LANES), F32)]
        + [jax.ShapeDtypeStruct((b, t, n), dt) for n, dt in ((512, F32), (512, BF16), (512, F32), (nk, F32))],
        compiler_params=_params(2),
        name="ev_in",
    )(x, mod, g2, w, wg2, bg)


def _gla_kernel(qk_ref, v_ref, gate_ref, r_ref, s0_ref, ng_ref, o_ref, sfin_ref, st_scr, *, n_chunks):
    t = pl.program_id(1)
    dk = qk_ref.shape[2] // 2 // GLA_HEADS
    dv = v_ref.shape[2] // GLA_HEADS
    nk = GLA_HEADS * dk
    nv = GLA_HEADS * dv
    blk = (lax.broadcasted_iota(jnp.int32, (nv, nk), 0) // dv
           == lax.broadcasted_iota(jnp.int32, (nv, nk), 1) // dk)

    @pl.when(t == 0)
    def _():
        s0 = jnp.concatenate([s0_ref[0, h] for h in range(GLA_HEADS)], axis=0)
        st_scr[...] = jnp.where(blk, jnp.concatenate([s0] * GLA_HEADS, axis=1), 0.0)

    n_hs = GLA_HEADS * CHUNK
    row_head = lax.broadcasted_iota(jnp.int32, (n_hs, 1), 0) // CHUNK
    own_k = row_head == lax.broadcasted_iota(jnp.int32, (1, nk), 1) // dk
    own_v = row_head == lax.broadcasted_iota(jnp.int32, (1, nv), 1) // dv
    causal = (lax.broadcasted_iota(jnp.int32, (CHUNK, 1), 0)
              >= lax.broadcasted_iota(jnp.int32, (1, n_hs), 1) % CHUNK)
    ng = ng_ref[...]

    def chunk(c, carry):
        rows = pl.ds(pl.multiple_of(c * CHUNK, CHUNK), CHUNK)
        k = qk_ref[0, rows, nk:2 * nk]
        v_b = v_ref[0, rows, :]
        cum = _cumsum_rows(gate_ref[0, rows, :])
        cum_last = cum[CHUNK - 1:CHUNK, :]
        qe = (qk_ref[0, rows, 0:nk] * dk ** -0.5 * jnp.exp(cum)).astype(BF16)
        ke = (k * jnp.exp(-cum)).astype(BF16)
        kd = (k * jnp.exp(cum_last - cum)).astype(BF16)
        st = st_scr[...]
        ke_bd = jnp.where(own_k, jnp.concatenate([ke] * GLA_HEADS, axis=0), jnp.zeros((), BF16))
        v_bd = jnp.where(own_v, jnp.concatenate([v_b] * GLA_HEADS, axis=0), jnp.zeros((), BF16))
        att = lax.dot_general(qe, ke_bd, _NT, preferred_element_type=F32)
        att = jnp.where(causal, att, 0.0).astype(BF16)
        o = (jnp.dot(att, v_bd, preferred_element_type=F32)
             + lax.dot_general(qe, st.astype(BF16), _NT, preferred_element_type=F32))
        for h in range(GLA_HEADS):
            cols = slice(h * dv, (h + 1) * dv)
            oh = o[:, cols]
            ms = jnp.mean(oh * oh, axis=-1, keepdims=True)
            oh = oh * lax.rsqrt(ms + RMS_EPS) * ng
            o_ref[0, rows, cols] = (oh * _silu(r_ref[0, rows, cols])).astype(o_ref.dtype)
        upd = jnp.dot(v_b.astype(F32).T.astype(BF16), kd, preferred_element_type=F32)
        st_scr[...] = jnp.where(blk, st * jnp.exp(cum_last) + upd, 0.0)
        return carry

    lax.fori_loop(0, n_chunks, chunk, 0, unroll=math.gcd(n_chunks, CHUNK_UNROLL))

    @pl.when(t == pl.num_programs(1) - 1)
    def _():
        for h in range(GLA_HEADS):
            sfin_ref[0, h] = st_scr[h * dv:(h + 1) * dv, h * dk:(h + 1) * dk]


def _gla(qk, v, gate, r, s0, norm_g, tb):
    b, t, nv = v.shape
    nk = gate.shape[2]
    dk, dv = nk // GLA_HEADS, nv // GLA_HEADS
    s0t = jnp.swapaxes(s0, 2, 3)
    tile = lambda n: pl.BlockSpec((1, tb, n), lambda i, j: (i, j, 0))
    sspec = pl.BlockSpec((1, GLA_HEADS, dv, dk), lambda i, j: (i, 0, 0, 0))
    ng = norm_g.reshape(1, dv)
    o, sfin = pl.pallas_call(
        functools.partial(_gla_kernel, n_chunks=tb // CHUNK),
        grid=(b, t // tb),
        in_specs=[tile(2 * nk), tile(nv), tile(nk), tile(nv), sspec,
                  pl.BlockSpec((1, dv), lambda i, j: (0, 0))],
        out_specs=[tile(nv), sspec],
        out_shape=[jax.ShapeDtypeStruct((b, t, nv), BF16),
                   jax.ShapeDtypeStruct((b, GLA_HEADS, dv, dk), F32)],
        scratch_shapes=[pltpu.VMEM((nv, nk), F32)],
        compiler_params=_params(2),
        name="gla",
    )(qk, v, gate, r, s0t, ng)
    return o, jnp.swapaxes(sfin, 2, 3)


def _chunk_transpose(x):
    s = lax.broadcasted_iota(jnp.int32, (1, SUBLANES, LANES), 1)
    c = lax.broadcasted_iota(jnp.int32, (1, SUBLANES, LANES), 2) // S5_GROUP
    for d in (4, 2, 1):
        sb = (s & d) != 0
        cb = (c & d) != 0
        if 2 * d == SUBLANES:
            t = pltpu.roll(pltpu.roll(x, d, axis=1), S5_GROUP * d, axis=2)
        else:
            xs = jnp.where(sb, pltpu.roll(x, d, axis=1), pltpu.roll(x, SUBLANES - d, axis=1))
            t = jnp.where(cb, pltpu.roll(xs, S5_GROUP * d, axis=2), pltpu.roll(xs, LANES - S5_GROUP * d, axis=2))
        x = jnp.where(sb != cb, t, x)
    return x


def _s5_kernel(u_ref, w1_ref, n_ref, a_ref, d_ref, x0_ref, y_ref, xf_ref, mm_scr, xs_scr, st_scr, *, tk, bsz):
    j = pl.program_id(1)
    n_g = st_scr.shape[0]
    rows = tk * bsz
    of_group = lambda g: pl.ds(g, rows, stride=n_g)

    @pl.when(j == 0)
    def _():
        st_scr[...] = x0_ref[0]

    for g in range(n_g):
        ub = u_ref[of_group(g), :].astype(BF16)
        mm_scr[g] = jnp.dot(ub, w1_ref[0, g], preferred_element_type=F32)

    half = n_g // 2
    for g0 in (0, half):
        coef = [a_ref[0, g] for g in range(g0, g0 + half)]

        def step(k, carry, g0=g0, coef=coef):
            r = pl.ds(pl.multiple_of(k * bsz, bsz), bsz)
            out = []
            for i in range(half):
                x, xsw = carry[2 * i], carry[2 * i + 1]
                a1, a2, a2s = coef[i][0:1], coef[i][1:2], coef[i][2:3]
                xs_scr[g0 + i, r, :] = x
                out.append(a1 * x + a2 * xsw + mm_scr[g0 + i, r, LANES:2 * LANES])
                out.append(a1 * xsw + a2s * x + mm_scr[g0 + i, r, 2 * LANES:3 * LANES])
            return tuple(out)

        fin = lax.fori_loop(0, tk, step, tuple(st_scr[g0 + i, v] for i in range(half) for v in range(2)))
        for i in range(half):
            st_scr[g0 + i, 0] = fin[2 * i]
            st_scr[g0 + i, 1] = fin[2 * i + 1]

    for g in range(n_g):
        y = (mm_scr[g, :, 0:LANES] + jnp.dot(xs_scr[g].astype(BF16), n_ref[0, g], preferred_element_type=F32)
             + d_ref[0, g] * u_ref[of_group(g), :])
        y_ref[of_group(g), :] = jax.nn.gelu(y)

    @pl.when(j == pl.num_programs(1) - 1)
    def _():
        xf_ref[0] = st_scr[:, 0]


def _cmul(ar, ai, br, bi):
    return ar * br - ai * bi, ar * bi + ai * br


def _s5_matrices(a_re, a_im, log_dt, b_re, b_im, c_re, c_im):
    g, p = a_re.shape
    hdim = b_re.shape[-1]
    dt = jnp.exp(log_dt)[:, None]
    mag = jnp.exp(a_re * dt)
    ab_re, ab_im = mag * jnp.cos(a_im * dt), mag * jnp.sin(a_im * dt)
    den = a_re * a_re + a_im * a_im
    num_re, num_im = ab_re - 1.0, ab_im
    g_re = (num_re * a_re + num_im * a_im) / den
    g_im = (num_im * a_re - num_re * a_im) / den
    bb_re = g_re[..., None] * b_re - g_im[..., None] * b_im
    bb_im = g_re[..., None] * b_im + g_im[..., None] * b_re
    pw_re, pw_im = jnp.ones((1, g, p), F32), jnp.zeros((1, g, p), F32)
    sq_re, sq_im = ab_re, ab_im
    while pw_re.shape[0] < S5_SUB + 1:
        nr, ni = _cmul(pw_re, pw_im, sq_re, sq_im)
        pw_re, pw_im = jnp.concatenate([pw_re, nr]), jnp.concatenate([pw_im, ni])
        sq_re, sq_im = _cmul(sq_re, sq_im, sq_re, sq_im)
    pw_re, pw_im = pw_re[:S5_SUB + 1], pw_im[:S5_SUB + 1]
    ca_re, ca_im = _cmul(c_re[None], c_im[None], pw_re[:, :, None, :], pw_im[:, :, None, :])
    kern = (jnp.einsum('tghp,gpk->tghk', ca_re[:S5_SUB], bb_re, precision=_HI)
            - jnp.einsum('tghp,gpk->tghk', ca_im[:S5_SUB], bb_im, precision=_HI))
    s_in = np.arange(S5_SUB)[:, None, None]
    s_out = np.arange(S5_SUB)[None, :, None]
    lag = jnp.asarray(s_out - s_in == np.arange(S5_SUB), F32)
    m = jnp.einsum('iot,tghk->ioghk', lag, kern, precision=_HI)
    m = jnp.transpose(m, (2, 0, 4, 1, 3)).reshape(g, S5_SUB * hdim, S5_SUB * hdim)
    pr, pi = _cmul(pw_re[S5_SUB - 1::-1][..., None], pw_im[S5_SUB - 1::-1][..., None], bb_re[None], bb_im[None])
    flat = lambda a: jnp.transpose(a, (1, 0, 3, 2)).reshape(g, S5_SUB * hdim, p)
    nm = jnp.concatenate([ca_re[1:], -ca_im[1:]], axis=3)
    nm = jnp.transpose(nm, (1, 3, 0, 2)).reshape(g, 2 * p, S5_SUB * hdim)
    w1 = jnp.concatenate([m, flat(pr), flat(pi), flat(pi), flat(pr)], axis=2)
    a_r, a_i = pw_re[S5_SUB], pw_im[S5_SUB]
    coef = jnp.stack([jnp.concatenate([a_r, a_r], axis=1), jnp.concatenate([-a_i, a_i], axis=1),
                      jnp.concatenate([a_i, -a_i], axis=1)], axis=1)
    return w1.astype(BF16), nm.astype(BF16), coef


def _s5(u2, mats, d_skip, x0_re, x0_im):
    w1, nm, coef = mats
    n_ct, n_sub, b, n_g, lanes = u2.shape
    g, p2 = nm.shape[0], nm.shape[1]
    hdim = lanes // S5_SUB
    tk = min(n_sub, S5_MAX_SUBS)
    x0 = jnp.transpose(jnp.concatenate([x0_re, x0_im], axis=2), (1, 0, 2))
    x0 = jnp.stack([x0, jnp.roll(x0, p2 // 2, axis=2)], axis=1)
    dt = jnp.tile(d_skip.reshape(g, 1, hdim), (1, 1, S5_SUB))
    by_tile = lambda a: a.reshape((n_ct, n_g) + a.shape[1:])
    pspec = lambda a: pl.BlockSpec((1,) + a.shape[1:], lambda c, j: (c,) + (0,) * (a.ndim - 1))
    rows = tk * b
    uspec = pl.BlockSpec((None, rows * n_g, lanes), lambda c, j: (c, j, 0))
    params = [by_tile(a) for a in (w1, nm, coef, dt, x0)]
    y, xf = pl.pallas_call(
        functools.partial(_s5_kernel, tk=tk, bsz=b),
        grid=(n_ct, n_sub // tk),
        in_specs=[uspec] + [pspec(a) for a in params],
        out_specs=[uspec, pl.BlockSpec((1, n_g, b, p2), lambda c, j: (c, 0, 0, 0))],
        out_shape=[jax.ShapeDtypeStruct((n_ct, n_sub * b * n_g, lanes), F32),
                   jax.ShapeDtypeStruct((n_ct, n_g, b, p2), F32)],
        scratch_shapes=[pltpu.VMEM((n_g, rows, 3 * lanes), F32), pltpu.VMEM((n_g, rows, lanes), F32),
                        pltpu.VMEM((n_g, 2, b, p2), F32)],
        compiler_params=_params(2),
        name="s5",
    )(u2.reshape(n_ct, n_sub * b * n_g, lanes), *params)
    xf = jnp.transpose(xf.reshape(g, b, p2), (1, 0, 2))
    return y.reshape(u2.shape), xf[:, :, :p2 // 2], xf[:, :, p2 // 2:]


def _out_ffn_kernel(*refs, with_s5, cb):
    if with_s5:
        (x_ref, mod_ref, m1_ref, m2_ref, wglu_ref, bglu_ref, wout_ref, n2g_ref,
         wup_ref, cw_ref, cbias_ref, wd_ref, pre_ref, xo_ref, ulast_ref, carry_scr, h_scr, row_scr) = refs
    else:
        (x_ref, mod_ref, m1_ref, m2_ref, wout_ref, n2g_ref,
         wup_ref, cw_ref, cbias_ref, wd_ref, pre_ref, xo_ref, ulast_ref, carry_scr, h_scr, row_scr) = refs
    t = pl.program_id(1)

    @pl.when(t == 0)
    def _():
        carry_scr[...] = pre_ref[0]

    mod = mod_ref[0]
    half = m2_ref.shape[2]
    out2 = jnp.dot(m2_ref[0], wout_ref[half:, :], preferred_element_type=F32)
    if with_s5:
        tiles = []
        glu = bglu_ref[...]
        for c in range(m1_ref.shape[0]):
            tiles.append(_chunk_transpose(m1_ref[c]).reshape(-1, LANES))
            glu = glu + jnp.dot(tiles[-1].astype(BF16), wglu_ref[c * LANES:(c + 1) * LANES, :],
                                preferred_element_type=F32)
        m1 = (jnp.concatenate(tiles, axis=1) * jax.nn.sigmoid(glu)).astype(BF16)
    else:
        m1 = m1_ref[0]
    out = jnp.dot(m1, wout_ref[0:half, :], preferred_element_type=F32) + out2
    x1 = x_ref[0] + mod[2:3] * out
    tb = x1.shape[0]
    _to_segments(row_scr, _norm_mod(x1, n2g_ref[...], mod[4:5], mod[3:4]))
    hn = _load_interleaved(row_scr, tb).astype(BF16)
    dff = wd_ref.shape[0]
    n_cb = dff // cb
    a_cols = lambda j: slice(j * cb, (j + 1) * cb)
    g_cols = lambda j: slice(dff + j * cb, dff + (j + 1) * cb)

    def up(j):
        return (jnp.dot(hn, wup_ref[:, a_cols(j)], preferred_element_type=F32),
                jnp.dot(hn, wup_ref[:, g_cols(j)], preferred_element_type=F32))

    def conv(u, cols):
        out, carry_scr[:, cols] = _causal_conv_interleaved(u, carry_scr[:, cols], cw_ref[:, cols], cbias_ref[:, cols])
        return out

    nxt = up(0)
    for j in range(n_cb):
        ua, ug = nxt
        if j + 1 < n_cb:
            nxt = up(j + 1)
        h_scr[:, a_cols(j)] = (_silu(conv(ua, a_cols(j))) * conv(ug, g_cols(j))).astype(BF16)
    _store_interleaved(row_scr, jnp.dot(h_scr[...], wd_ref[...], preferred_element_type=F32))
    xo_ref[0] = x1 + mod[5:6] * _from_segments(row_scr, tb)

    @pl.when(t == pl.num_programs(1) - 1)
    def _():
        ulast_ref[0] = carry_scr[...]


def _out_ffn(x, mod, m1, m2, w_out, norm2_g, w_up, conv_w, conv_b, w_down, prefix, tb, s5_extra=None):
    b, t, d = x.shape
    dff = w_down.shape[0]
    assert dff % FFN_COL_BLOCK == 0
    width = conv_w.shape[0]
    half = m2.shape[2]
    w_up_b = w_up.astype(BF16)
    cbias = conv_b.reshape(1, 2 * dff)
    wd = w_down.astype(BF16)
    pre8 = jnp.concatenate([jnp.zeros((b, SUBLANES - (width - 1), 2 * dff), F32), prefix], axis=1)
    tile = lambda n: pl.BlockSpec((1, tb, n), lambda i, j: (i, j, 0))
    const = lambda a: pl.BlockSpec(a.shape, lambda i, j: (0,) * a.ndim, pipeline_mode=pl.Buffered(1))
    bspec = lambda a: pl.BlockSpec((1,) + a.shape[1:], lambda i, j: (i,) + (0,) * (a.ndim - 1))
    wo = w_out.astype(BF16)
    n2g = norm2_g.reshape(1, d)
    args = [x, mod]
    specs = [tile(d), bspec(mod)]
    if s5_extra is not None:
        w_glu, b_glu = s5_extra
        w_glu, b_glu = w_glu.astype(BF16), b_glu.reshape(1, half)
        args += [m1, m2, w_glu, b_glu]
        specs += [pl.BlockSpec((m1.shape[0], tb // S5_SUB, None) + m1.shape[3:], lambda i, j: (0, j, i, 0, 0)),
                  tile(m2.shape[2]), const(w_glu), const(b_glu)]
    else:
        args += [m1, m2]
        specs += [tile(half), tile(m2.shape[2])]
    weights = [wo, n2g, w_up_b, conv_w, cbias, wd]
    args += weights + [pre8]
    specs += [const(a) for a in weights] + [bspec(pre8)]
    xo, ulast = pl.pallas_call(
        functools.partial(_out_ffn_kernel, with_s5=s5_extra is not None, cb=FFN_COL_BLOCK),
        grid=(b, t // tb),
        in_specs=specs,
        out_specs=[tile(d), bspec(pre8)],
        out_shape=[jax.ShapeDtypeStruct((b, t, d), F32), jax.ShapeDtypeStruct(pre8.shape, F32)],
        scratch_shapes=[pltpu.VMEM(pre8.shape[1:], F32), pltpu.VMEM((tb, dff), BF16),
                        pltpu.VMEM((d // LANES, SUBLANES * _segment_pitch(tb), LANES), F32)],
        compiler_params=_params(2),
        name="out_ffn_s5" if s5_extra is not None else "out_ffn",
    )(*args)
    return xo, ulast[:, SUBLANES - (width - 1):]


def _head_rms(x, g):
    left = lax.broadcasted_iota(jnp.int32, (1, LANES), 1) < HEAD_DIM
    out = []
    for c in range(x.shape[1] // LANES):
        xt = x[:, c * LANES:(c + 1) * LANES]
        sq = xt * xt
        s_left = jnp.sum(jnp.where(left, sq, 0.0), axis=-1, keepdims=True)
        s_right = jnp.sum(jnp.where(left, 0.0, sq), axis=-1, keepdims=True)
        ms = jnp.where(left, s_left, s_right) * (1.0 / HEAD_DIM)
        out.append(xt * lax.rsqrt(ms + RMS_EPS))
    return jnp.concatenate(out, axis=1) * g


def _od_in_kernel(x_ref, mod_ref, g_ref, w_ref, qg_ref, kg_ref, dtb_ref,
                  q_ref, k_ref, v_ref, zg_ref, xbc_ref, dt_ref):
    mod = mod_ref[0]
    nq = q_ref.shape[2]
    nkv = k_ref.shape[2]
    hn = _norm_mod(x_ref[0], g_ref[...], mod[1:2], mod[0:1]).astype(BF16)
    dot = lambda lo, hi: jnp.dot(hn, w_ref[:, lo:hi], preferred_element_type=F32)
    o_zg = nq + 2 * nkv
    o_xbc = o_zg + zg_ref.shape[2]
    o_dt = o_xbc + xbc_ref.shape[2]
    dt_ref[0] = _softplus(dot(o_dt, o_dt + LANES) + dtb_ref[...])
    q_ref[0] = _head_rms(dot(0, nq), qg_ref[...]).astype(BF16)
    kv = dot(nq, nq + 2 * nkv)
    k_ref[0] = _head_rms(kv[:, 0:nkv], kg_ref[...])
    v_ref[0] = kv[:, nkv:2 * nkv]
    zg_ref[0] = dot(o_zg, o_xbc)
    xbc_ref[0] = dot(o_xbc, o_dt)


def _od_in(x, mod, norm_g, w_in, q_norm, k_norm, dt_bias, dims, tb):
    b, t, d = x.shape
    nq, nkv, nz, nxbc, nh = dims
    w = jnp.concatenate([w_in, jnp.zeros((d, LANES - nh), F32)], axis=1).astype(BF16)
    qg = jnp.tile(q_norm, nq // HEAD_DIM).reshape(1, nq)
    kg = jnp.tile(k_norm, nkv // HEAD_DIM).reshape(1, nkv)
    dtb = jnp.concatenate([dt_bias, jnp.zeros((LANES - nh,), F32)]).reshape(1, LANES)
    g2 = norm_g.reshape(1, d)
    tile = lambda n: pl.BlockSpec((1, tb, n), lambda i, j: (i, j, 0))
    const = lambda a: pl.BlockSpec(a.shape, lambda i, j: (0,) * a.ndim)
    widths = (nq, nkv, nkv, nz, nxbc, LANES)
    return pl.pallas_call(
        _od_in_kernel,
        grid=(b, t // tb),
        in_specs=[tile(d), pl.BlockSpec((1, 6, d), lambda i, j: (i, 0, 0)), const(g2), const(w),
                  const(qg), const(kg), const(dtb)],
        out_specs=[tile(n) for n in widths],
        out_shape=[jax.ShapeDtypeStruct((b, t, n), BF16 if i == 0 else F32) for i, n in enumerate(widths)],
        compiler_params=_params(2),
        name="od_in",
    )(x, mod, g2, w, qg, kg, dtb)


def _swa_kernel(q_ref, k_ref, v_ref, k0_ref, v0_ref, bias_ref, sink_ref, o_ref, kx, vx,
                *, n_blocks, nq, mask_start):
    t = pl.program_id(1)
    tb = q_ref.shape[1]
    n_kv = k_ref.shape[2] // HEAD_DIM
    rows_q = nq * CHUNK
    tail = kx.shape[1] - WINDOW - tb
    left = lax.broadcasted_iota(jnp.int32, (1, LANES), 1) < HEAD_DIM
    first_side = lax.broadcasted_iota(jnp.int32, (2 * SWA_KEYS, 1), 0) < SWA_KEYS
    side_ones = jnp.where(first_side == left, 1.0, 0.0).astype(BF16)
    col = lax.broadcasted_iota(jnp.int32, (1, SWA_KEYS), 1)

    def place(dst, rows, x):
        rolled = pltpu.roll(x, HEAD_DIM, axis=1)
        dst[0, rows, :] = jnp.where(left, x, 0.0).astype(BF16)
        dst[1, rows, :] = jnp.where(left, 0.0, rolled).astype(BF16)
        dst[2, rows, :] = jnp.where(left, rolled, 0.0).astype(BF16)
        dst[3, rows, :] = jnp.where(left, 0.0, x).astype(BF16)

    @pl.when(t == 0)
    def _():
        place(kx, slice(0, WINDOW), k0_ref[0])
        place(vx, slice(0, WINDOW), v0_ref[0])
        for i in range(2 * n_kv if tail else 0):
            kx[i, WINDOW + tb:, :] = jnp.zeros((tail, LANES), BF16)
            vx[i, WINDOW + tb:, :] = jnp.zeros((tail, LANES), BF16)

    place(kx, slice(WINDOW, WINDOW + tb), k_ref[0])
    place(vx, slice(WINDOW, WINDOW + tb), v_ref[0])

    def block(blk, carry):
        r0 = pl.multiple_of(blk * rows_q, rows_q)
        keys = pl.ds(r0, SWA_KEYS)
        if mask_start:
            valid = t * tb + r0 - WINDOW + col >= 0
        probs = []
        sinks = []
        for j in range(n_kv):
            lo = 2 * j * LANES
            qg = jnp.concatenate([q_ref[0, pl.ds(r0 + cq * CHUNK, CHUNK), lo + r * LANES:lo + (r + 1) * LANES]
                                  for cq in range(nq) for r in range(2)], axis=0).astype(BF16)
            kcat = jnp.concatenate([kx[2 * j, keys, :], kx[2 * j + 1, keys, :]], axis=0)
            s_both = lax.dot_general(qg, kcat, _NT, preferred_element_type=F32) * HEAD_DIM ** -0.5
            for side in range(2):
                s = s_both[:, side * SWA_KEYS:(side + 1) * SWA_KEYS] + bias_ref[j, side]
                if mask_start:
                    s = jnp.where(valid, s, NEG_INF)
                m = jnp.max(s, axis=-1, keepdims=True)
                probs.append(jnp.exp(s - m).astype(BF16))
                sinks.append(jnp.exp(sink_ref[j, side] - m))
        for j in range(n_kv):
            lo = 2 * j * LANES
            vcat = jnp.concatenate(
                [jnp.concatenate([vx[2 * j, keys, :], vx[2 * j + 1, keys, :]], axis=0), side_ones], axis=1)
            pv = jnp.dot(jnp.concatenate(probs[2 * j:2 * j + 2], axis=1), vcat, preferred_element_type=F32)
            den = pv[:, LANES:] + jnp.where(left, sinks[2 * j], sinks[2 * j + 1])
            out = pv[:, 0:LANES] / den
            for cq in range(nq):
                for r in range(2):
                    o_ref[0, pl.ds(r0 + cq * CHUNK, CHUNK), lo + r * LANES:lo + (r + 1) * LANES] = (
                        out[(2 * cq + r) * CHUNK:(2 * cq + r + 1) * CHUNK].astype(o_ref.dtype))
        return carry

    lax.fori_loop(0, n_blocks, block, 0, unroll=math.gcd(n_blocks, 4))
    for i in range(2 * n_kv):
        kx[i, 0:WINDOW, :] = kx[i, tb:tb + WINDOW, :]
        vx[i, 0:WINDOW, :] = vx[i, tb:tb + WINDOW, :]


def _t5_bucket(rel):
    nb = T5_BUCKETS // 2
    max_exact = nb // 2
    ret = (rel > 0).astype(jnp.int32) * nb
    n = jnp.abs(rel)
    nf = jnp.maximum(n, 1).astype(F32)
    large = max_exact + (jnp.log(nf / max_exact) / math.log(T5_MAX_DIST / max_exact)
                         * (nb - max_exact)).astype(jnp.int32)
    large = jnp.minimum(large, nb - 1)
    return ret + jnp.where(n < max_exact, n, large)


def _rel_bias(table):
    rel = (jnp.arange(WINDOW + CHUNK)[None, :] - WINDOW) - jnp.arange(CHUNK)[:, None]
    onehot = (_t5_bucket(rel)[..., None] == jnp.arange(T5_BUCKETS)).astype(F32)
    return jnp.einsum('qkb,bh->hqk', onehot, table, precision=_HI)


def _swa(q, k, v, k0, v0, bias, sink, tb, mask_start):
    b, t, nq = q.shape
    nkv = k.shape[2]
    n_kv = nkv // HEAD_DIM
    assert nkv == LANES and nq == 2 * n_kv * LANES
    by_side = lambda a: jnp.transpose(a.reshape((n_kv, 2, 2) + a.shape[1:]), (0, 2, 1) + tuple(range(3, a.ndim + 2)))
    span = WINDOW + CHUNK
    n_chunks = tb // CHUNK
    cpb = 2 if n_chunks % 2 == 0 else 1
    assert WINDOW + cpb * CHUNK <= SWA_KEYS
    bias2 = by_side(bias).reshape(n_kv, 2, 2 * CHUNK, span)
    bias3 = jnp.concatenate(
        [jnp.pad(bias2, ((0, 0), (0, 0), (0, 0), (cq * CHUNK, SWA_KEYS - span - cq * CHUNK)),
                 constant_values=NEG_INF) for cq in range(cpb)], axis=2)
    sink3 = jnp.tile(jnp.repeat(by_side(sink), CHUNK, axis=2), (1, 1, cpb))
    sink3 = jnp.broadcast_to(sink3[..., None], sink3.shape + (LANES,))
    tile = lambda n: pl.BlockSpec((1, tb, n), lambda i, j: (i, j, 0))
    wspec = pl.BlockSpec((1, WINDOW, nkv), lambda i, j: (i, 0, 0))
    const = lambda a: pl.BlockSpec(a.shape, lambda i, j: (0,) * a.ndim)
    rows = WINDOW + max(tb, WINDOW) + SWA_KEYS - WINDOW - cpb * CHUNK
    return pl.pallas_call(
        functools.partial(_swa_kernel, n_blocks=n_chunks // cpb, nq=cpb, mask_start=mask_start),
        grid=(b, t // tb),
        in_specs=[tile(nq), tile(nkv), tile(nkv), wspec, wspec, const(bias3), const(sink3)],
        out_specs=tile(nq),
        out_shape=jax.ShapeDtypeStruct((b, t, nq), BF16),
        scratch_shapes=[pltpu.VMEM((2 * n_kv, rows, LANES), BF16), pltpu.VMEM((2 * n_kv, rows, LANES), BF16)],
        compiler_params=_params(2),
        name="swa",
    )(q, k, v, k0, v0, bias3, sink3)


def _ssd_kernel(xbc_ref, zg_ref, dt_ref, pre_ref, cw_ref, cb_ref, aexp_ref, dexp_ref,
                e_ref, ng_ref, s0_ref, y_ref, sfin_ref, clast_ref, st_scr, hist_scr, seg_scr, dte_scr,
                *, n_chunks):
    t = pl.program_id(1)
    tb = xbc_ref.shape[1]
    inner = zg_ref.shape[2]
    n_heads = inner // SSD_HEAD_DIM
    hpg = n_heads // SSD_GROUPS
    gn = SSD_GROUPS * SSD_STATE
    n_hs = n_heads * CHUNK

    @pl.when(t == 0)
    def _():
        st_scr[...] = s0_ref[0]
        hist_scr[...] = pre_ref[0]

    _to_segments(seg_scr, xbc_ref[0])
    conv, hist_scr[...] = _causal_conv_interleaved(_load_interleaved(seg_scr, tb), hist_scr[...],
                                                   cw_ref[...], cb_ref[...])
    _store_interleaved(seg_scr, _silu(conv))
    dte_scr[...] = _split_dot(dt_ref[0], e_ref[...])

    row_hs = lax.broadcasted_iota(jnp.int32, (n_hs, 1), 0)
    own_group = row_hs // (hpg * CHUNK) == lax.broadcasted_iota(jnp.int32, (1, gn), 1) // SSD_STATE
    own_head = row_hs // CHUNK == lax.broadcasted_iota(jnp.int32, (1, inner), 1) // SSD_HEAD_DIM
    state_group = (lax.broadcasted_iota(jnp.int32, (gn, 1), 0) // SSD_STATE
                   == lax.broadcasted_iota(jnp.int32, (1, inner), 1) // (hpg * SSD_HEAD_DIM))
    first_group = lax.broadcasted_iota(jnp.int32, (1, inner), 1) < hpg * SSD_HEAD_DIM
    step_row = lax.broadcasted_iota(jnp.int32, (CHUNK, 1), 0)
    step_lane = lax.broadcasted_iota(jnp.int32, (1, n_hs), 1) % CHUNK
    zero = jnp.zeros((), BF16)
    aexp = aexp_ref[...]

    for c in range(n_chunks):
        rows = slice(c * CHUNK, (c + 1) * CHUNK)
        xc = _rows_from_segments(seg_scr, c * CHUNK, CHUNK, tb)
        xs = xc[:, 0:inner]
        bm = xc[:, inner:inner + gn]
        cm = xc[:, inner + gn:inner + 2 * gn].astype(BF16)
        dte = dte_scr[rows, :]
        cum = _cumsum_rows(dte * aexp)
        cum_last = cum[CHUNK - 1:CHUNK, :]
        cum_at_step = jnp.sum(jnp.where(step_row == step_lane, cum, 0.0), axis=0, keepdims=True)
        decay = jnp.exp(jnp.where(step_row >= step_lane, cum - cum_at_step, NEG_INF))
        bm_bd = jnp.where(own_group, jnp.concatenate([bm.astype(BF16)] * n_heads, axis=0), zero)
        cb = lax.dot_general(cm, bm_bd, _NT, preferred_element_type=F32)
        xdt = (xs * dte).astype(BF16)
        xdt_bd = jnp.where(own_head, jnp.concatenate([xdt] * n_heads, axis=0), zero)
        st = st_scr[...]
        st_bd = jnp.where(state_group, jnp.concatenate([st.astype(BF16)] * SSD_GROUPS, axis=0), zero)
        y = (jnp.dot((cb * decay).astype(BF16), xdt_bd, preferred_element_type=F32)
             + jnp.exp(cum) * jnp.dot(cm, st_bd, preferred_element_type=F32))
        xw = (xs * (jnp.exp(cum_last - cum) * dte)).astype(BF16)
        upd = jnp.dot(bm.T.astype(BF16), xw, preferred_element_type=F32)
        st_scr[...] = st * jnp.exp(cum_last) + jnp.where(first_group, upd[0:SSD_STATE], upd[SSD_STATE:])
        yd = (y + dexp_ref[...] * xs) * _silu(zg_ref[0, rows, :])
        ms = jnp.mean(yd * yd, axis=-1, keepdims=True)
        y_ref[0, rows, :] = (yd * lax.rsqrt(ms + RMS_EPS) * ng_ref[...]).astype(y_ref.dtype)

    @pl.when(t == pl.num_programs(1) - 1)
    def _():
        sfin_ref[0] = st_scr[...]
        clast_ref[0] = hist_scr[...]


def _ssd(xbc, zg, dt, conv_prefix, conv_w, conv_b, a_log, d_skip, norm_g, s0, tb):
    b, t, nxbc = xbc.shape
    inner = zg.shape[2]
    nh = a_log.shape[0]
    width = conv_w.shape[0]
    assert SSD_GROUPS == 2 and SSD_HEAD_DIM == CHUNK
    pre8 = jnp.concatenate([jnp.zeros((b, SUBLANES - (width - 1), nxbc), F32), conv_prefix], axis=1)
    a = -jnp.exp(a_log)
    aexp = jnp.repeat(a, SSD_HEAD_DIM).reshape(1, inner)
    dexp = jnp.repeat(d_skip, SSD_HEAD_DIM).reshape(1, inner)
    expand = (np.arange(LANES)[:, None] == np.arange(inner)[None, :] // SSD_HEAD_DIM)
    expand = jnp.asarray(expand, BF16)
    s0t = jnp.transpose(s0, (0, 3, 1, 2)).reshape(b, SSD_STATE, inner)
    ng = norm_g.reshape(1, inner)
    cb2 = conv_b.reshape(1, nxbc)
    tile = lambda n: pl.BlockSpec((1, tb, n), lambda i, j: (i, j, 0))
    const = lambda arr: pl.BlockSpec(arr.shape, lambda i, j: (0,) * arr.ndim)
    bspec = lambda arr: pl.BlockSpec((1,) + arr.shape[1:], lambda i, j: (i,) + (0,) * (arr.ndim - 1))
    y, sfin, clast = pl.pallas_call(
        functools.partial(_ssd_kernel, n_chunks=tb // CHUNK),
        grid=(b, t // tb),
        in_specs=[tile(nxbc), tile(inner), tile(LANES),
                  bspec(pre8), const(conv_w), const(cb2), const(aexp), const(dexp),
                  const(expand), const(ng), bspec(s0t)],
        out_specs=[tile(inner), bspec(s0t), bspec(pre8)],
        out_shape=[jax.ShapeDtypeStruct((b, t, inner), BF16),
                   jax.ShapeDtypeStruct(s0t.shape, F32),
                   jax.ShapeDtypeStruct(pre8.shape, F32)],
        scratch_shapes=[pltpu.VMEM((SSD_STATE, inner), F32), pltpu.VMEM((SUBLANES, nxbc), F32),
                        pltpu.VMEM((nxbc // LANES, SUBLANES * _segment_pitch(tb), LANES), F32),
                        pltpu.VMEM((tb, inner), F32)],
        compiler_params=_params(2),
        name="ssd",
    )(xbc, zg, dt, pre8, conv_w, cb2, aexp, dexp, expand, ng, s0t)
    sfin = jnp.transpose(sfin.reshape(b, SSD_STATE, nh, SSD_HEAD_DIM), (0, 2, 3, 1))
    return y, sfin, clast[:, SUBLANES - (width - 1):]


def _trunk(x, mods, P, st, sample):
    b, t, d = x.shape
    tb = min(MAX_TILE, t)
    tbm = min(MIXER_TILE, t)
    depth = P['w_mod'].shape[0]
    new = {name: [] for name in ('s5_re', 's5_im', 'gla', 'swa_k', 'swa_v', 'ssd', 'ssd_conv', 'ffn_conv')}
    for layer in range(depth):
        i = layer // 2
        mod = mods[layer].reshape(b, 6, d)
        ffn = (P['norm2_g'][layer], P['ffn_w_up'][layer], P['ffn_conv_w'][layer], P['ffn_conv_b'][layer],
               P['ffn_w_down'][layer], st['ffn_conv'][layer], tb)
        if layer % 2 == 0:
            u, qk, v, r, gate = _ev_in(x, mod, P['norm1_g'][layer], P['ev_w_in'][i], P['gla_w_gate2'][i],
                                       P['gla_b_gate'][i], tbm)
            mats = _s5_matrices(P['s5_a_re'][i], P['s5_a_im'][i], P['s5_log_dt'][i], P['s5_b_re'][i],
                                P['s5_b_im'][i], P['s5_c_re'][i], P['s5_c_im'][i])
            ya, sr, si = _s5(u, mats, P['s5_d'][i], st['s5_re'][i], st['s5_im'][i])
            ob, sg = _gla(qk, v, gate, r, st['gla'][i], P['gla_norm_g'][i], tb)
            new['s5_re'].append(sr)
            new['s5_im'].append(si)
            new['gla'].append(sg)
            x, fp = _out_ffn(x, mod, ya, ob, P['ev_w_out'][i], *ffn,
                             s5_extra=(P['s5_w_glu'][i], P['s5_b_glu'][i]))
        else:
            nq = P['swa_sink'].shape[1] * HEAD_DIM
            nkv = SWA_KV_HEADS * HEAD_DIM
            inner = P['ssd_norm_g'].shape[1]
            nxbc = P['ssd_conv_w'].shape[2]
            nh = P['ssd_a_log'].shape[1]
            q, k, v, zg, xbc, dt = _od_in(x, mod, P['norm1_g'][layer], P['od_w_in'][i], P['swa_q_norm'][i],
                                               P['swa_k_norm'][i], P['ssd_dt_bias'][i],
                                               (nq, nkv, inner, nxbc, nh), tbm)
            bias = _rel_bias(P['t5_bias'])
            if sample:
                k0 = st['swa_k'][i].reshape(b, WINDOW, nkv)
                v0 = st['swa_v'][i].reshape(b, WINDOW, nkv)
            else:
                k0 = v0 = jnp.zeros((b, WINDOW, nkv), F32)
            oc = _swa(q, k, v, k0, v0, bias, P['swa_sink'][i], tbm, mask_start=not sample)
            yd, ss, sc = _ssd(xbc, zg, dt, st['ssd_conv'][i], P['ssd_conv_w'][i], P['ssd_conv_b'][i],
                              P['ssd_a_log'][i], P['ssd_d'][i], P['ssd_norm_g'][i], st['ssd'][i], tbm)
            keep = slice(None) if sample else slice(t - WINDOW, t)
            new['swa_k'].append(k[:, keep].reshape(b, -1, SWA_KV_HEADS, HEAD_DIM))
            new['swa_v'].append(v[:, keep].reshape(b, -1, SWA_KV_HEADS, HEAD_DIM))
            new['ssd'].append(ss)
            new['ssd_conv'].append(sc)
            x, fp = _out_ffn(x, mod, oc, yd, P['od_w_out'][i], *ffn)
        new['ffn_conv'].append(fp)
    return x, {name: jnp.stack(vals) for name, vals in new.items()}


def kernel(x_prompt, x_sample, state_s5_re, state_s5_im, state_gla, cache_swa_k, cache_swa_v, state_ssd, state_ssd_conv, state_ffn_conv, c_prompt, c_sample, t5_bias, norm1_g, norm2_g, w_mod, b_mod, ffn_w_up, ffn_conv_w, ffn_conv_b, ffn_w_down, ev_w_in, ev_w_out, s5_a_re, s5_a_im, s5_log_dt, s5_b_re, s5_b_im, s5_c_re, s5_c_im, s5_d, s5_w_glu, s5_b_glu, gla_w_gate2, gla_b_gate, gla_norm_g, od_w_in, od_w_out, swa_q_norm, swa_k_norm, swa_sink, ssd_conv_w, ssd_conv_b, ssd_dt_bias, ssd_a_log, ssd_d, ssd_norm_g):
    P = dict(t5_bias=t5_bias, norm1_g=norm1_g, norm2_g=norm2_g, w_mod=w_mod, b_mod=b_mod,
             ffn_w_up=ffn_w_up, ffn_conv_w=ffn_conv_w, ffn_conv_b=ffn_conv_b, ffn_w_down=ffn_w_down,
             ev_w_in=ev_w_in, ev_w_out=ev_w_out, s5_a_re=s5_a_re, s5_a_im=s5_a_im, s5_log_dt=s5_log_dt,
             s5_b_re=s5_b_re, s5_b_im=s5_b_im, s5_c_re=s5_c_re, s5_c_im=s5_c_im, s5_d=s5_d,
             s5_w_glu=s5_w_glu, s5_b_glu=s5_b_glu, gla_w_gate2=gla_w_gate2, gla_b_gate=gla_b_gate,
             gla_norm_g=gla_norm_g, od_w_in=od_w_in, od_w_out=od_w_out, swa_q_norm=swa_q_norm,
             swa_k_norm=swa_k_norm, swa_sink=swa_sink, ssd_conv_w=ssd_conv_w, ssd_conv_b=ssd_conv_b,
             ssd_dt_bias=ssd_dt_bias, ssd_a_log=ssd_a_log, ssd_d=ssd_d, ssd_norm_g=ssd_norm_g)
    bp = x_prompt.shape[0]
    n_even, n_odd = state_s5_re.shape[0], state_ssd.shape[0]
    depth = w_mod.shape[0]
    zeros_like_b = lambda a: jnp.zeros((a.shape[0], bp) + a.shape[2:], F32)
    zero_st = dict(s5_re=zeros_like_b(state_s5_re), s5_im=zeros_like_b(state_s5_im), gla=zeros_like_b(state_gla),
                   ssd=zeros_like_b(state_ssd), ssd_conv=zeros_like_b(state_ssd_conv),
                   ffn_conv=zeros_like_b(state_ffn_conv))
    sample_st = dict(s5_re=state_s5_re, s5_im=state_s5_im, gla=state_gla, swa_k=cache_swa_k,
                     swa_v=cache_swa_v, ssd=state_ssd, ssd_conv=state_ssd_conv, ffn_conv=state_ffn_conv)
    mods = _modulation(jnp.concatenate([c_prompt, c_sample], axis=0), w_mod, b_mod)
    y_prompt, stp = _trunk(x_prompt, mods[:, :bp], P, zero_st, False)
    y_sample, sts = _trunk(x_sample, mods[:, bp:], P, sample_st, True)
    names = ('s5_re', 's5_im', 'gla', 'swa_k', 'swa_v', 'ssd', 'ssd_conv', 'ffn_conv')
    return (y_prompt, y_sample) + tuple(stp[n] for n in names) + tuple(sts[n] for n in names)
```

```python
import functools
import math

import jax
import jax.numpy as jnp
import numpy as np
from jax import lax
from jax.experimental import pallas as pl
from jax.experimental.pallas import tpu as pltpu

F32 = jnp.float32
BF16 = jnp.bfloat16

CHUNK = 64
WINDOW = 128
S5_GROUP = 16
S5_SUB = 8
S5_MAX_SUBS = 64
GLA_HEADS = 4
GLA_GATE_NORM = 16.0
HEAD_DIM = 64
SWA_KV_HEADS = 2
SWA_KEYS = 256
SSD_HEAD_DIM = 64
SSD_STATE = 128
SSD_GROUPS = 2
T5_BUCKETS = 32
T5_MAX_DIST = 128
RMS_EPS = 1e-6
NEG_INF = -1e30
LANES = 128
SUBLANES = 8
assert S5_SUB == SUBLANES and S5_GROUP * SUBLANES == LANES
MAX_TILE = 512
MIXER_TILE = 1024
CHUNK_UNROLL = 8
FFN_COL_BLOCK = 256
VMEM_LIMIT = 56 * 1024 * 1024

_NT = (((1,), (1,)), ((), ()))
_HI = lax.Precision.HIGHEST


def _params(n_axes=2):
    sem = ("parallel",) + ("arbitrary",) * (n_axes - 1)
    return pltpu.CompilerParams(dimension_semantics=sem, vmem_limit_bytes=VMEM_LIMIT)


def _bdot(a, b):
    return jnp.dot(a.astype(BF16), b.astype(BF16), preferred_element_type=F32)


def _split_dot(x, c, parts=3):
    pieces = []
    rest = x
    for _ in range(parts):
        piece = rest.astype(BF16)
        pieces.append(piece)
        rest = rest - piece.astype(F32)
    n = x.shape[0]
    d = jnp.dot(jnp.concatenate(pieces, axis=0), c, preferred_element_type=F32)
    out = d[0:n]
    for i in range(1, parts):
        out = out + d[i * n:(i + 1) * n]
    return out


def _silu(x):
    return x * jax.nn.sigmoid(x)


def _softplus(x):
    return jnp.maximum(x, 0.0) + jnp.log1p(jnp.exp(-jnp.abs(x)))


def _log_sigmoid(x):
    return jnp.minimum(x, 0.0) - jnp.log1p(jnp.exp(-jnp.abs(x)))


def _norm_mod(x, g, scale, shift):
    ms = jnp.mean(x * x, axis=-1, keepdims=True)
    return (x * lax.rsqrt(ms + RMS_EPS)) * (g * (1.0 + scale)) + shift


def _cumsum_rows(x):
    n, m = x.shape
    tiles = x.reshape(n // SUBLANES, SUBLANES, m)
    sub = lax.broadcasted_iota(jnp.int32, (1, SUBLANES, 1), 1)
    d = 1
    while d < SUBLANES:
        tiles = tiles + jnp.where(sub >= d, pltpu.roll(tiles, d, axis=1), 0.0)
        d *= 2
    out = [tiles[0]]
    for i in range(1, n // SUBLANES):
        out.append(tiles[i] + out[-1][SUBLANES - 1:SUBLANES, :])
    return jnp.concatenate(out, axis=0)


def _segment_pitch(n):
    pitch = n // SUBLANES + SUBLANES
    return pitch if (pitch // SUBLANES) % 2 else pitch + SUBLANES


def _to_segments(ref, val):
    nv = val.shape[0] // SUBLANES
    pitch = ref.shape[1] // SUBLANES
    for c in range(ref.shape[0]):
        for s in range(SUBLANES):
            ref[c, s * pitch:s * pitch + nv, :] = val[s * nv:(s + 1) * nv, c * LANES:(c + 1) * LANES]


def _from_segments(ref, n):
    nv = n // SUBLANES
    pitch = ref.shape[1] // SUBLANES
    return jnp.concatenate(
        [jnp.concatenate([ref[c, s * pitch:s * pitch + nv, :] for s in range(SUBLANES)], axis=0)
         for c in range(ref.shape[0])], axis=1)


def _rows_from_segments(ref, r0, n_rows, n):
    nv = n // SUBLANES
    pitch = ref.shape[1] // SUBLANES
    spans = []
    r = r0
    while r < r0 + n_rows:
        s, off = divmod(r, nv)
        take = min(nv - off, r0 + n_rows - r)
        spans.append(slice(s * pitch + off, s * pitch + off + take))
        r += take
    return jnp.concatenate(
        [jnp.concatenate([ref[c, sp, :] for sp in spans], axis=0) for c in range(ref.shape[0])], axis=1)


def _load_interleaved(ref, n):
    pitch = ref.shape[1] // SUBLANES
    return jnp.concatenate(
        [jnp.concatenate([ref[c, pl.ds(i, SUBLANES, stride=pitch), :] for c in range(ref.shape[0])], axis=1)
         for i in range(n // SUBLANES)], axis=0)


def _store_interleaved(ref, val):
    pitch = ref.shape[1] // SUBLANES
    for i in range(val.shape[0] // SUBLANES):
        for c in range(ref.shape[0]):
            ref[c, pl.ds(i, SUBLANES, stride=pitch), :] = val[i * SUBLANES:(i + 1) * SUBLANES, c * LANES:(c + 1) * LANES]


def _causal_conv_interleaved(u, prev8, w, b):
    width = w.shape[0]
    n, m = u.shape
    nv = n // SUBLANES
    sub = lax.broadcasted_iota(jnp.int32, (SUBLANES, 1), 0)
    tiles = u.reshape(nv, SUBLANES, m)

    def back_one(a, before):
        first = jnp.where(sub == 0, before, pltpu.roll(a[nv - 1], 1, axis=0))
        return jnp.concatenate([first[None], a[:nv - 1]], axis=0)

    delayed = [tiles]
    for k in range(1, width):
        delayed.append(back_one(delayed[-1], prev8[SUBLANES - k:SUBLANES - k + 1]))
    out = b
    for j in range(width):
        out = out + delayed[width - 1 - j] * w[j:j + 1]
    hist = tiles[nv - 1]
    for k in range(2, width):
        hist = jnp.where(sub == SUBLANES - k, pltpu.roll(tiles[nv - k], SUBLANES - k + 1, axis=0), hist)
    return out.reshape(n, m), hist


def _mod_kernel(c_ref, w_ref, b_ref, o_ref):
    o_ref[0] = _bdot(_silu(c_ref[...]), w_ref[0]) + b_ref[0]


def _modulation(c, w_mod, b_mod):
    depth, d, n = w_mod.shape
    bc = c.shape[0]
    tn = n // 4
    return pl.pallas_call(
        _mod_kernel,
        grid=(depth, n // tn),
        in_specs=[pl.BlockSpec((bc, d), lambda l, j: (0, 0)),
                  pl.BlockSpec((1, d, tn), lambda l, j: (l, 0, j)),
                  pl.BlockSpec((1, 1, tn), lambda l, j: (l, 0, j))],
        out_specs=pl.BlockSpec((1, bc, tn), lambda l, j: (l, 0, j)),
        out_shape=jax.ShapeDtypeStruct((depth, bc, n), F32),
        compiler_params=_params(2),
        name="modulation",
    )(c, w_mod, b_mod.reshape(depth, 1, n))


def _ev_in_kernel(x_ref, mod_ref, g_ref, w_ref, wg2_ref, bg_ref,
                  u_ref, qk_ref, v_ref, r_ref, gate_ref):
    mod = mod_ref[0]
    hn = _norm_mod(x_ref[0], g_ref[...], mod[1:2], mod[0:1]).astype(BF16)
    dot = lambda lo, hi: jnp.dot(hn, w_ref[:, lo:hi], preferred_element_type=F32)
    gl = dot(2048, 2048 + LANES)
    gate_ref[0] = _log_sigmoid(_bdot(gl, wg2_ref[...]) + bg_ref[...]) * (1.0 / GLA_GATE_NORM)
    u = dot(0, 512)
    for c in range(u_ref.shape[0]):
        u_ref[c] = _chunk_transpose(u[:, c * LANES:(c + 1) * LANES].reshape(-1, SUBLANES, LANES))
    qk_ref[0] = dot(512, 1024)
    v_ref[0] = dot(1024, 1536).astype(BF16)
    r_ref[0] = dot(1536, 2048)


def _ev_in(x, mod, norm_g, w_in, w_gate2, b_gate, tb):
    b, t, d = x.shape
    rank = w_gate2.shape[0]
    nk = w_gate2.shape[1]
    wu, wq, wk, wv, wgl, wr = jnp.split(w_in, [512, 768, 1024, 1536, 1536 + rank], axis=1)
    w = jnp.concatenate([wu, wq, wk, wv, wr, wgl, jnp.zeros((d, LANES - rank), F32)], axis=1).astype(BF16)
    wg2 = jnp.concatenate([w_gate2, jnp.zeros((LANES - rank, nk), F32)], axis=0).astype(BF16)
    tile = lambda n: pl.BlockSpec((1, tb, n), lambda i, j: (i, j, 0))
    const = lambda a: pl.BlockSpec(a.shape, lambda i, j: (0,) * a.ndim)
    g2 = norm_g.reshape(1, d)
    bg = b_gate.reshape(1, nk)
    return pl.pallas_call(
        _ev_in_kernel,
        grid=(b, t // tb),
        in_specs=[tile(d), pl.BlockSpec((1, 6, d), lambda i, j: (i, 0, 0)), const(g2), const(w),
                  const(wg2), const(bg)],
        out_specs=[pl.BlockSpec((512 // LANES, tb // S5_SUB, None, SUBLANES, LANES), lambda i, j: (0, j, i, 0, 0)),
                   tile(512), tile(512), tile(512), tile(nk)],
        out_shape=[jax.ShapeDtypeStruct((512 // LANES, t // S5_SUB, b, SUBLANES, LANES), F32)]
        + [jax.ShapeDtypeStruct((b, t, n), dt) for n, dt in ((512, F32), (512, BF16), (512, F32), (nk, F32))],
        compiler_params=_params(2),
        name="ev_in",
    )(x, mod, g2, w, wg2, bg)


def _gla_kernel(qk_ref, v_ref, gate_ref, r_ref, s0_ref, ng_ref, o_ref, sfin_ref, st_scr, *, n_chunks):
    t = pl.program_id(1)
    dk = qk_ref.shape[2] // 2 // GLA_HEADS
    dv = v_ref.shape[2] // GLA_HEADS
    nk = GLA_HEADS * dk
    nv = GLA_HEADS * dv
    blk = (lax.broadcasted_iota(jnp.int32, (nv, nk), 0) // dv
           == lax.broadcasted_iota(jnp.int32, (nv, nk), 1) // dk)

    @pl.when(t == 0)
    def _():
        s0 = jnp.concatenate([s0_ref[0, h] for h in range(GLA_HEADS)], axis=0)
        st_scr[...] = jnp.where(blk, jnp.concatenate([s0] * GLA_HEADS, axis=1), 0.0)

    n_hs = GLA_HEADS * CHUNK
    row_head = lax.broadcasted_iota(jnp.int32, (n_hs, 1), 0) // CHUNK
    own_k = row_head == lax.broadcasted_iota(jnp.int32, (1, nk), 1) // dk
    own_v = row_head == lax.broadcasted_iota(jnp.int32, (1, nv), 1) // dv
    causal = (lax.broadcasted_iota(jnp.int32, (CHUNK, 1), 0)
              >= lax.broadcasted_iota(jnp.int32, (1, n_hs), 1) % CHUNK)
    ng = ng_ref[...]

    def chunk(c, carry):
        rows = pl.ds(pl.multiple_of(c * CHUNK, CHUNK), CHUNK)
        k = qk_ref[0, rows, nk:2 * nk]
        v_b = v_ref[0, rows, :]
        cum = _cumsum_rows(gate_ref[0, rows, :])
        cum_last = cum[CHUNK - 1:CHUNK, :]
        qe = (qk_ref[0, rows, 0:nk] * dk ** -0.5 * jnp.exp(cum)).astype(BF16)
        ke = (k * jnp.exp(-cum)).astype(BF16)
        kd = (k * jnp.exp(cum_last - cum)).astype(BF16)
        st = st_scr[...]
        ke_bd = jnp.where(own_k, jnp.concatenate([ke] * GLA_HEADS, axis=0), jnp.zeros((), BF16))
        v_bd = jnp.where(own_v, jnp.concatenate([v_b] * GLA_HEADS, axis=0), jnp.zeros((), BF16))
        att = lax.dot_general(qe, ke_bd, _NT, preferred_element_type=F32)
        att = jnp.where(causal, att, 0.0).astype(BF16)
        o = (jnp.dot(att, v_bd, preferred_element_type=F32)
             + lax.dot_general(qe, st.astype(BF16), _NT, preferred_element_type=F32))
        for h in range(GLA_HEADS):
            cols = slice(h * dv, (h + 1) * dv)
            oh = o[:, cols]
            ms = jnp.mean(oh * oh, axis=-1, keepdims=True)
            oh = oh * lax.rsqrt(ms + RMS_EPS) * ng
            o_ref[0, rows, cols] = (oh * _silu(r_ref[0, rows, cols])).astype(o_ref.dtype)
        upd = jnp.dot(v_b.astype(F32).T.astype(BF16), kd, preferred_element_type=F32)
        st_scr[...] = jnp.where(blk, st * jnp.exp(cum_last) + upd, 0.0)
        return carry

    lax.fori_loop(0, n_chunks, chunk, 0, unroll=math.gcd(n_chunks, CHUNK_UNROLL))

    @pl.when(t == pl.num_programs(1) - 1)
    def _():
        for h in range(GLA_HEADS):
            sfin_ref[0, h] = st_scr[h * dv:(h + 1) * dv, h * dk:(h + 1) * dk]


def _gla(qk, v, gate, r, s0, norm_g, tb):
    b, t, nv = v.shape
    nk = gate.shape[2]
    dk, dv = nk // GLA_HEADS, nv // GLA_HEADS
    s0t = jnp.swapaxes(s0, 2, 3)
    tile = lambda n: pl.BlockSpec((1, tb, n), lambda i, j: (i, j, 0))
    sspec = pl.BlockSpec((1, GLA_HEADS, dv, dk), lambda i, j: (i, 0, 0, 0))
    ng = norm_g.reshape(1, dv)
    o, sfin = pl.pallas_call(
        functools.partial(_gla_kernel, n_chunks=tb // CHUNK),
        grid=(b, t // tb),
        in_specs=[tile(2 * nk), tile(nv), tile(nk), tile(nv), sspec,
                  pl.BlockSpec((1, dv), lambda i, j: (0, 0))],
        out_specs=[tile(nv), sspec],
        out_shape=[jax.ShapeDtypeStruct((b, t, nv), BF16),
                   jax.ShapeDtypeStruct((b, GLA_HEADS, dv, dk), F32)],
        scratch_shapes=[pltpu.VMEM((nv, nk), F32)],
        compiler_params=_params(2),
        name="gla",
    )(qk, v, gate, r, s0t, ng)
    return o, jnp.swapaxes(sfin, 2, 3)


def _chunk_transpose(x):
    s = lax.broadcasted_iota(jnp.int32, (1, SUBLANES, LANES), 1)
    c = lax.broadcasted_iota(jnp.int32, (1, SUBLANES, LANES), 2) // S5_GROUP
    for d in (4, 2, 1):
        sb = (s & d) != 0
        cb = (c & d) != 0
        if 2 * d == SUBLANES:
            t = pltpu.roll(pltpu.roll(x, d, axis=1), S5_GROUP * d, axis=2)
        else:
            xs = jnp.where(sb, pltpu.roll(x, d, axis=1), pltpu.roll(x, SUBLANES - d, axis=1))
            t = jnp.where(cb, pltpu.roll(xs, S5_GROUP * d, axis=2), pltpu.roll(xs, LANES - S5_GROUP * d, axis=2))
        x = jnp.where(sb != cb, t, x)
    return x


def _s5_kernel(u_ref, w1_ref, n_ref, a_ref, d_ref, x0_ref, y_ref, xf_ref, mm_scr, xs_scr, st_scr, *, tk, bsz):
    j = pl.program_id(1)
    n_g = st_scr.shape[0]
    rows = tk * bsz
    of_group = lambda g: pl.ds(g, rows, stride=n_g)

    @pl.when(j == 0)
    def _():
        st_scr[...] = x0_ref[0]

    for g in range(n_g):
        ub = u_ref[of_group(g), :].astype(BF16)
        mm_scr[g] = jnp.dot(ub, w1_ref[0, g], preferred_element_type=F32)

    half = n_g // 2
    for g0 in (0, half):
        coef = [a_ref[0, g] for g in range(g0, g0 + half)]

        def step(k, carry, g0=g0, coef=coef):
            r = pl.ds(pl.multiple_of(k * bsz, bsz), bsz)
            out = []
            for i in range(half):
                x, xsw = carry[2 * i], carry[2 * i + 1]
                a1, a2, a2s = coef[i][0:1], coef[i][1:2], coef[i][2:3]
                xs_scr[g0 + i, r, :] = x
                out.append(a1 * x + a2 * xsw + mm_scr[g0 + i, r, LANES:2 * LANES])
                out.append(a1 * xsw + a2s * x + mm_scr[g0 + i, r, 2 * LANES:3 * LANES])
            return tuple(out)

        fin = lax.fori_loop(0, tk, step, tuple(st_scr[g0 + i, v] for i in range(half) for v in range(2)))
        for i in range(half):
            st_scr[g0 + i, 0] = fin[2 * i]
            st_scr[g0 + i, 1] = fin[2 * i + 1]

    for g in range(n_g):
        y = (mm_scr[g, :, 0:LANES] + jnp.dot(xs_scr[g].astype(BF16), n_ref[0, g], preferred_element_type=F32)
             + d_ref[0, g] * u_ref[of_group(g), :])
        y_ref[of_group(g), :] = jax.nn.gelu(y)

    @pl.when(j == pl.num_programs(1) - 1)
    def _():
        xf_ref[0] = st_scr[:, 0]


def _cmul(ar, ai, br, bi):
    return ar * br - ai * bi, ar * bi + ai * br


def _s5_matrices(a_re, a_im, log_dt, b_re, b_im, c_re, c_im):
    g, p = a_re.shape
    hdim = b_re.shape[-1]
    dt = jnp.exp(log_dt)[:, None]
    mag = jnp.exp(a_re * dt)
    ab_re, ab_im = mag * jnp.cos(a_im * dt), mag * jnp.sin(a_im * dt)
    den = a_re * a_re + a_im * a_im
    num_re, num_im = ab_re - 1.0, ab_im
    g_re = (num_re * a_re + num_im * a_im) / den
    g_im = (num_im * a_re - num_re * a_im) / den
    bb_re = g_re[..., None] * b_re - g_im[..., None] * b_im
    bb_im = g_re[..., None] * b_im + g_im[..., None] * b_re
    pw_re, pw_im = jnp.ones((1, g, p), F32), jnp.zeros((1, g, p), F32)
    sq_re, sq_im = ab_re, ab_im
    while pw_re.shape[0] < S5_SUB + 1:
        nr, ni = _cmul(pw_re, pw_im, sq_re, sq_im)
        pw_re, pw_im = jnp.concatenate([pw_re, nr]), jnp.concatenate([pw_im, ni])
        sq_re, sq_im = _cmul(sq_re, sq_im, sq_re, sq_im)
    pw_re, pw_im = pw_re[:S5_SUB + 1], pw_im[:S5_SUB + 1]
    ca_re, ca_im = _cmul(c_re[None], c_im[None], pw_re[:, :, None, :], pw_im[:, :, None, :])
    kern = (jnp.einsum('tghp,gpk->tghk', ca_re[:S5_SUB], bb_re, precision=_HI)
            - jnp.einsum('tghp,gpk->tghk', ca_im[:S5_SUB], bb_im, precision=_HI))
    s_in = np.arange(S5_SUB)[:, None, None]
    s_out = np.arange(S5_SUB)[None, :, None]
    lag = jnp.asarray(s_out - s_in == np.arange(S5_SUB), F32)
    m = jnp.einsum('iot,tghk->ioghk', lag, kern, precision=_HI)
    m = jnp.transpose(m, (2, 0, 4, 1, 3)).reshape(g, S5_SUB * hdim, S5_SUB * hdim)
    pr, pi = _cmul(pw_re[S5_SUB - 1::-1][..., None], pw_im[S5_SUB - 1::-1][..., None], bb_re[None], bb_im[None])
    flat = lambda a: jnp.transpose(a, (1, 0, 3, 2)).reshape(g, S5_SUB * hdim, p)
    nm = jnp.concatenate([ca_re[1:], -ca_im[1:]], axis=3)
    nm = jnp.transpose(nm, (1, 3, 0, 2)).reshape(g, 2 * p, S5_SUB * hdim)
    w1 = jnp.concatenate([m, flat(pr), flat(pi), flat(pi), flat(pr)], axis=2)
    a_r, a_i = pw_re[S5_SUB], pw_im[S5_SUB]
    coef = jnp.stack([jnp.concatenate([a_r, a_r], axis=1), jnp.concatenate([-a_i, a_i], axis=1),
                      jnp.concatenate([a_i, -a_i], axis=1)], axis=1)
    return w1.astype(BF16), nm.astype(BF16), coef


def _s5(u2, mats, d_skip, x0_re, x0_im):
    w1, nm, coef = mats
    n_ct, n_sub, b, n_g, lanes = u2.shape
    g, p2 = nm.shape[0], nm.shape[1]
    hdim = lanes // S5_SUB
    tk = min(n_sub, S5_MAX_SUBS)
    x0 = jnp.transpose(jnp.concatenate([x0_re, x0_im], axis=2), (1, 0, 2))
    x0 = jnp.stack([x0, jnp.roll(x0, p2 // 2, axis=2)], axis=1)
    dt = jnp.tile(d_skip.reshape(g, 1, hdim), (1, 1, S5_SUB))
    by_tile = lambda a: a.reshape((n_ct, n_g) + a.shape[1:])
    pspec = lambda a: pl.BlockSpec((1,) + a.shape[1:], lambda c, j: (c,) + (0,) * (a.ndim - 1))
    rows = tk * b
    uspec = pl.BlockSpec((None, rows * n_g, lanes), lambda c, j: (c, j, 0))
    params = [by_tile(a) for a in (w1, nm, coef, dt, x0)]
    y, xf = pl.pallas_call(
        functools.partial(_s5_kernel, tk=tk, bsz=b),
        grid=(n_ct, n_sub // tk),
        in_specs=[uspec] + [pspec(a) for a in params],
        out_specs=[uspec, pl.BlockSpec((1, n_g, b, p2), lambda c, j: (c, 0, 0, 0))],
        out_shape=[jax.ShapeDtypeStruct((n_ct, n_sub * b * n_g, lanes), F32),
                   jax.ShapeDtypeStruct((n_ct, n_g, b, p2), F32)],
        scratch_shapes=[pltpu.VMEM((n_g, rows, 3 * lanes), F32), pltpu.VMEM((n_g, rows, lanes), F32),
                        pltpu.VMEM((n_g, 2, b, p2), F32)],
        compiler_params=_params(2),
        name="s5",
    )(u2.reshape(n_ct, n_sub * b * n_g, lanes), *params)
    xf = jnp.transpose(xf.reshape(g, b, p2), (1, 0, 2))
    return y.reshape(u2.shape), xf[:, :, :p2 // 2], xf[:, :, p2 // 2:]


def _out_ffn_kernel(*refs, with_s5, cb):
    if with_s5:
        (x_ref, mod_ref, m1_ref, m2_ref, wglu_ref, bglu_ref, wout_ref, n2g_ref,
         wup_ref, cw_ref, cbias_ref, wd_ref, pre_ref, xo_ref, ulast_ref, carry_scr, h_scr, row_scr) = refs
    else:
        (x_ref, mod_ref, m1_ref, m2_ref, wout_ref, n2g_ref,
         wup_ref, cw_ref, cbias_ref, wd_ref, pre_ref, xo_ref, ulast_ref, carry_scr, h_scr, row_scr) = refs
    t = pl.program_id(1)

    @pl.when(t == 0)
    def _():
        carry_scr[...] = pre_ref[0]

    mod = mod_ref[0]
    half = m2_ref.shape[2]
    out2 = jnp.dot(m2_ref[0], wout_ref[half:, :], preferred_element_type=F32)
    if with_s5:
        tiles = []
        glu = bglu_ref[...]
        for c in range(m1_ref.shape[0]):
            tiles.append(_chunk_transpose(m1_ref[c]).reshape(-1, LANES))
            glu = glu + jnp.dot(tiles[-1].astype(BF16), wglu_ref[c * LANES:(c + 1) * LANES, :],
                                preferred_element_type=F32)
        m1 = (jnp.concatenate(tiles, axis=1) * jax.nn.sigmoid(glu)).astype(BF16)
    else:
        m1 = m1_ref[0]
    out = jnp.dot(m1, wout_ref[0:half, :], preferred_element_type=F32) + out2
    x1 = x_ref[0] + mod[2:3] * out
    tb = x1.shape[0]
    _to_segments(row_scr, _norm_mod(x1, n2g_ref[...], mod[4:5], mod[3:4]))
    hn = _load_interleaved(row_scr, tb).astype(BF16)
    dff = wd_ref.shape[0]
    n_cb = dff // cb
    a_cols = lambda j: slice(j * cb, (j + 1) * cb)
    g_cols = lambda j: slice(dff + j * cb, dff + (j + 1) * cb)

    def up(j):
        return (jnp.dot(hn, wup_ref[:, a_cols(j)], preferred_element_type=F32),
                jnp.dot(hn, wup_ref[:, g_cols(j)], preferred_element_type=F32))

    def conv(u, cols):
        out, carry_scr[:, cols] = _causal_conv_interleaved(u, carry_scr[:, cols], cw_ref[:, cols], cbias_ref[:, cols])
        return out

    nxt = up(0)
    for j in range(n_cb):
        ua, ug = nxt
        if j + 1 < n_cb:
            nxt = up(j + 1)
        h_scr[:, a_cols(j)] = (_silu(conv(ua, a_cols(j))) * conv(ug, g_cols(j))).astype(BF16)
    _store_interleaved(row_scr, jnp.dot(h_scr[...], wd_ref[...], preferred_element_type=F32))
    xo_ref[0] = x1 + mod[5:6] * _from_segments(row_scr, tb)

    @pl.when(t == pl.num_programs(1) - 1)
    def _():
        ulast_ref[0] = carry_scr[...]


def _out_ffn(x, mod, m1, m2, w_out, norm2_g, w_up, conv_w, conv_b, w_down, prefix, tb, s5_extra=None):
    b, t, d = x.shape
    dff = w_down.shape[0]
    assert dff % FFN_COL_BLOCK == 0
    width = conv_w.shape[0]
    half = m2.shape[2]
    w_up_b = w_up.astype(BF16)
    cbias = conv_b.reshape(1, 2 * dff)
    wd = w_down.astype(BF16)
    pre8 = jnp.concatenate([jnp.zeros((b, SUBLANES - (width - 1), 2 * dff), F32), prefix], axis=1)
    tile = lambda n: pl.BlockSpec((1, tb, n), lambda i, j: (i, j, 0))
    const = lambda a: pl.BlockSpec(a.shape, lambda i, j: (0,) * a.ndim, pipeline_mode=pl.Buffered(1))
    bspec = lambda a: pl.BlockSpec((1,) + a.shape[1:], lambda i, j: (i,) + (0,) * (a.ndim - 1))
    wo = w_out.astype(BF16)
    n2g = norm2_g.reshape(1, d)
    args = [x, mod]
    specs = [tile(d), bspec(mod)]
    if s5_extra is not None:
        w_glu, b_glu = s5_extra
        w_glu, b_glu = w_glu.astype(BF16), b_glu.reshape(1, half)
        args += [m1, m2, w_glu, b_glu]
        specs += [pl.BlockSpec((m1.shape[0], tb // S5_SUB, None) + m1.shape[3:], lambda i, j: (0, j, i, 0, 0)),
                  tile(m2.shape[2]), const(w_glu), const(b_glu)]
    else:
        args += [m1, m2]
        specs += [tile(half), tile(m2.shape[2])]
    weights = [wo, n2g, w_up_b, conv_w, cbias, wd]
    args += weights + [pre8]
    specs += [const(a) for a in weights] + [bspec(pre8)]
    xo, ulast = pl.pallas_call(
        functools.partial(_out_ffn_kernel, with_s5=s5_extra is not None, cb=FFN_COL_BLOCK),
        grid=(b, t // tb),
        in_specs=specs,
        out_specs=[tile(d), bspec(pre8)],
        out_shape=[jax.ShapeDtypeStruct((b, t, d), F32), jax.ShapeDtypeStruct(pre8.shape, F32)],
        scratch_shapes=[pltpu.VMEM(pre8.shape[1:], F32), pltpu.VMEM((tb, dff), BF16),
                        pltpu.VMEM((d // LANES, SUBLANES * _segment_pitch(tb), LANES), F32)],
        compiler_params=_params(2),
        name="out_ffn_s5" if s5_extra is not None else "out_ffn",
    )(*args)
    return xo, ulast[:, SUBLANES - (width - 1):]


def _head_rms(x, g):
    left = lax.broadcasted_iota(jnp.int32, (1, LANES), 1) < HEAD_DIM
    out = []
    for c in range(x.shape[1] // LANES):
        xt = x[:, c * LANES:(c + 1) * LANES]
        sq = xt * xt
        s_left = jnp.sum(jnp.where(left, sq, 0.0), axis=-1, keepdims=True)
        s_right = jnp.sum(jnp.where(left, 0.0, sq), axis=-1, keepdims=True)
        ms = jnp.where(left, s_left, s_right) * (1.0 / HEAD_DIM)
        out.append(xt * lax.rsqrt(ms + RMS_EPS))
    return jnp.concatenate(out, axis=1) * g


def _od_in_kernel(x_ref, mod_ref, g_ref, w_ref, qg_ref, kg_ref, dtb_ref,
                  q_ref, k_ref, v_ref, zg_ref, xbc_ref, dt_ref):
    mod = mod_ref[0]
    nq = q_ref.shape[2]
    nkv = k_ref.shape[2]
    hn = _norm_mod(x_ref[0], g_ref[...], mod[1:2], mod[0:1]).astype(BF16)
    dot = lambda lo, hi: jnp.dot(hn, w_ref[:, lo:hi], preferred_element_type=F32)
    o_zg = nq + 2 * nkv
    o_xbc = o_zg + zg_ref.shape[2]
    o_dt = o_xbc + xbc_ref.shape[2]
    dt_ref[0] = _softplus(dot(o_dt, o_dt + LANES) + dtb_ref[...])
    q_ref[0] = _head_rms(dot(0, nq), qg_ref[...]).astype(BF16)
    kv = dot(nq, nq + 2 * nkv)
    k_ref[0] = _head_rms(kv[:, 0:nkv], kg_ref[...])
    v_ref[0] = kv[:, nkv:2 * nkv]
    zg_ref[0] = dot(o_zg, o_xbc)
    xbc_ref[0] = dot(o_xbc, o_dt)


def _od_in(x, mod, norm_g, w_in, q_norm, k_norm, dt_bias, dims, tb):
    b, t, d = x.shape
    nq, nkv, nz, nxbc, nh = dims
    w = jnp.concatenate([w_in, jnp.zeros((d, LANES - nh), F32)], axis=1).astype(BF16)
    qg = jnp.tile(q_norm, nq // HEAD_DIM).reshape(1, nq)
    kg = jnp.tile(k_norm, nkv // HEAD_DIM).reshape(1, nkv)
    dtb = jnp.concatenate([dt_bias, jnp.zeros((LANES - nh,), F32)]).reshape(1, LANES)
    g2 = norm_g.reshape(1, d)
    tile = lambda n: pl.BlockSpec((1, tb, n), lambda i, j: (i, j, 0))
    const = lambda a: pl.BlockSpec(a.shape, lambda i, j: (0,) * a.ndim)
    widths = (nq, nkv, nkv, nz, nxbc, LANES)
    return pl.pallas_call(
        _od_in_kernel,
        grid=(b, t // tb),
        in_specs=[tile(d), pl.BlockSpec((1, 6, d), lambda i, j: (i, 0, 0)), const(g2), const(w),
                  const(qg), const(kg), const(dtb)],
        out_specs=[tile(n) for n in widths],
        out_shape=[jax.ShapeDtypeStruct((b, t, n), BF16 if i == 0 else F32) for i, n in enumerate(widths)],
        compiler_params=_params(2),
        name="od_in",
    )(x, mod, g2, w, qg, kg, dtb)


def _swa_kernel(q_ref, k_ref, v_ref, k0_ref, v0_ref, bias_ref, sink_ref, o_ref, kx, vx,
                *, n_blocks, nq, mask_start):
    t = pl.program_id(1)
    tb = q_ref.shape[1]
    n_kv = k_ref.shape[2] // HEAD_DIM
    rows_q = nq * CHUNK
    tail = kx.shape[1] - WINDOW - tb
    left = lax.broadcasted_iota(jnp.int32, (1, LANES), 1) < HEAD_DIM
    first_side = lax.broadcasted_iota(jnp.int32, (2 * SWA_KEYS, 1), 0) < SWA_KEYS
    side_ones = jnp.where(first_side == left, 1.0, 0.0).astype(BF16)
    col = lax.broadcasted_iota(jnp.int32, (1, SWA_KEYS), 1)

    def place(dst, rows, x):
        rolled = pltpu.roll(x, HEAD_DIM, axis=1)
        dst[0, rows, :] = jnp.where(left, x, 0.0).astype(BF16)
        dst[1, rows, :] = jnp.where(left, 0.0, rolled).astype(BF16)
        dst[2, rows, :] = jnp.where(left, rolled, 0.0).astype(BF16)
        dst[3, rows, :] = jnp.where(left, 0.0, x).astype(BF16)

    @pl.when(t == 0)
    def _():
        place(kx, slice(0, WINDOW), k0_ref[0])
        place(vx, slice(0, WINDOW), v0_ref[0])
        for i in range(2 * n_kv if tail else 0):
            kx[i, WINDOW + tb:, :] = jnp.zeros((tail, LANES), BF16)
            vx[i, WINDOW + tb:, :] = jnp.zeros((tail, LANES), BF16)

    place(kx, slice(WINDOW, WINDOW + tb), k_ref[0])
    place(vx, slice(WINDOW, WINDOW + tb), v_ref[0])

    def block(blk, carry):
        r0 = pl.multiple_of(blk * rows_q, rows_q)
        keys = pl.ds(r0, SWA_KEYS)
        if mask_start:
            valid = t * tb + r0 - WINDOW + col >= 0
        probs = []
        sinks = []
        for j in range(n_kv):
            lo = 2 * j * LANES
            qg = jnp.concatenate([q_ref[0, pl.ds(r0 + cq * CHUNK, CHUNK), lo + r * LANES:lo + (r + 1) * LANES]
                                  for cq in range(nq) for r in range(2)], axis=0).astype(BF16)
            kcat = jnp.concatenate([kx[2 * j, keys, :], kx[2 * j + 1, keys, :]], axis=0)
            s_both = lax.dot_general(qg, kcat, _NT, preferred_element_type=F32) * HEAD_DIM ** -0.5
            for side in range(2):
                s = s_both[:, side * SWA_KEYS:(side + 1) * SWA_KEYS] + bias_ref[j, side]
                if mask_start:
                    s = jnp.where(valid, s, NEG_INF)
                m = jnp.max(s, axis=-1, keepdims=True)
                probs.append(jnp.exp(s - m).astype(BF16))
                sinks.append(jnp.exp(sink_ref[j, side] - m))
        for j in range(n_kv):
            lo = 2 * j * LANES
            vcat = jnp.concatenate(
                [jnp.concatenate([vx[2 * j, keys, :], vx[2 * j + 1, keys, :]], axis=0), side_ones], axis=1)
            pv = jnp.dot(jnp.concatenate(probs[2 * j:2 * j + 2], axis=1), vcat, preferred_element_type=F32)
            den = pv[:, LANES:] + jnp.where(left, sinks[2 * j], sinks[2 * j + 1])
            out = pv[:, 0:LANES] / den
            for cq in range(nq):
                for r in range(2):
                    o_ref[0, pl.ds(r0 + cq * CHUNK, CHUNK), lo + r * LANES:lo + (r + 1) * LANES] = (
                        out[(2 * cq + r) * CHUNK:(2 * cq + r + 1) * CHUNK].astype(o_ref.dtype))
        return carry

    lax.fori_loop(0, n_blocks, block, 0, unroll=math.gcd(n_blocks, 8))
    for i in range(2 * n_kv):
        kx[i, 0:WINDOW, :] = kx[i, tb:tb + WINDOW, :]
        vx[i, 0:WINDOW, :] = vx[i, tb:tb + WINDOW, :]


def _t5_bucket(rel):
    nb = T5_BUCKETS // 2
    max_exact = nb // 2
    ret = (rel > 0).astype(jnp.int32) * nb
    n = jnp.abs(rel)
    nf = jnp.maximum(n, 1).astype(F32)
    large = max_exact + (jnp.log(nf / max_exact) / math.log(T5_MAX_DIST / max_exact)
                         * (nb - max_exact)).astype(jnp.int32)
    large = jnp.minimum(large, nb - 1)
    return ret + jnp.where(n < max_exact, n, large)


def _rel_bias(table):
    rel = (jnp.arange(WINDOW + CHUNK)[None, :] - WINDOW) - jnp.arange(CHUNK)[:, None]
    onehot = (_t5_bucket(rel)[..., None] == jnp.arange(T5_BUCKETS)).astype(F32)
    return jnp.einsum('qkb,bh->hqk', onehot, table, precision=_HI)


def _swa(q, k, v, k0, v0, bias, sink, tb, mask_start):
    b, t, nq = q.shape
    nkv = k.shape[2]
    n_kv = nkv // HEAD_DIM
    assert nkv == LANES and nq == 2 * n_kv * LANES
    by_side = lambda a: jnp.transpose(a.reshape((n_kv, 2, 2) + a.shape[1:]), (0, 2, 1) + tuple(range(3, a.ndim + 2)))
    span = WINDOW + CHUNK
    n_chunks = tb // CHUNK
    cpb = 2 if n_chunks % 2 == 0 else 1
    assert WINDOW + cpb * CHUNK <= SWA_KEYS
    bias2 = by_side(bias).reshape(n_kv, 2, 2 * CHUNK, span)
    bias3 = jnp.concatenate(
        [jnp.pad(bias2, ((0, 0), (0, 0), (0, 0), (cq * CHUNK, SWA_KEYS - span - cq * CHUNK)),
                 constant_values=NEG_INF) for cq in range(cpb)], axis=2)
    sink3 = jnp.tile(jnp.repeat(by_side(sink), CHUNK, axis=2), (1, 1, cpb))
    sink3 = jnp.broadcast_to(sink3[..., None], sink3.shape + (LANES,))
    tile = lambda n: pl.BlockSpec((1, tb, n), lambda i, j: (i, j, 0))
    wspec = pl.BlockSpec((1, WINDOW, nkv), lambda i, j: (i, 0, 0))
    const = lambda a: pl.BlockSpec(a.shape, lambda i, j: (0,) * a.ndim)
    rows = WINDOW + max(tb, WINDOW) + SWA_KEYS - WINDOW - cpb * CHUNK
    return pl.pallas_call(
        functools.partial(_swa_kernel, n_blocks=n_chunks // cpb, nq=cpb, mask_start=mask_start),
        grid=(b, t // tb),
        in_specs=[tile(nq), tile(nkv), tile(nkv), wspec, wspec, const(bias3), const(sink3)],
        out_specs=tile(nq),
        out_shape=jax.ShapeDtypeStruct((b, t, nq), BF16),
        scratch_shapes=[pltpu.VMEM((2 * n_kv, rows, LANES), BF16), pltpu.VMEM((2 * n_kv, rows, LANES), BF16)],
        compiler_params=_params(2),
        name="swa",
    )(q, k, v, k0, v0, bias3, sink3)


def _ssd_kernel(xbc_ref, zg_ref, dt_ref, pre_ref, cw_ref, cb_ref, aexp_ref, dexp_ref,
                e_ref, ng_ref, s0_ref, y_ref, sfin_ref, clast_ref, st_scr, hist_scr, seg_scr, dte_scr,
                *, n_chunks):
    t = pl.program_id(1)
    tb = xbc_ref.shape[1]
    inner = zg_ref.shape[2]
    n_heads = inner // SSD_HEAD_DIM
    hpg = n_heads // SSD_GROUPS
    gn = SSD_GROUPS * SSD_STATE
    n_hs = n_heads * CHUNK

    @pl.when(t == 0)
    def _():
        st_scr[...] = s0_ref[0]
        hist_scr[...] = pre_ref[0]

    _to_segments(seg_scr, xbc_ref[0])
    conv, hist_scr[...] = _causal_conv_interleaved(_load_interleaved(seg_scr, tb), hist_scr[...],
                                                   cw_ref[...], cb_ref[...])
    _store_interleaved(seg_scr, _silu(conv))
    dte_scr[...] = _split_dot(dt_ref[0], e_ref[...])

    row_hs = lax.broadcasted_iota(jnp.int32, (n_hs, 1), 0)
    own_group = row_hs // (hpg * CHUNK) == lax.broadcasted_iota(jnp.int32, (1, gn), 1) // SSD_STATE
    own_head = row_hs // CHUNK == lax.broadcasted_iota(jnp.int32, (1, inner), 1) // SSD_HEAD_DIM
    state_group = (lax.broadcasted_iota(jnp.int32, (gn, 1), 0) // SSD_STATE
                   == lax.broadcasted_iota(jnp.int32, (1, inner), 1) // (hpg * SSD_HEAD_DIM))
    first_group = lax.broadcasted_iota(jnp.int32, (1, inner), 1) < hpg * SSD_HEAD_DIM
    step_row = lax.broadcasted_iota(jnp.int32, (CHUNK, 1), 0)
    step_lane = lax.broadcasted_iota(jnp.int32, (1, n_hs), 1) % CHUNK
    zero = jnp.zeros((), BF16)
    aexp = aexp_ref[...]

    for c in range(n_chunks):
        rows = slice(c * CHUNK, (c + 1) * CHUNK)
        xc = _rows_from_segments(seg_scr, c * CHUNK, CHUNK, tb)
        xs = xc[:, 0:inner]
        bm = xc[:, inner:inner + gn]
        cm = xc[:, inner + gn:inner + 2 * gn].astype(BF16)
        dte = dte_scr[rows, :]
        cum = _cumsum_rows(dte * aexp)
        cum_last = cum[CHUNK - 1:CHUNK, :]
        cum_at_step = jnp.sum(jnp.where(step_row == step_lane, cum, 0.0), axis=0, keepdims=True)
        decay = jnp.exp(jnp.where(step_row >= step_lane, cum - cum_at_step, NEG_INF))
        bm_bd = jnp.where(own_group, jnp.concatenate([bm.astype(BF16)] * n_heads, axis=0), zero)
        cb = lax.dot_general(cm, bm_bd, _NT, preferred_element_type=F32)
        xdt = (xs * dte).astype(BF16)
        xdt_bd = jnp.where(own_head, jnp.concatenate([xdt] * n_heads, axis=0), zero)
        st = st_scr[...]
        st_bd = jnp.where(state_group, jnp.concatenate([st.astype(BF16)] * SSD_GROUPS, axis=0), zero)
        y = (jnp.dot((cb * decay).astype(BF16), xdt_bd, preferred_element_type=F32)
             + jnp.exp(cum) * jnp.dot(cm, st_bd, preferred_element_type=F32))
        xw = (xs * (jnp.exp(cum_last - cum) * dte)).astype(BF16)
        upd = jnp.dot(bm.T.astype(BF16), xw, preferred_element_type=F32)
        st_scr[...] = st * jnp.exp(cum_last) + jnp.where(first_group, upd[0:SSD_STATE], upd[SSD_STATE:])
        yd = (y + dexp_ref[...] * xs) * _silu(zg_ref[0, rows, :])
        ms = jnp.mean(yd * yd, axis=-1, keepdims=True)
        y_ref[0, rows, :] = (yd * lax.rsqrt(ms + RMS_EPS) * ng_ref[...]).astype(y_ref.dtype)

    @pl.when(t == pl.num_programs(1) - 1)
    def _():
        sfin_ref[0] = st_scr[...]
        clast_ref[0] = hist_scr[...]


def _ssd(xbc, zg, dt, conv_prefix, conv_w, conv_b, a_log, d_skip, norm_g, s0, tb):
    b, t, nxbc = xbc.shape
    inner = zg.shape[2]
    nh = a_log.shape[0]
    width = conv_w.shape[0]
    assert SSD_GROUPS == 2 and SSD_HEAD_DIM == CHUNK
    pre8 = jnp.concatenate([jnp.zeros((b, SUBLANES - (width - 1), nxbc), F32), conv_prefix], axis=1)
    a = -jnp.exp(a_log)
    aexp = jnp.repeat(a, SSD_HEAD_DIM).reshape(1, inner)
    dexp = jnp.repeat(d_skip, SSD_HEAD_DIM).reshape(1, inner)
    expand = (np.arange(LANES)[:, None] == np.arange(inner)[None, :] // SSD_HEAD_DIM)
    expand = jnp.asarray(expand, BF16)
    s0t = jnp.transpose(s0, (0, 3, 1, 2)).reshape(b, SSD_STATE, inner)
    ng = norm_g.reshape(1, inner)
    cb2 = conv_b.reshape(1, nxbc)
    tile = lambda n: pl.BlockSpec((1, tb, n), lambda i, j: (i, j, 0))
    const = lambda arr: pl.BlockSpec(arr.shape, lambda i, j: (0,) * arr.ndim)
    bspec = lambda arr: pl.BlockSpec((1,) + arr.shape[1:], lambda i, j: (i,) + (0,) * (arr.ndim - 1))
    y, sfin, clast = pl.pallas_call(
        functools.partial(_ssd_kernel, n_chunks=tb // CHUNK),
        grid=(b, t // tb),
        in_specs=[tile(nxbc), tile(inner), tile(LANES),
                  bspec(pre8), const(conv_w), const(cb2), const(aexp), const(dexp),
                  const(expand), const(ng), bspec(s0t)],
        out_specs=[tile(inner), bspec(s0t), bspec(pre8)],
        out_shape=[jax.ShapeDtypeStruct((b, t, inner), BF16),
                   jax.ShapeDtypeStruct(s0t.shape, F32),
                   jax.ShapeDtypeStruct(pre8.shape, F32)],
        scratch_shapes=[pltpu.VMEM((SSD_STATE, inner), F32), pltpu.VMEM((SUBLANES, nxbc), F32),
                        pltpu.VMEM((nxbc // LANES, SUBLANES * _segment_pitch(tb), LANES), F32),
                        pltpu.VMEM((tb, inner), F32)],
        compiler_params=_params(2),
        name="ssd",
    )(xbc, zg, dt, pre8, conv_w, cb2, aexp, dexp, expand, ng, s0t)
    sfin = jnp.transpose(sfin.reshape(b, SSD_STATE, nh, SSD_HEAD_DIM), (0, 2, 3, 1))
    return y, sfin, clast[:, SUBLANES - (width - 1):]


def _trunk(x, mods, P, st, sample):
    b, t, d = x.shape
    tb = min(MAX_TILE, t)
    tbm = min(MIXER_TILE, t)
    depth = P['w_mod'].shape[0]
    new = {name: [] for name in ('s5_re', 's5_im', 'gla', 'swa_k', 'swa_v', 'ssd', 'ssd_conv', 'ffn_conv')}
    for layer in range(depth):
        i = layer // 2
        mod = mods[layer].reshape(b, 6, d)
        ffn = (P['norm2_g'][layer], P['ffn_w_up'][layer], P['ffn_conv_w'][layer], P['ffn_conv_b'][layer],
               P['ffn_w_down'][layer], st['ffn_conv'][layer], tb)
        if layer % 2 == 0:
            u, qk, v, r, gate = _ev_in(x, mod, P['norm1_g'][layer], P['ev_w_in'][i], P['gla_w_gate2'][i],
                                       P['gla_b_gate'][i], tbm)
            mats = _s5_matrices(P['s5_a_re'][i], P['s5_a_im'][i], P['s5_log_dt'][i], P['s5_b_re'][i],
                                P['s5_b_im'][i], P['s5_c_re'][i], P['s5_c_im'][i])
            ya, sr, si = _s5(u, mats, P['s5_d'][i], st['s5_re'][i], st['s5_im'][i])
            ob, sg = _gla(qk, v, gate, r, st['gla'][i], P['gla_norm_g'][i], tb)
            new['s5_re'].append(sr)
            new['s5_im'].append(si)
            new['gla'].append(sg)
            x, fp = _out_ffn(x, mod, ya, ob, P['ev_w_out'][i], *ffn,
                             s5_extra=(P['s5_w_glu'][i], P['s5_b_glu'][i]))
        else:
            nq = P['swa_sink'].shape[1] * HEAD_DIM
            nkv = SWA_KV_HEADS * HEAD_DIM
            inner = P['ssd_norm_g'].shape[1]
            nxbc = P['ssd_conv_w'].shape[2]
            nh = P['ssd_a_log'].shape[1]
            q, k, v, zg, xbc, dt = _od_in(x, mod, P['norm1_g'][layer], P['od_w_in'][i], P['swa_q_norm'][i],
                                               P['swa_k_norm'][i], P['ssd_dt_bias'][i],
                                               (nq, nkv, inner, nxbc, nh), tbm)
            bias = _rel_bias(P['t5_bias'])
            if sample:
                k0 = st['swa_k'][i].reshape(b, WINDOW, nkv)
                v0 = st['swa_v'][i].reshape(b, WINDOW, nkv)
            else:
                k0 = v0 = jnp.zeros((b, WINDOW, nkv), F32)
            oc = _swa(q, k, v, k0, v0, bias, P['swa_sink'][i], tbm, mask_start=not sample)
            yd, ss, sc = _ssd(xbc, zg, dt, st['ssd_conv'][i], P['ssd_conv_w'][i], P['ssd_conv_b'][i],
                              P['ssd_a_log'][i], P['ssd_d'][i], P['ssd_norm_g'][i], st['ssd'][i], tbm)
            keep = slice(None) if sample else slice(t - WINDOW, t)
            new['swa_k'].append(k[:, keep].reshape(b, -1, SWA_KV_HEADS, HEAD_DIM))
            new['swa_v'].append(v[:, keep].reshape(b, -1, SWA_KV_HEADS, HEAD_DIM))
            new['ssd'].append(ss)
            new['ssd_conv'].append(sc)
            x, fp = _out_ffn(x, mod, oc, yd, P['od_w_out'][i], *ffn)
        new['ffn_conv'].append(fp)
    return x, {name: jnp.stack(vals) for name, vals in new.items()}


def kernel(x_prompt, x_sample, state_s5_re, state_s5_im, state_gla, cache_swa_k, cache_swa_v, state_ssd, state_ssd_conv, state_ffn_conv, c_prompt, c_sample, t5_bias, norm1_g, norm2_g, w_mod, b_mod, ffn_w_up, ffn_conv_w, ffn_conv_b, ffn_w_down, ev_w_in, ev_w_out, s5_a_re, s5_a_im, s5_log_dt, s5_b_re, s5_b_im, s5_c_re, s5_c_im, s5_d, s5_w_glu, s5_b_glu, gla_w_gate2, gla_b_gate, gla_norm_g, od_w_in, od_w_out, swa_q_norm, swa_k_norm, swa_sink, ssd_conv_w, ssd_conv_b, ssd_dt_bias, ssd_a_log, ssd_d, ssd_norm_g):
    P = dict(t5_bias=t5_bias, norm1_g=norm1_g, norm2_g=norm2_g, w_mod=w_mod, b_mod=b_mod,
             ffn_w_up=ffn_w_up, ffn_conv_w=ffn_conv_w, ffn_conv_b=ffn_conv_b, ffn_w_down=ffn_w_down,
             ev_w_in=ev_w_in, ev_w_out=ev_w_out, s5_a_re=s5_a_re, s5_a_im=s5_a_im, s5_log_dt=s5_log_dt,
             s5_b_re=s5_b_re, s5_b_im=s5_b_im, s5_c_re=s5_c_re, s5_c_im=s5_c_im, s5_d=s5_d,
             s5_w_glu=s5_w_glu, s5_b_glu=s5_b_glu, gla_w_gate2=gla_w_gate2, gla_b_gate=gla_b_gate,
             gla_norm_g=gla_norm_g, od_w_in=od_w_in, od_w_out=od_w_out, swa_q_norm=swa_q_norm,
             swa_k_norm=swa_k_norm, swa_sink=swa_sink, ssd_conv_w=ssd_conv_w, ssd_conv_b=ssd_conv_b,
             ssd_dt_bias=ssd_dt_bias, ssd_a_log=ssd_a_log, ssd_d=ssd_d, ssd_norm_g=ssd_norm_g)
    bp = x_prompt.shape[0]
    n_even, n_odd = state_s5_re.shape[0], state_ssd.shape[0]
    depth = w_mod.shape[0]
    zeros_like_b = lambda a: jnp.zeros((a.shape[0], bp) + a.shape[2:], F32)
    zero_st = dict(s5_re=zeros_like_b(state_s5_re), s5_im=zeros_like_b(state_s5_im), gla=zeros_like_b(state_gla),
                   ssd=zeros_like_b(state_ssd), ssd_conv=zeros_like_b(state_ssd_conv),
                   ffn_conv=zeros_like_b(state_ffn_conv))
    sample_st = dict(s5_re=state_s5_re, s5_im=state_s5_im, gla=state_gla, swa_k=cache_swa_k,
                     swa_v=cache_swa_v, ssd=state_ssd, ssd_conv=state_ssd_conv, ffn_conv=state_ffn_conv)
    mods = _modulation(jnp.concatenate([c_prompt, c_sample], axis=0), w_mod, b_mod)
    y_prompt, stp = _trunk(x_prompt, mods[:, :bp], P, zero_st, False)
    y_sample, sts = _trunk(x_sample, mods[:, bp:], P, sample_st, True)
    names = ('s5_re', 's5_im', 'gla', 'swa_k', 'swa_v', 'ssd', 'ssd_conv', 'ffn_conv')
    return (y_prompt, y_sample) + tuple(stp[n] for n in names) + tuple(sts[n] for n in names)
```
